```python
import math
import jax, jax.numpy as jnp
from jax import lax
import numpy as np

D_MODEL = 1024
BATCH = 1
SEQ = 16384
DEPTH = 1

MIX_WIDTH = D_MODEL
HG_HEADS = 4
HG_VAL_DIM = (MIX_WIDTH // 2) // HG_HEADS
HG_KEY_DIM = HG_VAL_DIM
HG_CHUNK = 64
DA_HEADS = 4
DA_V_DIM = (MIX_WIDTH // 2) // DA_HEADS
DA_QK_DIM = DA_V_DIM // 2
Q_BLOCK = 128
ROPE_THETA = 500000.0
ROT_DIM = DA_QK_DIM // 4
N_GROUPS = 4
EXPERTS_PER_GROUP = 8
N_EXPERTS = N_GROUPS * EXPERTS_PER_GROUP
TOP_K_IN_GROUP = 2
D_EXPERT = D_MODEL // 2
NORM_EPS = 1e-6
SUBLN_EPS = 1e-5

IN_SPLITS = (HG_HEADS * HG_KEY_DIM, HG_HEADS * HG_KEY_DIM, HG_HEADS * HG_KEY_DIM,
             HG_HEADS * HG_VAL_DIM, HG_HEADS * HG_VAL_DIM,
             DA_HEADS * 2 * DA_QK_DIM, DA_HEADS * 2 * DA_QK_DIM, DA_HEADS * DA_V_DIM)
IN_WIDTH = sum(IN_SPLITS)

kernel_name = 'hybrid_hgrn2_diffattn_hmoe_encoder'


def rms_norm(x, gain, eps=NORM_EPS):
    xf = x.astype(jnp.float32)
    y = xf * lax.rsqrt(jnp.mean(xf * xf, axis=-1, keepdims=True) + eps)
    return (y * gain.astype(jnp.float32)).astype(x.dtype)


def partial_rotary(t, positions):
    half = ROT_DIM // 2
    inv_freq = jnp.float32(ROPE_THETA) ** (-jnp.arange(0, ROT_DIM, 2, dtype=jnp.float32) / ROT_DIM)
    ang = positions.astype(jnp.float32)[:, :, None, None] * inv_freq
    cos = jnp.cos(ang).astype(t.dtype)
    sin = jnp.sin(ang).astype(t.dtype)
    t1 = t[..., :half]
    t2 = t[..., half:ROT_DIM]
    return jnp.concatenate([t1 * cos - t2 * sin, t2 * cos + t1 * sin, t[..., ROT_DIM:]], axis=-1)


def hgrn2_chunkwise(q, k, v, log_f):
    G, S, K = q.shape
    V = v.shape[-1]
    n_chunks = S // HG_CHUNK

    def split(a):
        return jnp.moveaxis(a.reshape(G, n_chunks, HG_CHUNK, a.shape[-1]), 1, 0)

    tri = jnp.tril(jnp.ones((HG_CHUNK, HG_CHUNK), dtype=bool))[None, :, :, None]

    def step(state, blk):
        qc, kc, vc, lfc = blk
        b = jnp.cumsum(lfc, axis=1)
        o_inter = jnp.einsum('gck,gkv->gcv', qc * jnp.exp(b), state)
        rel = jnp.where(tri, b[:, :, None, :] - b[:, None, :, :], -jnp.inf)
        scores = jnp.einsum('gtk,gsk,gtsk->gts', qc, kc, jnp.exp(rel))
        o_intra = jnp.einsum('gts,gsv->gtv', scores, vc)
        b_last = b[:, -1]
        new_state = jnp.exp(b_last)[:, :, None] * state + jnp.einsum(
            'gsk,gsv->gkv', kc * jnp.exp(b_last[:, None, :] - b), vc)
        return new_state, o_inter + o_intra

    init = jnp.zeros((G, K, V), jnp.float32)
    _, o = lax.scan(step, init, (split(q), split(k), split(v), split(log_f)))
    return jnp.moveaxis(o, 0, 1).reshape(G, S, V)


def hgrn2_bidirectional(q, i, f_fwd, f_bwd):
    B, S, H, _ = q.shape

    def flip(a):
        return a[:, ::-1]

    def to_groups(a):
        return jnp.transpose(a, (0, 1, 3, 2, 4)).reshape(2 * B * H, S, a.shape[-1])

    f2 = jnp.stack([f_fwd, flip(f_bwd)]).astype(jnp.float32)
    q2 = to_groups(jnp.stack([q, flip(q)]).astype(jnp.float32))
    v2 = to_groups(jnp.stack([i, flip(i)]).astype(jnp.float32))
    o = hgrn2_chunkwise(q2, to_groups(1.0 - f2), v2, to_groups(jnp.log(f2)))
    o = jnp.transpose(o.reshape(2, B, H, S, -1), (0, 1, 3, 2, 4))
    return o[0] + flip(o[1])


def diff_attention(q, k, v, lam):
    B, S, H, _, d = q.shape
    n_blocks = S // Q_BLOCK
    qb = jnp.moveaxis(q.reshape(B, n_blocks, Q_BLOCK, H, 2, d), 1, 0)
    kf = k.astype(jnp.float32)
    vf = v.astype(jnp.float32)
    scale = d ** -0.5

    def block(q_blk):
        s = jnp.einsum('bqhcd,bkhcd->bhcqk', q_blk.astype(jnp.float32), kf) * scale
        p = jax.nn.softmax(s, axis=-1)
        a = p[:, :, 0] - lam * p[:, :, 1]
        return jnp.einsum('bhqk,bkhv->bqhv', a, vf)

    o = lax.map(block, qb)
    return jnp.moveaxis(o, 0, 1).reshape(B, S, H, -1)


def hierarchical_moe(h, rg_w, rg_b, re_w, re_b, w_gate, w_up, w_down):
    B, S, D = h.shape
    hf = h.reshape(B * S, D)
    group_prob = jax.nn.softmax((hf @ rg_w + rg_b).astype(jnp.float32), axis=-1)
    g_idx = jnp.argmax(group_prob, axis=-1)
    g_p = jnp.take_along_axis(group_prob, g_idx[:, None], axis=-1)
    exp_logits = (hf @ re_w + re_b).astype(jnp.float32).reshape(-1, N_GROUPS, EXPERTS_PER_GROUP)
    in_group = jnp.take_along_axis(exp_logits, g_idx[:, None, None], axis=1)[:, 0]
    top_p, top_e = lax.top_k(jax.nn.softmax(in_group, axis=-1), TOP_K_IN_GROUP)
    weights = g_p * top_p / jnp.sum(top_p, axis=-1, keepdims=True)
    expert_id = g_idx[:, None] * EXPERTS_PER_GROUP + top_e
    gates = jnp.sum(jax.nn.one_hot(expert_id, N_EXPERTS, dtype=jnp.float32) * weights[..., None], axis=1)
    y = jnp.zeros(hf.shape, jnp.float32)
    for e in range(N_EXPERTS):
        a = jax.nn.silu(hf @ w_gate[e]) * (hf @ w_up[e])
        y = y + gates[:, e:e + 1] * (a @ w_down[e]).astype(jnp.float32)
    return y.reshape(B, S, D).astype(h.dtype)


def setup_inputs(seed: int = 0) -> dict:
    key = jax.random.key(seed)
    ks = jax.random.split(key, 20)

    def nrm(k, shape, scale):
        return jax.random.normal(k, shape, jnp.float32) * scale

    return {
        'x': nrm(ks[0], (BATCH, SEQ, D_MODEL), 1.0),
        'positions': jnp.broadcast_to(jnp.arange(SEQ, dtype=jnp.int32), (BATCH, SEQ)),
        'norm1_gain': 1.0 + nrm(ks[1], (DEPTH, D_MODEL), 0.02),
        'w_in': nrm(ks[2], (DEPTH, D_MODEL, IN_WIDTH), D_MODEL ** -0.5),
        'hg_lower_bounds': 1.0 + nrm(ks[3], (2, DEPTH + 1, HG_HEADS * HG_KEY_DIM), 0.1),
        'hg_norm_gain': 1.0 + nrm(ks[4], (DEPTH, HG_VAL_DIM), 0.02),
        'diff_lambda': nrm(ks[5], (DEPTH, 4, DA_QK_DIM), 0.1),
        'diff_subln_gain': 1.0 + nrm(ks[6], (DEPTH, DA_V_DIM), 0.02),
        'w_out': nrm(ks[7], (DEPTH, MIX_WIDTH, D_MODEL), MIX_WIDTH ** -0.5),
        'norm2_gain': 1.0 + nrm(ks[8], (DEPTH, D_MODEL), 0.02),
        'router_group_w': nrm(ks[9], (DEPTH, D_MODEL, N_GROUPS), D_MODEL ** -0.5),
        'router_group_b': nrm(ks[10], (DEPTH, N_GROUPS), 0.01),
        'router_expert_w': nrm(ks[11], (DEPTH, D_MODEL, N_EXPERTS), D_MODEL ** -0.5),
        'router_expert_b': nrm(ks[12], (DEPTH, N_EXPERTS), 0.01),
        'moe_w_gate': nrm(ks[13], (DEPTH, N_EXPERTS, D_MODEL, D_EXPERT), D_MODEL ** -0.5),
        'moe_w_up': nrm(ks[14], (DEPTH, N_EXPERTS, D_MODEL, D_EXPERT), D_MODEL ** -0.5),
        'moe_w_down': nrm(ks[15], (DEPTH, N_EXPERTS, D_EXPERT, D_MODEL), D_EXPERT ** -0.5),
        'final_norm_gain': 1.0 + nrm(ks[16], (D_MODEL,), 0.02),
    }


def reference(x, positions, norm1_gain, w_in, hg_lower_bounds, hg_norm_gain, diff_lambda,
              diff_subln_gain, w_out, norm2_gain, router_group_w, router_group_b,
              router_expert_w, router_expert_b, moe_w_gate, moe_w_up, moe_w_down,
              final_norm_gain):
    B, S, _ = x.shape
    split_at = np.cumsum(IN_SPLITS)[:-1].tolist()
    lb_cum = jnp.cumsum(jax.nn.softmax(hg_lower_bounds.astype(jnp.float32), axis=1), axis=1)

    def heads(a, n):
        return a.reshape(B, S, n, -1)

    for layer in range(DEPTH):
        h = rms_norm(x, norm1_gain[layer])
        proj = h @ w_in[layer]
        hq, hff, hfb, hi, hg, dq, dk, dv = jnp.split(proj, split_at, axis=-1)

        lb = lb_cum[:, layer + 1] - lb_cum[:, 0]
        f_fwd = lb[0] + (1.0 - lb[0]) * jax.nn.sigmoid(hff.astype(jnp.float32))
        f_bwd = lb[1] + (1.0 - lb[1]) * jax.nn.sigmoid(hfb.astype(jnp.float32))
        o_hg = hgrn2_bidirectional(heads(hq, HG_HEADS), heads(hi, HG_HEADS),
                                   heads(f_fwd, HG_HEADS), heads(f_bwd, HG_HEADS))
        o_hg = (rms_norm(o_hg, hg_norm_gain[layer])
                * jax.nn.silu(heads(hg, HG_HEADS).astype(jnp.float32))).reshape(B, S, -1)

        qd = partial_rotary(heads(dq, 2 * DA_HEADS), positions).reshape(B, S, DA_HEADS, 2, DA_QK_DIM)
        kd = partial_rotary(heads(dk, 2 * DA_HEADS), positions).reshape(B, S, DA_HEADS, 2, DA_QK_DIM)
        lam_p = diff_lambda[layer].astype(jnp.float32)
        lambda_init = 0.8 - 0.6 * math.exp(-0.3 * layer)
        lam = (jnp.exp(jnp.sum(lam_p[0] * lam_p[1])) - jnp.exp(jnp.sum(lam_p[2] * lam_p[3]))
               + lambda_init)
        o_da = diff_attention(qd, kd, heads(dv, DA_HEADS), lam)
        o_da = (rms_norm(o_da, diff_subln_gain[layer], SUBLN_EPS) * (1.0 - lambda_init)).reshape(B, S, -1)

        mixed = jnp.concatenate([o_hg, o_da], axis=-1).astype(x.dtype)
        x = x + mixed @ w_out[layer]

        h = rms_norm(x, norm2_gain[layer])
        x = x + hierarchical_moe(h, router_group_w[layer], router_group_b[layer],
                                 router_expert_w[layer], router_expert_b[layer],
                                 moe_w_gate[layer], moe_w_up[layer], moe_w_down[layer])

    return rms_norm(x, final_norm_gain)
```

```python
import functools
import math

import jax
import jax.numpy as jnp
from jax import lax
from jax.experimental import pallas as pl
from jax.experimental.pallas import tpu as pltpu

F32 = jnp.float32
BF16 = jnp.bfloat16

D_MODEL = 1024
HEAD_W = 128
N_HEADS = 4
HALF_W = N_HEADS * HEAD_W
QK_DIM = 64
ROT_DIM = 16
ROPE_THETA = 500000.0
HG_CHUNK = 64
N_GROUPS = 4
EXPERTS_PER_GROUP = 8
N_EXPERTS = 32
D_EXPERT = 512
NORM_EPS = 1e-6
SUBLN_EPS = 1e-5
LAMBDA_INIT = 0.8 - 0.6 * math.exp(-0.3 * 0)
LOG2E = 1.4426950408889634
ROUTER_LANES = 128
V7X_VMEM_LIMIT = 56 * 1024 * 1024

NT_DIMS = (((1,), (1,)), ((), ()))
TN_DIMS = (((0,), (0,)), ((), ()))


def _rms(x, eps):
    return x * lax.rsqrt(jnp.mean(x * x, axis=-1, keepdims=True) + eps)


def _inproj_kernel(x_ref, pos_ref, g1_ref, w_ref, lb_ref, invf_ref, sgn_ref,
                   hq_ref, kf_ref, kb_ref, lff_ref, lfb_ref, hi_ref, hg_ref,
                   dq_ref, dk_ref, dv_ref):
    x = x_ref[...]
    h = (_rms(x, NORM_EPS) * g1_ref[...]).astype(BF16)

    def proj(i):
        return jnp.dot(h, w_ref[:, i * HALF_W:(i + 1) * HALF_W], preferred_element_type=F32)

    hq_ref[...] = proj(0)
    for i, (k_ref, lf_ref) in enumerate(((kf_ref, lff_ref), (kb_ref, lfb_ref))):
        z = proj(1 + i)
        lb = lb_ref[i:i + 1, :]
        sg = jax.nn.sigmoid(z)
        lf_ref[...] = jnp.log(lb + (1.0 - lb) * sg)
        k_ref[...] = (1.0 - lb) * (1.0 - sg)
    hi_ref[...] = proj(3)
    hg_ref[...] = proj(4)

    ang = pos_ref[...] * invf_ref[...]
    cos = jnp.concatenate([jnp.cos(ang)] * N_HEADS, axis=1)
    sgn = sgn_ref[...]
    sin = jnp.concatenate([jnp.sin(ang) * sgn] * N_HEADS, axis=1)
    take_hi = jnp.concatenate([jnp.broadcast_to(sgn, ang.shape)] * N_HEADS, axis=1) < 0.0

    def rotary(t):
        partner = jnp.where(take_hi, pltpu.roll(t, HALF_W - ROT_DIM // 2, axis=1),
                            pltpu.roll(t, ROT_DIM // 2, axis=1))
        return t * cos + partner * sin

    dq_ref[...] = (rotary(proj(5)) * (QK_DIM ** -0.5 * LOG2E)).astype(BF16)
    dk_ref[...] = rotary(proj(6)).astype(BF16)
    v = proj(7)
    lane = lax.broadcasted_iota(jnp.int32, (v.shape[0], HEAD_W), 1)
    ones_col = jnp.where(lane == 0, 1.0, 0.0)
    pieces = []
    for hd in range(N_HEADS):
        pieces += [v[:, hd * HEAD_W:(hd + 1) * HEAD_W], ones_col]
    dv_ref[...] = jnp.concatenate(pieces, axis=1).astype(BF16)


def _inproj(x, pos, g1, w_in, lb, invf, sgn, tm=512):
    S = x.shape[0]
    row = lambda w: pl.BlockSpec((tm, w), lambda i: (i, 0))
    full = lambda a: pl.BlockSpec(a.shape, lambda i: (0, 0))
    f32o = jax.ShapeDtypeStruct((S, HALF_W), F32)
    bf16o = jax.ShapeDtypeStruct((S, HALF_W), BF16)
    return pl.pallas_call(
        _inproj_kernel,
        grid=(S // tm,),
        in_specs=[row(D_MODEL), row(1), full(g1), full(w_in), full(lb), full(invf), full(sgn)],
        out_specs=[row(HALF_W)] * 9 + [row(2 * HALF_W)],
        out_shape=[f32o] * 7 + [bf16o, bf16o, jax.ShapeDtypeStruct((S, 2 * HALF_W), BF16)],
        compiler_params=pltpu.CompilerParams(dimension_semantics=("arbitrary",),
                                             vmem_limit_bytes=V7X_VMEM_LIMIT),
        name="inproj",
    )(x, pos, g1, w_in, lb, invf, sgn)


def _chunk_cumsum(x, rev):
    n = x.shape[0]
    row = lax.broadcasted_iota(jnp.int32, x.shape, 0)
    d = 1
    while d < n:
        if rev:
            x = x + jnp.where(row < n - d, pltpu.roll(x, n - d, axis=0), 0.0)
        else:
            x = x + jnp.where(row >= d, pltpu.roll(x, d, axis=0), 0.0)
        d *= 2
    return x


def _pivot_rows(b, blk, rev):
    n = b.shape[0]
    half = blk // 2
    groups = []
    for g0 in range(0, n, 8):
        def piv(r):
            base = (r // blk) * blk
            return base + half if rev else base + half - 1
        if blk >= 16 or blk == 8:
            p = piv(g0)
            groups.append(jnp.broadcast_to(b[p:p + 1, :], (8, b.shape[1])))
        else:
            sub = lax.broadcasted_iota(jnp.int32, (8, b.shape[1]), 0)
            acc = None
            for s0 in range(0, 8, blk):
                p = piv(g0 + s0)
                rowv = jnp.broadcast_to(b[p:p + 1, :], (8, b.shape[1]))
                acc = rowv if acc is None else jnp.where(sub >= s0, rowv, acc)
            groups.append(acc)
    return jnp.concatenate(groups, axis=0)


def _hgrn_kernel(q_ref, k_ref, lf_ref, v_ref, o_ref, st_ref, *, rev, n_chunks):
    @pl.when(pl.program_id(1) == 0)
    def _():
        st_ref[...] = jnp.zeros_like(st_ref)

    C = HG_CHUNK
    rt = lax.broadcasted_iota(jnp.int32, (C, C), 0)
    ct = lax.broadcasted_iota(jnp.int32, (C, C), 1)
    rowi = lax.broadcasted_iota(jnp.int32, (C, HEAD_W), 0)

    def chunk(ci, carry):
        c = (n_chunks - 1 - ci) if rev else ci
        sl = pl.ds(pl.multiple_of(c * C, C), C)
        q = q_ref[sl, :]
        k = k_ref[sl, :]
        v = v_ref[sl, :]
        vb = v.astype(BF16)
        b = _chunk_cumsum(lf_ref[sl, :], rev)
        tot = b[0:1, :] if rev else b[C - 1:C, :]

        st = st_ref[...]
        o = lax.dot_general((q * jnp.exp(b)).astype(BF16), st.astype(BF16), NT_DIMS,
                            preferred_element_type=F32)
        kdec = (k * jnp.exp(tot - b)).astype(BF16)
        st_ref[...] = st * jnp.exp(tot) + lax.dot_general(vb, kdec, TN_DIMS,
                                                          preferred_element_type=F32)

        scores = jnp.where(rt == ct, jnp.sum(q * k, axis=1, keepdims=True), 0.0)
        blk = C
        while blk >= 2:
            half = blk // 2
            w = jnp.exp(-jnp.abs(b - _pivot_rows(b, blk, rev)))
            in_q = ((rowi % blk) < half) if rev else ((rowi % blk) >= half)
            qs = jnp.where(in_q, q * w, 0.0).astype(BF16)
            ks = jnp.where(in_q, 0.0, k * w).astype(BF16)
            g = lax.dot_general(qs, ks, NT_DIMS, preferred_element_type=F32)
            scores = scores + jnp.where((rt // blk) == (ct // blk), g, 0.0)
            blk = half
        o_ref[sl, :] = o + jnp.dot(scores.astype(BF16), vb, preferred_element_type=F32)
        return carry

    lax.fori_loop(0, n_chunks, chunk, 0)


def _hgrn(q, k, lf, v, rev, rb=1024):
    S = q.shape[0]
    nb = S // rb
    idx = (lambda h, j: (nb - 1 - j, h)) if rev else (lambda h, j: (j, h))
    spec = pl.BlockSpec((rb, HEAD_W), idx)
    return pl.pallas_call(
        functools.partial(_hgrn_kernel, rev=rev, n_chunks=rb // HG_CHUNK),
        grid=(N_HEADS, nb),
        in_specs=[spec] * 4,
        out_specs=spec,
        out_shape=jax.ShapeDtypeStruct((S, HALF_W), F32),
        scratch_shapes=[pltpu.VMEM((HEAD_W, HEAD_W), F32)],
        compiler_params=pltpu.CompilerParams(dimension_semantics=("arbitrary", "arbitrary"),
                                             vmem_limit_bytes=V7X_VMEM_LIMIT),
        name="hgrn_bwd" if rev else "hgrn_fwd",
    )(q, k, lf, v)


def _attn_kernel(lam_ref, q_ref, k_ref, v_ref, gain_ref, o_ref, m_ref, acc_ref, *, tk):
    q = q_ref[...]
    lane = lax.broadcasted_iota(jnp.int32, q.shape, 1)
    qm = (jnp.where(lane < QK_DIM, q, jnp.zeros_like(q)),
          jnp.where(lane >= QK_DIM, q, jnp.zeros_like(q)))
    m_ref[...] = jnp.full(m_ref.shape, -jnp.inf, F32)
    acc_ref[...] = jnp.zeros_like(acc_ref)

    def kv_step(j, carry):
        sl = pl.ds(pl.multiple_of(j * tk, tk), tk)
        kc = k_ref[sl, :]
        vc = v_ref[sl, :]
        for c in range(2):
            s = lax.dot_general(qm[c], kc, NT_DIMS, preferred_element_type=F32)
            m_old = m_ref[c]
            m_new = jnp.maximum(m_old, jnp.max(s, axis=1, keepdims=True))
            p = jnp.exp2(s - m_new).astype(BF16)
            acc_ref[c] = jnp.exp2(m_old - m_new) * acc_ref[c] + jnp.dot(
                p, vc, preferred_element_type=F32)
            m_ref[c] = m_new
        return carry

    lax.fori_loop(0, k_ref.shape[0] // tk, kv_step, 0)

    outs = []
    for c in range(2):
        a = acc_ref[c]
        outs.append(a[:, :HEAD_W] / a[:, HEAD_W:HEAD_W + 1])
    o = outs[0] - lam_ref[0] * outs[1]
    o_ref[...] = _rms(o, SUBLN_EPS) * gain_ref[...] * (1.0 - LAMBDA_INIT)


def _attn(lam, dq, dk, dv, gain, tq=512, tk=1024):
    S = dq.shape[0]
    return pl.pallas_call(
        functools.partial(_attn_kernel, tk=tk),
        grid=(N_HEADS, S // tq),
        in_specs=[pl.BlockSpec(memory_space=pltpu.SMEM),
                  pl.BlockSpec((tq, HEAD_W), lambda h, i: (i, h)),
                  pl.BlockSpec((S, HEAD_W), lambda h, i: (0, h)),
                  pl.BlockSpec((S, 2 * HEAD_W), lambda h, i: (0, h)),
                  pl.BlockSpec((1, HEAD_W), lambda h, i: (0, 0))],
        out_specs=pl.BlockSpec((tq, HEAD_W), lambda h, i: (i, h)),
        out_shape=jax.ShapeDtypeStruct((S, HALF_W), F32),
        scratch_shapes=[pltpu.VMEM((2, tq, 1), F32), pltpu.VMEM((2, tq, 2 * HEAD_W), F32)],
        compiler_params=pltpu.CompilerParams(dimension_semantics=("arbitrary", "arbitrary"),
                                             vmem_limit_bytes=V7X_VMEM_LIMIT),
        name="diff_attn",
    )(lam, dq, dk, dv, gain)


def _outproj_kernel(x_ref, of_ref, ob_ref, hg_ref, oda_ref, hgain_ref, wo_ref, g2_ref,
                    wr_ref, br_ref, x2_ref, gates_ref):
    o = of_ref[...] + ob_ref[...]
    gate = hg_ref[...]
    gate = gate * jax.nn.sigmoid(gate)
    parts = []
    for hd in range(N_HEADS):
        sl = slice(hd * HEAD_W, (hd + 1) * HEAD_W)
        parts.append(_rms(o[:, sl], NORM_EPS) * hgain_ref[...] * gate[:, sl])
    parts.append(oda_ref[...])
    mixed = jnp.concatenate(parts, axis=1).astype(BF16)
    x2 = x_ref[...] + jnp.dot(mixed, wo_ref[...], preferred_element_type=F32)
    x2_ref[...] = x2

    h2 = _rms(x2, NORM_EPS) * g2_ref[...]
    logits = jnp.dot(h2, wr_ref[...], preferred_element_type=F32,
                     precision=lax.Precision.HIGHEST) + br_ref[...]
    lane = lax.broadcasted_iota(jnp.int32, logits.shape, 1)
    neg = -jnp.inf
    big = ROUTER_LANES

    def first_max(vals):
        mx = jnp.max(vals, axis=1, keepdims=True)
        idx = jnp.min(jnp.where(vals == mx, lane, big), axis=1, keepdims=True)
        return mx, idx

    gl = jnp.where((lane >= N_EXPERTS) & (lane < N_EXPERTS + N_GROUPS), logits, neg)
    gm, gidx = first_max(gl)
    g_p = 1.0 / jnp.sum(jnp.exp(gl - gm), axis=1, keepdims=True)
    el = jnp.where((lane < N_EXPERTS) & ((lane // EXPERTS_PER_GROUP) == (gidx - N_EXPERTS)),
                   logits, neg)
    em1, idx1 = first_max(el)
    em2, idx2 = first_max(jnp.where(lane == idx1, neg, el))
    e2 = jnp.exp(em2 - em1)
    w1 = g_p / (1.0 + e2)
    w2 = g_p * e2 / (1.0 + e2)
    gates_ref[...] = jnp.where(lane == idx1, w1, 0.0) + jnp.where(lane == idx2, w2, 0.0)


def _outproj(x, o_f, o_b, hg, o_da, hgain, w_out, g2, wr, br, tm=512):
    S = x.shape[0]
    row = lambda w: pl.BlockSpec((tm, w), lambda i: (i, 0))
    full = lambda a: pl.BlockSpec(a.shape, lambda i: (0, 0))
    return pl.pallas_call(
        _outproj_kernel,
        grid=(S // tm,),
        in_specs=[row(D_MODEL), row(HALF_W), row(HALF_W), row(HALF_W), row(HALF_W),
                  full(hgain), full(w_out), full(g2), full(wr), full(br)],
        out_specs=[row(D_MODEL), row(ROUTER_LANES)],
        out_shape=[jax.ShapeDtypeStruct((S, D_MODEL), F32),
                   jax.ShapeDtypeStruct((S, ROUTER_LANES), F32)],
        compiler_params=pltpu.CompilerParams(dimension_semantics=("arbitrary",),
                                             vmem_limit_bytes=V7X_VMEM_LIMIT),
        name="outproj_router",
    )(x, o_f, o_b, hg, o_da, hgain, w_out, g2, wr, br)


def _moe_kernel(x2_ref, gates_ref, g2_ref, wg_ref, wu_ref, wd_ref, gf_ref, out_ref,
                h2_ref, acc_ref):
    e = pl.program_id(1)

    @pl.when(e == 0)
    def _():
        h2_ref[...] = (_rms(x2_ref[...], NORM_EPS) * g2_ref[...]).astype(BF16)
        acc_ref[...] = jnp.zeros_like(acc_ref)

    h2 = h2_ref[...]
    gates = gates_ref[...]
    lane = lax.broadcasted_iota(jnp.int32, gates.shape, 1)
    gcol = jnp.sum(jnp.where(lane == e, gates, 0.0), axis=1, keepdims=True)
    a = jnp.dot(h2, wg_ref[0], preferred_element_type=F32)
    a = a * jax.nn.sigmoid(a) * jnp.dot(h2, wu_ref[0], preferred_element_type=F32)
    acc_ref[...] += gcol * jnp.dot(a.astype(BF16), wd_ref[0], preferred_element_type=F32)

    @pl.when(e == N_EXPERTS - 1)
    def _():
        out_ref[...] = _rms(x2_ref[...] + acc_ref[...], NORM_EPS) * gf_ref[...]


def _moe(x2, gates, g2, wg, wu, wd, gf, tb=1024):
    S = x2.shape[0]
    return pl.pallas_call(
        _moe_kernel,
        grid=(S // tb, N_EXPERTS),
        in_specs=[pl.BlockSpec((tb, D_MODEL), lambda b, e: (b, 0)),
                  pl.BlockSpec((tb, ROUTER_LANES), lambda b, e: (b, 0)),
                  pl.BlockSpec((1, D_MODEL), lambda b, e: (0, 0)),
                  pl.BlockSpec((1, D_MODEL, D_EXPERT), lambda b, e: (e, 0, 0)),
                  pl.BlockSpec((1, D_MODEL, D_EXPERT), lambda b, e: (e, 0, 0)),
                  pl.BlockSpec((1, D_EXPERT, D_MODEL), lambda b, e: (e, 0, 0)),
                  pl.BlockSpec((1, D_MODEL), lambda b, e: (0, 0))],
        out_specs=pl.BlockSpec((tb, D_MODEL), lambda b, e: (b, 0)),
        out_shape=jax.ShapeDtypeStruct((S, D_MODEL), F32),
        scratch_shapes=[pltpu.VMEM((tb, D_MODEL), BF16), pltpu.VMEM((tb, D_MODEL), F32)],
        compiler_params=pltpu.CompilerParams(dimension_semantics=("arbitrary", "arbitrary"),
                                             vmem_limit_bytes=V7X_VMEM_LIMIT),
        name="moe",
    )(x2, gates, g2, wg, wu, wd, gf)


def kernel(x, positions, norm1_gain, w_in, hg_lower_bounds, hg_norm_gain, diff_lambda,
           diff_subln_gain, w_out, norm2_gain, router_group_w, router_group_b,
           router_expert_w, router_expert_b, moe_w_gate, moe_w_up, moe_w_down,
           final_norm_gain):
    B, S, _ = x.shape
    assert B == 1 and norm1_gain.shape[0] == 1
    layer = 0
    xs = x.reshape(S, D_MODEL)
    pos = positions.reshape(S, 1).astype(F32)

    lb_cum = jnp.cumsum(jax.nn.softmax(hg_lower_bounds.astype(F32), axis=1), axis=1)
    lb = lb_cum[:, layer + 1] - lb_cum[:, 0]
    lam_p = diff_lambda[layer].astype(F32)
    lam = (jnp.exp(jnp.sum(lam_p[0] * lam_p[1])) - jnp.exp(jnp.sum(lam_p[2] * lam_p[3]))
           + LAMBDA_INIT).reshape(1)
    lane = jnp.arange(HEAD_W)
    inv_freq = jnp.float32(ROPE_THETA) ** (-jnp.arange(0, ROT_DIM, 2, dtype=F32) / ROT_DIM)
    in_rot = (lane % QK_DIM) < ROT_DIM
    invf = jnp.where(in_rot, inv_freq[lane % (ROT_DIM // 2)], 0.0).reshape(1, HEAD_W)
    sgn = jnp.where(in_rot, jnp.where((lane % QK_DIM) < ROT_DIM // 2, -1.0, 1.0),
                    0.0).astype(F32).reshape(1, HEAD_W)
    wr = jnp.zeros((D_MODEL, ROUTER_LANES), F32)
    wr = wr.at[:, :N_EXPERTS].set(router_expert_w[layer])
    wr = wr.at[:, N_EXPERTS:N_EXPERTS + N_GROUPS].set(router_group_w[layer])
    br = jnp.zeros((1, ROUTER_LANES), F32)
    br = br.at[0, :N_EXPERTS].set(router_expert_b[layer])
    br = br.at[0, N_EXPERTS:N_EXPERTS + N_GROUPS].set(router_group_b[layer])

    hq, kf, kb, lff, lfb, hi, hg, dq, dk, dv = _inproj(
        xs, pos, norm1_gain[layer].reshape(1, -1), w_in[layer].astype(BF16), lb, invf, sgn)
    o_f = _hgrn(hq, kf, lff, hi, rev=False)
    o_b = _hgrn(hq, kb, lfb, hi, rev=True)
    o_da = _attn(lam, dq, dk, dv, diff_subln_gain[layer].reshape(1, -1))
    x2, gates = _outproj(xs, o_f, o_b, hg, o_da, hg_norm_gain[layer].reshape(1, -1),
                         w_out[layer].astype(BF16), norm2_gain[layer].reshape(1, -1), wr, br)
    out = _moe(x2, gates, norm2_gain[layer].reshape(1, -1), moe_w_gate[layer].astype(BF16),
               moe_w_up[layer].astype(BF16), moe_w_down[layer].astype(BF16),
               final_norm_gain.reshape(1, -1))
    return out.reshape(B, S, D_MODEL)
```

```python
import functools
import math

import jax
import jax.numpy as jnp
from jax import lax
from jax.experimental import pallas as pl
from jax.experimental.pallas import tpu as pltpu

F32 = jnp.float32
BF16 = jnp.bfloat16

D_MODEL = 1024
HEAD_W = 128
N_HEADS = 4
HALF_W = N_HEADS * HEAD_W
QK_DIM = 64
ROT_DIM = 16
ROPE_THETA = 500000.0
HG_CHUNK = 64
N_GROUPS = 4
EXPERTS_PER_GROUP = 8
N_EXPERTS = 32
D_EXPERT = 512
NORM_EPS = 1e-6
SUBLN_EPS = 1e-5
LAMBDA_INIT = 0.8 - 0.6 * math.exp(-0.3 * 0)
LOG2E = 1.4426950408889634
ROUTER_LANES = 128
V7X_VMEM_LIMIT = 56 * 1024 * 1024

NT_DIMS = (((1,), (1,)), ((), ()))
TN_DIMS = (((0,), (0,)), ((), ()))


def _rms(x, eps):
    return x * lax.rsqrt(jnp.mean(x * x, axis=-1, keepdims=True) + eps)


def _inproj_kernel(x_ref, pos_ref, g1_ref, w_ref, lb_ref, invf_ref, sgn_ref,
                   hq_ref, kf_ref, kb_ref, lff_ref, lfb_ref, hi_ref, hg_ref,
                   dq_ref, dk_ref, dv_ref):
    x = x_ref[...]
    h = (_rms(x, NORM_EPS) * g1_ref[...]).astype(BF16)

    def proj(i):
        return jnp.dot(h, w_ref[:, i * HALF_W:(i + 1) * HALF_W], preferred_element_type=F32)

    hq_ref[...] = proj(0)
    for i, (k_ref, lf_ref) in enumerate(((kf_ref, lff_ref), (kb_ref, lfb_ref))):
        z = proj(1 + i)
        lb = lb_ref[i:i + 1, :]
        sg = jax.nn.sigmoid(z)
        lf_ref[...] = jnp.log(lb + (1.0 - lb) * sg)
        k_ref[...] = (1.0 - lb) * (1.0 - sg)
    hi_ref[...] = proj(3)
    hg_ref[...] = proj(4)

    ang = pos_ref[...] * invf_ref[...]
    cos = jnp.concatenate([jnp.cos(ang)] * N_HEADS, axis=1)
    sgn = sgn_ref[...]
    sin = jnp.concatenate([jnp.sin(ang) * sgn] * N_HEADS, axis=1)
    take_hi = jnp.concatenate([jnp.broadcast_to(sgn, ang.shape)] * N_HEADS, axis=1) < 0.0

    def rotary(t):
        partner = jnp.where(take_hi, pltpu.roll(t, HALF_W - ROT_DIM // 2, axis=1),
                            pltpu.roll(t, ROT_DIM // 2, axis=1))
        return t * cos + partner * sin

    dq_ref[...] = (rotary(proj(5)) * (QK_DIM ** -0.5 * LOG2E)).astype(BF16)
    dk_ref[...] = rotary(proj(6)).astype(BF16)
    v = proj(7)
    lane = lax.broadcasted_iota(jnp.int32, (v.shape[0], HEAD_W), 1)
    ones_col = jnp.where(lane == 0, 1.0, 0.0)
    pieces = []
    for hd in range(N_HEADS):
        pieces += [v[:, hd * HEAD_W:(hd + 1) * HEAD_W], ones_col]
    dv_ref[...] = jnp.concatenate(pieces, axis=1).astype(BF16)


def _inproj(x, pos, g1, w_in, lb, invf, sgn, tm=512):
    S = x.shape[0]
    row = lambda w: pl.BlockSpec((tm, w), lambda i: (i, 0))
    full = lambda a: pl.BlockSpec(a.shape, lambda i: (0, 0))
    f32o = jax.ShapeDtypeStruct((S, HALF_W), F32)
    bf16o = jax.ShapeDtypeStruct((S, HALF_W), BF16)
    return pl.pallas_call(
        _inproj_kernel,
        grid=(S // tm,),
        in_specs=[row(D_MODEL), row(1), full(g1), full(w_in), full(lb), full(invf), full(sgn)],
        out_specs=[row(HALF_W)] * 9 + [row(2 * HALF_W)],
        out_shape=[f32o] * 7 + [bf16o, bf16o, jax.ShapeDtypeStruct((S, 2 * HALF_W), BF16)],
        compiler_params=pltpu.CompilerParams(dimension_semantics=("arbitrary",),
                                             vmem_limit_bytes=V7X_VMEM_LIMIT),
        name="inproj",
    )(x, pos, g1, w_in, lb, invf, sgn)


def _chunk_cumsum(x, rev):
    n = x.shape[0]
    row = lax.broadcasted_iota(jnp.int32, x.shape, 0)
    d = 1
    while d < n:
        if rev:
            x = x + jnp.where(row < n - d, pltpu.roll(x, n - d, axis=0), 0.0)
        else:
            x = x + jnp.where(row >= d, pltpu.roll(x, d, axis=0), 0.0)
        d *= 2
    return x


def _pivot_rows(b, blk, rev):
    n = b.shape[0]
    half = blk // 2
    groups = []
    for g0 in range(0, n, 8):
        def piv(r):
            base = (r // blk) * blk
            return base + half if rev else base + half - 1
        if blk >= 16 or blk == 8:
            p = piv(g0)
            groups.append(jnp.broadcast_to(b[p:p + 1, :], (8, b.shape[1])))
        else:
            sub = lax.broadcasted_iota(jnp.int32, (8, b.shape[1]), 0)
            acc = None
            for s0 in range(0, 8, blk):
                p = piv(g0 + s0)
                rowv = jnp.broadcast_to(b[p:p + 1, :], (8, b.shape[1]))
                acc = rowv if acc is None else jnp.where(sub >= s0, rowv, acc)
            groups.append(acc)
    return jnp.concatenate(groups, axis=0)


def _hgrn_kernel(q_ref, k_ref, lf_ref, v_ref, o_ref, st_ref, *, rev, n_chunks):
    @pl.when(pl.program_id(1) == 0)
    def _():
        st_ref[...] = jnp.zeros_like(st_ref)

    C = HG_CHUNK
    rt = lax.broadcasted_iota(jnp.int32, (C, C), 0)
    ct = lax.broadcasted_iota(jnp.int32, (C, C), 1)
    rowi = lax.broadcasted_iota(jnp.int32, (C, HEAD_W), 0)

    def chunk(ci, carry):
        c = (n_chunks - 1 - ci) if rev else ci
        sl = pl.ds(pl.multiple_of(c * C, C), C)
        q = q_ref[sl, :]
        k = k_ref[sl, :]
        v = v_ref[sl, :]
        vb = v.astype(BF16)
        b = _chunk_cumsum(lf_ref[sl, :], rev)
        tot = b[0:1, :] if rev else b[C - 1:C, :]

        st = st_ref[...]
        o = lax.dot_general((q * jnp.exp(b)).astype(BF16), st.astype(BF16), NT_DIMS,
                            preferred_element_type=F32)
        kdec = (k * jnp.exp(tot - b)).astype(BF16)
        st_ref[...] = st * jnp.exp(tot) + lax.dot_general(vb, kdec, TN_DIMS,
                                                          preferred_element_type=F32)

        scores = jnp.where(rt == ct, jnp.sum(q * k, axis=1, keepdims=True), 0.0)
        blk = C
        while blk >= 2:
            half = blk // 2
            w = jnp.exp(-jnp.abs(b - _pivot_rows(b, blk, rev)))
            in_q = ((rowi % blk) < half) if rev else ((rowi % blk) >= half)
            qs = jnp.where(in_q, q * w, 0.0).astype(BF16)
            ks = jnp.where(in_q, 0.0, k * w).astype(BF16)
            g = lax.dot_general(qs, ks, NT_DIMS, preferred_element_type=F32)
            scores = scores + jnp.where((rt // blk) == (ct // blk), g, 0.0)
            blk = half
        o_ref[sl, :] = o + jnp.dot(scores.astype(BF16), vb, preferred_element_type=F32)
        return carry

    lax.fori_loop(0, n_chunks, chunk, 0)


def _hgrn(q, k, lf, v, rev, rb=1024):
    S = q.shape[0]
    nb = S // rb
    idx = (lambda h, j: (nb - 1 - j, h)) if rev else (lambda h, j: (j, h))
    spec = pl.BlockSpec((rb, HEAD_W), idx)
    return pl.pallas_call(
        functools.partial(_hgrn_kernel, rev=rev, n_chunks=rb // HG_CHUNK),
        grid=(N_HEADS, nb),
        in_specs=[spec] * 4,
        out_specs=spec,
        out_shape=jax.ShapeDtypeStruct((S, HALF_W), F32),
        scratch_shapes=[pltpu.VMEM((HEAD_W, HEAD_W), F32)],
        compiler_params=pltpu.CompilerParams(dimension_semantics=("arbitrary", "arbitrary"),
                                             vmem_limit_bytes=V7X_VMEM_LIMIT),
        name="hgrn_bwd" if rev else "hgrn_fwd",
    )(q, k, lf, v)


def _attn_kernel(lam_ref, q_ref, k_ref, v_ref, gain_ref, o_ref, sa_ref, sb_ref, m_ref, acc_ref,
                 *, tk):
    n_kv = k_ref.shape[0] // tk
    assert n_kv % 2 == 0
    q = q_ref[...]
    lane = lax.broadcasted_iota(jnp.int32, q.shape, 1)
    qm = (jnp.where(lane < QK_DIM, q, jnp.zeros_like(q)),
          jnp.where(lane >= QK_DIM, q, jnp.zeros_like(q)))
    m_ref[...] = jnp.full(m_ref.shape, -jnp.inf, F32)
    acc_ref[...] = jnp.zeros_like(acc_ref)

    def chunk(j):
        return pl.ds(pl.multiple_of(j * tk, tk), tk)

    def scores(j, s_ref):
        kc = k_ref[chunk(j), :]
        for c in range(2):
            s_ref[c] = lax.dot_general(qm[c], kc, NT_DIMS, preferred_element_type=F32)

    def softmax_pv(j, s_ref):
        vc = v_ref[chunk(j), :]
        for c in range(2):
            s = s_ref[c]
            m_old = m_ref[c]
            m_new = jnp.maximum(m_old, jnp.max(s, axis=1, keepdims=True))
            p = jnp.exp2(s - m_new).astype(BF16)
            acc_ref[c] = jnp.exp2(m_old - m_new) * acc_ref[c] + jnp.dot(
                p, vc, preferred_element_type=F32)
            m_ref[c] = m_new

    scores(0, sa_ref)

    def two_steps(i, carry):
        scores(2 * i + 1, sb_ref)
        softmax_pv(2 * i, sa_ref)
        scores(2 * i + 2, sa_ref)
        softmax_pv(2 * i + 1, sb_ref)
        return carry

    lax.fori_loop(0, n_kv // 2 - 1, two_steps, 0)
    scores(n_kv - 1, sb_ref)
    softmax_pv(n_kv - 2, sa_ref)
    softmax_pv(n_kv - 1, sb_ref)

    outs = []
    for c in range(2):
        a = acc_ref[c]
        outs.append(a[:, :HEAD_W] / a[:, HEAD_W:HEAD_W + 1])
    o = outs[0] - lam_ref[0] * outs[1]
    o_ref[...] = _rms(o, SUBLN_EPS) * gain_ref[...] * (1.0 - LAMBDA_INIT)


def _attn(lam, dq, dk, dv, gain, tq=512, tk=1024):
    S = dq.shape[0]
    return pl.pallas_call(
        functools.partial(_attn_kernel, tk=tk),
        grid=(N_HEADS, S // tq),
        in_specs=[pl.BlockSpec(memory_space=pltpu.SMEM),
                  pl.BlockSpec((tq, HEAD_W), lambda h, i: (i, h)),
                  pl.BlockSpec((S, HEAD_W), lambda h, i: (0, h)),
                  pl.BlockSpec((S, 2 * HEAD_W), lambda h, i: (0, h)),
                  pl.BlockSpec((1, HEAD_W), lambda h, i: (0, 0))],
        out_specs=pl.BlockSpec((tq, HEAD_W), lambda h, i: (i, h)),
        out_shape=jax.ShapeDtypeStruct((S, HALF_W), F32),
        scratch_shapes=[pltpu.VMEM((2, tq, tk), F32), pltpu.VMEM((2, tq, tk), F32),
                        pltpu.VMEM((2, tq, 1), F32), pltpu.VMEM((2, tq, 2 * HEAD_W), F32)],
        compiler_params=pltpu.CompilerParams(dimension_semantics=("arbitrary", "arbitrary"),
                                             vmem_limit_bytes=V7X_VMEM_LIMIT),
        name="diff_attn",
    )(lam, dq, dk, dv, gain)


def _outproj_kernel(x_ref, of_ref, ob_ref, hg_ref, oda_ref, hgain_ref, wo_ref, g2_ref,
                    wr_ref, br_ref, x2_ref, gates_ref):
    o = of_ref[...] + ob_ref[...]
    gate = hg_ref[...]
    gate = gate * jax.nn.sigmoid(gate)
    parts = []
    for hd in range(N_HEADS):
        sl = slice(hd * HEAD_W, (hd + 1) * HEAD_W)
        parts.append(_rms(o[:, sl], NORM_EPS) * hgain_ref[...] * gate[:, sl])
    parts.append(oda_ref[...])
    mixed = jnp.concatenate(parts, axis=1).astype(BF16)
    x2 = x_ref[...] + jnp.dot(mixed, wo_ref[...], preferred_element_type=F32)
    x2_ref[...] = x2

    h2 = _rms(x2, NORM_EPS) * g2_ref[...]
    logits = jnp.dot(h2, wr_ref[...], preferred_element_type=F32,
                     precision=lax.Precision.HIGHEST) + br_ref[...]
    lane = lax.broadcasted_iota(jnp.int32, logits.shape, 1)
    neg = -jnp.inf
    big = ROUTER_LANES

    def first_max(vals):
        mx = jnp.max(vals, axis=1, keepdims=True)
        idx = jnp.min(jnp.where(vals == mx, lane, big), axis=1, keepdims=True)
        return mx, idx

    gl = jnp.where((lane >= N_EXPERTS) & (lane < N_EXPERTS + N_GROUPS), logits, neg)
    gm, gidx = first_max(gl)
    g_p = 1.0 / jnp.sum(jnp.exp(gl - gm), axis=1, keepdims=True)
    el = jnp.where((lane < N_EXPERTS) & ((lane // EXPERTS_PER_GROUP) == (gidx - N_EXPERTS)),
                   logits, neg)
    em1, idx1 = first_max(el)
    em2, idx2 = first_max(jnp.where(lane == idx1, neg, el))
    e2 = jnp.exp(em2 - em1)
    w1 = g_p / (1.0 + e2)
    w2 = g_p * e2 / (1.0 + e2)
    gates_ref[...] = jnp.where(lane == idx1, w1, 0.0) + jnp.where(lane == idx2, w2, 0.0)


def _outproj(x, o_f, o_b, hg, o_da, hgain, w_out, g2, wr, br, tm=512):
    S = x.shape[0]
    row = lambda w: pl.BlockSpec((tm, w), lambda i: (i, 0))
    full = lambda a: pl.BlockSpec(a.shape, lambda i: (0, 0))
    return pl.pallas_call(
        _outproj_kernel,
        grid=(S // tm,),
        in_specs=[row(D_MODEL), row(HALF_W), row(HALF_W), row(HALF_W), row(HALF_W),
                  full(hgain), full(w_out), full(g2), full(wr), full(br)],
        out_specs=[row(D_MODEL), row(ROUTER_LANES)],
        out_shape=[jax.ShapeDtypeStruct((S, D_MODEL), F32),
                   jax.ShapeDtypeStruct((S, ROUTER_LANES), F32)],
        compiler_params=pltpu.CompilerParams(dimension_semantics=("arbitrary",),
                                             vmem_limit_bytes=V7X_VMEM_LIMIT),
        name="outproj_router",
    )(x, o_f, o_b, hg, o_da, hgain, w_out, g2, wr, br)


def _moe_kernel(x2_ref, gates_ref, g2_ref, wg_ref, wu_ref, wd_ref, gf_ref, out_ref,
                h2_ref, acc_ref):
    e = pl.program_id(1)

    @pl.when(e == 0)
    def _():
        h2_ref[...] = (_rms(x2_ref[...], NORM_EPS) * g2_ref[...]).astype(BF16)
        acc_ref[...] = jnp.zeros_like(acc_ref)

    h2 = h2_ref[...]
    gates = gates_ref[...]
    lane = lax.broadcasted_iota(jnp.int32, gates.shape, 1)
    gcol = jnp.sum(jnp.where(lane == e, gates, 0.0), axis=1, keepdims=True)
    a = jnp.dot(h2, wg_ref[0], preferred_element_type=F32)
    a = a * jax.nn.sigmoid(a) * jnp.dot(h2, wu_ref[0], preferred_element_type=F32)
    acc_ref[...] += gcol * jnp.dot(a.astype(BF16), wd_ref[0], preferred_element_type=F32)

    @pl.when(e == N_EXPERTS - 1)
    def _():
        out_ref[...] = _rms(x2_ref[...] + acc_ref[...], NORM_EPS) * gf_ref[...]


def _moe(x2, gates, g2, wg, wu, wd, gf, tb=1024):
    S = x2.shape[0]
    return pl.pallas_call(
        _moe_kernel,
        grid=(S // tb, N_EXPERTS),
        in_specs=[pl.BlockSpec((tb, D_MODEL), lambda b, e: (b, 0)),
                  pl.BlockSpec((tb, ROUTER_LANES), lambda b, e: (b, 0)),
                  pl.BlockSpec((1, D_MODEL), lambda b, e: (0, 0)),
                  pl.BlockSpec((1, D_MODEL, D_EXPERT), lambda b, e: (e, 0, 0)),
                  pl.BlockSpec((1, D_MODEL, D_EXPERT), lambda b, e: (e, 0, 0)),
                  pl.BlockSpec((1, D_EXPERT, D_MODEL), lambda b, e: (e, 0, 0)),
                  pl.BlockSpec((1, D_MODEL), lambda b, e: (0, 0))],
        out_specs=pl.BlockSpec((tb, D_MODEL), lambda b, e: (b, 0)),
        out_shape=jax.ShapeDtypeStruct((S, D_MODEL), F32),
        scratch_shapes=[pltpu.VMEM((tb, D_MODEL), BF16), pltpu.VMEM((tb, D_MODEL), F32)],
        compiler_params=pltpu.CompilerParams(dimension_semantics=("arbitrary", "arbitrary"),
                                             vmem_limit_bytes=V7X_VMEM_LIMIT),
        name="moe",
    )(x2, gates, g2, wg, wu, wd, gf)


def kernel(x, positions, norm1_gain, w_in, hg_lower_bounds, hg_norm_gain, diff_lambda,
           diff_subln_gain, w_out, norm2_gain, router_group_w, router_group_b,
           router_expert_w, router_expert_b, moe_w_gate, moe_w_up, moe_w_down,
           final_norm_gain):
    B, S, _ = x.shape
    assert B == 1 and norm1_gain.shape[0] == 1
    layer = 0
    xs = x.reshape(S, D_MODEL)
    pos = positions.reshape(S, 1).astype(F32)

    lb_cum = jnp.cumsum(jax.nn.softmax(hg_lower_bounds.astype(F32), axis=1), axis=1)
    lb = lb_cum[:, layer + 1] - lb_cum[:, 0]
    lam_p = diff_lambda[layer].astype(F32)
    lam = (jnp.exp(jnp.sum(lam_p[0] * lam_p[1])) - jnp.exp(jnp.sum(lam_p[2] * lam_p[3]))
           + LAMBDA_INIT).reshape(1)
    lane = jnp.arange(HEAD_W)
    inv_freq = jnp.float32(ROPE_THETA) ** (-jnp.arange(0, ROT_DIM, 2, dtype=F32) / ROT_DIM)
    in_rot = (lane % QK_DIM) < ROT_DIM
    invf = jnp.where(in_rot, inv_freq[lane % (ROT_DIM // 2)], 0.0).reshape(1, HEAD_W)
    sgn = jnp.where(in_rot, jnp.where((lane % QK_DIM) < ROT_DIM // 2, -1.0, 1.0),
                    0.0).astype(F32).reshape(1, HEAD_W)
    wr = jnp.zeros((D_MODEL, ROUTER_LANES), F32)
    wr = wr.at[:, :N_EXPERTS].set(router_expert_w[layer])
    wr = wr.at[:, N_EXPERTS:N_EXPERTS + N_GROUPS].set(router_group_w[layer])
    br = jnp.zeros((1, ROUTER_LANES), F32)
    br = br.at[0, :N_EXPERTS].set(router_expert_b[layer])
    br = br.at[0, N_EXPERTS:N_EXPERTS + N_GROUPS].set(router_group_b[layer])

    hq, kf, kb, lff, lfb, hi, hg, dq, dk, dv = _inproj(
        xs, pos, norm1_gain[layer].reshape(1, -1), w_in[layer].astype(BF16), lb, invf, sgn)
    o_f = _hgrn(hq, kf, lff, hi, rev=False)
    o_b = _hgrn(hq, kb, lfb, hi, rev=True)
    o_da = _attn(lam, dq, dk, dv, diff_subln_gain[layer].reshape(1, -1))
    x2, gates = _outproj(xs, o_f, o_b, hg, o_da, hg_norm_gain[layer].reshape(1, -1),
                         w_out[layer].astype(BF16), norm2_gain[layer].reshape(1, -1), wr, br)
    out = _moe(x2, gates, norm2_gain[layer].reshape(1, -1), moe_w_gate[layer].astype(BF16),
               moe_w_up[layer].astype(BF16), moe_w_down[layer].astype(BF16),
               final_norm_gain.reshape(1, -1))
    return out.reshape(B, S, D_MODEL)
```

```python
import functools
import math

import jax
import jax.numpy as jnp
from jax import lax
from jax.experimental import pallas as pl
from jax.experimental.pallas import tpu as pltpu

F32 = jnp.float32
BF16 = jnp.bfloat16

D_MODEL = 1024
HEAD_W = 128
N_HEADS = 4
HALF_W = N_HEADS * HEAD_W
QK_DIM = 64
ROT_DIM = 16
ROPE_THETA = 500000.0
HG_CHUNK = 64
HG_UNROLL = 8
N_GROUPS = 4
EXPERTS_PER_GROUP = 8
N_EXPERTS = 32
D_EXPERT = 512
NORM_EPS = 1e-6
SUBLN_EPS = 1e-5
LAMBDA_INIT = 0.8 - 0.6 * math.exp(-0.3 * 0)
LOG2E = 1.4426950408889634
ROUTER_LANES = 128
MOE_TOKEN_BLOCK = 1024
V7X_VMEM_LIMIT = 56 * 1024 * 1024

NT_DIMS = (((1,), (1,)), ((), ()))
TN_DIMS = (((0,), (0,)), ((), ()))


def _rms(x, eps):
    return x * lax.rsqrt(jnp.mean(x * x, axis=-1, keepdims=True) + eps)


def _inproj_kernel(x_ref, pos_ref, g1_ref, w_ref, lb_ref, invf_ref, sgn_ref,
                   hq_ref, kf_ref, kb_ref, lff_ref, lfb_ref, hi_ref, hg_ref,
                   dq_ref, dk_ref, dv_ref):
    x = x_ref[...]
    h = (_rms(x, NORM_EPS) * g1_ref[...]).astype(BF16)

    def proj(i):
        return jnp.dot(h, w_ref[:, i * HALF_W:(i + 1) * HALF_W], preferred_element_type=F32)

    hq_ref[...] = proj(0)
    for i, (k_ref, lf_ref) in enumerate(((kf_ref, lff_ref), (kb_ref, lfb_ref))):
        z = proj(1 + i)
        lb = lb_ref[i:i + 1, :]
        sg = jax.nn.sigmoid(z)
        lf_ref[...] = jnp.log(lb + (1.0 - lb) * sg)
        k_ref[...] = (1.0 - lb) * (1.0 - sg)
    hi_ref[...] = proj(3)
    hg_ref[...] = proj(4)

    ang = pos_ref[...] * invf_ref[...]
    cos = jnp.concatenate([jnp.cos(ang)] * N_HEADS, axis=1)
    sgn = sgn_ref[...]
    sin = jnp.concatenate([jnp.sin(ang) * sgn] * N_HEADS, axis=1)
    take_hi = jnp.concatenate([jnp.broadcast_to(sgn, ang.shape)] * N_HEADS, axis=1) < 0.0

    def rotary(t):
        partner = jnp.where(take_hi, pltpu.roll(t, HALF_W - ROT_DIM // 2, axis=1),
                            pltpu.roll(t, ROT_DIM // 2, axis=1))
        return t * cos + partner * sin

    dq_ref[...] = (rotary(proj(5)) * (QK_DIM ** -0.5 * LOG2E)).astype(BF16)
    dk_ref[...] = rotary(proj(6)).astype(BF16)
    v = proj(7)
    lane = lax.broadcasted_iota(jnp.int32, (v.shape[0], HEAD_W), 1)
    ones_col = jnp.where(lane == 0, 1.0, 0.0)
    pieces = []
    for hd in range(N_HEADS):
        pieces += [v[:, hd * HEAD_W:(hd + 1) * HEAD_W], ones_col]
    dv_ref[...] = jnp.concatenate(pieces, axis=1).astype(BF16)


def _inproj(x, pos, g1, w_in, lb, invf, sgn, tm=512):
    S = x.shape[0]
    row = lambda w: pl.BlockSpec((tm, w), lambda i: (i, 0))
    full = lambda a: pl.BlockSpec(a.shape, lambda i: (0, 0))
    f32o = jax.ShapeDtypeStruct((S, HALF_W), F32)
    bf16o = jax.ShapeDtypeStruct((S, HALF_W), BF16)
    return pl.pallas_call(
        _inproj_kernel,
        grid=(S // tm,),
        in_specs=[row(D_MODEL), row(1), full(g1), full(w_in), full(lb), full(invf), full(sgn)],
        out_specs=[row(HALF_W)] * 9 + [row(2 * HALF_W)],
        out_shape=[f32o] * 7 + [bf16o, bf16o, jax.ShapeDtypeStruct((S, 2 * HALF_W), BF16)],
        compiler_params=pltpu.CompilerParams(dimension_semantics=("arbitrary",),
                                             vmem_limit_bytes=V7X_VMEM_LIMIT),
        name="inproj",
    )(x, pos, g1, w_in, lb, invf, sgn)


def _chunk_cumsum(x, rev):
    n = x.shape[0]
    row = lax.broadcasted_iota(jnp.int32, x.shape, 0)
    d = 1
    while d < n:
        if rev:
            x = x + jnp.where(row < n - d, pltpu.roll(x, n - d, axis=0), 0.0)
        else:
            x = x + jnp.where(row >= d, pltpu.roll(x, d, axis=0), 0.0)
        d *= 2
    return x


def _pivot_rows(b, blk, rev):
    n = b.shape[0]
    half = blk // 2
    groups = []
    for g0 in range(0, n, 8):
        def piv(r):
            base = (r // blk) * blk
            return base + half if rev else base + half - 1
        if blk >= 16 or blk == 8:
            p = piv(g0)
            groups.append(jnp.broadcast_to(b[p:p + 1, :], (8, b.shape[1])))
        else:
            sub = lax.broadcasted_iota(jnp.int32, (8, b.shape[1]), 0)
            acc = None
            for s0 in range(0, 8, blk):
                p = piv(g0 + s0)
                rowv = jnp.broadcast_to(b[p:p + 1, :], (8, b.shape[1]))
                acc = rowv if acc is None else jnp.where(sub >= s0, rowv, acc)
            groups.append(acc)
    return jnp.concatenate(groups, axis=0)


def _hgrn_kernel(q_ref, k_ref, lf_ref, v_ref, o_ref, st_ref, *, rev, n_chunks):
    @pl.when(pl.program_id(1) == 0)
    def _():
        st_ref[...] = jnp.zeros_like(st_ref)

    C = HG_CHUNK
    rt = lax.broadcasted_iota(jnp.int32, (C, C), 0)
    ct = lax.broadcasted_iota(jnp.int32, (C, C), 1)
    rowi = lax.broadcasted_iota(jnp.int32, (C, HEAD_W), 0)

    def chunk(ci, st):
        c = (n_chunks - 1 - ci) if rev else ci
        sl = pl.ds(pl.multiple_of(c * C, C), C)
        q = q_ref[sl, :]
        k = k_ref[sl, :]
        v = v_ref[sl, :]
        vb = v.astype(BF16)
        b = _chunk_cumsum(lf_ref[sl, :], rev)
        tot = b[0:1, :] if rev else b[C - 1:C, :]

        o = lax.dot_general((q * jnp.exp(b)).astype(BF16), st.astype(BF16), NT_DIMS,
                            preferred_element_type=F32)
        kdec = (k * jnp.exp(tot - b)).astype(BF16)
        st_next = st * jnp.exp(tot) + lax.dot_general(vb, kdec, TN_DIMS,
                                                      preferred_element_type=F32)

        scores = jnp.where(rt == ct, jnp.sum(q * k, axis=1, keepdims=True), 0.0)
        blk = C
        while blk >= 2:
            half = blk // 2
            w = jnp.exp(-jnp.abs(b - _pivot_rows(b, blk, rev)))
            in_q = ((rowi % blk) < half) if rev else ((rowi % blk) >= half)
            qs = jnp.where(in_q, q * w, 0.0).astype(BF16)
            ks = jnp.where(in_q, 0.0, k * w).astype(BF16)
            g = lax.dot_general(qs, ks, NT_DIMS, preferred_element_type=F32)
            scores = scores + jnp.where((rt // blk) == (ct // blk), g, 0.0)
            blk = half
        o_ref[sl, :] = o + jnp.dot(scores.astype(BF16), vb, preferred_element_type=F32)
        return st_next

    st_ref[...] = lax.fori_loop(0, n_chunks, chunk, st_ref[...], unroll=HG_UNROLL)


def _hgrn(q, k, lf, v, rev, rb=1024):
    S = q.shape[0]
    nb = S // rb
    idx = (lambda h, j: (nb - 1 - j, h)) if rev else (lambda h, j: (j, h))
    spec = pl.BlockSpec((rb, HEAD_W), idx)
    return pl.pallas_call(
        functools.partial(_hgrn_kernel, rev=rev, n_chunks=rb // HG_CHUNK),
        grid=(N_HEADS, nb),
        in_specs=[spec] * 4,
        out_specs=spec,
        out_shape=jax.ShapeDtypeStruct((S, HALF_W), F32),
        scratch_shapes=[pltpu.VMEM((HEAD_W, HEAD_W), F32)],
        compiler_params=pltpu.CompilerParams(dimension_semantics=("arbitrary", "arbitrary"),
                                             vmem_limit_bytes=V7X_VMEM_LIMIT),
        name="hgrn_bwd" if rev else "hgrn_fwd",
    )(q, k, lf, v)


def _attn_kernel(lam_ref, q_ref, k_ref, v_ref, gain_ref, o_ref, sa_ref, sb_ref, m_ref, acc_ref,
                 *, tk):
    n_kv = k_ref.shape[0] // tk
    assert n_kv % 2 == 0
    q = q_ref[...]
    lane = lax.broadcasted_iota(jnp.int32, q.shape, 1)
    qm = (jnp.where(lane < QK_DIM, q, jnp.zeros_like(q)),
          jnp.where(lane >= QK_DIM, q, jnp.zeros_like(q)))
    m_ref[...] = jnp.full(m_ref.shape, -jnp.inf, F32)
    acc_ref[...] = jnp.zeros_like(acc_ref)

    def chunk(j):
        return pl.ds(pl.multiple_of(j * tk, tk), tk)

    def scores(j, s_ref):
        kc = k_ref[chunk(j), :]
        for c in range(2):
            s_ref[c] = lax.dot_general(qm[c], kc, NT_DIMS, preferred_element_type=F32)

    def softmax_pv(j, s_ref):
        vc = v_ref[chunk(j), :]
        for c in range(2):
            s = s_ref[c]
            m_old = m_ref[c]
            m_new = jnp.maximum(m_old, jnp.max(s, axis=1, keepdims=True))
            p = jnp.exp2(s - m_new).astype(BF16)
            acc_ref[c] = jnp.exp2(m_old - m_new) * acc_ref[c] + jnp.dot(
                p, vc, preferred_element_type=F32)
            m_ref[c] = m_new

    scores(0, sa_ref)

    def two_steps(i, carry):
        scores(2 * i + 1, sb_ref)
        softmax_pv(2 * i, sa_ref)
        scores(2 * i + 2, sa_ref)
        softmax_pv(2 * i + 1, sb_ref)
        return carry

    lax.fori_loop(0, n_kv // 2 - 1, two_steps, 0)
    scores(n_kv - 1, sb_ref)
    softmax_pv(n_kv - 2, sa_ref)
    softmax_pv(n_kv - 1, sb_ref)

    outs = []
    for c in range(2):
        a = acc_ref[c]
        outs.append(a[:, :HEAD_W] / a[:, HEAD_W:HEAD_W + 1])
    o = outs[0] - lam_ref[0] * outs[1]
    o_ref[...] = _rms(o, SUBLN_EPS) * gain_ref[...] * (1.0 - LAMBDA_INIT)


def _attn(lam, dq, dk, dv, gain, tq=512, tk=1024):
    S = dq.shape[0]
    return pl.pallas_call(
        functools.partial(_attn_kernel, tk=tk),
        grid=(N_HEADS, S // tq),
        in_specs=[pl.BlockSpec(memory_space=pltpu.SMEM),
                  pl.BlockSpec((tq, HEAD_W), lambda h, i: (i, h)),
                  pl.BlockSpec((S, HEAD_W), lambda h, i: (0, h)),
                  pl.BlockSpec((S, 2 * HEAD_W), lambda h, i: (0, h)),
                  pl.BlockSpec((1, HEAD_W), lambda h, i: (0, 0))],
        out_specs=pl.BlockSpec((tq, HEAD_W), lambda h, i: (i, h)),
        out_shape=jax.ShapeDtypeStruct((S, HALF_W), F32),
        scratch_shapes=[pltpu.VMEM((2, tq, tk), F32), pltpu.VMEM((2, tq, tk), F32),
                        pltpu.VMEM((2, tq, 1), F32), pltpu.VMEM((2, tq, 2 * HEAD_W), F32)],
        compiler_params=pltpu.CompilerParams(dimension_semantics=("arbitrary", "arbitrary"),
                                             vmem_limit_bytes=V7X_VMEM_LIMIT),
        name="diff_attn",
    )(lam, dq, dk, dv, gain)


def _outproj_kernel(x_ref, of_ref, ob_ref, hg_ref, oda_ref, hgain_ref, wo_ref, g2_ref,
                    wr_ref, br_ref, x2_ref, rank_ref, rank_t_ref, gates_t_ref, cnt_ref,
                    carry_ref, *, sub_blocks):
    i = pl.program_id(0)
    o = of_ref[...] + ob_ref[...]
    gate = hg_ref[...]
    gate = gate * jax.nn.sigmoid(gate)
    parts = []
    for hd in range(N_HEADS):
        sl = slice(hd * HEAD_W, (hd + 1) * HEAD_W)
        parts.append(_rms(o[:, sl], NORM_EPS) * hgain_ref[...] * gate[:, sl])
    parts.append(oda_ref[...])
    mixed = jnp.concatenate(parts, axis=1).astype(BF16)
    x2 = x_ref[...] + jnp.dot(mixed, wo_ref[...], preferred_element_type=F32)
    x2_ref[...] = x2

    h2 = _rms(x2, NORM_EPS) * g2_ref[...]
    logits = jnp.dot(h2, wr_ref[...], preferred_element_type=F32,
                     precision=lax.Precision.HIGHEST) + br_ref[...]
    lane = lax.broadcasted_iota(jnp.int32, logits.shape, 1)
    neg = -jnp.inf
    big = ROUTER_LANES

    def first_max(vals):
        mx = jnp.max(vals, axis=1, keepdims=True)
        idx = jnp.min(jnp.where(vals == mx, lane, big), axis=1, keepdims=True)
        return mx, idx

    gl = jnp.where((lane >= N_EXPERTS) & (lane < N_EXPERTS + N_GROUPS), logits, neg)
    gm, gidx = first_max(gl)
    g_p = 1.0 / jnp.sum(jnp.exp(gl - gm), axis=1, keepdims=True)
    el = jnp.where((lane < N_EXPERTS) & ((lane // EXPERTS_PER_GROUP) == (gidx - N_EXPERTS)),
                   logits, neg)
    em1, idx1 = first_max(el)
    em2, idx2 = first_max(jnp.where(lane == idx1, neg, el))
    e2 = jnp.exp(em2 - em1)
    w1 = g_p / (1.0 + e2)
    w2 = g_p * e2 / (1.0 + e2)
    is1 = lane == idx1
    is2 = lane == idx2
    gates = jnp.where(is1, w1, 0.0) + jnp.where(is2, w2, 0.0)

    @pl.when(i % sub_blocks == 0)
    def _():
        carry_ref[...] = jnp.zeros_like(carry_ref)

    assigned = jnp.where(is1 | is2, 1.0, 0.0)
    tm = assigned.shape[0]
    tri = jnp.where(lax.broadcasted_iota(jnp.int32, (tm, tm), 0)
                    >= lax.broadcasted_iota(jnp.int32, (tm, tm), 1), 1.0, 0.0).astype(BF16)
    cum = jnp.dot(tri, assigned.astype(BF16), preferred_element_type=F32)
    carry = carry_ref[...]
    rank = jnp.where(assigned > 0.0, carry + cum - 1.0, -1.0)
    carry = carry + cum[tm - 1:tm, :]
    carry_ref[...] = carry
    cnt_ref[pl.ds(i // sub_blocks, 1), :] = carry.astype(jnp.int32)
    rank_ref[...] = rank
    rank_t_ref[...] = rank.T
    gates_t_ref[...] = gates.T


def _outproj(x, o_f, o_b, hg, o_da, hgain, w_out, g2, wr, br, tb, tm=512):
    S = x.shape[0]
    row = lambda w: pl.BlockSpec((tm, w), lambda i: (i, 0))
    col = pl.BlockSpec((ROUTER_LANES, tm), lambda i: (0, i))
    full = lambda a: pl.BlockSpec(a.shape, lambda i: (0, 0))
    return pl.pallas_call(
        functools.partial(_outproj_kernel, sub_blocks=tb // tm),
        grid=(S // tm,),
        in_specs=[row(D_MODEL), row(HALF_W), row(HALF_W), row(HALF_W), row(HALF_W),
                  full(hgain), full(w_out), full(g2), full(wr), full(br)],
        out_specs=[row(D_MODEL), row(ROUTER_LANES), col, col,
                   pl.BlockSpec((S // tb, ROUTER_LANES), lambda i: (0, 0))],
        out_shape=[jax.ShapeDtypeStruct((S, D_MODEL), F32),
                   jax.ShapeDtypeStruct((S, ROUTER_LANES), F32),
                   jax.ShapeDtypeStruct((ROUTER_LANES, S), F32),
                   jax.ShapeDtypeStruct((ROUTER_LANES, S), F32),
                   jax.ShapeDtypeStruct((S // tb, ROUTER_LANES), jnp.int32)],
        scratch_shapes=[pltpu.VMEM((1, ROUTER_LANES), F32)],
        compiler_params=pltpu.CompilerParams(dimension_semantics=("arbitrary",),
                                             vmem_limit_bytes=V7X_VMEM_LIMIT),
        name="outproj_router",
    )(x, o_f, o_b, hg, o_da, hgain, w_out, g2, wr, br)


def _moe_kernel(cnt_ref, x2_ref, rank_ref, rank_t_ref, gates_t_ref, g2_ref, wg_ref, wu_ref,
                wd_ref, gf_ref, out_ref, h2_ref, acc_ref, *, tile_rows):
    b = pl.program_id(0)
    p = pl.program_id(1)
    R = tile_rows
    n_exp = wg_ref.shape[0]

    @pl.when(p == 0)
    def _():
        h2_ref[...] = (_rms(x2_ref[...], NORM_EPS) * g2_ref[...]).astype(BF16)
        acc_ref[...] = jnp.zeros_like(acc_ref)

    rank_blk = rank_ref[...]
    lane = lax.broadcasted_iota(jnp.int32, rank_blk.shape, 1)
    experts = [n_exp * p + i for i in range(n_exp)]
    rank_col = [jnp.sum(jnp.where(lane == e, rank_blk, 0.0), axis=1, keepdims=True)
                for e in experts]
    rank_row = [rank_t_ref[pl.ds(e, 1), :] for e in experts]
    gate_row = [gates_t_ref[pl.ds(e, 1), :] for e in experts]
    n_max = cnt_ref[b, experts[0]]
    for e in experts[1:]:
        n_max = jnp.maximum(n_max, cnt_ref[b, e])

    def tile(t, carry):
        base = (t * R).astype(F32)
        slot_col = lax.broadcasted_iota(jnp.int32, (R, 1), 0).astype(F32) + base
        slot_row = lax.broadcasted_iota(jnp.int32, (1, R), 1).astype(F32) + base
        sel = [rank_row[i] == slot_col for i in range(n_exp)]
        gather = jnp.concatenate([jnp.where(s, 1.0, 0.0).astype(BF16) for s in sel], axis=0)
        xg = jnp.dot(gather, h2_ref[...], preferred_element_type=F32).astype(BF16)
        ys = []
        for i in range(n_exp):
            xi = xg[i * R:(i + 1) * R]
            a = jnp.dot(xi, wg_ref[i], preferred_element_type=F32)
            a = a * jax.nn.sigmoid(a) * jnp.dot(xi, wu_ref[i], preferred_element_type=F32)
            y = jnp.dot(a.astype(BF16), wd_ref[i], preferred_element_type=F32)
            g = jnp.sum(jnp.where(sel[i], gate_row[i], 0.0), axis=1, keepdims=True)
            ys.append((y * g).astype(BF16))
        scatter = jnp.concatenate(
            [jnp.where(rank_col[i] == slot_row, 1.0, 0.0).astype(BF16) for i in range(n_exp)],
            axis=1)
        acc_ref[...] += jnp.dot(scatter, jnp.concatenate(ys, axis=0),
                                preferred_element_type=F32)
        return carry

    lax.fori_loop(0, (n_max + R - 1) // R, tile, 0)

    @pl.when(p == pl.num_programs(1) - 1)
    def _():
        out_ref[...] = _rms(x2_ref[...] + acc_ref[...], NORM_EPS) * gf_ref[...]


def _moe(cnt, x2, rank, rank_t, gates_t, g2, wg, wu, wd, gf, tb, tile_rows=128,
         experts_per_step=2):
    S = x2.shape[0]
    eps = experts_per_step
    grid_spec = pltpu.PrefetchScalarGridSpec(
        num_scalar_prefetch=1,
        grid=(S // tb, N_EXPERTS // eps),
        in_specs=[pl.BlockSpec((tb, D_MODEL), lambda b, p, c: (b, 0)),
                  pl.BlockSpec((tb, ROUTER_LANES), lambda b, p, c: (b, 0)),
                  pl.BlockSpec((N_EXPERTS, tb), lambda b, p, c: (0, b)),
                  pl.BlockSpec((N_EXPERTS, tb), lambda b, p, c: (0, b)),
                  pl.BlockSpec((1, D_MODEL), lambda b, p, c: (0, 0)),
                  pl.BlockSpec((eps, D_MODEL, D_EXPERT), lambda b, p, c: (p, 0, 0)),
                  pl.BlockSpec((eps, D_MODEL, D_EXPERT), lambda b, p, c: (p, 0, 0)),
                  pl.BlockSpec((eps, D_EXPERT, D_MODEL), lambda b, p, c: (p, 0, 0)),
                  pl.BlockSpec((1, D_MODEL), lambda b, p, c: (0, 0))],
        out_specs=pl.BlockSpec((tb, D_MODEL), lambda b, p, c: (b, 0)),
        scratch_shapes=[pltpu.VMEM((tb, D_MODEL), BF16), pltpu.VMEM((tb, D_MODEL), F32)])
    return pl.pallas_call(
        functools.partial(_moe_kernel, tile_rows=tile_rows),
        grid_spec=grid_spec,
        out_shape=jax.ShapeDtypeStruct((S, D_MODEL), F32),
        compiler_params=pltpu.CompilerParams(dimension_semantics=("arbitrary", "arbitrary"),
                                             vmem_limit_bytes=V7X_VMEM_LIMIT),
        name="moe",
    )(cnt, x2, rank, rank_t, gates_t, g2, wg, wu, wd, gf)


def kernel(x, positions, norm1_gain, w_in, hg_lower_bounds, hg_norm_gain, diff_lambda,
           diff_subln_gain, w_out, norm2_gain, router_group_w, router_group_b,
           router_expert_w, router_expert_b, moe_w_gate, moe_w_up, moe_w_down,
           final_norm_gain):
    B, S, _ = x.shape
    assert B == 1 and norm1_gain.shape[0] == 1
    layer = 0
    xs = x.reshape(S, D_MODEL)
    pos = positions.reshape(S, 1).astype(F32)

    lb_cum = jnp.cumsum(jax.nn.softmax(hg_lower_bounds.astype(F32), axis=1), axis=1)
    lb = lb_cum[:, layer + 1] - lb_cum[:, 0]
    lam_p = diff_lambda[layer].astype(F32)
    lam = (jnp.exp(jnp.sum(lam_p[0] * lam_p[1])) - jnp.exp(jnp.sum(lam_p[2] * lam_p[3]))
           + LAMBDA_INIT).reshape(1)
    lane = jnp.arange(HEAD_W)
    inv_freq = jnp.float32(ROPE_THETA) ** (-jnp.arange(0, ROT_DIM, 2, dtype=F32) / ROT_DIM)
    in_rot = (lane % QK_DIM) < ROT_DIM
    invf = jnp.where(in_rot, inv_freq[lane % (ROT_DIM // 2)], 0.0).reshape(1, HEAD_W)
    sgn = jnp.where(in_rot, jnp.where((lane % QK_DIM) < ROT_DIM // 2, -1.0, 1.0),
                    0.0).astype(F32).reshape(1, HEAD_W)
    wr = jnp.zeros((D_MODEL, ROUTER_LANES), F32)
    wr = wr.at[:, :N_EXPERTS].set(router_expert_w[layer])
    wr = wr.at[:, N_EXPERTS:N_EXPERTS + N_GROUPS].set(router_group_w[layer])
    br = jnp.zeros((1, ROUTER_LANES), F32)
    br = br.at[0, :N_EXPERTS].set(router_expert_b[layer])
    br = br.at[0, N_EXPERTS:N_EXPERTS + N_GROUPS].set(router_group_b[layer])

    hq, kf, kb, lff, lfb, hi, hg, dq, dk, dv = _inproj(
        xs, pos, norm1_gain[layer].reshape(1, -1), w_in[layer].astype(BF16), lb, invf, sgn)
    o_f = _hgrn(hq, kf, lff, hi, rev=False)
    o_b = _hgrn(hq, kb, lfb, hi, rev=True)
    o_da = _attn(lam, dq, dk, dv, diff_subln_gain[layer].reshape(1, -1))
    x2, rank, rank_t, gates_t, cnt = _outproj(
        xs, o_f, o_b, hg, o_da, hg_norm_gain[layer].reshape(1, -1), w_out[layer].astype(BF16),
        norm2_gain[layer].reshape(1, -1), wr, br, tb=MOE_TOKEN_BLOCK)
    out = _moe(cnt, x2, rank, rank_t, gates_t, norm2_gain[layer].reshape(1, -1),
               moe_w_gate[layer].astype(BF16), moe_w_up[layer].astype(BF16),
               moe_w_down[layer].astype(BF16), final_norm_gain.reshape(1, -1),
               tb=MOE_TOKEN_BLOCK)
    return out.reshape(B, S, D_MODEL)
```

```python
import functools
import math

import jax
import jax.numpy as jnp
from jax import lax
from jax.experimental import pallas as pl
from jax.experimental.pallas import tpu as pltpu

F32 = jnp.float32
BF16 = jnp.bfloat16

D_MODEL = 1024
HEAD_W = 128
N_HEADS = 4
HALF_W = N_HEADS * HEAD_W
QK_DIM = 64
ROT_DIM = 16
ROPE_THETA = 500000.0
VT_ROWS = HEAD_W + 16
HG_CHUNK = 64
HG_UNROLL = 8
N_GROUPS = 4
EXPERTS_PER_GROUP = 8
N_EXPERTS = 32
D_EXPERT = 512
NORM_EPS = 1e-6
SUBLN_EPS = 1e-5
LAMBDA_INIT = 0.8 - 0.6 * math.exp(-0.3 * 0)
LOG2E = 1.4426950408889634
ROUTER_LANES = 128
MOE_TOKEN_BLOCK = 1024
V7X_VMEM_LIMIT = 56 * 1024 * 1024

NT_DIMS = (((1,), (1,)), ((), ()))
TN_DIMS = (((0,), (0,)), ((), ()))


def _rms(x, eps):
    return x * lax.rsqrt(jnp.mean(x * x, axis=-1, keepdims=True) + eps)


def _inproj_kernel(x_ref, pos_ref, g1_ref, w_ref, lb_ref, invf_ref, sgn_ref,
                   hq_ref, kf_ref, kb_ref, lff_ref, lfb_ref, hi_ref, hg_ref,
                   dq_ref, dk_ref, dv_ref):
    x = x_ref[...]
    h = (_rms(x, NORM_EPS) * g1_ref[...]).astype(BF16)

    def proj(i):
        return jnp.dot(h, w_ref[:, i * HALF_W:(i + 1) * HALF_W], preferred_element_type=F32)

    hq_ref[...] = proj(0)
    for i, (k_ref, lf_ref) in enumerate(((kf_ref, lff_ref), (kb_ref, lfb_ref))):
        z = proj(1 + i)
        lb = lb_ref[i:i + 1, :]
        sg = jax.nn.sigmoid(z)
        lf_ref[...] = jnp.log(lb + (1.0 - lb) * sg)
        k_ref[...] = (1.0 - lb) * (1.0 - sg)
    hi_ref[...] = proj(3)
    hg_ref[...] = proj(4)

    ang = pos_ref[...] * invf_ref[...]
    cos = jnp.concatenate([jnp.cos(ang)] * N_HEADS, axis=1)
    sgn = sgn_ref[...]
    sin = jnp.concatenate([jnp.sin(ang) * sgn] * N_HEADS, axis=1)
    take_hi = jnp.concatenate([jnp.broadcast_to(sgn, ang.shape)] * N_HEADS, axis=1) < 0.0

    def rotary(t):
        partner = jnp.where(take_hi, pltpu.roll(t, HALF_W - ROT_DIM // 2, axis=1),
                            pltpu.roll(t, ROT_DIM // 2, axis=1))
        return t * cos + partner * sin

    dq_ref[...] = (rotary(proj(5)) * (QK_DIM ** -0.5 * LOG2E)).astype(BF16)
    dk_ref[...] = rotary(proj(6)).astype(BF16)
    v = proj(7)
    sub = lax.broadcasted_iota(jnp.int32, (VT_ROWS - HEAD_W, v.shape[0]), 0)
    ones_rows = jnp.where(sub == 0, 1.0, 0.0)
    pieces = []
    for hd in range(N_HEADS):
        pieces += [v[:, hd * HEAD_W:(hd + 1) * HEAD_W].T, ones_rows]
    dv_ref[...] = jnp.concatenate(pieces, axis=0).astype(BF16)


def _inproj(x, pos, g1, w_in, lb, invf, sgn, tm=512):
    S = x.shape[0]
    row = lambda w: pl.BlockSpec((tm, w), lambda i: (i, 0))
    full = lambda a: pl.BlockSpec(a.shape, lambda i: (0, 0))
    f32o = jax.ShapeDtypeStruct((S, HALF_W), F32)
    bf16o = jax.ShapeDtypeStruct((S, HALF_W), BF16)
    return pl.pallas_call(
        _inproj_kernel,
        grid=(S // tm,),
        in_specs=[row(D_MODEL), row(1), full(g1), full(w_in), full(lb), full(invf), full(sgn)],
        out_specs=[row(HALF_W)] * 9 + [pl.BlockSpec((N_HEADS * VT_ROWS, tm), lambda i: (0, i))],
        out_shape=[f32o] * 7 + [bf16o, bf16o,
                                jax.ShapeDtypeStruct((N_HEADS * VT_ROWS, S), BF16)],
        compiler_params=pltpu.CompilerParams(dimension_semantics=("arbitrary",),
                                             vmem_limit_bytes=V7X_VMEM_LIMIT),
        name="inproj",
    )(x, pos, g1, w_in, lb, invf, sgn)


def _chunk_cumsum(x, rev):
    n = x.shape[0]
    row = lax.broadcasted_iota(jnp.int32, x.shape, 0)
    d = 1
    while d < n:
        if rev:
            x = x + jnp.where(row < n - d, pltpu.roll(x, n - d, axis=0), 0.0)
        else:
            x = x + jnp.where(row >= d, pltpu.roll(x, d, axis=0), 0.0)
        d *= 2
    return x


def _pivot_rows(b, blk, rev):
    n = b.shape[0]
    half = blk // 2
    groups = []
    for g0 in range(0, n, 8):
        def piv(r):
            base = (r // blk) * blk
            return base + half if rev else base + half - 1
        if blk >= 16 or blk == 8:
            p = piv(g0)
            groups.append(jnp.broadcast_to(b[p:p + 1, :], (8, b.shape[1])))
        else:
            sub = lax.broadcasted_iota(jnp.int32, (8, b.shape[1]), 0)
            acc = None
            for s0 in range(0, 8, blk):
                p = piv(g0 + s0)
                rowv = jnp.broadcast_to(b[p:p + 1, :], (8, b.shape[1]))
                acc = rowv if acc is None else jnp.where(sub >= s0, rowv, acc)
            groups.append(acc)
    return jnp.concatenate(groups, axis=0)


def _hgrn_kernel(q_ref, k_ref, lf_ref, v_ref, o_ref, st_ref, *, rev, n_chunks):
    @pl.when(pl.program_id(1) == 0)
    def _():
        st_ref[...] = jnp.zeros_like(st_ref)

    C = HG_CHUNK
    rt = lax.broadcasted_iota(jnp.int32, (C, C), 0)
    ct = lax.broadcasted_iota(jnp.int32, (C, C), 1)
    rowi = lax.broadcasted_iota(jnp.int32, (C, HEAD_W), 0)
    eye = jnp.where(rt == ct, 1.0, 0.0)
    levels = []
    blk = C
    while blk >= 2:
        half = blk // 2
        q_side = (lambda r: (r % blk) < half) if rev else (lambda r: (r % blk) >= half)
        pair = ((rt // blk) == (ct // blk)) & q_side(rt) & jnp.logical_not(q_side(ct))
        levels.append((blk, jnp.where(q_side(rowi), 1.0, 0.0), jnp.where(pair, 1.0, 0.0)))
        blk = half

    def chunk(ci, st):
        c = (n_chunks - 1 - ci) if rev else ci
        sl = pl.ds(pl.multiple_of(c * C, C), C)
        q = q_ref[sl, :]
        k = k_ref[sl, :]
        v = v_ref[sl, :]
        vb = v.astype(BF16)
        b = _chunk_cumsum(lf_ref[sl, :], rev)
        tot = b[0:1, :] if rev else b[C - 1:C, :]

        o = lax.dot_general((q * jnp.exp(b)).astype(BF16), st.astype(BF16), NT_DIMS,
                            preferred_element_type=F32)
        kdec = (k * jnp.exp(tot - b)).astype(BF16)
        st_next = st * jnp.exp(tot) + lax.dot_general(vb, kdec, TN_DIMS,
                                                      preferred_element_type=F32)

        scores = eye * jnp.sum(q * k, axis=1, keepdims=True)
        for blk, q_side, pair in levels:
            w = jnp.exp2(jnp.abs(b - _pivot_rows(b, blk, rev)) * (-LOG2E))
            z = (jnp.where(q_side > 0.5, q, k) * w).astype(BF16)
            g = lax.dot_general(z, z, NT_DIMS, preferred_element_type=F32)
            scores = scores + pair * g
        o_ref[sl, :] = o + jnp.dot(scores.astype(BF16), vb, preferred_element_type=F32)
        return st_next

    st_ref[...] = lax.fori_loop(0, n_chunks, chunk, st_ref[...], unroll=HG_UNROLL)


def _hgrn(q, k, lf, v, rev, rb=1024):
    S = q.shape[0]
    nb = S // rb
    idx = (lambda h, j: (nb - 1 - j, h)) if rev else (lambda h, j: (j, h))
    spec = pl.BlockSpec((rb, HEAD_W), idx)
    return pl.pallas_call(
        functools.partial(_hgrn_kernel, rev=rev, n_chunks=rb // HG_CHUNK),
        grid=(N_HEADS, nb),
        in_specs=[spec] * 4,
        out_specs=spec,
        out_shape=jax.ShapeDtypeStruct((S, HALF_W), F32),
        scratch_shapes=[pltpu.VMEM((HEAD_W, HEAD_W), F32)],
        compiler_params=pltpu.CompilerParams(dimension_semantics=("arbitrary", "arbitrary"),
                                             vmem_limit_bytes=V7X_VMEM_LIMIT),
        name="hgrn_bwd" if rev else "hgrn_fwd",
    )(q, k, lf, v)


def _attn_kernel(lam_ref, q_ref, k_ref, vt_ref, gain_ref, o_ref, sa_ref, sb_ref, m_ref, acc_ref,
                 *, tk):
    n_kv = k_ref.shape[0] // tk
    assert n_kv % 2 == 0
    q = q_ref[...]
    lane = lax.broadcasted_iota(jnp.int32, q.shape, 1)
    qm = (jnp.where(lane < QK_DIM, q, jnp.zeros_like(q)),
          jnp.where(lane >= QK_DIM, q, jnp.zeros_like(q)))
    m_ref[...] = jnp.full(m_ref.shape, -jnp.inf, F32)
    acc_ref[...] = jnp.zeros_like(acc_ref)

    def chunk(j):
        return pl.ds(pl.multiple_of(j * tk, tk), tk)

    def scores(j, s_ref):
        kc = k_ref[chunk(j), :]
        for c in range(2):
            s_ref[c] = lax.dot_general(kc, qm[c], NT_DIMS, preferred_element_type=F32)

    def softmax_pv(j, s_ref):
        vt = vt_ref[:, chunk(j)]
        for c in range(2):
            s = s_ref[c]
            m_old = m_ref[c]
            m_new = jnp.maximum(m_old, jnp.max(s, axis=0, keepdims=True))
            p = jnp.exp2(s - m_new).astype(BF16)
            acc_ref[c] = jnp.exp2(m_old - m_new) * acc_ref[c] + jnp.dot(
                vt, p, preferred_element_type=F32)
            m_ref[c] = m_new

    scores(0, sa_ref)

    def two_steps(i, carry):
        scores(2 * i + 1, sb_ref)
        softmax_pv(2 * i, sa_ref)
        scores(2 * i + 2, sa_ref)
        softmax_pv(2 * i + 1, sb_ref)
        return carry

    lax.fori_loop(0, n_kv // 2 - 1, two_steps, 0)
    scores(n_kv - 1, sb_ref)
    softmax_pv(n_kv - 2, sa_ref)
    softmax_pv(n_kv - 1, sb_ref)

    outs = []
    for c in range(2):
        a = acc_ref[c]
        outs.append(a[:HEAD_W, :] / a[HEAD_W:HEAD_W + 1, :])
    o = (outs[0] - lam_ref[0] * outs[1]).T
    o_ref[...] = _rms(o, SUBLN_EPS) * gain_ref[...] * (1.0 - LAMBDA_INIT)


def _attn(lam, dq, dk, dvt, gain, tq=1024, tk=1024):
    S = dq.shape[0]
    return pl.pallas_call(
        functools.partial(_attn_kernel, tk=tk),
        grid=(N_HEADS, S // tq),
        in_specs=[pl.BlockSpec(memory_space=pltpu.SMEM),
                  pl.BlockSpec((tq, HEAD_W), lambda h, i: (i, h)),
                  pl.BlockSpec((S, HEAD_W), lambda h, i: (0, h)),
                  pl.BlockSpec((VT_ROWS, S), lambda h, i: (h, 0)),
                  pl.BlockSpec((1, HEAD_W), lambda h, i: (0, 0))],
        out_specs=pl.BlockSpec((tq, HEAD_W), lambda h, i: (i, h)),
        out_shape=jax.ShapeDtypeStruct((S, HALF_W), F32),
        scratch_shapes=[pltpu.VMEM((2, tk, tq), F32), pltpu.VMEM((2, tk, tq), F32),
                        pltpu.VMEM((2, 1, tq), F32), pltpu.VMEM((2, VT_ROWS, tq), F32)],
        compiler_params=pltpu.CompilerParams(dimension_semantics=("arbitrary", "arbitrary"),
                                             vmem_limit_bytes=V7X_VMEM_LIMIT),
        name="diff_attn",
    )(lam, dq, dk, dvt, gain)


def _outproj_kernel(x_ref, of_ref, ob_ref, hg_ref, oda_ref, hgain_ref, wo_ref, g2_ref,
                    wr_ref, br_ref, x2_ref, rank_ref, rank_t_ref, gates_t_ref, cnt_ref,
                    carry_ref, *, sub_blocks):
    i = pl.program_id(0)
    o = of_ref[...] + ob_ref[...]
    gate = hg_ref[...]
    gate = gate * jax.nn.sigmoid(gate)
    parts = []
    for hd in range(N_HEADS):
        sl = slice(hd * HEAD_W, (hd + 1) * HEAD_W)
        parts.append(_rms(o[:, sl], NORM_EPS) * hgain_ref[...] * gate[:, sl])
    parts.append(oda_ref[...])
    mixed = jnp.concatenate(parts, axis=1).astype(BF16)
    x2 = x_ref[...] + jnp.dot(mixed, wo_ref[...], preferred_element_type=F32)
    x2_ref[...] = x2

    h2 = _rms(x2, NORM_EPS) * g2_ref[...]
    logits = jnp.dot(h2, wr_ref[...], preferred_element_type=F32,
                     precision=lax.Precision.HIGHEST) + br_ref[...]
    lane = lax.broadcasted_iota(jnp.int32, logits.shape, 1)
    neg = -jnp.inf
    big = ROUTER_LANES

    def first_max(vals):
        mx = jnp.max(vals, axis=1, keepdims=True)
        idx = jnp.min(jnp.where(vals == mx, lane, big), axis=1, keepdims=True)
        return mx, idx

    gl = jnp.where((lane >= N_EXPERTS) & (lane < N_EXPERTS + N_GROUPS), logits, neg)
    gm, gidx = first_max(gl)
    g_p = 1.0 / jnp.sum(jnp.exp(gl - gm), axis=1, keepdims=True)
    el = jnp.where((lane < N_EXPERTS) & ((lane // EXPERTS_PER_GROUP) == (gidx - N_EXPERTS)),
                   logits, neg)
    em1, idx1 = first_max(el)
    em2, idx2 = first_max(jnp.where(lane == idx1, neg, el))
    e2 = jnp.exp(em2 - em1)
    w1 = g_p / (1.0 + e2)
    w2 = g_p * e2 / (1.0 + e2)
    is1 = lane == idx1
    is2 = lane == idx2
    gates = jnp.where(is1, w1, 0.0) + jnp.where(is2, w2, 0.0)

    @pl.when(i % sub_blocks == 0)
    def _():
        carry_ref[...] = jnp.zeros_like(carry_ref)

    assigned = jnp.where(is1 | is2, 1.0, 0.0)
    tm = assigned.shape[0]
    tri = jnp.where(lax.broadcasted_iota(jnp.int32, (tm, tm), 0)
                    >= lax.broadcasted_iota(jnp.int32, (tm, tm), 1), 1.0, 0.0).astype(BF16)
    cum = jnp.dot(tri, assigned.astype(BF16), preferred_element_type=F32)
    carry = carry_ref[...]
    rank = jnp.where(assigned > 0.0, carry + cum - 1.0, -1.0)
    carry = carry + cum[tm - 1:tm, :]
    carry_ref[...] = carry
    cnt_ref[pl.ds(i // sub_blocks, 1), :] = carry.astype(jnp.int32)
    rank_ref[...] = rank
    rank_t_ref[...] = rank.T
    gates_t_ref[...] = gates.T


def _outproj(x, o_f, o_b, hg, o_da, hgain, w_out, g2, wr, br, tb, tm=512):
    S = x.shape[0]
    row = lambda w: pl.BlockSpec((tm, w), lambda i: (i, 0))
    col = pl.BlockSpec((ROUTER_LANES, tm), lambda i: (0, i))
    full = lambda a: pl.BlockSpec(a.shape, lambda i: (0, 0))
    return pl.pallas_call(
        functools.partial(_outproj_kernel, sub_blocks=tb // tm),
        grid=(S // tm,),
        in_specs=[row(D_MODEL), row(HALF_W), row(HALF_W), row(HALF_W), row(HALF_W),
                  full(hgain), full(w_out), full(g2), full(wr), full(br)],
        out_specs=[row(D_MODEL), row(ROUTER_LANES), col, col,
                   pl.BlockSpec((S // tb, ROUTER_LANES), lambda i: (0, 0))],
        out_shape=[jax.ShapeDtypeStruct((S, D_MODEL), F32),
                   jax.ShapeDtypeStruct((S, ROUTER_LANES), F32),
                   jax.ShapeDtypeStruct((ROUTER_LANES, S), F32),
                   jax.ShapeDtypeStruct((ROUTER_LANES, S), F32),
                   jax.ShapeDtypeStruct((S // tb, ROUTER_LANES), jnp.int32)],
        scratch_shapes=[pltpu.VMEM((1, ROUTER_LANES), F32)],
        compiler_params=pltpu.CompilerParams(dimension_semantics=("arbitrary",),
                                             vmem_limit_bytes=V7X_VMEM_LIMIT),
        name="outproj_router",
    )(x, o_f, o_b, hg, o_da, hgain, w_out, g2, wr, br)


def _moe_kernel(cnt_ref, x2_ref, rank_ref, rank_t_ref, gates_t_ref, g2_ref, wg_ref, wu_ref,
                wd_ref, gf_ref, out_ref, h2_ref, acc_ref, *, tile_rows):
    b = pl.program_id(0)
    p = pl.program_id(1)
    R = tile_rows
    n_exp = wg_ref.shape[0]

    @pl.when(p == 0)
    def _():
        h2_ref[...] = (_rms(x2_ref[...], NORM_EPS) * g2_ref[...]).astype(BF16)
        acc_ref[...] = jnp.zeros_like(acc_ref)

    rank_blk = rank_ref[...]
    lane = lax.broadcasted_iota(jnp.int32, rank_blk.shape, 1)
    experts = [n_exp * p + i for i in range(n_exp)]
    rank_col = [jnp.sum(jnp.where(lane == e, rank_blk, 0.0), axis=1, keepdims=True)
                for e in experts]
    rank_row = [rank_t_ref[pl.ds(e, 1), :] for e in experts]
    gate_row = [gates_t_ref[pl.ds(e, 1), :] for e in experts]
    n_max = cnt_ref[b, experts[0]]
    for e in experts[1:]:
        n_max = jnp.maximum(n_max, cnt_ref[b, e])

    def tile(t, carry):
        base = (t * R).astype(F32)
        slot_col = lax.broadcasted_iota(jnp.int32, (R, 1), 0).astype(F32) + base
        slot_row = lax.broadcasted_iota(jnp.int32, (1, R), 1).astype(F32) + base
        sel = [rank_row[i] == slot_col for i in range(n_exp)]
        gather = jnp.concatenate([jnp.where(s, 1.0, 0.0).astype(BF16) for s in sel], axis=0)
        xg = jnp.dot(gather, h2_ref[...], preferred_element_type=F32).astype(BF16)
        ys = []
        for i in range(n_exp):
            xi = xg[i * R:(i + 1) * R]
            a = jnp.dot(xi, wg_ref[i], preferred_element_type=F32)
            a = a * jax.nn.sigmoid(a) * jnp.dot(xi, wu_ref[i], preferred_element_type=F32)
            y = jnp.dot(a.astype(BF16), wd_ref[i], preferred_element_type=F32)
            g = jnp.sum(jnp.where(sel[i], gate_row[i], 0.0), axis=1, keepdims=True)
            ys.append((y * g).astype(BF16))
        scatter = jnp.concatenate(
            [jnp.where(rank_col[i] == slot_row, 1.0, 0.0).astype(BF16) for i in range(n_exp)],
            axis=1)
        acc_ref[...] += jnp.dot(scatter, jnp.concatenate(ys, axis=0),
                                preferred_element_type=F32)
        return carry

    lax.fori_loop(0, (n_max + R - 1) // R, tile, 0)

    @pl.when(p == pl.num_programs(1) - 1)
    def _():
        out_ref[...] = _rms(x2_ref[...] + acc_ref[...], NORM_EPS) * gf_ref[...]


def _moe(cnt, x2, rank, rank_t, gates_t, g2, wg, wu, wd, gf, tb, tile_rows=128,
         experts_per_step=2):
    S = x2.shape[0]
    eps = experts_per_step
    grid_spec = pltpu.PrefetchScalarGridSpec(
        num_scalar_prefetch=1,
        grid=(S // tb, N_EXPERTS // eps),
        in_specs=[pl.BlockSpec((tb, D_MODEL), lambda b, p, c: (b, 0)),
                  pl.BlockSpec((tb, ROUTER_LANES), lambda b, p, c: (b, 0)),
                  pl.BlockSpec((N_EXPERTS, tb), lambda b, p, c: (0, b)),
                  pl.BlockSpec((N_EXPERTS, tb), lambda b, p, c: (0, b)),
                  pl.BlockSpec((1, D_MODEL), lambda b, p, c: (0, 0)),
                  pl.BlockSpec((eps, D_MODEL, D_EXPERT), lambda b, p, c: (p, 0, 0)),
                  pl.BlockSpec((eps, D_MODEL, D_EXPERT), lambda b, p, c: (p, 0, 0)),
                  pl.BlockSpec((eps, D_EXPERT, D_MODEL), lambda b, p, c: (p, 0, 0)),
                  pl.BlockSpec((1, D_MODEL), lambda b, p, c: (0, 0))],
        out_specs=pl.BlockSpec((tb, D_MODEL), lambda b, p, c: (b, 0)),
        scratch_shapes=[pltpu.VMEM((tb, D_MODEL), BF16), pltpu.VMEM((tb, D_MODEL), F32)])
    return pl.pallas_call(
        functools.partial(_moe_kernel, tile_rows=tile_rows),
        grid_spec=grid_spec,
        out_shape=jax.ShapeDtypeStruct((S, D_MODEL), F32),
        compiler_params=pltpu.CompilerParams(dimension_semantics=("arbitrary", "arbitrary"),
                                             vmem_limit_bytes=V7X_VMEM_LIMIT),
        name="moe",
    )(cnt, x2, rank, rank_t, gates_t, g2, wg, wu, wd, gf)


def kernel(x, positions, norm1_gain, w_in, hg_lower_bounds, hg_norm_gain, diff_lambda,
           diff_subln_gain, w_out, norm2_gain, router_group_w, router_group_b,
           router_expert_w, router_expert_b, moe_w_gate, moe_w_up, moe_w_down,
           final_norm_gain):
    B, S, _ = x.shape
    assert B == 1 and norm1_gain.shape[0] == 1
    layer = 0
    xs = x.reshape(S, D_MODEL)
    pos = positions.reshape(S, 1).astype(F32)

    lb_cum = jnp.cumsum(jax.nn.softmax(hg_lower_bounds.astype(F32), axis=1), axis=1)
    lb = lb_cum[:, layer + 1] - lb_cum[:, 0]
    lam_p = diff_lambda[layer].astype(F32)
    lam = (jnp.exp(jnp.sum(lam_p[0] * lam_p[1])) - jnp.exp(jnp.sum(lam_p[2] * lam_p[3]))
           + LAMBDA_INIT).reshape(1)
    lane = jnp.arange(HEAD_W)
    inv_freq = jnp.float32(ROPE_THETA) ** (-jnp.arange(0, ROT_DIM, 2, dtype=F32) / ROT_DIM)
    in_rot = (lane % QK_DIM) < ROT_DIM
    invf = jnp.where(in_rot, inv_freq[lane % (ROT_DIM // 2)], 0.0).reshape(1, HEAD_W)
    sgn = jnp.where(in_rot, jnp.where((lane % QK_DIM) < ROT_DIM // 2, -1.0, 1.0),
                    0.0).astype(F32).reshape(1, HEAD_W)
    wr = jnp.zeros((D_MODEL, ROUTER_LANES), F32)
    wr = wr.at[:, :N_EXPERTS].set(router_expert_w[layer])
    wr = wr.at[:, N_EXPERTS:N_EXPERTS + N_GROUPS].set(router_group_w[layer])
    br = jnp.zeros((1, ROUTER_LANES), F32)
    br = br.at[0, :N_EXPERTS].set(router_expert_b[layer])
    br = br.at[0, N_EXPERTS:N_EXPERTS + N_GROUPS].set(router_group_b[layer])

    hq, kf, kb, lff, lfb, hi, hg, dq, dk, dv = _inproj(
        xs, pos, norm1_gain[layer].reshape(1, -1), w_in[layer].astype(BF16), lb, invf, sgn)
    o_f = _hgrn(hq, kf, lff, hi, rev=False)
    o_b = _hgrn(hq, kb, lfb, hi, rev=True)
    o_da = _attn(lam, dq, dk, dv, diff_subln_gain[layer].reshape(1, -1))
    x2, rank, rank_t, gates_t, cnt = _outproj(
        xs, o_f, o_b, hg, o_da, hg_norm_gain[layer].reshape(1, -1), w_out[layer].astype(BF16),
        norm2_gain[layer].reshape(1, -1), wr, br, tb=MOE_TOKEN_BLOCK)
    out = _moe(cnt, x2, rank, rank_t, gates_t, norm2_gain[layer].reshape(1, -1),
               moe_w_gate[layer].astype(BF16), moe_w_up[layer].astype(BF16),
               moe_w_down[layer].astype(BF16), final_norm_gain.reshape(1, -1),
               tb=MOE_TOKEN_BLOCK)
    return out.reshape(B, S, D_MODEL)
```

```python
import functools
import math

import jax
import jax.numpy as jnp
from jax import lax
from jax.experimental import pallas as pl
from jax.experimental.pallas import tpu as pltpu

F32 = jnp.float32
BF16 = jnp.bfloat16

D_MODEL = 1024
HEAD_W = 128
N_HEADS = 4
HALF_W = N_HEADS * HEAD_W
QK_DIM = 64
ROT_DIM = 16
ROPE_THETA = 500000.0
QUERY_GROUP = 256
VT_ROWS = HEAD_W + 16
HG_CHUNK = 128
HG_UNROLL = 8
N_GROUPS = 4
EXPERTS_PER_GROUP = 8
N_EXPERTS = 32
D_EXPERT = 512
NORM_EPS = 1e-6
SUBLN_EPS = 1e-5
LAMBDA_INIT = 0.8 - 0.6 * math.exp(-0.3 * 0)
LOG2E = 1.4426950408889634
ROUTER_LANES = 128
MOE_TOKEN_BLOCK = 1024
V7X_VMEM_LIMIT = 56 * 1024 * 1024

NT_DIMS = (((1,), (1,)), ((), ()))
TN_DIMS = (((0,), (0,)), ((), ()))


def _rms(x, eps):
    return x * lax.rsqrt(jnp.mean(x * x, axis=-1, keepdims=True) + eps)


def _inproj_kernel(x_ref, pos_ref, g1_ref, w_ref, lb_ref, invf_ref, sgn_ref,
                   hq_ref, kf_ref, kb_ref, lff_ref, lfb_ref, hi_ref, hg_ref,
                   dq_ref, dk_ref, dv_ref):
    x = x_ref[...]
    h = (_rms(x, NORM_EPS) * g1_ref[...]).astype(BF16)

    def proj(i):
        return jnp.dot(h, w_ref[:, i * HALF_W:(i + 1) * HALF_W], preferred_element_type=F32)

    hq_ref[...] = proj(0)
    for i, (k_ref, lf_ref) in enumerate(((kf_ref, lff_ref), (kb_ref, lfb_ref))):
        z = proj(1 + i)
        lb = lb_ref[i:i + 1, :]
        sg = jax.nn.sigmoid(z)
        lf_ref[...] = jnp.log(lb + (1.0 - lb) * sg)
        k_ref[...] = (1.0 - lb) * (1.0 - sg)
    hi_ref[...] = proj(3)
    hg_ref[...] = proj(4)

    ang = pos_ref[...] * invf_ref[...]
    cos = jnp.concatenate([jnp.cos(ang)] * N_HEADS, axis=1)
    sgn = sgn_ref[...]
    sin = jnp.concatenate([jnp.sin(ang) * sgn] * N_HEADS, axis=1)
    take_hi = jnp.concatenate([jnp.broadcast_to(sgn, ang.shape)] * N_HEADS, axis=1) < 0.0

    def rotary(t):
        partner = jnp.where(take_hi, pltpu.roll(t, HALF_W - ROT_DIM // 2, axis=1),
                            pltpu.roll(t, ROT_DIM // 2, axis=1))
        return t * cos + partner * sin

    dq_ref[...] = (rotary(proj(5)) * (QK_DIM ** -0.5 * LOG2E)).astype(BF16)
    dk_ref[...] = rotary(proj(6)).astype(BF16)
    v = proj(7)
    sub = lax.broadcasted_iota(jnp.int32, (VT_ROWS - HEAD_W, v.shape[0]), 0)
    ones_rows = jnp.where(sub == 0, 1.0, 0.0)
    pieces = []
    for hd in range(N_HEADS):
        pieces += [v[:, hd * HEAD_W:(hd + 1) * HEAD_W].T, ones_rows]
    dv_ref[...] = jnp.concatenate(pieces, axis=0).astype(BF16)


def _inproj(x, pos, g1, w_in, lb, invf, sgn, tm=512):
    S = x.shape[0]
    row = lambda w: pl.BlockSpec((tm, w), lambda i: (i, 0))
    full = lambda a: pl.BlockSpec(a.shape, lambda i: (0, 0))
    f32o = jax.ShapeDtypeStruct((S, HALF_W), F32)
    bf16o = jax.ShapeDtypeStruct((S, HALF_W), BF16)
    return pl.pallas_call(
        _inproj_kernel,
        grid=(S // tm,),
        in_specs=[row(D_MODEL), row(1), full(g1), full(w_in), full(lb), full(invf), full(sgn)],
        out_specs=[row(HALF_W)] * 9 + [pl.BlockSpec((N_HEADS * VT_ROWS, tm), lambda i: (0, i))],
        out_shape=[f32o] * 7 + [bf16o, bf16o,
                                jax.ShapeDtypeStruct((N_HEADS * VT_ROWS, S), BF16)],
        compiler_params=pltpu.CompilerParams(dimension_semantics=("arbitrary",),
                                             vmem_limit_bytes=V7X_VMEM_LIMIT),
        name="inproj",
    )(x, pos, g1, w_in, lb, invf, sgn)


def _chunk_cumsum(x, rev):
    n = x.shape[0]
    row = lax.broadcasted_iota(jnp.int32, x.shape, 0)
    d = 1
    while d < n:
        if rev:
            x = x + jnp.where(row < n - d, pltpu.roll(x, n - d, axis=0), 0.0)
        else:
            x = x + jnp.where(row >= d, pltpu.roll(x, d, axis=0), 0.0)
        d *= 2
    return x


def _pivot_rows(b, blk, rev):
    n = b.shape[0]
    half = blk // 2
    groups = []
    for g0 in range(0, n, 8):
        def piv(r):
            base = (r // blk) * blk
            return base + half if rev else base + half - 1
        if blk >= 16 or blk == 8:
            p = piv(g0)
            groups.append(jnp.broadcast_to(b[p:p + 1, :], (8, b.shape[1])))
        else:
            sub = lax.broadcasted_iota(jnp.int32, (8, b.shape[1]), 0)
            acc = None
            for s0 in range(0, 8, blk):
                p = piv(g0 + s0)
                rowv = jnp.broadcast_to(b[p:p + 1, :], (8, b.shape[1]))
                acc = rowv if acc is None else jnp.where(sub >= s0, rowv, acc)
            groups.append(acc)
    return jnp.concatenate(groups, axis=0)


def _hgrn_kernel(q_ref, k_ref, lf_ref, v_ref, o_ref, st_ref, *, rev, n_chunks):
    @pl.when(pl.program_id(1) == 0)
    def _():
        st_ref[...] = jnp.zeros_like(st_ref)

    C = HG_CHUNK
    rt = lax.broadcasted_iota(jnp.int32, (C, C), 0)
    ct = lax.broadcasted_iota(jnp.int32, (C, C), 1)
    rowi = lax.broadcasted_iota(jnp.int32, (C, HEAD_W), 0)
    eye = jnp.where(rt == ct, 1.0, 0.0)
    levels = []
    blk = C
    while blk >= 2:
        half = blk // 2
        q_side = (lambda r: (r % blk) < half) if rev else (lambda r: (r % blk) >= half)
        pair = ((rt // blk) == (ct // blk)) & q_side(rt) & jnp.logical_not(q_side(ct))
        levels.append((blk, jnp.where(q_side(rowi), 1.0, 0.0), jnp.where(pair, 1.0, 0.0)))
        blk = half

    def chunk(ci, st):
        c = (n_chunks - 1 - ci) if rev else ci
        sl = pl.ds(pl.multiple_of(c * C, C), C)
        q = q_ref[sl, :]
        k = k_ref[sl, :]
        v = v_ref[sl, :]
        vb = v.astype(BF16)
        b = _chunk_cumsum(lf_ref[sl, :], rev)
        tot = b[0:1, :] if rev else b[C - 1:C, :]

        o = lax.dot_general((q * jnp.exp(b)).astype(BF16), st.astype(BF16), NT_DIMS,
                            preferred_element_type=F32)
        kdec = (k * jnp.exp(tot - b)).astype(BF16)
        st_next = st * jnp.exp(tot) + lax.dot_general(vb, kdec, TN_DIMS,
                                                      preferred_element_type=F32)

        scores = eye * jnp.sum(q * k, axis=1, keepdims=True)
        for blk, q_side, pair in levels:
            w = jnp.exp2(jnp.abs(b - _pivot_rows(b, blk, rev)) * (-LOG2E))
            z = (jnp.where(q_side > 0.5, q, k) * w).astype(BF16)
            g = lax.dot_general(z, z, NT_DIMS, preferred_element_type=F32)
            scores = scores + pair * g
        o_ref[sl, :] = o + jnp.dot(scores.astype(BF16), vb, preferred_element_type=F32)
        return st_next

    st_ref[...] = lax.fori_loop(0, n_chunks, chunk, st_ref[...], unroll=HG_UNROLL)


def _hgrn(q, k, lf, v, rev, rb=1024):
    S = q.shape[0]
    nb = S // rb
    idx = (lambda h, j: (nb - 1 - j, h)) if rev else (lambda h, j: (j, h))
    spec = pl.BlockSpec((rb, HEAD_W), idx)
    return pl.pallas_call(
        functools.partial(_hgrn_kernel, rev=rev, n_chunks=rb // HG_CHUNK),
        grid=(N_HEADS, nb),
        in_specs=[spec] * 4,
        out_specs=spec,
        out_shape=jax.ShapeDtypeStruct((S, HALF_W), F32),
        scratch_shapes=[pltpu.VMEM((HEAD_W, HEAD_W), F32)],
        compiler_params=pltpu.CompilerParams(dimension_semantics=("arbitrary", "arbitrary"),
                                             vmem_limit_bytes=V7X_VMEM_LIMIT),
        name="hgrn_bwd" if rev else "hgrn_fwd",
    )(q, k, lf, v)


def _attn_kernel(lam_ref, q_ref, k_ref, vt_ref, gain_ref, o_ref, sa_ref, sb_ref, m_ref, acc_ref,
                 *, tk):
    n_kv = k_ref.shape[0] // tk
    tq = q_ref.shape[0]
    assert n_kv % 2 == 0 and tq % QUERY_GROUP == 0
    q = q_ref[...]
    lane = lax.broadcasted_iota(jnp.int32, q.shape, 1)
    qm = (jnp.where(lane < QK_DIM, q, jnp.zeros_like(q)),
          jnp.where(lane >= QK_DIM, q, jnp.zeros_like(q)))
    m_ref[...] = jnp.full(m_ref.shape, -jnp.inf, F32)
    acc_ref[...] = jnp.zeros_like(acc_ref)

    def chunk(j):
        return pl.ds(pl.multiple_of(j * tk, tk), tk)

    def scores(j, s_ref):
        kc = k_ref[chunk(j), :]
        for c in range(2):
            s_ref[c] = lax.dot_general(kc, qm[c], NT_DIMS, preferred_element_type=F32)

    def softmax_pv(j, s_ref):
        vt = vt_ref[:, chunk(j)]
        for c in range(2):
            for g in range(0, tq, QUERY_GROUP):
                cols = slice(g, g + QUERY_GROUP)
                s = s_ref[c, :, cols]
                m_old = m_ref[c, :, cols]
                m_new = jnp.maximum(m_old, jnp.max(s, axis=0, keepdims=True))
                p = jnp.exp2(s - m_new).astype(BF16)
                acc_ref[c, :, cols] = jnp.exp2(m_old - m_new) * acc_ref[c, :, cols] + jnp.dot(
                    vt, p, preferred_element_type=F32)
                m_ref[c, :, cols] = m_new

    scores(0, sa_ref)

    def two_steps(i, carry):
        scores(2 * i + 1, sb_ref)
        softmax_pv(2 * i, sa_ref)
        scores(2 * i + 2, sa_ref)
        softmax_pv(2 * i + 1, sb_ref)
        return carry

    lax.fori_loop(0, n_kv // 2 - 1, two_steps, 0)
    scores(n_kv - 1, sb_ref)
    softmax_pv(n_kv - 2, sa_ref)
    softmax_pv(n_kv - 1, sb_ref)

    outs = []
    for c in range(2):
        a = acc_ref[c]
        outs.append(a[:HEAD_W, :] / a[HEAD_W:HEAD_W + 1, :])
    o = (outs[0] - lam_ref[0] * outs[1]).T
    o_ref[...] = _rms(o, SUBLN_EPS) * gain_ref[...] * (1.0 - LAMBDA_INIT)


def _attn(lam, dq, dk, dvt, gain, tq=2048, tk=512):
    S = dq.shape[0]
    return pl.pallas_call(
        functools.partial(_attn_kernel, tk=tk),
        grid=(N_HEADS, S // tq),
        in_specs=[pl.BlockSpec(memory_space=pltpu.SMEM),
                  pl.BlockSpec((tq, HEAD_W), lambda h, i: (i, h)),
                  pl.BlockSpec((S, HEAD_W), lambda h, i: (0, h)),
                  pl.BlockSpec((VT_ROWS, S), lambda h, i: (h, 0)),
                  pl.BlockSpec((1, HEAD_W), lambda h, i: (0, 0))],
        out_specs=pl.BlockSpec((tq, HEAD_W), lambda h, i: (i, h)),
        out_shape=jax.ShapeDtypeStruct((S, HALF_W), F32),
        scratch_shapes=[pltpu.VMEM((2, tk, tq), F32), pltpu.VMEM((2, tk, tq), F32),
                        pltpu.VMEM((2, 1, tq), F32), pltpu.VMEM((2, VT_ROWS, tq), F32)],
        compiler_params=pltpu.CompilerParams(dimension_semantics=("arbitrary", "arbitrary"),
                                             vmem_limit_bytes=V7X_VMEM_LIMIT),
        name="diff_attn",
    )(lam, dq, dk, dvt, gain)


def _outproj_kernel(x_ref, of_ref, ob_ref, hg_ref, oda_ref, hgain_ref, wo_ref, g2_ref,
                    wr_ref, br_ref, x2_ref, rank_ref, rank_t_ref, gates_t_ref, cnt_ref,
                    carry_ref, *, sub_blocks):
    i = pl.program_id(0)
    o = of_ref[...] + ob_ref[...]
    gate = hg_ref[...]
    gate = gate * jax.nn.sigmoid(gate)
    parts = []
    for hd in range(N_HEADS):
        sl = slice(hd * HEAD_W, (hd + 1) * HEAD_W)
        parts.append(_rms(o[:, sl], NORM_EPS) * hgain_ref[...] * gate[:, sl])
    parts.append(oda_ref[...])
    mixed = jnp.concatenate(parts, axis=1).astype(BF16)
    x2 = x_ref[...] + jnp.dot(mixed, wo_ref[...], preferred_element_type=F32)
    x2_ref[...] = x2

    h2 = _rms(x2, NORM_EPS) * g2_ref[...]
    h_hi = h2.astype(BF16)
    h_lo = (h2 - h_hi.astype(F32)).astype(BF16)
    t = jnp.dot(h_hi, wr_ref[...], preferred_element_type=F32)
    logits = (t[:, :ROUTER_LANES] + t[:, ROUTER_LANES:] + br_ref[...]
              + jnp.dot(h_lo, wr_ref[:, :ROUTER_LANES], preferred_element_type=F32))
    lane = lax.broadcasted_iota(jnp.int32, logits.shape, 1)
    neg = -jnp.inf
    big = ROUTER_LANES

    def first_max(vals):
        mx = jnp.max(vals, axis=1, keepdims=True)
        idx = jnp.min(jnp.where(vals == mx, lane, big), axis=1, keepdims=True)
        return mx, idx

    gl = jnp.where((lane >= N_EXPERTS) & (lane < N_EXPERTS + N_GROUPS), logits, neg)
    gm, gidx = first_max(gl)
    g_p = 1.0 / jnp.sum(jnp.exp(gl - gm), axis=1, keepdims=True)
    el = jnp.where((lane < N_EXPERTS) & ((lane // EXPERTS_PER_GROUP) == (gidx - N_EXPERTS)),
                   logits, neg)
    em1, idx1 = first_max(el)
    em2, idx2 = first_max(jnp.where(lane == idx1, neg, el))
    e2 = jnp.exp(em2 - em1)
    w1 = g_p / (1.0 + e2)
    w2 = g_p * e2 / (1.0 + e2)
    is1 = lane == idx1
    is2 = lane == idx2
    gates = jnp.where(is1, w1, 0.0) + jnp.where(is2, w2, 0.0)

    @pl.when(i % sub_blocks == 0)
    def _():
        carry_ref[...] = jnp.zeros_like(carry_ref)

    assigned = jnp.where(is1 | is2, 1.0, 0.0)
    tm = assigned.shape[0]
    tri = jnp.where(lax.broadcasted_iota(jnp.int32, (tm, tm), 0)
                    >= lax.broadcasted_iota(jnp.int32, (tm, tm), 1), 1.0, 0.0).astype(BF16)
    cum = jnp.dot(tri, assigned.astype(BF16), preferred_element_type=F32)
    carry = carry_ref[...]
    rank = jnp.where(assigned > 0.0, carry + cum - 1.0, -1.0)
    carry = carry + cum[tm - 1:tm, :]
    carry_ref[...] = carry
    cnt_ref[pl.ds(i // sub_blocks, 1), :] = carry.astype(jnp.int32)
    rank_ref[...] = rank
    rank_t_ref[...] = rank.T
    gates_t_ref[...] = gates.T


def _outproj(x, o_f, o_b, hg, o_da, hgain, w_out, g2, wr, br, tb, tm=512):
    S = x.shape[0]
    row = lambda w: pl.BlockSpec((tm, w), lambda i: (i, 0))
    col = pl.BlockSpec((ROUTER_LANES, tm), lambda i: (0, i))
    full = lambda a: pl.BlockSpec(a.shape, lambda i: (0, 0))
    return pl.pallas_call(
        functools.partial(_outproj_kernel, sub_blocks=tb // tm),
        grid=(S // tm,),
        in_specs=[row(D_MODEL), row(HALF_W), row(HALF_W), row(HALF_W), row(HALF_W),
                  full(hgain), full(w_out), full(g2), full(wr), full(br)],
        out_specs=[row(D_MODEL), row(ROUTER_LANES), col, col,
                   pl.BlockSpec((S // tb, ROUTER_LANES), lambda i: (0, 0))],
        out_shape=[jax.ShapeDtypeStruct((S, D_MODEL), F32),
                   jax.ShapeDtypeStruct((S, ROUTER_LANES), F32),
                   jax.ShapeDtypeStruct((ROUTER_LANES, S), F32),
                   jax.ShapeDtypeStruct((ROUTER_LANES, S), F32),
                   jax.ShapeDtypeStruct((S // tb, ROUTER_LANES), jnp.int32)],
        scratch_shapes=[pltpu.VMEM((1, ROUTER_LANES), F32)],
        compiler_params=pltpu.CompilerParams(dimension_semantics=("arbitrary",),
                                             vmem_limit_bytes=V7X_VMEM_LIMIT),
        name="outproj_router",
    )(x, o_f, o_b, hg, o_da, hgain, w_out, g2, wr, br)


def _moe_kernel(cnt_ref, x2_ref, rank_ref, rank_t_ref, gates_t_ref, g2_ref, wg_ref, wu_ref,
                wd_ref, gf_ref, out_ref, h2_ref, acc_ref, *, tile_rows):
    b = pl.program_id(0)
    p = pl.program_id(1)
    R = tile_rows
    n_exp = wg_ref.shape[0]

    @pl.when(p == 0)
    def _():
        h2_ref[...] = (_rms(x2_ref[...], NORM_EPS) * g2_ref[...]).astype(BF16)
        acc_ref[...] = jnp.zeros_like(acc_ref)

    rank_blk = rank_ref[...]
    lane = lax.broadcasted_iota(jnp.int32, rank_blk.shape, 1)
    experts = [n_exp * p + i for i in range(n_exp)]
    rank_col = [jnp.sum(jnp.where(lane == e, rank_blk, 0.0), axis=1, keepdims=True)
                for e in experts]
    rank_row = [rank_t_ref[pl.ds(e, 1), :] for e in experts]
    gate_row = [gates_t_ref[pl.ds(e, 1), :] for e in experts]
    n_max = cnt_ref[b, experts[0]]
    for e in experts[1:]:
        n_max = jnp.maximum(n_max, cnt_ref[b, e])

    def tile(t, carry):
        base = (t * R).astype(F32)
        slot_col = lax.broadcasted_iota(jnp.int32, (R, 1), 0).astype(F32) + base
        slot_row = lax.broadcasted_iota(jnp.int32, (1, R), 1).astype(F32) + base
        sel = [rank_row[i] == slot_col for i in range(n_exp)]
        gather = jnp.concatenate([jnp.where(s, 1.0, 0.0).astype(BF16) for s in sel], axis=0)
        xg = jnp.dot(gather, h2_ref[...], preferred_element_type=F32).astype(BF16)
        ys = []
        for i in range(n_exp):
            xi = xg[i * R:(i + 1) * R]
            a = jnp.dot(xi, wg_ref[i], preferred_element_type=F32)
            a = a * jax.nn.sigmoid(a) * jnp.dot(xi, wu_ref[i], preferred_element_type=F32)
            y = jnp.dot(a.astype(BF16), wd_ref[i], preferred_element_type=F32)
            g = jnp.sum(jnp.where(sel[i], gate_row[i], 0.0), axis=1, keepdims=True)
            ys.append((y * g).astype(BF16))
        scatter = jnp.concatenate(
            [jnp.where(rank_col[i] == slot_row, 1.0, 0.0).astype(BF16) for i in range(n_exp)],
            axis=1)
        acc_ref[...] += jnp.dot(scatter, jnp.concatenate(ys, axis=0),
                                preferred_element_type=F32)
        return carry

    lax.fori_loop(0, (n_max + R - 1) // R, tile, 0)

    @pl.when(p == pl.num_programs(1) - 1)
    def _():
        out_ref[...] = _rms(x2_ref[...] + acc_ref[...], NORM_EPS) * gf_ref[...]


def _moe(cnt, x2, rank, rank_t, gates_t, g2, wg, wu, wd, gf, tb, tile_rows=128,
         experts_per_step=2):
    S = x2.shape[0]
    eps = experts_per_step
    grid_spec = pltpu.PrefetchScalarGridSpec(
        num_scalar_prefetch=1,
        grid=(S // tb, N_EXPERTS // eps),
        in_specs=[pl.BlockSpec((tb, D_MODEL), lambda b, p, c: (b, 0)),
                  pl.BlockSpec((tb, ROUTER_LANES), lambda b, p, c: (b, 0)),
                  pl.BlockSpec((N_EXPERTS, tb), lambda b, p, c: (0, b)),
                  pl.BlockSpec((N_EXPERTS, tb), lambda b, p, c: (0, b)),
                  pl.BlockSpec((1, D_MODEL), lambda b, p, c: (0, 0)),
                  pl.BlockSpec((eps, D_MODEL, D_EXPERT), lambda b, p, c: (p, 0, 0)),
                  pl.BlockSpec((eps, D_MODEL, D_EXPERT), lambda b, p, c: (p, 0, 0)),
                  pl.BlockSpec((eps, D_EXPERT, D_MODEL), lambda b, p, c: (p, 0, 0)),
                  pl.BlockSpec((1, D_MODEL), lambda b, p, c: (0, 0))],
        out_specs=pl.BlockSpec((tb, D_MODEL), lambda b, p, c: (b, 0)),
        scratch_shapes=[pltpu.VMEM((tb, D_MODEL), BF16), pltpu.VMEM((tb, D_MODEL), F32)])
    return pl.pallas_call(
        functools.partial(_moe_kernel, tile_rows=tile_rows),
        grid_spec=grid_spec,
        out_shape=jax.ShapeDtypeStruct((S, D_MODEL), F32),
        compiler_params=pltpu.CompilerParams(dimension_semantics=("arbitrary", "arbitrary"),
                                             vmem_limit_bytes=V7X_VMEM_LIMIT),
        name="moe",
    )(cnt, x2, rank, rank_t, gates_t, g2, wg, wu, wd, gf)


def kernel(x, positions, norm1_gain, w_in, hg_lower_bounds, hg_norm_gain, diff_lambda,
           diff_subln_gain, w_out, norm2_gain, router_group_w, router_group_b,
           router_expert_w, router_expert_b, moe_w_gate, moe_w_up, moe_w_down,
           final_norm_gain):
    B, S, _ = x.shape
    assert B == 1 and norm1_gain.shape[0] == 1
    layer = 0
    xs = x.reshape(S, D_MODEL)
    pos = positions.reshape(S, 1).astype(F32)

    lb_cum = jnp.cumsum(jax.nn.softmax(hg_lower_bounds.astype(F32), axis=1), axis=1)
    lb = lb_cum[:, layer + 1] - lb_cum[:, 0]
    lam_p = diff_lambda[layer].astype(F32)
    lam = (jnp.exp(jnp.sum(lam_p[0] * lam_p[1])) - jnp.exp(jnp.sum(lam_p[2] * lam_p[3]))
           + LAMBDA_INIT).reshape(1)
    lane = jnp.arange(HEAD_W)
    inv_freq = jnp.float32(ROPE_THETA) ** (-jnp.arange(0, ROT_DIM, 2, dtype=F32) / ROT_DIM)
    in_rot = (lane % QK_DIM) < ROT_DIM
    invf = jnp.where(in_rot, inv_freq[lane % (ROT_DIM // 2)], 0.0).reshape(1, HEAD_W)
    sgn = jnp.where(in_rot, jnp.where((lane % QK_DIM) < ROT_DIM // 2, -1.0, 1.0),
                    0.0).astype(F32).reshape(1, HEAD_W)
    wr = jnp.zeros((D_MODEL, ROUTER_LANES), F32)
    wr = wr.at[:, :N_EXPERTS].set(router_expert_w[layer])
    wr = wr.at[:, N_EXPERTS:N_EXPERTS + N_GROUPS].set(router_group_w[layer])
    wr_hi = wr.astype(BF16)
    wr = jnp.concatenate([wr_hi, (wr - wr_hi.astype(F32)).astype(BF16)], axis=1)
    br = jnp.zeros((1, ROUTER_LANES), F32)
    br = br.at[0, :N_EXPERTS].set(router_expert_b[layer])
    br = br.at[0, N_EXPERTS:N_EXPERTS + N_GROUPS].set(router_group_b[layer])

    hq, kf, kb, lff, lfb, hi, hg, dq, dk, dv = _inproj(
        xs, pos, norm1_gain[layer].reshape(1, -1), w_in[layer].astype(BF16), lb, invf, sgn)
    o_f = _hgrn(hq, kf, lff, hi, rev=False)
    o_b = _hgrn(hq, kb, lfb, hi, rev=True)
    o_da = _attn(lam, dq, dk, dv, diff_subln_gain[layer].reshape(1, -1))
    x2, rank, rank_t, gates_t, cnt = _outproj(
        xs, o_f, o_b, hg, o_da, hg_norm_gain[layer].reshape(1, -1), w_out[layer].astype(BF16),
        norm2_gain[layer].reshape(1, -1), wr, br, tb=MOE_TOKEN_BLOCK)
    out = _moe(cnt, x2, rank, rank_t, gates_t, norm2_gain[layer].reshape(1, -1),
               moe_w_gate[layer].astype(BF16), moe_w_up[layer].astype(BF16),
               moe_w_down[layer].astype(BF16), final_norm_gain.reshape(1, -1),
               tb=MOE_TOKEN_BLOCK)
    return out.reshape(B, S, D_MODEL)
```

```python
import functools
import math

import jax
import jax.numpy as jnp
from jax import lax
from jax.experimental import pallas as pl
from jax.experimental.pallas import tpu as pltpu

F32 = jnp.float32
BF16 = jnp.bfloat16

D_MODEL = 1024
HEAD_W = 128
N_HEADS = 4
HALF_W = N_HEADS * HEAD_W
QK_DIM = 64
ROT_DIM = 16
ROPE_THETA = 500000.0
QUERY_GROUP = 512
VT_ROWS = HEAD_W + 16
HG_CHUNK = 128
HG_UNROLL = 8
N_GROUPS = 4
EXPERTS_PER_GROUP = 8
N_EXPERTS = 32
D_EXPERT = 512
NORM_EPS = 1e-6
SUBLN_EPS = 1e-5
LAMBDA_INIT = 0.8 - 0.6 * math.exp(-0.3 * 0)
LOG2E = 1.4426950408889634
ROUTER_LANES = 128
MOE_TOKEN_BLOCK = 1024
V7X_VMEM_LIMIT = 56 * 1024 * 1024

NT_DIMS = (((1,), (1,)), ((), ()))
TN_DIMS = (((0,), (0,)), ((), ()))


def _rms(x, eps):
    return x * lax.rsqrt(jnp.mean(x * x, axis=-1, keepdims=True) + eps)


def _inproj_kernel(x_ref, pos_ref, g1_ref, w_ref, lb_ref, invf_ref, sgn_ref,
                   hq_ref, kf_ref, kb_ref, lff_ref, lfb_ref, hi_ref, hg_ref,
                   dq_ref, dk_ref, dv_ref):
    x = x_ref[...]
    h = (_rms(x, NORM_EPS) * g1_ref[...]).astype(BF16)

    def proj(i):
        return jnp.dot(h, w_ref[:, i * HALF_W:(i + 1) * HALF_W], preferred_element_type=F32)

    hq_ref[...] = proj(0)
    for i, (k_ref, lf_ref) in enumerate(((kf_ref, lff_ref), (kb_ref, lfb_ref))):
        z = proj(1 + i)
        lb = lb_ref[i:i + 1, :]
        sg = jax.nn.sigmoid(z)
        lf_ref[...] = jnp.log(lb + (1.0 - lb) * sg)
        k_ref[...] = (1.0 - lb) * (1.0 - sg)
    hi_ref[...] = proj(3)
    hg_ref[...] = proj(4)

    ang = pos_ref[...] * invf_ref[...]
    cos = jnp.concatenate([jnp.cos(ang)] * N_HEADS, axis=1)
    sgn = sgn_ref[...]
    sin = jnp.concatenate([jnp.sin(ang) * sgn] * N_HEADS, axis=1)
    take_hi = jnp.concatenate([jnp.broadcast_to(sgn, ang.shape)] * N_HEADS, axis=1) < 0.0

    def rotary(t):
        partner = jnp.where(take_hi, pltpu.roll(t, HALF_W - ROT_DIM // 2, axis=1),
                            pltpu.roll(t, ROT_DIM // 2, axis=1))
        return t * cos + partner * sin

    dq_ref[...] = (rotary(proj(5)) * (QK_DIM ** -0.5 * LOG2E)).astype(BF16)
    dk_ref[...] = rotary(proj(6)).astype(BF16)
    v = proj(7)
    sub = lax.broadcasted_iota(jnp.int32, (VT_ROWS - HEAD_W, v.shape[0]), 0)
    ones_rows = jnp.where(sub == 0, 1.0, 0.0)
    pieces = []
    for hd in range(N_HEADS):
        pieces += [v[:, hd * HEAD_W:(hd + 1) * HEAD_W].T, ones_rows]
    dv_ref[...] = jnp.concatenate(pieces, axis=0).astype(BF16)


def _inproj(x, pos, g1, w_in, lb, invf, sgn, tm=512):
    S = x.shape[0]
    row = lambda w: pl.BlockSpec((tm, w), lambda i: (i, 0))
    full = lambda a: pl.BlockSpec(a.shape, lambda i: (0, 0))
    f32o = jax.ShapeDtypeStruct((S, HALF_W), F32)
    bf16o = jax.ShapeDtypeStruct((S, HALF_W), BF16)
    return pl.pallas_call(
        _inproj_kernel,
        grid=(S // tm,),
        in_specs=[row(D_MODEL), row(1), full(g1), full(w_in), full(lb), full(invf), full(sgn)],
        out_specs=[row(HALF_W)] * 9 + [pl.BlockSpec((N_HEADS * VT_ROWS, tm), lambda i: (0, i))],
        out_shape=[f32o] * 7 + [bf16o, bf16o,
                                jax.ShapeDtypeStruct((N_HEADS * VT_ROWS, S), BF16)],
        compiler_params=pltpu.CompilerParams(dimension_semantics=("arbitrary",),
                                             vmem_limit_bytes=V7X_VMEM_LIMIT),
        name="inproj",
    )(x, pos, g1, w_in, lb, invf, sgn)


def _chunk_cumsum(x, rev):
    n = x.shape[0]
    row = lax.broadcasted_iota(jnp.int32, x.shape, 0)
    d = 1
    while d < n:
        if rev:
            x = x + jnp.where(row < n - d, pltpu.roll(x, n - d, axis=0), 0.0)
        else:
            x = x + jnp.where(row >= d, pltpu.roll(x, d, axis=0), 0.0)
        d *= 2
    return x


def _pivot_rows(b, blk, rev):
    n = b.shape[0]
    half = blk // 2
    groups = []
    for g0 in range(0, n, 8):
        def piv(r):
            base = (r // blk) * blk
            return base + half if rev else base + half - 1
        if blk >= 16 or blk == 8:
            p = piv(g0)
            groups.append(jnp.broadcast_to(b[p:p + 1, :], (8, b.shape[1])))
        else:
            sub = lax.broadcasted_iota(jnp.int32, (8, b.shape[1]), 0)
            acc = None
            for s0 in range(0, 8, blk):
                p = piv(g0 + s0)
                rowv = jnp.broadcast_to(b[p:p + 1, :], (8, b.shape[1]))
                acc = rowv if acc is None else jnp.where(sub >= s0, rowv, acc)
            groups.append(acc)
    return jnp.concatenate(groups, axis=0)


def _hgrn_kernel(q_ref, k_ref, lf_ref, v_ref, o_ref, st_ref, *, rev, n_chunks):
    @pl.when(pl.program_id(1) == 0)
    def _():
        st_ref[...] = jnp.zeros_like(st_ref)

    C = HG_CHUNK
    rt = lax.broadcasted_iota(jnp.int32, (C, C), 0)
    ct = lax.broadcasted_iota(jnp.int32, (C, C), 1)
    rowi = lax.broadcasted_iota(jnp.int32, (C, HEAD_W), 0)
    eye = jnp.where(rt == ct, 1.0, 0.0)
    levels = []
    blk = C
    while blk >= 2:
        half = blk // 2
        q_side = (lambda r: (r % blk) < half) if rev else (lambda r: (r % blk) >= half)
        pair = ((rt // blk) == (ct // blk)) & q_side(rt) & jnp.logical_not(q_side(ct))
        levels.append((blk, jnp.where(q_side(rowi), 1.0, 0.0), jnp.where(pair, 1.0, 0.0)))
        blk = half

    def chunk(ci, st):
        c = (n_chunks - 1 - ci) if rev else ci
        sl = pl.ds(pl.multiple_of(c * C, C), C)
        q = q_ref[sl, :]
        k = k_ref[sl, :]
        v = v_ref[sl, :]
        vb = v.astype(BF16)
        b = _chunk_cumsum(lf_ref[sl, :], rev)
        tot = b[0:1, :] if rev else b[C - 1:C, :]

        o = lax.dot_general((q * jnp.exp(b)).astype(BF16), st.astype(BF16), NT_DIMS,
                            preferred_element_type=F32)
        kdec = (k * jnp.exp(tot - b)).astype(BF16)
        st_next = st * jnp.exp(tot) + lax.dot_general(vb, kdec, TN_DIMS,
                                                      preferred_element_type=F32)

        scores = eye * jnp.sum(q * k, axis=1, keepdims=True)
        for blk, q_side, pair in levels:
            w = jnp.exp2(jnp.abs(b - _pivot_rows(b, blk, rev)) * (-LOG2E))
            z = (jnp.where(q_side > 0.5, q, k) * w).astype(BF16)
            g = lax.dot_general(z, z, NT_DIMS, preferred_element_type=F32)
            scores = scores + pair * g
        o_ref[sl, :] = o + jnp.dot(scores.astype(BF16), vb, preferred_element_type=F32)
        return st_next

    st_ref[...] = lax.fori_loop(0, n_chunks, chunk, st_ref[...], unroll=HG_UNROLL)


def _hgrn(q, k, lf, v, rev, rb=1024):
    S = q.shape[0]
    nb = S // rb
    idx = (lambda h, j: (nb - 1 - j, h)) if rev else (lambda h, j: (j, h))
    spec = pl.BlockSpec((rb, HEAD_W), idx)
    return pl.pallas_call(
        functools.partial(_hgrn_kernel, rev=rev, n_chunks=rb // HG_CHUNK),
        grid=(N_HEADS, nb),
        in_specs=[spec] * 4,
        out_specs=spec,
        out_shape=jax.ShapeDtypeStruct((S, HALF_W), F32),
        scratch_shapes=[pltpu.VMEM((HEAD_W, HEAD_W), F32)],
        compiler_params=pltpu.CompilerParams(dimension_semantics=("arbitrary", "arbitrary"),
                                             vmem_limit_bytes=V7X_VMEM_LIMIT),
        name="hgrn_bwd" if rev else "hgrn_fwd",
    )(q, k, lf, v)


def _attn_kernel(lam_ref, q_ref, k_ref, vt_ref, gain_ref, o_ref, sa_ref, sb_ref, m_ref, acc_ref,
                 *, tk):
    n_kv = k_ref.shape[0] // tk
    tq = q_ref.shape[0]
    assert n_kv % 2 == 0 and tq % QUERY_GROUP == 0
    q = q_ref[...]
    lane = lax.broadcasted_iota(jnp.int32, q.shape, 1)
    qm = (jnp.where(lane < QK_DIM, q, jnp.zeros_like(q)),
          jnp.where(lane >= QK_DIM, q, jnp.zeros_like(q)))
    m_ref[...] = jnp.full(m_ref.shape, -jnp.inf, F32)
    acc_ref[...] = jnp.zeros_like(acc_ref)

    def chunk(j):
        return pl.ds(pl.multiple_of(j * tk, tk), tk)

    def scores(j, s_ref):
        kc = k_ref[chunk(j), :]
        for c in range(2):
            s_ref[c] = lax.dot_general(kc, qm[c], NT_DIMS, preferred_element_type=F32)

    def softmax_pv(j, s_ref):
        vt = vt_ref[:, chunk(j)]
        for c in range(2):
            for g in range(0, tq, QUERY_GROUP):
                cols = slice(g, g + QUERY_GROUP)
                s = s_ref[c, :, cols]
                m_old = m_ref[c, :, cols]
                m_new = jnp.maximum(m_old, jnp.max(s, axis=0, keepdims=True))
                p = jnp.exp2(s - m_new).astype(BF16)
                acc_ref[c, :, cols] = jnp.exp2(m_old - m_new) * acc_ref[c, :, cols] + jnp.dot(
                    vt, p, preferred_element_type=F32)
                m_ref[c, :, cols] = m_new

    def scores_and_softmax_pv(js, dst_ref, jp, src_ref):
        kc = k_ref[chunk(js), :]
        vt = vt_ref[:, chunk(jp)]
        for c in range(2):
            for g in range(0, tq, QUERY_GROUP):
                cols = slice(g, g + QUERY_GROUP)
                dst_ref[c, :, cols] = lax.dot_general(kc, qm[c][cols, :], NT_DIMS,
                                                      preferred_element_type=F32)
                s = src_ref[c, :, cols]
                m_old = m_ref[c, :, cols]
                m_new = jnp.maximum(m_old, jnp.max(s, axis=0, keepdims=True))
                p = jnp.exp2(s - m_new).astype(BF16)
                acc_ref[c, :, cols] = jnp.exp2(m_old - m_new) * acc_ref[c, :, cols] + jnp.dot(
                    vt, p, preferred_element_type=F32)
                m_ref[c, :, cols] = m_new

    scores(0, sa_ref)

    def two_steps(i, carry):
        scores_and_softmax_pv(2 * i + 1, sb_ref, 2 * i, sa_ref)
        scores_and_softmax_pv(2 * i + 2, sa_ref, 2 * i + 1, sb_ref)
        return carry

    lax.fori_loop(0, n_kv // 2 - 1, two_steps, 0)
    scores_and_softmax_pv(n_kv - 1, sb_ref, n_kv - 2, sa_ref)
    softmax_pv(n_kv - 1, sb_ref)

    outs = []
    for c in range(2):
        a = acc_ref[c]
        outs.append(a[:HEAD_W, :] / a[HEAD_W:HEAD_W + 1, :])
    o = (outs[0] - lam_ref[0] * outs[1]).T
    o_ref[...] = _rms(o, SUBLN_EPS) * gain_ref[...] * (1.0 - LAMBDA_INIT)


def _attn(lam, dq, dk, dvt, gain, tq=2048, tk=512):
    S = dq.shape[0]
    return pl.pallas_call(
        functools.partial(_attn_kernel, tk=tk),
        grid=(N_HEADS, S // tq),
        in_specs=[pl.BlockSpec(memory_space=pltpu.SMEM),
                  pl.BlockSpec((tq, HEAD_W), lambda h, i: (i, h)),
                  pl.BlockSpec((S, HEAD_W), lambda h, i: (0, h)),
                  pl.BlockSpec((VT_ROWS, S), lambda h, i: (h, 0)),
                  pl.BlockSpec((1, HEAD_W), lambda h, i: (0, 0))],
        out_specs=pl.BlockSpec((tq, HEAD_W), lambda h, i: (i, h)),
        out_shape=jax.ShapeDtypeStruct((S, HALF_W), F32),
        scratch_shapes=[pltpu.VMEM((2, tk, tq), F32), pltpu.VMEM((2, tk, tq), F32),
                        pltpu.VMEM((2, 1, tq), F32), pltpu.VMEM((2, VT_ROWS, tq), F32)],
        compiler_params=pltpu.CompilerParams(dimension_semantics=("arbitrary", "arbitrary"),
                                             vmem_limit_bytes=V7X_VMEM_LIMIT),
        name="diff_attn",
    )(lam, dq, dk, dvt, gain)


def _outproj_kernel(x_ref, of_ref, ob_ref, hg_ref, oda_ref, hgain_ref, wo_ref, g2_ref,
                    wr_ref, br_ref, x2_ref, rank_ref, rank_t_ref, gates_t_ref, cnt_ref,
                    carry_ref, *, sub_blocks):
    i = pl.program_id(0)
    o = of_ref[...] + ob_ref[...]
    gate = hg_ref[...]
    gate = gate * jax.nn.sigmoid(gate)
    parts = []
    for hd in range(N_HEADS):
        sl = slice(hd * HEAD_W, (hd + 1) * HEAD_W)
        parts.append(_rms(o[:, sl], NORM_EPS) * hgain_ref[...] * gate[:, sl])
    parts.append(oda_ref[...])
    mixed = jnp.concatenate(parts, axis=1).astype(BF16)
    x2 = x_ref[...] + jnp.dot(mixed, wo_ref[...], preferred_element_type=F32)
    x2_ref[...] = x2

    h2 = _rms(x2, NORM_EPS) * g2_ref[...]
    h_hi = h2.astype(BF16)
    h_lo = (h2 - h_hi.astype(F32)).astype(BF16)
    t = jnp.dot(h_hi, wr_ref[...], preferred_element_type=F32)
    logits = (t[:, :ROUTER_LANES] + t[:, ROUTER_LANES:] + br_ref[...]
              + jnp.dot(h_lo, wr_ref[:, :ROUTER_LANES], preferred_element_type=F32))
    lane = lax.broadcasted_iota(jnp.int32, logits.shape, 1)
    neg = -jnp.inf
    big = ROUTER_LANES

    def first_max(vals):
        mx = jnp.max(vals, axis=1, keepdims=True)
        idx = jnp.min(jnp.where(vals == mx, lane, big), axis=1, keepdims=True)
        return mx, idx

    gl = jnp.where((lane >= N_EXPERTS) & (lane < N_EXPERTS + N_GROUPS), logits, neg)
    gm, gidx = first_max(gl)
    g_p = 1.0 / jnp.sum(jnp.exp(gl - gm), axis=1, keepdims=True)
    el = jnp.where((lane < N_EXPERTS) & ((lane // EXPERTS_PER_GROUP) == (gidx - N_EXPERTS)),
                   logits, neg)
    em1, idx1 = first_max(el)
    em2, idx2 = first_max(jnp.where(lane == idx1, neg, el))
    e2 = jnp.exp(em2 - em1)
    w1 = g_p / (1.0 + e2)
    w2 = g_p * e2 / (1.0 + e2)
    is1 = lane == idx1
    is2 = lane == idx2
    gates = jnp.where(is1, w1, 0.0) + jnp.where(is2, w2, 0.0)

    @pl.when(i % sub_blocks == 0)
    def _():
        carry_ref[...] = jnp.zeros_like(carry_ref)

    assigned = jnp.where(is1 | is2, 1.0, 0.0)
    tm = assigned.shape[0]
    tri = jnp.where(lax.broadcasted_iota(jnp.int32, (tm, tm), 0)
                    >= lax.broadcasted_iota(jnp.int32, (tm, tm), 1), 1.0, 0.0).astype(BF16)
    cum = jnp.dot(tri, assigned.astype(BF16), preferred_element_type=F32)
    carry = carry_ref[...]
    rank = jnp.where(assigned > 0.0, carry + cum - 1.0, -1.0)
    carry = carry + cum[tm - 1:tm, :]
    carry_ref[...] = carry
    cnt_ref[pl.ds(i // sub_blocks, 1), :] = carry.astype(jnp.int32)
    rank_ref[...] = rank
    rank_t_ref[...] = rank.T
    gates_t_ref[...] = gates.T


def _outproj(x, o_f, o_b, hg, o_da, hgain, w_out, g2, wr, br, tb, tm=512):
    S = x.shape[0]
    row = lambda w: pl.BlockSpec((tm, w), lambda i: (i, 0))
    col = pl.BlockSpec((ROUTER_LANES, tm), lambda i: (0, i))
    full = lambda a: pl.BlockSpec(a.shape, lambda i: (0, 0))
    return pl.pallas_call(
        functools.partial(_outproj_kernel, sub_blocks=tb // tm),
        grid=(S // tm,),
        in_specs=[row(D_MODEL), row(HALF_W), row(HALF_W), row(HALF_W), row(HALF_W),
                  full(hgain), full(w_out), full(g2), full(wr), full(br)],
        out_specs=[row(D_MODEL), row(ROUTER_LANES), col, col,
                   pl.BlockSpec((S // tb, ROUTER_LANES), lambda i: (0, 0))],
        out_shape=[jax.ShapeDtypeStruct((S, D_MODEL), F32),
                   jax.ShapeDtypeStruct((S, ROUTER_LANES), F32),
                   jax.ShapeDtypeStruct((ROUTER_LANES, S), F32),
                   jax.ShapeDtypeStruct((ROUTER_LANES, S), F32),
                   jax.ShapeDtypeStruct((S // tb, ROUTER_LANES), jnp.int32)],
        scratch_shapes=[pltpu.VMEM((1, ROUTER_LANES), F32)],
        compiler_params=pltpu.CompilerParams(dimension_semantics=("arbitrary",),
                                             vmem_limit_bytes=V7X_VMEM_LIMIT),
        name="outproj_router",
    )(x, o_f, o_b, hg, o_da, hgain, w_out, g2, wr, br)


def _moe_kernel(cnt_ref, x2_ref, rank_ref, rank_t_ref, gates_t_ref, g2_ref, wg_ref, wu_ref,
                wd_ref, gf_ref, out_ref, h2_ref, acc_ref, *, tile_rows):
    b = pl.program_id(0)
    p = pl.program_id(1)
    R = tile_rows
    n_exp = wg_ref.shape[0]

    @pl.when(p == 0)
    def _():
        h2_ref[...] = (_rms(x2_ref[...], NORM_EPS) * g2_ref[...]).astype(BF16)
        acc_ref[...] = jnp.zeros_like(acc_ref)

    rank_blk = rank_ref[...]
    lane = lax.broadcasted_iota(jnp.int32, rank_blk.shape, 1)
    experts = [n_exp * p + i for i in range(n_exp)]
    rank_col = [jnp.sum(jnp.where(lane == e, rank_blk, 0.0), axis=1, keepdims=True)
                for e in experts]
    rank_row = [rank_t_ref[pl.ds(e, 1), :] for e in experts]
    gate_row = [gates_t_ref[pl.ds(e, 1), :] for e in experts]
    n_max = cnt_ref[b, experts[0]]
    for e in experts[1:]:
        n_max = jnp.maximum(n_max, cnt_ref[b, e])

    def tile(t, carry):
        base = (t * R).astype(F32)
        slot_col = lax.broadcasted_iota(jnp.int32, (R, 1), 0).astype(F32) + base
        slot_row = lax.broadcasted_iota(jnp.int32, (1, R), 1).astype(F32) + base
        sel = [rank_row[i] == slot_col for i in range(n_exp)]
        gather = jnp.concatenate([jnp.where(s, 1.0, 0.0).astype(BF16) for s in sel], axis=0)
        xg = jnp.dot(gather, h2_ref[...], preferred_element_type=F32).astype(BF16)
        ys = []
        for i in range(n_exp):
            xi = xg[i * R:(i + 1) * R]
            a = jnp.dot(xi, wg_ref[i], preferred_element_type=F32)
            a = a * jax.nn.sigmoid(a) * jnp.dot(xi, wu_ref[i], preferred_element_type=F32)
            y = jnp.dot(a.astype(BF16), wd_ref[i], preferred_element_type=F32)
            g = jnp.sum(jnp.where(sel[i], gate_row[i], 0.0), axis=1, keepdims=True)
            ys.append((y * g).astype(BF16))
        scatter = jnp.concatenate(
            [jnp.where(rank_col[i] == slot_row, 1.0, 0.0).astype(BF16) for i in range(n_exp)],
            axis=1)
        acc_ref[...] += jnp.dot(scatter, jnp.concatenate(ys, axis=0),
                                preferred_element_type=F32)
        return carry

    lax.fori_loop(0, (n_max + R - 1) // R, tile, 0)

    @pl.when(p == pl.num_programs(1) - 1)
    def _():
        out_ref[...] = _rms(x2_ref[...] + acc_ref[...], NORM_EPS) * gf_ref[...]


def _moe(cnt, x2, rank, rank_t, gates_t, g2, wg, wu, wd, gf, tb, tile_rows=96,
         experts_per_step=4):
    S = x2.shape[0]
    eps = experts_per_step
    grid_spec = pltpu.PrefetchScalarGridSpec(
        num_scalar_prefetch=1,
        grid=(S // tb, N_EXPERTS // eps),
        in_specs=[pl.BlockSpec((tb, D_MODEL), lambda b, p, c: (b, 0)),
                  pl.BlockSpec((tb, ROUTER_LANES), lambda b, p, c: (b, 0)),
                  pl.BlockSpec((N_EXPERTS, tb), lambda b, p, c: (0, b)),
                  pl.BlockSpec((N_EXPERTS, tb), lambda b, p, c: (0, b)),
                  pl.BlockSpec((1, D_MODEL), lambda b, p, c: (0, 0)),
                  pl.BlockSpec((eps, D_MODEL, D_EXPERT), lambda b, p, c: (p, 0, 0)),
                  pl.BlockSpec((eps, D_MODEL, D_EXPERT), lambda b, p, c: (p, 0, 0)),
                  pl.BlockSpec((eps, D_EXPERT, D_MODEL), lambda b, p, c: (p, 0, 0)),
                  pl.BlockSpec((1, D_MODEL), lambda b, p, c: (0, 0))],
        out_specs=pl.BlockSpec((tb, D_MODEL), lambda b, p, c: (b, 0)),
        scratch_shapes=[pltpu.VMEM((tb, D_MODEL), BF16), pltpu.VMEM((tb, D_MODEL), F32)])
    return pl.pallas_call(
        functools.partial(_moe_kernel, tile_rows=tile_rows),
        grid_spec=grid_spec,
        out_shape=jax.ShapeDtypeStruct((S, D_MODEL), F32),
        compiler_params=pltpu.CompilerParams(dimension_semantics=("arbitrary", "arbitrary"),
                                             vmem_limit_bytes=V7X_VMEM_LIMIT),
        name="moe",
    )(cnt, x2, rank, rank_t, gates_t, g2, wg, wu, wd, gf)


def kernel(x, positions, norm1_gain, w_in, hg_lower_bounds, hg_norm_gain, diff_lambda,
           diff_subln_gain, w_out, norm2_gain, router_group_w, router_group_b,
           router_expert_w, router_expert_b, moe_w_gate, moe_w_up, moe_w_down,
           final_norm_gain):
    B, S, _ = x.shape
    assert B == 1 and norm1_gain.shape[0] == 1
    layer = 0
    xs = x.reshape(S, D_MODEL)
    pos = positions.reshape(S, 1).astype(F32)

    lb_cum = jnp.cumsum(jax.nn.softmax(hg_lower_bounds.astype(F32), axis=1), axis=1)
    lb = lb_cum[:, layer + 1] - lb_cum[:, 0]
    lam_p = diff_lambda[layer].astype(F32)
    lam = (jnp.exp(jnp.sum(lam_p[0] * lam_p[1])) - jnp.exp(jnp.sum(lam_p[2] * lam_p[3]))
           + LAMBDA_INIT).reshape(1)
    lane = jnp.arange(HEAD_W)
    inv_freq = jnp.float32(ROPE_THETA) ** (-jnp.arange(0, ROT_DIM, 2, dtype=F32) / ROT_DIM)
    in_rot = (lane % QK_DIM) < ROT_DIM
    invf = jnp.where(in_rot, inv_freq[lane % (ROT_DIM // 2)], 0.0).reshape(1, HEAD_W)
    sgn = jnp.where(in_rot, jnp.where((lane % QK_DIM) < ROT_DIM // 2, -1.0, 1.0),
                    0.0).astype(F32).reshape(1, HEAD_W)
    wr = jnp.zeros((D_MODEL, ROUTER_LANES), F32)
    wr = wr.at[:, :N_EXPERTS].set(router_expert_w[layer])
    wr = wr.at[:, N_EXPERTS:N_EXPERTS + N_GROUPS].set(router_group_w[layer])
    wr_hi = wr.astype(BF16)
    wr = jnp.concatenate([wr_hi, (wr - wr_hi.astype(F32)).astype(BF16)], axis=1)
    br = jnp.zeros((1, ROUTER_LANES), F32)
    br = br.at[0, :N_EXPERTS].set(router_expert_b[layer])
    br = br.at[0, N_EXPERTS:N_EXPERTS + N_GROUPS].set(router_group_b[layer])

    hq, kf, kb, lff, lfb, hi, hg, dq, dk, dv = _inproj(
        xs, pos, norm1_gain[layer].reshape(1, -1), w_in[layer].astype(BF16), lb, invf, sgn)
    o_f = _hgrn(hq, kf, lff, hi, rev=False)
    o_b = _hgrn(hq, kb, lfb, hi, rev=True)
    o_da = _attn(lam, dq, dk, dv, diff_subln_gain[layer].reshape(1, -1))
    x2, rank, rank_t, gates_t, cnt = _outproj(
        xs, o_f, o_b, hg, o_da, hg_norm_gain[layer].reshape(1, -1), w_out[layer].astype(BF16),
        norm2_gain[layer].reshape(1, -1), wr, br, tb=MOE_TOKEN_BLOCK)
    out = _moe(cnt, x2, rank, rank_t, gates_t, norm2_gain[layer].reshape(1, -1),
               moe_w_gate[layer].astype(BF16), moe_w_up[layer].astype(BF16),
               moe_w_down[layer].astype(BF16), final_norm_gain.reshape(1, -1),
               tb=MOE_TOKEN_BLOCK)
    return out.reshape(B, S, D_MODEL)
```

```python
import functools
import math

import jax
import jax.numpy as jnp
from jax import lax
from jax.experimental import pallas as pl
from jax.experimental.pallas import tpu as pltpu

F32 = jnp.float32
BF16 = jnp.bfloat16

D_MODEL = 1024
HEAD_W = 128
N_HEADS = 4
HALF_W = N_HEADS * HEAD_W
QK_DIM = 64
ROT_DIM = 16
ROPE_THETA = 500000.0
QUERY_GROUP = 512
VT_ROWS = HEAD_W + 16
HG_CHUNK = 128
HG_UNROLL = 8
N_GROUPS = 4
EXPERTS_PER_GROUP = 8
N_EXPERTS = 32
D_EXPERT = 512
NORM_EPS = 1e-6
SUBLN_EPS = 1e-5
LAMBDA_INIT = 0.8 - 0.6 * math.exp(-0.3 * 0)
LOG2E = 1.4426950408889634
ROUTER_LANES = 128
MOE_TOKEN_BLOCK = 1024
V7X_VMEM_LIMIT = 56 * 1024 * 1024

NT_DIMS = (((1,), (1,)), ((), ()))
TN_DIMS = (((0,), (0,)), ((), ()))


def _rms(x, eps):
    return x * lax.rsqrt(jnp.mean(x * x, axis=-1, keepdims=True) + eps)


def _inproj_kernel(x_ref, pos_ref, g1_ref, w_ref, lb_ref, invf_ref, sgn_ref,
                   hq_ref, kf_ref, kb_ref, lff_ref, lfb_ref, hi_ref, hg_ref,
                   dq_ref, dk_ref, dv_ref):
    x = x_ref[...]
    h = (_rms(x, NORM_EPS) * g1_ref[...]).astype(BF16)

    def proj(i):
        return jnp.dot(h, w_ref[:, i * HALF_W:(i + 1) * HALF_W], preferred_element_type=F32)

    hq_ref[...] = proj(0)
    for i, (k_ref, lf_ref) in enumerate(((kf_ref, lff_ref), (kb_ref, lfb_ref))):
        z = proj(1 + i)
        lb = lb_ref[i:i + 1, :]
        sg = jax.nn.sigmoid(z)
        lf_ref[...] = jnp.log(lb + (1.0 - lb) * sg)
        k_ref[...] = (1.0 - lb) * (1.0 - sg)
    hi_ref[...] = proj(3)
    hg_ref[...] = proj(4)

    ang = pos_ref[...] * invf_ref[...]
    cos = jnp.concatenate([jnp.cos(ang)] * N_HEADS, axis=1)
    sgn = sgn_ref[...]
    sin = jnp.concatenate([jnp.sin(ang) * sgn] * N_HEADS, axis=1)
    take_hi = jnp.concatenate([jnp.broadcast_to(sgn, ang.shape)] * N_HEADS, axis=1) < 0.0

    def rotary(t):
        partner = jnp.where(take_hi, pltpu.roll(t, HALF_W - ROT_DIM // 2, axis=1),
                            pltpu.roll(t, ROT_DIM // 2, axis=1))
        return t * cos + partner * sin

    dq_ref[...] = (rotary(proj(5)) * (QK_DIM ** -0.5 * LOG2E)).astype(BF16)
    dk_ref[...] = rotary(proj(6)).astype(BF16)
    v = proj(7)
    sub = lax.broadcasted_iota(jnp.int32, (VT_ROWS - HEAD_W, v.shape[0]), 0)
    ones_rows = jnp.where(sub == 0, 1.0, 0.0)
    pieces = []
    for hd in range(N_HEADS):
        pieces += [v[:, hd * HEAD_W:(hd + 1) * HEAD_W].T, ones_rows]
    dv_ref[...] = jnp.concatenate(pieces, axis=0).astype(BF16)


def _inproj(x, pos, g1, w_in, lb, invf, sgn, tm=512):
    S = x.shape[0]
    row = lambda w: pl.BlockSpec((tm, w), lambda i: (i, 0))
    full = lambda a: pl.BlockSpec(a.shape, lambda i: (0, 0))
    f32o = jax.ShapeDtypeStruct((S, HALF_W), F32)
    bf16o = jax.ShapeDtypeStruct((S, HALF_W), BF16)
    return pl.pallas_call(
        _inproj_kernel,
        grid=(S // tm,),
        in_specs=[row(D_MODEL), row(1), full(g1), full(w_in), full(lb), full(invf), full(sgn)],
        out_specs=[row(HALF_W)] * 9 + [pl.BlockSpec((N_HEADS * VT_ROWS, tm), lambda i: (0, i))],
        out_shape=[f32o] * 7 + [bf16o, bf16o,
                                jax.ShapeDtypeStruct((N_HEADS * VT_ROWS, S), BF16)],
        compiler_params=pltpu.CompilerParams(dimension_semantics=("arbitrary",),
                                             vmem_limit_bytes=V7X_VMEM_LIMIT),
        name="inproj",
    )(x, pos, g1, w_in, lb, invf, sgn)


def _chunk_cumsum(x, rev):
    n = x.shape[0]
    row = lax.broadcasted_iota(jnp.int32, x.shape, 0)
    d = 1
    while d < n:
        if rev:
            x = x + jnp.where(row < n - d, pltpu.roll(x, n - d, axis=0), 0.0)
        else:
            x = x + jnp.where(row >= d, pltpu.roll(x, d, axis=0), 0.0)
        d *= 2
    return x


def _pivot_rows(b, blk, rev):
    n = b.shape[0]
    half = blk // 2
    groups = []
    for g0 in range(0, n, 8):
        def piv(r):
            base = (r // blk) * blk
            return base + half if rev else base + half - 1
        if blk >= 16 or blk == 8:
            p = piv(g0)
            groups.append(jnp.broadcast_to(b[p:p + 1, :], (8, b.shape[1])))
        else:
            sub = lax.broadcasted_iota(jnp.int32, (8, b.shape[1]), 0)
            acc = None
            for s0 in range(0, 8, blk):
                p = piv(g0 + s0)
                rowv = jnp.broadcast_to(b[p:p + 1, :], (8, b.shape[1]))
                acc = rowv if acc is None else jnp.where(sub >= s0, rowv, acc)
            groups.append(acc)
    return jnp.concatenate(groups, axis=0)


def _hgrn_kernel(q_ref, k_ref, lf_ref, v_ref, o_ref, st_ref, *, rev, n_chunks):
    @pl.when(pl.program_id(1) == 0)
    def _():
        st_ref[...] = jnp.zeros_like(st_ref)

    C = HG_CHUNK
    rt = lax.broadcasted_iota(jnp.int32, (C, C), 0)
    ct = lax.broadcasted_iota(jnp.int32, (C, C), 1)
    rowi = lax.broadcasted_iota(jnp.int32, (C, HEAD_W), 0)
    eye = jnp.where(rt == ct, 1.0, 0.0)
    levels = []
    blk = C
    while blk >= 2:
        half = blk // 2
        q_side = (lambda r: (r % blk) < half) if rev else (lambda r: (r % blk) >= half)
        pair = ((rt // blk) == (ct // blk)) & q_side(rt) & jnp.logical_not(q_side(ct))
        levels.append((blk, jnp.where(q_side(rowi), 1.0, 0.0), jnp.where(pair, 1.0, 0.0)))
        blk = half

    def chunk(ci, st):
        c = (n_chunks - 1 - ci) if rev else ci
        sl = pl.ds(pl.multiple_of(c * C, C), C)
        q = q_ref[sl, :]
        k = k_ref[sl, :]
        v = v_ref[sl, :]
        vb = v.astype(BF16)
        b = _chunk_cumsum(lf_ref[sl, :], rev)
        tot = b[0:1, :] if rev else b[C - 1:C, :]

        o = lax.dot_general((q * jnp.exp(b)).astype(BF16), st.astype(BF16), NT_DIMS,
                            preferred_element_type=F32)
        kdec = (k * jnp.exp(tot - b)).astype(BF16)
        st_next = st * jnp.exp(tot) + lax.dot_general(vb, kdec, TN_DIMS,
                                                      preferred_element_type=F32)

        scores = eye * jnp.sum(q * k, axis=1, keepdims=True)
        for blk, q_side, pair in levels:
            w = jnp.exp2(jnp.abs(b - _pivot_rows(b, blk, rev)) * (-LOG2E))
            z = (jnp.where(q_side > 0.5, q, k) * w).astype(BF16)
            g = lax.dot_general(z, z, NT_DIMS, preferred_element_type=F32)
            scores = scores + pair * g
        o_ref[sl, :] = o + jnp.dot(scores.astype(BF16), vb, preferred_element_type=F32)
        return st_next

    st_ref[...] = lax.fori_loop(0, n_chunks, chunk, st_ref[...], unroll=HG_UNROLL)


def _hgrn(q, k, lf, v, rev, rb=1024):
    S = q.shape[0]
    nb = S // rb
    idx = (lambda h, j: (nb - 1 - j, h)) if rev else (lambda h, j: (j, h))
    spec = pl.BlockSpec((rb, HEAD_W), idx)
    return pl.pallas_call(
        functools.partial(_hgrn_kernel, rev=rev, n_chunks=rb // HG_CHUNK),
        grid=(N_HEADS, nb),
        in_specs=[spec] * 4,
        out_specs=spec,
        out_shape=jax.ShapeDtypeStruct((S, HALF_W), F32),
        scratch_shapes=[pltpu.VMEM((HEAD_W, HEAD_W), F32)],
        compiler_params=pltpu.CompilerParams(dimension_semantics=("arbitrary", "arbitrary"),
                                             vmem_limit_bytes=V7X_VMEM_LIMIT),
        name="hgrn_bwd" if rev else "hgrn_fwd",
    )(q, k, lf, v)


def _attn_kernel(lam_ref, q_ref, k_ref, vt_ref, gain_ref, o_ref, sa_ref, sb_ref, m_ref, acc_ref,
                 *, tk):
    n_kv = k_ref.shape[0] // tk
    tq = q_ref.shape[0]
    assert n_kv % 2 == 0 and tq % QUERY_GROUP == 0
    q = q_ref[...]
    lane = lax.broadcasted_iota(jnp.int32, q.shape, 1)
    qm = (jnp.where(lane < QK_DIM, q, jnp.zeros_like(q)),
          jnp.where(lane >= QK_DIM, q, jnp.zeros_like(q)))
    m_ref[...] = jnp.full(m_ref.shape, -jnp.inf, F32)
    acc_ref[...] = jnp.zeros_like(acc_ref)

    def chunk(j):
        return pl.ds(pl.multiple_of(j * tk, tk), tk)

    def scores(j, s_ref):
        kc = k_ref[chunk(j), :]
        for c in range(2):
            s_ref[c] = lax.dot_general(kc, qm[c], NT_DIMS, preferred_element_type=F32)

    def softmax_pv(j, s_ref):
        vt = vt_ref[:, chunk(j)]
        for c in range(2):
            for g in range(0, tq, QUERY_GROUP):
                cols = slice(g, g + QUERY_GROUP)
                s = s_ref[c, :, cols]
                m_old = m_ref[c, :, cols]
                m_new = jnp.maximum(m_old, jnp.max(s, axis=0, keepdims=True))
                p = jnp.exp2(s - m_new).astype(BF16)
                acc_ref[c, :, cols] = jnp.exp2(m_old - m_new) * acc_ref[c, :, cols] + jnp.dot(
                    vt, p, preferred_element_type=F32)
                m_ref[c, :, cols] = m_new

    def scores_and_softmax_pv(js, dst_ref, jp, src_ref):
        kc = k_ref[chunk(js), :]
        vt = vt_ref[:, chunk(jp)]
        for c in range(2):
            for g in range(0, tq, QUERY_GROUP):
                cols = slice(g, g + QUERY_GROUP)
                dst_ref[c, :, cols] = lax.dot_general(kc, qm[c][cols, :], NT_DIMS,
                                                      preferred_element_type=F32)
                s = src_ref[c, :, cols]
                m_old = m_ref[c, :, cols]
                m_new = jnp.maximum(m_old, jnp.max(s, axis=0, keepdims=True))
                p = jnp.exp2(s - m_new).astype(BF16)
                acc_ref[c, :, cols] = jnp.exp2(m_old - m_new) * acc_ref[c, :, cols] + jnp.dot(
                    vt, p, preferred_element_type=F32)
                m_ref[c, :, cols] = m_new

    scores(0, sa_ref)

    def two_steps(i, carry):
        scores_and_softmax_pv(2 * i + 1, sb_ref, 2 * i, sa_ref)
        scores_and_softmax_pv(2 * i + 2, sa_ref, 2 * i + 1, sb_ref)
        return carry

    lax.fori_loop(0, n_kv // 2 - 1, two_steps, 0)
    scores_and_softmax_pv(n_kv - 1, sb_ref, n_kv - 2, sa_ref)
    softmax_pv(n_kv - 1, sb_ref)

    outs = []
    for c in range(2):
        a = acc_ref[c]
        outs.append(a[:HEAD_W, :] / a[HEAD_W:HEAD_W + 1, :])
    o = (outs[0] - lam_ref[0] * outs[1]).T
    o_ref[...] = _rms(o, SUBLN_EPS) * gain_ref[...] * (1.0 - LAMBDA_INIT)


def _attn(lam, dq, dk, dvt, gain, tq=2048, tk=512):
    S = dq.shape[0]
    return pl.pallas_call(
        functools.partial(_attn_kernel, tk=tk),
        grid=(N_HEADS, S // tq),
        in_specs=[pl.BlockSpec(memory_space=pltpu.SMEM),
                  pl.BlockSpec((tq, HEAD_W), lambda h, i: (i, h)),
                  pl.BlockSpec((S, HEAD_W), lambda h, i: (0, h)),
                  pl.BlockSpec((VT_ROWS, S), lambda h, i: (h, 0)),
                  pl.BlockSpec((1, HEAD_W), lambda h, i: (0, 0))],
        out_specs=pl.BlockSpec((tq, HEAD_W), lambda h, i: (i, h)),
        out_shape=jax.ShapeDtypeStruct((S, HALF_W), F32),
        scratch_shapes=[pltpu.VMEM((2, tk, tq), F32), pltpu.VMEM((2, tk, tq), F32),
                        pltpu.VMEM((2, 1, tq), F32), pltpu.VMEM((2, VT_ROWS, tq), F32)],
        compiler_params=pltpu.CompilerParams(dimension_semantics=("arbitrary", "arbitrary"),
                                             vmem_limit_bytes=V7X_VMEM_LIMIT),
        name="diff_attn",
    )(lam, dq, dk, dvt, gain)


def _outproj_kernel(x_ref, of_ref, ob_ref, hg_ref, oda_ref, hgain_ref, wo_ref, g2_ref,
                    wr_ref, br_ref, x2_ref, rank_t_ref, gates_t_ref, cnt_ref,
                    carry_ref, *, sub_blocks):
    i = pl.program_id(0)
    o = of_ref[...] + ob_ref[...]
    gate = hg_ref[...]
    gate = gate * jax.nn.sigmoid(gate)
    parts = []
    for hd in range(N_HEADS):
        sl = slice(hd * HEAD_W, (hd + 1) * HEAD_W)
        parts.append(_rms(o[:, sl], NORM_EPS) * hgain_ref[...] * gate[:, sl])
    parts.append(oda_ref[...])
    mixed = jnp.concatenate(parts, axis=1).astype(BF16)
    x2 = x_ref[...] + jnp.dot(mixed, wo_ref[...], preferred_element_type=F32)
    x2_ref[...] = x2

    h2 = _rms(x2, NORM_EPS) * g2_ref[...]
    h_hi = h2.astype(BF16)
    h_lo = (h2 - h_hi.astype(F32)).astype(BF16)
    t = jnp.dot(h_hi, wr_ref[...], preferred_element_type=F32)
    logits = (t[:, :ROUTER_LANES] + t[:, ROUTER_LANES:] + br_ref[...]
              + jnp.dot(h_lo, wr_ref[:, :ROUTER_LANES], preferred_element_type=F32))
    lane = lax.broadcasted_iota(jnp.int32, logits.shape, 1)
    neg = -jnp.inf
    big = ROUTER_LANES

    def first_max(vals):
        mx = jnp.max(vals, axis=1, keepdims=True)
        idx = jnp.min(jnp.where(vals == mx, lane, big), axis=1, keepdims=True)
        return mx, idx

    gl = jnp.where((lane >= N_EXPERTS) & (lane < N_EXPERTS + N_GROUPS), logits, neg)
    gm, gidx = first_max(gl)
    g_p = 1.0 / jnp.sum(jnp.exp(gl - gm), axis=1, keepdims=True)
    el = jnp.where((lane < N_EXPERTS) & ((lane // EXPERTS_PER_GROUP) == (gidx - N_EXPERTS)),
                   logits, neg)
    em1, idx1 = first_max(el)
    em2, idx2 = first_max(jnp.where(lane == idx1, neg, el))
    e2 = jnp.exp(em2 - em1)
    w1 = g_p / (1.0 + e2)
    w2 = g_p * e2 / (1.0 + e2)
    is1 = lane == idx1
    is2 = lane == idx2
    gates = jnp.where(is1, w1, 0.0) + jnp.where(is2, w2, 0.0)

    @pl.when(i % sub_blocks == 0)
    def _():
        carry_ref[...] = jnp.zeros_like(carry_ref)

    assigned = jnp.where(is1 | is2, 1.0, 0.0)
    tm = assigned.shape[0]
    tri = jnp.where(lax.broadcasted_iota(jnp.int32, (tm, tm), 0)
                    >= lax.broadcasted_iota(jnp.int32, (tm, tm), 1), 1.0, 0.0).astype(BF16)
    cum = jnp.dot(tri, assigned.astype(BF16), preferred_element_type=F32)
    carry = carry_ref[...]
    rank = jnp.where(assigned > 0.0, carry + cum - 1.0, -1.0)
    carry = carry + cum[tm - 1:tm, :]
    carry_ref[...] = carry
    cnt_ref[pl.ds(i // sub_blocks, 1), :] = carry.astype(jnp.int32)
    rank_t_ref[...] = rank.T
    gates_t_ref[...] = gates.T


def _outproj(x, o_f, o_b, hg, o_da, hgain, w_out, g2, wr, br, tb, tm=512):
    S = x.shape[0]
    row = lambda w: pl.BlockSpec((tm, w), lambda i: (i, 0))
    col = pl.BlockSpec((ROUTER_LANES, tm), lambda i: (0, i))
    full = lambda a: pl.BlockSpec(a.shape, lambda i: (0, 0))
    return pl.pallas_call(
        functools.partial(_outproj_kernel, sub_blocks=tb // tm),
        grid=(S // tm,),
        in_specs=[row(D_MODEL), row(HALF_W), row(HALF_W), row(HALF_W), row(HALF_W),
                  full(hgain), full(w_out), full(g2), full(wr), full(br)],
        out_specs=[row(D_MODEL), col, col,
                   pl.BlockSpec((S // tb, ROUTER_LANES), lambda i: (0, 0))],
        out_shape=[jax.ShapeDtypeStruct((S, D_MODEL), F32),
                   jax.ShapeDtypeStruct((ROUTER_LANES, S), F32),
                   jax.ShapeDtypeStruct((ROUTER_LANES, S), F32),
                   jax.ShapeDtypeStruct((S // tb, ROUTER_LANES), jnp.int32)],
        scratch_shapes=[pltpu.VMEM((1, ROUTER_LANES), F32)],
        compiler_params=pltpu.CompilerParams(dimension_semantics=("arbitrary",),
                                             vmem_limit_bytes=V7X_VMEM_LIMIT),
        name="outproj_router",
    )(x, o_f, o_b, hg, o_da, hgain, w_out, g2, wr, br)


def _moe_kernel(cnt_ref, x2_ref, rank_t_ref, gates_t_ref, g2_ref, wg_ref, wu_ref,
                wd_ref, gf_ref, out_ref, h2_ref, acc_ref, *, tile_rows):
    b = pl.program_id(0)
    p = pl.program_id(1)
    R = tile_rows
    n_exp = wg_ref.shape[0]

    @pl.when(p == 0)
    def _():
        h2_ref[...] = (_rms(x2_ref[...], NORM_EPS) * g2_ref[...]).astype(BF16)
        acc_ref[...] = jnp.zeros_like(acc_ref)

    experts = [n_exp * p + i for i in range(n_exp)]
    rank_row = [rank_t_ref[pl.ds(e, 1), :] for e in experts]
    gate_row = [gates_t_ref[pl.ds(e, 1), :] for e in experts]
    n_max = cnt_ref[b, experts[0]]
    for e in experts[1:]:
        n_max = jnp.maximum(n_max, cnt_ref[b, e])

    def tile(t, carry):
        base = (t * R).astype(F32)
        slot_col = lax.broadcasted_iota(jnp.int32, (R, 1), 0).astype(F32) + base
        sel = [rank_row[i] == slot_col for i in range(n_exp)]
        gather = jnp.concatenate([jnp.where(s, 1.0, 0.0).astype(BF16) for s in sel], axis=0)
        xg = jnp.dot(gather, h2_ref[...], preferred_element_type=F32).astype(BF16)
        ys = []
        for i in range(n_exp):
            xi = xg[i * R:(i + 1) * R]
            a = jnp.dot(xi, wg_ref[i], preferred_element_type=F32)
            a = a * jax.nn.sigmoid(a) * jnp.dot(xi, wu_ref[i], preferred_element_type=F32)
            y = jnp.dot(a.astype(BF16), wd_ref[i], preferred_element_type=F32)
            g = jnp.sum(jnp.where(sel[i], gate_row[i], 0.0), axis=1, keepdims=True)
            ys.append((y * g).astype(BF16))
        acc_ref[...] += lax.dot_general(gather, jnp.concatenate(ys, axis=0), TN_DIMS,
                                        preferred_element_type=F32)
        return carry

    lax.fori_loop(0, (n_max + R - 1) // R, tile, 0)

    @pl.when(p == pl.num_programs(1) - 1)
    def _():
        out_ref[...] = _rms(x2_ref[...] + acc_ref[...], NORM_EPS) * gf_ref[...]


def _moe(cnt, x2, rank_t, gates_t, g2, wg, wu, wd, gf, tb, tile_rows=96,
         experts_per_step=4):
    S = x2.shape[0]
    eps = experts_per_step
    grid_spec = pltpu.PrefetchScalarGridSpec(
        num_scalar_prefetch=1,
        grid=(S // tb, N_EXPERTS // eps),
        in_specs=[pl.BlockSpec((tb, D_MODEL), lambda b, p, c: (b, 0)),
                  pl.BlockSpec((N_EXPERTS, tb), lambda b, p, c: (0, b)),
                  pl.BlockSpec((N_EXPERTS, tb), lambda b, p, c: (0, b)),
                  pl.BlockSpec((1, D_MODEL), lambda b, p, c: (0, 0)),
                  pl.BlockSpec((eps, D_MODEL, D_EXPERT), lambda b, p, c: (p, 0, 0)),
                  pl.BlockSpec((eps, D_MODEL, D_EXPERT), lambda b, p, c: (p, 0, 0)),
                  pl.BlockSpec((eps, D_EXPERT, D_MODEL), lambda b, p, c: (p, 0, 0)),
                  pl.BlockSpec((1, D_MODEL), lambda b, p, c: (0, 0))],
        out_specs=pl.BlockSpec((tb, D_MODEL), lambda b, p, c: (b, 0)),
        scratch_shapes=[pltpu.VMEM((tb, D_MODEL), BF16), pltpu.VMEM((tb, D_MODEL), F32)])
    return pl.pallas_call(
        functools.partial(_moe_kernel, tile_rows=tile_rows),
        grid_spec=grid_spec,
        out_shape=jax.ShapeDtypeStruct((S, D_MODEL), F32),
        compiler_params=pltpu.CompilerParams(dimension_semantics=("arbitrary", "arbitrary"),
                                             vmem_limit_bytes=V7X_VMEM_LIMIT),
        name="moe",
    )(cnt, x2, rank_t, gates_t, g2, wg, wu, wd, gf)


def kernel(x, positions, norm1_gain, w_in, hg_lower_bounds, hg_norm_gain, diff_lambda,
           diff_subln_gain, w_out, norm2_gain, router_group_w, router_group_b,
           router_expert_w, router_expert_b, moe_w_gate, moe_w_up, moe_w_down,
           final_norm_gain):
    B, S, _ = x.shape
    assert B == 1 and norm1_gain.shape[0] == 1
    layer = 0
    xs = x.reshape(S, D_MODEL)
    pos = positions.reshape(S, 1).astype(F32)

    lb_cum = jnp.cumsum(jax.nn.softmax(hg_lower_bounds.astype(F32), axis=1), axis=1)
    lb = lb_cum[:, layer + 1] - lb_cum[:, 0]
    lam_p = diff_lambda[layer].astype(F32)
    lam = (jnp.exp(jnp.sum(lam_p[0] * lam_p[1])) - jnp.exp(jnp.sum(lam_p[2] * lam_p[3]))
           + LAMBDA_INIT).reshape(1)
    lane = jnp.arange(HEAD_W)
    inv_freq = jnp.float32(ROPE_THETA) ** (-jnp.arange(0, ROT_DIM, 2, dtype=F32) / ROT_DIM)
    in_rot = (lane % QK_DIM) < ROT_DIM
    invf = jnp.where(in_rot, inv_freq[lane % (ROT_DIM // 2)], 0.0).reshape(1, HEAD_W)
    sgn = jnp.where(in_rot, jnp.where((lane % QK_DIM) < ROT_DIM // 2, -1.0, 1.0),
                    0.0).astype(F32).reshape(1, HEAD_W)
    wr = jnp.zeros((D_MODEL, ROUTER_LANES), F32)
    wr = wr.at[:, :N_EXPERTS].set(router_expert_w[layer])
    wr = wr.at[:, N_EXPERTS:N_EXPERTS + N_GROUPS].set(router_group_w[layer])
    wr_hi = wr.astype(BF16)
    wr = jnp.concatenate([wr_hi, (wr - wr_hi.astype(F32)).astype(BF16)], axis=1)
    br = jnp.zeros((1, ROUTER_LANES), F32)
    br = br.at[0, :N_EXPERTS].set(router_expert_b[layer])
    br = br.at[0, N_EXPERTS:N_EXPERTS + N_GROUPS].set(router_group_b[layer])

    hq, kf, kb, lff, lfb, hi, hg, dq, dk, dv = _inproj(
        xs, pos, norm1_gain[layer].reshape(1, -1), w_in[layer].astype(BF16), lb, invf, sgn)
    o_f = _hgrn(hq, kf, lff, hi, rev=False)
    o_b = _hgrn(hq, kb, lfb, hi, rev=True)
    o_da = _attn(lam, dq, dk, dv, diff_subln_gain[layer].reshape(1, -1))
    x2, rank_t, gates_t, cnt = _outproj(
        xs, o_f, o_b, hg, o_da, hg_norm_gain[layer].reshape(1, -1), w_out[layer].astype(BF16),
        norm2_gain[layer].reshape(1, -1), wr, br, tb=MOE_TOKEN_BLOCK)
    out = _moe(cnt, x2, rank_t, gates_t, norm2_gain[layer].reshape(1, -1),
               moe_w_gate[layer].astype(BF16), moe_w_up[layer].astype(BF16),
               moe_w_down[layer].astype(BF16), final_norm_gain.reshape(1, -1),
               tb=MOE_TOKEN_BLOCK)
    return out.reshape(B, S, D_MODEL)
```

```python
import functools
import math

import jax
import jax.numpy as jnp
from jax import lax
from jax.experimental import pallas as pl
from jax.experimental.pallas import tpu as pltpu

F32 = jnp.float32
BF16 = jnp.bfloat16

D_MODEL = 1024
HEAD_W = 128
N_HEADS = 4
HALF_W = N_HEADS * HEAD_W
QK_DIM = 64
ROT_DIM = 16
ROPE_THETA = 500000.0
QUERY_GROUP = 512
VT_ROWS = HEAD_W + 16
HG_CHUNK = 128
HG_UNROLL = 8
N_GROUPS = 4
EXPERTS_PER_GROUP = 8
N_EXPERTS = 32
D_EXPERT = 512
NORM_EPS = 1e-6
SUBLN_EPS = 1e-5
LAMBDA_INIT = 0.8 - 0.6 * math.exp(-0.3 * 0)
LOG2E = 1.4426950408889634
ROUTER_LANES = 128
MOE_TOKEN_BLOCK = 1024
V7X_VMEM_LIMIT = 56 * 1024 * 1024

NT_DIMS = (((1,), (1,)), ((), ()))
TN_DIMS = (((0,), (0,)), ((), ()))


def _rms(x, eps):
    return x * lax.rsqrt(jnp.mean(x * x, axis=-1, keepdims=True) + eps)


def _inproj_kernel(x_ref, pos_ref, g1_ref, w_ref, lb_ref, invf_ref, sgn_ref,
                   hq_ref, kf_ref, kb_ref, lff_ref, lfb_ref, hi_ref, hg_ref,
                   dq_ref, dk_ref, dv_ref):
    x = x_ref[...]
    h = (_rms(x, NORM_EPS) * g1_ref[...]).astype(BF16)

    def proj(i):
        return jnp.dot(h, w_ref[:, i * HALF_W:(i + 1) * HALF_W], preferred_element_type=F32)

    hq_ref[...] = proj(0)
    for i, (k_ref, lf_ref) in enumerate(((kf_ref, lff_ref), (kb_ref, lfb_ref))):
        z = proj(1 + i)
        lb = lb_ref[i:i + 1, :]
        sg = jax.nn.sigmoid(z)
        lf_ref[...] = jnp.log(lb + (1.0 - lb) * sg)
        k_ref[...] = (1.0 - lb) * (1.0 - sg)
    hi_ref[...] = proj(3)
    hg_ref[...] = proj(4)

    ang = pos_ref[...] * invf_ref[...]
    cos = jnp.concatenate([jnp.cos(ang)] * N_HEADS, axis=1)
    sgn = sgn_ref[...]
    sin = jnp.concatenate([jnp.sin(ang) * sgn] * N_HEADS, axis=1)
    take_hi = jnp.concatenate([jnp.broadcast_to(sgn, ang.shape)] * N_HEADS, axis=1) < 0.0

    def rotary(t):
        partner = jnp.where(take_hi, pltpu.roll(t, HALF_W - ROT_DIM // 2, axis=1),
                            pltpu.roll(t, ROT_DIM // 2, axis=1))
        return t * cos + partner * sin

    dq_ref[...] = (rotary(proj(5)) * (QK_DIM ** -0.5 * LOG2E)).astype(BF16)
    dk_ref[...] = rotary(proj(6)).astype(BF16)
    v = proj(7)
    sub = lax.broadcasted_iota(jnp.int32, (VT_ROWS - HEAD_W, v.shape[0]), 0)
    ones_rows = jnp.where(sub == 0, 1.0, 0.0)
    pieces = []
    for hd in range(N_HEADS):
        pieces += [v[:, hd * HEAD_W:(hd + 1) * HEAD_W].T, ones_rows]
    dv_ref[...] = jnp.concatenate(pieces, axis=0).astype(BF16)


def _inproj(x, pos, g1, w_in, lb, invf, sgn, tm=512):
    S = x.shape[0]
    row = lambda w: pl.BlockSpec((tm, w), lambda i: (i, 0))
    full = lambda a: pl.BlockSpec(a.shape, lambda i: (0, 0))
    f32o = jax.ShapeDtypeStruct((S, HALF_W), F32)
    bf16o = jax.ShapeDtypeStruct((S, HALF_W), BF16)
    return pl.pallas_call(
        _inproj_kernel,
        grid=(S // tm,),
        in_specs=[row(D_MODEL), row(1), full(g1), full(w_in), full(lb), full(invf), full(sgn)],
        out_specs=[row(HALF_W)] * 9 + [pl.BlockSpec((N_HEADS * VT_ROWS, tm), lambda i: (0, i))],
        out_shape=[f32o] * 7 + [bf16o, bf16o,
                                jax.ShapeDtypeStruct((N_HEADS * VT_ROWS, S), BF16)],
        compiler_params=pltpu.CompilerParams(dimension_semantics=("arbitrary",),
                                             vmem_limit_bytes=V7X_VMEM_LIMIT),
        name="inproj",
    )(x, pos, g1, w_in, lb, invf, sgn)


def _chunk_cumsum(x, rev):
    n = x.shape[0]
    row = lax.broadcasted_iota(jnp.int32, x.shape, 0)
    d = 1
    while d < n:
        if rev:
            x = x + jnp.where(row < n - d, pltpu.roll(x, n - d, axis=0), 0.0)
        else:
            x = x + jnp.where(row >= d, pltpu.roll(x, d, axis=0), 0.0)
        d *= 2
    return x


def _pivot_rows(b, blk, rev):
    n = b.shape[0]
    half = blk // 2
    groups = []
    for g0 in range(0, n, 8):
        def piv(r):
            base = (r // blk) * blk
            return base + half if rev else base + half - 1
        if blk >= 8:
            p = piv(g0)
            groups.append(jnp.broadcast_to(b[p:p + 1, :], (8, b.shape[1])))
        else:
            sub = lax.broadcasted_iota(jnp.int32, (8, b.shape[1]), 0)
            acc = None
            for s0 in range(0, 8, blk):
                p = piv(g0 + s0)
                rowv = jnp.broadcast_to(b[p:p + 1, :], (8, b.shape[1]))
                acc = rowv if acc is None else jnp.where(sub >= s0, rowv, acc)
            groups.append(acc)
    return jnp.concatenate(groups, axis=0)


def _hgrn_kernel(q_ref, k_ref, lf_ref, v_ref, *rest, rev, n_chunks, n_cast):
    w_refs, o_ref, wb_refs, st_ref = (rest[:n_cast], rest[n_cast], rest[n_cast + 1:-1], rest[-1])
    for w_ref, wb_ref in zip(w_refs, wb_refs):
        wb_ref[...] = w_ref[...].astype(BF16)

    @pl.when(pl.program_id(1) == 0)
    def _():
        st_ref[...] = jnp.zeros_like(st_ref)

    C = HG_CHUNK
    rt = lax.broadcasted_iota(jnp.int32, (C, C), 0)
    ct = lax.broadcasted_iota(jnp.int32, (C, C), 1)
    rowi = lax.broadcasted_iota(jnp.int32, (C, HEAD_W), 0)
    eye = jnp.where(rt == ct, 1.0, 0.0)
    levels = []
    blk = C
    while blk >= 2:
        half = blk // 2
        q_side = (lambda r: (r % blk) < half) if rev else (lambda r: (r % blk) >= half)
        pair = ((rt // blk) == (ct // blk)) & q_side(rt) & jnp.logical_not(q_side(ct))
        levels.append((blk, jnp.where(q_side(rowi), 1.0, 0.0), jnp.where(pair, 1.0, 0.0)))
        blk = half

    def chunk(ci, st):
        c = (n_chunks - 1 - ci) if rev else ci
        sl = pl.ds(pl.multiple_of(c * C, C), C)
        q = q_ref[sl, :]
        k = k_ref[sl, :]
        v = v_ref[sl, :]
        vb = v.astype(BF16)
        b = _chunk_cumsum(lf_ref[sl, :], rev)
        tot = b[0:1, :] if rev else b[C - 1:C, :]

        o = lax.dot_general((q * jnp.exp(b)).astype(BF16), st.astype(BF16), NT_DIMS,
                            preferred_element_type=F32)
        kdec = (k * jnp.exp(tot - b)).astype(BF16)
        st_next = st * jnp.exp(tot) + lax.dot_general(vb, kdec, TN_DIMS,
                                                      preferred_element_type=F32)

        scores = eye * jnp.sum(q * k, axis=1, keepdims=True)
        for blk, q_side, pair in levels:
            w = jnp.exp2(jnp.abs(b - _pivot_rows(b, blk, rev)) * (-LOG2E))
            z = (jnp.where(q_side > 0.5, q, k) * w).astype(BF16)
            g = lax.dot_general(z, z, NT_DIMS, preferred_element_type=F32)
            scores = scores + pair * g
        o_ref[sl, :] = o + jnp.dot(scores.astype(BF16), vb, preferred_element_type=F32)
        return st_next

    st_ref[...] = lax.fori_loop(0, n_chunks, chunk, st_ref[...], unroll=HG_UNROLL)


def _hgrn(q, k, lf, v, rev, weights, rb=1024):
    S = q.shape[0]
    nb = S // rb
    steps = N_HEADS * nb
    idx = (lambda h, j: (nb - 1 - j, h)) if rev else (lambda h, j: (j, h))
    spec = pl.BlockSpec((rb, HEAD_W), idx)
    w_specs = []
    for w in weights:
        n_exp, rows, cols = w.shape
        per_exp = steps // n_exp
        assert per_exp * n_exp == steps and rows % per_exp == 0
        w_specs.append(pl.BlockSpec(
            (1, rows // per_exp, cols),
            lambda h, j, per_exp=per_exp: ((h * nb + j) // per_exp, (h * nb + j) % per_exp, 0)))
    return pl.pallas_call(
        functools.partial(_hgrn_kernel, rev=rev, n_chunks=rb // HG_CHUNK, n_cast=len(weights)),
        grid=(N_HEADS, nb),
        in_specs=[spec] * 4 + w_specs,
        out_specs=[spec] + w_specs,
        out_shape=[jax.ShapeDtypeStruct((S, HALF_W), F32)]
        + [jax.ShapeDtypeStruct(w.shape, BF16) for w in weights],
        scratch_shapes=[pltpu.VMEM((HEAD_W, HEAD_W), F32)],
        compiler_params=pltpu.CompilerParams(dimension_semantics=("arbitrary", "arbitrary"),
                                             vmem_limit_bytes=V7X_VMEM_LIMIT),
        name="hgrn_bwd" if rev else "hgrn_fwd",
    )(q, k, lf, v, *weights)


def _attn_kernel(lam_ref, q_ref, k_ref, vt_ref, gain_ref, o_ref, sa_ref, sb_ref, m_ref, acc_ref,
                 *, tk):
    n_kv = k_ref.shape[0] // tk
    tq = q_ref.shape[0]
    assert n_kv % 2 == 0 and tq % QUERY_GROUP == 0
    q = q_ref[...]
    lane = lax.broadcasted_iota(jnp.int32, q.shape, 1)
    qm = (jnp.where(lane < QK_DIM, q, jnp.zeros_like(q)),
          jnp.where(lane >= QK_DIM, q, jnp.zeros_like(q)))
    m_ref[...] = jnp.full(m_ref.shape, -jnp.inf, F32)
    acc_ref[...] = jnp.zeros_like(acc_ref)

    def chunk(j):
        return pl.ds(pl.multiple_of(j * tk, tk), tk)

    def softmax_pv(c, cols, s, vt):
        m_old = m_ref[c, :, cols]
        m_new = jnp.maximum(m_old, jnp.max(s, axis=0, keepdims=True))
        p = jnp.exp2(s - m_new).astype(BF16)
        acc_ref[c, :, cols] = jnp.exp2(m_old - m_new) * acc_ref[c, :, cols] + jnp.dot(
            vt, p, preferred_element_type=F32)
        m_ref[c, :, cols] = m_new

    def step(js, dst_ref, jp, src_ref):
        kc = None if dst_ref is None else k_ref[chunk(js), :]
        vt = None if src_ref is None else vt_ref[:, chunk(jp)]
        for c in range(2):
            for g in range(0, tq, QUERY_GROUP):
                cols = slice(g, g + QUERY_GROUP)
                if dst_ref is not None:
                    dst_ref[c, :, cols] = lax.dot_general(kc, qm[c][cols, :], NT_DIMS,
                                                          preferred_element_type=F32)
                if src_ref is not None:
                    softmax_pv(c, cols, src_ref[c, :, cols], vt)

    step(0, sa_ref, None, None)

    def two_steps(i, carry):
        step(2 * i + 1, sb_ref, 2 * i, sa_ref)
        step(2 * i + 2, sa_ref, 2 * i + 1, sb_ref)
        return carry

    lax.fori_loop(0, n_kv // 2 - 1, two_steps, 0)
    step(n_kv - 1, sb_ref, n_kv - 2, sa_ref)
    step(None, None, n_kv - 1, sb_ref)

    outs = []
    for c in range(2):
        a = acc_ref[c]
        outs.append(a[:HEAD_W, :] / a[HEAD_W:HEAD_W + 1, :])
    o = (outs[0] - lam_ref[0] * outs[1]).T
    o_ref[...] = _rms(o, SUBLN_EPS) * gain_ref[...] * (1.0 - LAMBDA_INIT)


def _attn(lam, dq, dk, dvt, gain, tq=2048, tk=512):
    S = dq.shape[0]
    return pl.pallas_call(
        functools.partial(_attn_kernel, tk=tk),
        grid=(N_HEADS, S // tq),
        in_specs=[pl.BlockSpec(memory_space=pltpu.SMEM),
                  pl.BlockSpec((tq, HEAD_W), lambda h, i: (i, h)),
                  pl.BlockSpec((S, HEAD_W), lambda h, i: (0, h)),
                  pl.BlockSpec((VT_ROWS, S), lambda h, i: (h, 0)),
                  pl.BlockSpec((1, HEAD_W), lambda h, i: (0, 0))],
        out_specs=pl.BlockSpec((tq, HEAD_W), lambda h, i: (i, h)),
        out_shape=jax.ShapeDtypeStruct((S, HALF_W), F32),
        scratch_shapes=[pltpu.VMEM((2, tk, tq), F32), pltpu.VMEM((2, tk, tq), F32),
                        pltpu.VMEM((2, 1, tq), F32), pltpu.VMEM((2, VT_ROWS, tq), F32)],
        compiler_params=pltpu.CompilerParams(dimension_semantics=("arbitrary", "arbitrary"),
                                             vmem_limit_bytes=V7X_VMEM_LIMIT),
        name="diff_attn",
    )(lam, dq, dk, dvt, gain)


def _outproj_kernel(x_ref, of_ref, ob_ref, hg_ref, oda_ref, hgain_ref, wo_ref, g2_ref,
                    wr_ref, br_ref, x2_ref, rank_t_ref, gates_t_ref, cnt_ref,
                    carry_ref, *, sub_blocks):
    i = pl.program_id(0)
    o = of_ref[...] + ob_ref[...]
    gate = hg_ref[...]
    gate = gate * jax.nn.sigmoid(gate)
    parts = []
    for hd in range(N_HEADS):
        sl = slice(hd * HEAD_W, (hd + 1) * HEAD_W)
        parts.append(_rms(o[:, sl], NORM_EPS) * hgain_ref[...] * gate[:, sl])
    parts.append(oda_ref[...])
    mixed = jnp.concatenate(parts, axis=1).astype(BF16)
    x2 = x_ref[...] + jnp.dot(mixed, wo_ref[...], preferred_element_type=F32)
    x2_ref[...] = x2

    h2 = _rms(x2, NORM_EPS) * g2_ref[...]
    h_hi = h2.astype(BF16)
    h_lo = (h2 - h_hi.astype(F32)).astype(BF16)
    t = jnp.dot(h_hi, wr_ref[...], preferred_element_type=F32)
    logits = (t[:, :ROUTER_LANES] + t[:, ROUTER_LANES:] + br_ref[...]
              + jnp.dot(h_lo, wr_ref[:, :ROUTER_LANES], preferred_element_type=F32))
    lane = lax.broadcasted_iota(jnp.int32, logits.shape, 1)
    neg = -jnp.inf
    big = ROUTER_LANES

    def first_max(vals):
        mx = jnp.max(vals, axis=1, keepdims=True)
        idx = jnp.min(jnp.where(vals == mx, lane, big), axis=1, keepdims=True)
        return mx, idx

    gl = jnp.where((lane >= N_EXPERTS) & (lane < N_EXPERTS + N_GROUPS), logits, neg)
    gm, gidx = first_max(gl)
    g_p = 1.0 / jnp.sum(jnp.exp(gl - gm), axis=1, keepdims=True)
    el = jnp.where((lane < N_EXPERTS) & ((lane // EXPERTS_PER_GROUP) == (gidx - N_EXPERTS)),
                   logits, neg)
    em1, idx1 = first_max(el)
    em2, idx2 = first_max(jnp.where(lane == idx1, neg, el))
    e2 = jnp.exp(em2 - em1)
    w1 = g_p / (1.0 + e2)
    w2 = g_p * e2 / (1.0 + e2)
    is1 = lane == idx1
    is2 = lane == idx2
    gates = jnp.where(is1, w1, 0.0) + jnp.where(is2, w2, 0.0)

    @pl.when(i % sub_blocks == 0)
    def _():
        carry_ref[...] = jnp.zeros_like(carry_ref)

    assigned = jnp.where(is1 | is2, 1.0, 0.0)
    tm = assigned.shape[0]
    tri = jnp.where(lax.broadcasted_iota(jnp.int32, (tm, tm), 0)
                    >= lax.broadcasted_iota(jnp.int32, (tm, tm), 1), 1.0, 0.0).astype(BF16)
    cum = jnp.dot(tri, assigned.astype(BF16), preferred_element_type=F32)
    carry = carry_ref[...]
    rank = jnp.where(assigned > 0.0, carry + cum - 1.0, -1.0)
    carry = carry + cum[tm - 1:tm, :]
    carry_ref[...] = carry
    cnt_ref[pl.ds(i // sub_blocks, 1), :] = carry.astype(jnp.int32)
    rank_t_ref[...] = rank.T
    gates_t_ref[...] = gates.T


def _outproj(x, o_f, o_b, hg, o_da, hgain, w_out, g2, wr, br, tb, tm=512):
    S = x.shape[0]
    row = lambda w: pl.BlockSpec((tm, w), lambda i: (i, 0))
    col = pl.BlockSpec((ROUTER_LANES, tm), lambda i: (0, i))
    full = lambda a: pl.BlockSpec(a.shape, lambda i: (0, 0))
    return pl.pallas_call(
        functools.partial(_outproj_kernel, sub_blocks=tb // tm),
        grid=(S // tm,),
        in_specs=[row(D_MODEL), row(HALF_W), row(HALF_W), row(HALF_W), row(HALF_W),
                  full(hgain), full(w_out), full(g2), full(wr), full(br)],
        out_specs=[row(D_MODEL), col, col,
                   pl.BlockSpec((S // tb, ROUTER_LANES), lambda i: (0, 0))],
        out_shape=[jax.ShapeDtypeStruct((S, D_MODEL), F32),
                   jax.ShapeDtypeStruct((ROUTER_LANES, S), F32),
                   jax.ShapeDtypeStruct((ROUTER_LANES, S), F32),
                   jax.ShapeDtypeStruct((S // tb, ROUTER_LANES), jnp.int32)],
        scratch_shapes=[pltpu.VMEM((1, ROUTER_LANES), F32)],
        compiler_params=pltpu.CompilerParams(dimension_semantics=("arbitrary",),
                                             vmem_limit_bytes=V7X_VMEM_LIMIT),
        name="outproj_router",
    )(x, o_f, o_b, hg, o_da, hgain, w_out, g2, wr, br)


def _moe_kernel(cnt_ref, x2_ref, rank_t_ref, gates_t_ref, g2_ref, wg_ref, wu_ref,
                wd_ref, gf_ref, out_ref, h2_ref, acc_ref, *, tile_rows):
    b = pl.program_id(0)
    p = pl.program_id(1)
    R = tile_rows
    n_exp = wg_ref.shape[0]

    @pl.when(p == 0)
    def _():
        h2_ref[...] = (_rms(x2_ref[...], NORM_EPS) * g2_ref[...]).astype(BF16)
        acc_ref[...] = jnp.zeros_like(acc_ref)

    experts = [n_exp * p + i for i in range(n_exp)]
    rank_row = [rank_t_ref[pl.ds(e, 1), :] for e in experts]
    gate_row = [gates_t_ref[pl.ds(e, 1), :] for e in experts]
    n_max = cnt_ref[b, experts[0]]
    for e in experts[1:]:
        n_max = jnp.maximum(n_max, cnt_ref[b, e])

    def tile(t, carry):
        base = (t * R).astype(F32)
        slot_col = lax.broadcasted_iota(jnp.int32, (R, 1), 0).astype(F32) + base
        sel = [rank_row[i] == slot_col for i in range(n_exp)]
        gather = jnp.concatenate([jnp.where(s, 1.0, 0.0).astype(BF16) for s in sel], axis=0)
        xg = jnp.dot(gather, h2_ref[...], preferred_element_type=F32).astype(BF16)
        ys = []
        for i in range(n_exp):
            xi = xg[i * R:(i + 1) * R]
            a = jnp.dot(xi, wg_ref[i], preferred_element_type=F32)
            a = a * jax.nn.sigmoid(a) * jnp.dot(xi, wu_ref[i], preferred_element_type=F32)
            y = jnp.dot(a.astype(BF16), wd_ref[i], preferred_element_type=F32)
            g = jnp.sum(jnp.where(sel[i], gate_row[i], 0.0), axis=1, keepdims=True)
            ys.append((y * g).astype(BF16))
        acc_ref[...] += lax.dot_general(gather, jnp.concatenate(ys, axis=0), TN_DIMS,
                                        preferred_element_type=F32)
        return carry

    lax.fori_loop(0, (n_max + R - 1) // R, tile, 0)

    @pl.when(p == pl.num_programs(1) - 1)
    def _():
        out_ref[...] = _rms(x2_ref[...] + acc_ref[...], NORM_EPS) * gf_ref[...]


def _moe(cnt, x2, rank_t, gates_t, g2, wg, wu, wd, gf, tb, tile_rows=96,
         experts_per_step=4):
    S = x2.shape[0]
    eps = experts_per_step
    grid_spec = pltpu.PrefetchScalarGridSpec(
        num_scalar_prefetch=1,
        grid=(S // tb, N_EXPERTS // eps),
        in_specs=[pl.BlockSpec((tb, D_MODEL), lambda b, p, c: (b, 0)),
                  pl.BlockSpec((N_EXPERTS, tb), lambda b, p, c: (0, b)),
                  pl.BlockSpec((N_EXPERTS, tb), lambda b, p, c: (0, b)),
                  pl.BlockSpec((1, D_MODEL), lambda b, p, c: (0, 0)),
                  pl.BlockSpec((eps, D_MODEL, D_EXPERT), lambda b, p, c: (p, 0, 0)),
                  pl.BlockSpec((eps, D_MODEL, D_EXPERT), lambda b, p, c: (p, 0, 0)),
                  pl.BlockSpec((eps, D_EXPERT, D_MODEL), lambda b, p, c: (p, 0, 0)),
                  pl.BlockSpec((1, D_MODEL), lambda b, p, c: (0, 0))],
        out_specs=pl.BlockSpec((tb, D_MODEL), lambda b, p, c: (b, 0)),
        scratch_shapes=[pltpu.VMEM((tb, D_MODEL), BF16), pltpu.VMEM((tb, D_MODEL), F32)])
    return pl.pallas_call(
        functools.partial(_moe_kernel, tile_rows=tile_rows),
        grid_spec=grid_spec,
        out_shape=jax.ShapeDtypeStruct((S, D_MODEL), F32),
        compiler_params=pltpu.CompilerParams(dimension_semantics=("arbitrary", "arbitrary"),
                                             vmem_limit_bytes=V7X_VMEM_LIMIT),
        name="moe",
    )(cnt, x2, rank_t, gates_t, g2, wg, wu, wd, gf)


def kernel(x, positions, norm1_gain, w_in, hg_lower_bounds, hg_norm_gain, diff_lambda,
           diff_subln_gain, w_out, norm2_gain, router_group_w, router_group_b,
           router_expert_w, router_expert_b, moe_w_gate, moe_w_up, moe_w_down,
           final_norm_gain):
    B, S, _ = x.shape
    assert B == 1 and norm1_gain.shape[0] == 1
    layer = 0
    xs = x.reshape(S, D_MODEL)
    pos = positions.reshape(S, 1).astype(F32)

    lb_cum = jnp.cumsum(jax.nn.softmax(hg_lower_bounds.astype(F32), axis=1), axis=1)
    lb = lb_cum[:, layer + 1] - lb_cum[:, 0]
    lam_p = diff_lambda[layer].astype(F32)
    lam = (jnp.exp(jnp.sum(lam_p[0] * lam_p[1])) - jnp.exp(jnp.sum(lam_p[2] * lam_p[3]))
           + LAMBDA_INIT).reshape(1)
    lane = jnp.arange(HEAD_W)
    inv_freq = jnp.float32(ROPE_THETA) ** (-jnp.arange(0, ROT_DIM, 2, dtype=F32) / ROT_DIM)
    in_rot = (lane % QK_DIM) < ROT_DIM
    invf = jnp.where(in_rot, inv_freq[lane % (ROT_DIM // 2)], 0.0).reshape(1, HEAD_W)
    sgn = jnp.where(in_rot, jnp.where((lane % QK_DIM) < ROT_DIM // 2, -1.0, 1.0),
                    0.0).astype(F32).reshape(1, HEAD_W)
    wr = jnp.zeros((D_MODEL, ROUTER_LANES), F32)
    wr = wr.at[:, :N_EXPERTS].set(router_expert_w[layer])
    wr = wr.at[:, N_EXPERTS:N_EXPERTS + N_GROUPS].set(router_group_w[layer])
    wr_hi = wr.astype(BF16)
    wr = jnp.concatenate([wr_hi, (wr - wr_hi.astype(F32)).astype(BF16)], axis=1)
    br = jnp.zeros((1, ROUTER_LANES), F32)
    br = br.at[0, :N_EXPERTS].set(router_expert_b[layer])
    br = br.at[0, N_EXPERTS:N_EXPERTS + N_GROUPS].set(router_group_b[layer])

    hq, kf, kb, lff, lfb, hi, hg, dq, dk, dv = _inproj(
        xs, pos, norm1_gain[layer].reshape(1, -1), w_in[layer].astype(BF16), lb, invf, sgn)
    o_f, wg, wu = _hgrn(hq, kf, lff, hi, False, (moe_w_gate[layer], moe_w_up[layer]))
    o_b, wd = _hgrn(hq, kb, lfb, hi, True, (moe_w_down[layer],))
    o_da = _attn(lam, dq, dk, dv, diff_subln_gain[layer].reshape(1, -1))
    x2, rank_t, gates_t, cnt = _outproj(
        xs, o_f, o_b, hg, o_da, hg_norm_gain[layer].reshape(1, -1), w_out[layer].astype(BF16),
        norm2_gain[layer].reshape(1, -1), wr, br, tb=MOE_TOKEN_BLOCK)
    out = _moe(cnt, x2, rank_t, gates_t, norm2_gain[layer].reshape(1, -1), wg, wu, wd,
               final_norm_gain.reshape(1, -1), tb=MOE_TOKEN_BLOCK)
    return out.reshape(B, S, D_MODEL)
```

```python
import functools
import math

import jax
import jax.numpy as jnp
from jax import lax
from jax.experimental import pallas as pl
from jax.experimental.pallas import tpu as pltpu

F32 = jnp.float32
BF16 = jnp.bfloat16

D_MODEL = 1024
HEAD_W = 128
N_HEADS = 4
HALF_W = N_HEADS * HEAD_W
QK_DIM = 64
ROT_DIM = 16
ROPE_THETA = 500000.0
QUERY_GROUP = 512
VT_ROWS = HEAD_W + 16
HG_CHUNK = 128
HG_UNROLL = 8
N_GROUPS = 4
EXPERTS_PER_GROUP = 8
N_EXPERTS = 32
D_EXPERT = 512
NORM_EPS = 1e-6
SUBLN_EPS = 1e-5
LAMBDA_INIT = 0.8 - 0.6 * math.exp(-0.3 * 0)
LOG2E = 1.4426950408889634
ROUTER_LANES = 128
MOE_TOKEN_BLOCK = 1024
V7X_VMEM_LIMIT = 56 * 1024 * 1024

NT_DIMS = (((1,), (1,)), ((), ()))
TN_DIMS = (((0,), (0,)), ((), ()))


def _rms(x, eps):
    return x * lax.rsqrt(jnp.mean(x * x, axis=-1, keepdims=True) + eps)


def _inproj_kernel(x_ref, pos_ref, g1_ref, w_ref, lb_ref, invf_ref, sgn_ref,
                   hq_ref, kf_ref, kb_ref, lff_ref, lfb_ref, hi_ref, hg_ref,
                   dq_ref, dk_ref, dv_ref):
    x = x_ref[...]
    h = (_rms(x, NORM_EPS) * g1_ref[...]).astype(BF16)

    def proj(i):
        return jnp.dot(h, w_ref[:, i * HALF_W:(i + 1) * HALF_W], preferred_element_type=F32)

    hq_ref[...] = proj(0)
    for i, (k_ref, lf_ref) in enumerate(((kf_ref, lff_ref), (kb_ref, lfb_ref))):
        z = proj(1 + i)
        lb = lb_ref[i:i + 1, :]
        sg = jax.nn.sigmoid(z)
        lf_ref[...] = jnp.log(lb + (1.0 - lb) * sg)
        k_ref[...] = (1.0 - lb) * (1.0 - sg)
    hi_ref[...] = proj(3)
    hg_ref[...] = proj(4)

    ang = pos_ref[...] * invf_ref[...]
    cos = jnp.concatenate([jnp.cos(ang)] * N_HEADS, axis=1)
    sgn = sgn_ref[...]
    sin = jnp.concatenate([jnp.sin(ang) * sgn] * N_HEADS, axis=1)
    take_hi = jnp.concatenate([jnp.broadcast_to(sgn, ang.shape)] * N_HEADS, axis=1) < 0.0

    def rotary(t):
        partner = jnp.where(take_hi, pltpu.roll(t, HALF_W - ROT_DIM // 2, axis=1),
                            pltpu.roll(t, ROT_DIM // 2, axis=1))
        return t * cos + partner * sin

    dq_ref[...] = (rotary(proj(5)) * (QK_DIM ** -0.5 * LOG2E)).astype(BF16)
    dk_ref[...] = rotary(proj(6)).astype(BF16)
    v = proj(7)
    sub = lax.broadcasted_iota(jnp.int32, (VT_ROWS - HEAD_W, v.shape[0]), 0)
    ones_rows = jnp.where(sub == 0, 1.0, 0.0)
    pieces = []
    for hd in range(N_HEADS):
        pieces += [v[:, hd * HEAD_W:(hd + 1) * HEAD_W].T, ones_rows]
    dv_ref[...] = jnp.concatenate(pieces, axis=0).astype(BF16)


def _inproj(x, pos, g1, w_in, lb, invf, sgn, tm=512):
    S = x.shape[0]
    row = lambda w: pl.BlockSpec((tm, w), lambda i: (i, 0))
    full = lambda a: pl.BlockSpec(a.shape, lambda i: (0, 0))
    f32o = jax.ShapeDtypeStruct((S, HALF_W), F32)
    bf16o = jax.ShapeDtypeStruct((S, HALF_W), BF16)
    return pl.pallas_call(
        _inproj_kernel,
        grid=(S // tm,),
        in_specs=[row(D_MODEL), row(1), full(g1), full(w_in), full(lb), full(invf), full(sgn)],
        out_specs=[row(HALF_W)] * 9 + [pl.BlockSpec((N_HEADS * VT_ROWS, tm), lambda i: (0, i))],
        out_shape=[f32o] * 7 + [bf16o, bf16o,
                                jax.ShapeDtypeStruct((N_HEADS * VT_ROWS, S), BF16)],
        compiler_params=pltpu.CompilerParams(dimension_semantics=("arbitrary",),
                                             vmem_limit_bytes=V7X_VMEM_LIMIT),
        name="inproj",
    )(x, pos, g1, w_in, lb, invf, sgn)


def _chunk_cumsum(x, rev):
    n = x.shape[0]
    row = lax.broadcasted_iota(jnp.int32, x.shape, 0)
    d = 1
    while d < n:
        if rev:
            x = x + jnp.where(row < n - d, pltpu.roll(x, n - d, axis=0), 0.0)
        else:
            x = x + jnp.where(row >= d, pltpu.roll(x, d, axis=0), 0.0)
        d *= 2
    return x


def _pivot_rows(b, blk, rev):
    n = b.shape[0]
    half = blk // 2
    groups = []
    for g0 in range(0, n, 8):
        def piv(r):
            base = (r // blk) * blk
            return base + half if rev else base + half - 1
        if blk >= 8:
            p = piv(g0)
            groups.append(jnp.broadcast_to(b[p:p + 1, :], (8, b.shape[1])))
        else:
            sub = lax.broadcasted_iota(jnp.int32, (8, b.shape[1]), 0)
            acc = None
            for s0 in range(0, 8, blk):
                p = piv(g0 + s0)
                rowv = jnp.broadcast_to(b[p:p + 1, :], (8, b.shape[1]))
                acc = rowv if acc is None else jnp.where(sub >= s0, rowv, acc)
            groups.append(acc)
    return jnp.concatenate(groups, axis=0)


def _hgrn_kernel(q_ref, k_ref, lf_ref, v_ref, *rest, rev, n_chunks, n_cast):
    w_refs, o_ref, wb_refs, st_ref = (rest[:n_cast], rest[n_cast], rest[n_cast + 1:-1], rest[-1])
    for w_ref, wb_ref in zip(w_refs, wb_refs):
        wb_ref[...] = w_ref[...].astype(BF16)

    @pl.when(pl.program_id(1) == 0)
    def _():
        st_ref[...] = jnp.zeros_like(st_ref)

    C = HG_CHUNK
    rt = lax.broadcasted_iota(jnp.int32, (C, C), 0)
    ct = lax.broadcasted_iota(jnp.int32, (C, C), 1)
    rowi = lax.broadcasted_iota(jnp.int32, (C, HEAD_W), 0)
    eye = jnp.where(rt == ct, 1.0, 0.0)
    levels = []
    blk = C
    while blk >= 2:
        half = blk // 2
        q_side = (lambda r: (r % blk) < half) if rev else (lambda r: (r % blk) >= half)
        pair = ((rt // blk) == (ct // blk)) & q_side(rt) & jnp.logical_not(q_side(ct))
        levels.append((blk, jnp.where(q_side(rowi), 1.0, 0.0), jnp.where(pair, 1.0, 0.0)))
        blk = half

    def chunk(ci, st):
        c = (n_chunks - 1 - ci) if rev else ci
        sl = pl.ds(pl.multiple_of(c * C, C), C)
        q = q_ref[sl, :]
        k = k_ref[sl, :]
        v = v_ref[sl, :]
        vb = v.astype(BF16)
        b = _chunk_cumsum(lf_ref[sl, :], rev)
        tot = b[0:1, :] if rev else b[C - 1:C, :]

        o = lax.dot_general((q * jnp.exp(b)).astype(BF16), st.astype(BF16), NT_DIMS,
                            preferred_element_type=F32)
        kdec = (k * jnp.exp(tot - b)).astype(BF16)
        st_next = st * jnp.exp(tot) + lax.dot_general(vb, kdec, TN_DIMS,
                                                      preferred_element_type=F32)

        scores = eye * jnp.sum(q * k, axis=1, keepdims=True)
        for blk, q_side, pair in levels:
            w = jnp.exp2(jnp.abs(b - _pivot_rows(b, blk, rev)) * (-LOG2E))
            z = (jnp.where(q_side > 0.5, q, k) * w).astype(BF16)
            g = lax.dot_general(z, z, NT_DIMS, preferred_element_type=F32)
            scores = scores + pair * g
        o_ref[sl, :] = o + jnp.dot(scores.astype(BF16), vb, preferred_element_type=F32)
        return st_next

    st_ref[...] = lax.fori_loop(0, n_chunks, chunk, st_ref[...], unroll=HG_UNROLL)


def _hgrn(q, k, lf, v, rev, weights, rb=2048):
    S = q.shape[0]
    nb = S // rb
    steps = N_HEADS * nb
    idx = (lambda h, j: (nb - 1 - j, h)) if rev else (lambda h, j: (j, h))
    spec = pl.BlockSpec((rb, HEAD_W), idx)
    w_specs = []
    for w in weights:
        n_exp, rows, cols = w.shape
        per_exp = steps // n_exp
        assert per_exp * n_exp == steps and rows % per_exp == 0
        w_specs.append(pl.BlockSpec(
            (1, rows // per_exp, cols),
            lambda h, j, per_exp=per_exp: ((h * nb + j) // per_exp, (h * nb + j) % per_exp, 0)))
    return pl.pallas_call(
        functools.partial(_hgrn_kernel, rev=rev, n_chunks=rb // HG_CHUNK, n_cast=len(weights)),
        grid=(N_HEADS, nb),
        in_specs=[spec] * 4 + w_specs,
        out_specs=[spec] + w_specs,
        out_shape=[jax.ShapeDtypeStruct((S, HALF_W), F32)]
        + [jax.ShapeDtypeStruct(w.shape, BF16) for w in weights],
        scratch_shapes=[pltpu.VMEM((HEAD_W, HEAD_W), F32)],
        compiler_params=pltpu.CompilerParams(dimension_semantics=("arbitrary", "arbitrary"),
                                             vmem_limit_bytes=V7X_VMEM_LIMIT),
        name="hgrn_bwd" if rev else "hgrn_fwd",
    )(q, k, lf, v, *weights)


def _attn_kernel(lam_ref, q_ref, k_ref, vt_ref, gain_ref, o_ref, sa_ref, sb_ref, m_ref, acc_ref,
                 *, tk):
    n_kv = k_ref.shape[0] // tk
    tq = q_ref.shape[0]
    assert n_kv % 2 == 0 and tq % QUERY_GROUP == 0
    q = q_ref[...]
    lane = lax.broadcasted_iota(jnp.int32, q.shape, 1)
    qm = (jnp.where(lane < QK_DIM, q, jnp.zeros_like(q)),
          jnp.where(lane >= QK_DIM, q, jnp.zeros_like(q)))
    m_ref[...] = jnp.full(m_ref.shape, -jnp.inf, F32)
    acc_ref[...] = jnp.zeros_like(acc_ref)

    def chunk(j):
        return pl.ds(pl.multiple_of(j * tk, tk), tk)

    def softmax_pv(c, cols, s, vt):
        m_old = m_ref[c, :, cols]
        m_new = jnp.maximum(m_old, jnp.max(s, axis=0, keepdims=True))
        p = jnp.exp2(s - m_new).astype(BF16)
        acc_ref[c, :, cols] = jnp.exp2(m_old - m_new) * acc_ref[c, :, cols] + jnp.dot(
            vt, p, preferred_element_type=F32)
        m_ref[c, :, cols] = m_new

    def step(js, dst_ref, jp, src_ref):
        kc = None if dst_ref is None else k_ref[chunk(js), :]
        vt = None if src_ref is None else vt_ref[:, chunk(jp)]
        for c in range(2):
            for g in range(0, tq, QUERY_GROUP):
                cols = slice(g, g + QUERY_GROUP)
                if dst_ref is not None:
                    dst_ref[c, :, cols] = lax.dot_general(kc, qm[c][cols, :], NT_DIMS,
                                                          preferred_element_type=F32)
                if src_ref is not None:
                    softmax_pv(c, cols, src_ref[c, :, cols], vt)

    step(0, sa_ref, None, None)

    def two_steps(i, carry):
        step(2 * i + 1, sb_ref, 2 * i, sa_ref)
        step(2 * i + 2, sa_ref, 2 * i + 1, sb_ref)
        return carry

    lax.fori_loop(0, n_kv // 2 - 1, two_steps, 0)
    step(n_kv - 1, sb_ref, n_kv - 2, sa_ref)
    step(None, None, n_kv - 1, sb_ref)

    outs = []
    for c in range(2):
        a = acc_ref[c]
        outs.append(a[:HEAD_W, :] / a[HEAD_W:HEAD_W + 1, :])
    o = (outs[0] - lam_ref[0] * outs[1]).T
    o_ref[...] = _rms(o, SUBLN_EPS) * gain_ref[...] * (1.0 - LAMBDA_INIT)


def _attn(lam, dq, dk, dvt, gain, tq=2048, tk=512):
    S = dq.shape[0]
    return pl.pallas_call(
        functools.partial(_attn_kernel, tk=tk),
        grid=(N_HEADS, S // tq),
        in_specs=[pl.BlockSpec(memory_space=pltpu.SMEM),
                  pl.BlockSpec((tq, HEAD_W), lambda h, i: (i, h)),
                  pl.BlockSpec((S, HEAD_W), lambda h, i: (0, h)),
                  pl.BlockSpec((VT_ROWS, S), lambda h, i: (h, 0)),
                  pl.BlockSpec((1, HEAD_W), lambda h, i: (0, 0))],
        out_specs=pl.BlockSpec((tq, HEAD_W), lambda h, i: (i, h)),
        out_shape=jax.ShapeDtypeStruct((S, HALF_W), F32),
        scratch_shapes=[pltpu.VMEM((2, tk, tq), F32), pltpu.VMEM((2, tk, tq), F32),
                        pltpu.VMEM((2, 1, tq), F32), pltpu.VMEM((2, VT_ROWS, tq), F32)],
        compiler_params=pltpu.CompilerParams(dimension_semantics=("arbitrary", "arbitrary"),
                                             vmem_limit_bytes=V7X_VMEM_LIMIT),
        name="diff_attn",
    )(lam, dq, dk, dvt, gain)


def _outproj_kernel(x_ref, of_ref, ob_ref, hg_ref, oda_ref, hgain_ref, wo_ref, g2_ref,
                    wr_ref, br_ref, x2_ref, rank_t_ref, gates_t_ref, cnt_ref,
                    carry_ref, *, sub_blocks):
    i = pl.program_id(0)
    o = of_ref[...] + ob_ref[...]
    gate = hg_ref[...]
    gate = gate * jax.nn.sigmoid(gate)
    parts = []
    for hd in range(N_HEADS):
        sl = slice(hd * HEAD_W, (hd + 1) * HEAD_W)
        parts.append(_rms(o[:, sl], NORM_EPS) * hgain_ref[...] * gate[:, sl])
    parts.append(oda_ref[...])
    mixed = jnp.concatenate(parts, axis=1).astype(BF16)
    x2 = x_ref[...] + jnp.dot(mixed, wo_ref[...], preferred_element_type=F32)
    x2_ref[...] = x2

    h2 = _rms(x2, NORM_EPS) * g2_ref[...]
    h_hi = h2.astype(BF16)
    h_lo = (h2 - h_hi.astype(F32)).astype(BF16)
    t = jnp.dot(h_hi, wr_ref[...], preferred_element_type=F32)
    logits = (t[:, :ROUTER_LANES] + t[:, ROUTER_LANES:] + br_ref[...]
              + jnp.dot(h_lo, wr_ref[:, :ROUTER_LANES], preferred_element_type=F32))
    lane = lax.broadcasted_iota(jnp.int32, logits.shape, 1)
    neg = -jnp.inf
    big = ROUTER_LANES

    def first_max(vals):
        mx = jnp.max(vals, axis=1, keepdims=True)
        idx = jnp.min(jnp.where(vals == mx, lane, big), axis=1, keepdims=True)
        return mx, idx

    gl = jnp.where((lane >= N_EXPERTS) & (lane < N_EXPERTS + N_GROUPS), logits, neg)
    gm, gidx = first_max(gl)
    g_p = 1.0 / jnp.sum(jnp.exp(gl - gm), axis=1, keepdims=True)
    el = jnp.where((lane < N_EXPERTS) & ((lane // EXPERTS_PER_GROUP) == (gidx - N_EXPERTS)),
                   logits, neg)
    em1, idx1 = first_max(el)
    em2, idx2 = first_max(jnp.where(lane == idx1, neg, el))
    e2 = jnp.exp(em2 - em1)
    w1 = g_p / (1.0 + e2)
    w2 = g_p * e2 / (1.0 + e2)
    is1 = lane == idx1
    is2 = lane == idx2
    gates = jnp.where(is1, w1, 0.0) + jnp.where(is2, w2, 0.0)

    @pl.when(i % sub_blocks == 0)
    def _():
        carry_ref[...] = jnp.zeros_like(carry_ref)

    assigned = jnp.where(is1 | is2, 1.0, 0.0)
    tm = assigned.shape[0]
    tri = jnp.where(lax.broadcasted_iota(jnp.int32, (tm, tm), 0)
                    >= lax.broadcasted_iota(jnp.int32, (tm, tm), 1), 1.0, 0.0).astype(BF16)
    cum = jnp.dot(tri, assigned.astype(BF16), preferred_element_type=F32)
    carry = carry_ref[...]
    rank = jnp.where(assigned > 0.0, carry + cum - 1.0, -1.0)
    carry = carry + cum[tm - 1:tm, :]
    carry_ref[...] = carry
    cnt_ref[pl.ds(i // sub_blocks, 1), :] = carry.astype(jnp.int32)
    rank_t_ref[...] = rank.T
    gates_t_ref[...] = gates.T


def _outproj(x, o_f, o_b, hg, o_da, hgain, w_out, g2, wr, br, tb, tm=512):
    S = x.shape[0]
    row = lambda w: pl.BlockSpec((tm, w), lambda i: (i, 0))
    col = pl.BlockSpec((ROUTER_LANES, tm), lambda i: (0, i))
    full = lambda a: pl.BlockSpec(a.shape, lambda i: (0, 0))
    return pl.pallas_call(
        functools.partial(_outproj_kernel, sub_blocks=tb // tm),
        grid=(S // tm,),
        in_specs=[row(D_MODEL), row(HALF_W), row(HALF_W), row(HALF_W), row(HALF_W),
                  full(hgain), full(w_out), full(g2), full(wr), full(br)],
        out_specs=[row(D_MODEL), col, col,
                   pl.BlockSpec((S // tb, ROUTER_LANES), lambda i: (0, 0))],
        out_shape=[jax.ShapeDtypeStruct((S, D_MODEL), F32),
                   jax.ShapeDtypeStruct((ROUTER_LANES, S), F32),
                   jax.ShapeDtypeStruct((ROUTER_LANES, S), F32),
                   jax.ShapeDtypeStruct((S // tb, ROUTER_LANES), jnp.int32)],
        scratch_shapes=[pltpu.VMEM((1, ROUTER_LANES), F32)],
        compiler_params=pltpu.CompilerParams(dimension_semantics=("arbitrary",),
                                             vmem_limit_bytes=V7X_VMEM_LIMIT),
        name="outproj_router",
    )(x, o_f, o_b, hg, o_da, hgain, w_out, g2, wr, br)


def _moe_kernel(cnt_ref, x2_ref, rank_t_ref, gates_t_ref, g2_ref, wg_ref, wu_ref,
                wd_ref, gf_ref, out_ref, h2_ref, *, tile_rows, tb):
    b = pl.program_id(0)
    p = pl.program_id(1)
    R = tile_rows
    n_exp = wg_ref.shape[0]
    n_sub = x2_ref.shape[0] // tb

    @pl.when(p == 0)
    def _():
        x2 = x2_ref[...]
        h2_ref[...] = (_rms(x2, NORM_EPS) * g2_ref[...]).astype(BF16)
        out_ref[...] = x2

    experts = [n_exp * p + i for i in range(n_exp)]
    subs = [slice(s * tb, (s + 1) * tb) for s in range(n_sub)]
    rank_row = [[rank_t_ref[pl.ds(e, 1), sub] for e in experts] for sub in subs]
    gate_row = [[gates_t_ref[pl.ds(e, 1), sub] for e in experts] for sub in subs]
    n_max = jnp.int32(0)
    for s in range(n_sub):
        for e in experts:
            n_max = jnp.maximum(n_max, cnt_ref[n_sub * b + s, e])

    def tile(t, carry):
        base = (t * R).astype(F32)
        slot_col = lax.broadcasted_iota(jnp.int32, (R, 1), 0).astype(F32) + base
        sel, gather, xg = [], [], []
        for s, sub in enumerate(subs):
            sel.append([rank_row[s][i] == slot_col for i in range(n_exp)])
            gather.append(jnp.concatenate(
                [jnp.where(m, 1.0, 0.0).astype(BF16) for m in sel[s]], axis=0))
            xg.append(jnp.dot(gather[s], h2_ref[sub, :],
                              preferred_element_type=F32).astype(BF16))
        ys = [[] for _ in subs]
        for i in range(n_exp):
            xi = jnp.concatenate([xg[s][i * R:(i + 1) * R] for s in range(n_sub)], axis=0)
            a = jnp.dot(xi, wg_ref[i], preferred_element_type=F32)
            a = a * jax.nn.sigmoid(a) * jnp.dot(xi, wu_ref[i], preferred_element_type=F32)
            y = jnp.dot(a.astype(BF16), wd_ref[i], preferred_element_type=F32)
            for s in range(n_sub):
                g = jnp.sum(jnp.where(sel[s][i], gate_row[s][i], 0.0), axis=1, keepdims=True)
                ys[s].append((y[s * R:(s + 1) * R] * g).astype(BF16))
        for s, sub in enumerate(subs):
            out_ref[sub, :] += lax.dot_general(gather[s], jnp.concatenate(ys[s], axis=0),
                                               TN_DIMS, preferred_element_type=F32)
        return carry

    lax.fori_loop(0, (n_max + R - 1) // R, tile, 0)

    @pl.when(p == pl.num_programs(1) - 1)
    def _():
        out_ref[...] = _rms(out_ref[...], NORM_EPS) * gf_ref[...]


def _moe(cnt, x2, rank_t, gates_t, g2, wg, wu, wd, gf, tb, tile_rows=96,
         experts_per_step=2, blocks_per_step=2):
    S = x2.shape[0]
    eps = experts_per_step
    rows = tb * blocks_per_step
    grid_spec = pltpu.PrefetchScalarGridSpec(
        num_scalar_prefetch=1,
        grid=(S // rows, N_EXPERTS // eps),
        in_specs=[pl.BlockSpec((rows, D_MODEL), lambda b, p, c: (b, 0)),
                  pl.BlockSpec((N_EXPERTS, rows), lambda b, p, c: (0, b)),
                  pl.BlockSpec((N_EXPERTS, rows), lambda b, p, c: (0, b)),
                  pl.BlockSpec((1, D_MODEL), lambda b, p, c: (0, 0)),
                  pl.BlockSpec((eps, D_MODEL, D_EXPERT), lambda b, p, c: (p, 0, 0)),
                  pl.BlockSpec((eps, D_MODEL, D_EXPERT), lambda b, p, c: (p, 0, 0)),
                  pl.BlockSpec((eps, D_EXPERT, D_MODEL), lambda b, p, c: (p, 0, 0)),
                  pl.BlockSpec((1, D_MODEL), lambda b, p, c: (0, 0))],
        out_specs=pl.BlockSpec((rows, D_MODEL), lambda b, p, c: (b, 0)),
        scratch_shapes=[pltpu.VMEM((rows, D_MODEL), BF16)])
    return pl.pallas_call(
        functools.partial(_moe_kernel, tile_rows=tile_rows, tb=tb),
        grid_spec=grid_spec,
        out_shape=jax.ShapeDtypeStruct((S, D_MODEL), F32),
        compiler_params=pltpu.CompilerParams(dimension_semantics=("arbitrary", "arbitrary"),
                                             vmem_limit_bytes=V7X_VMEM_LIMIT),
        name="moe",
    )(cnt, x2, rank_t, gates_t, g2, wg, wu, wd, gf)


def kernel(x, positions, norm1_gain, w_in, hg_lower_bounds, hg_norm_gain, diff_lambda,
           diff_subln_gain, w_out, norm2_gain, router_group_w, router_group_b,
           router_expert_w, router_expert_b, moe_w_gate, moe_w_up, moe_w_down,
           final_norm_gain):
    B, S, _ = x.shape
    assert B == 1 and norm1_gain.shape[0] == 1
    layer = 0
    xs = x.reshape(S, D_MODEL)
    pos = positions.reshape(S, 1).astype(F32)

    lb_cum = jnp.cumsum(jax.nn.softmax(hg_lower_bounds.astype(F32), axis=1), axis=1)
    lb = lb_cum[:, layer + 1] - lb_cum[:, 0]
    lam_p = diff_lambda[layer].astype(F32)
    lam = (jnp.exp(jnp.sum(lam_p[0] * lam_p[1])) - jnp.exp(jnp.sum(lam_p[2] * lam_p[3]))
           + LAMBDA_INIT).reshape(1)
    lane = jnp.arange(HEAD_W)
    inv_freq = jnp.float32(ROPE_THETA) ** (-jnp.arange(0, ROT_DIM, 2, dtype=F32) / ROT_DIM)
    in_rot = (lane % QK_DIM) < ROT_DIM
    invf = jnp.where(in_rot, inv_freq[lane % (ROT_DIM // 2)], 0.0).reshape(1, HEAD_W)
    sgn = jnp.where(in_rot, jnp.where((lane % QK_DIM) < ROT_DIM // 2, -1.0, 1.0),
                    0.0).astype(F32).reshape(1, HEAD_W)
    wr = jnp.zeros((D_MODEL, ROUTER_LANES), F32)
    wr = wr.at[:, :N_EXPERTS].set(router_expert_w[layer])
    wr = wr.at[:, N_EXPERTS:N_EXPERTS + N_GROUPS].set(router_group_w[layer])
    wr_hi = wr.astype(BF16)
    wr = jnp.concatenate([wr_hi, (wr - wr_hi.astype(F32)).astype(BF16)], axis=1)
    br = jnp.zeros((1, ROUTER_LANES), F32)
    br = br.at[0, :N_EXPERTS].set(router_expert_b[layer])
    br = br.at[0, N_EXPERTS:N_EXPERTS + N_GROUPS].set(router_group_b[layer])

    hq, kf, kb, lff, lfb, hi, hg, dq, dk, dv = _inproj(
        xs, pos, norm1_gain[layer].reshape(1, -1), w_in[layer].astype(BF16), lb, invf, sgn)
    o_f, wg, wu = _hgrn(hq, kf, lff, hi, False, (moe_w_gate[layer], moe_w_up[layer]))
    o_b, wd = _hgrn(hq, kb, lfb, hi, True, (moe_w_down[layer],))
    o_da = _attn(lam, dq, dk, dv, diff_subln_gain[layer].reshape(1, -1))
    x2, rank_t, gates_t, cnt = _outproj(
        xs, o_f, o_b, hg, o_da, hg_norm_gain[layer].reshape(1, -1), w_out[layer].astype(BF16),
        norm2_gain[layer].reshape(1, -1), wr, br, tb=MOE_TOKEN_BLOCK)
    out = _moe(cnt, x2, rank_t, gates_t, norm2_gain[layer].reshape(1, -1), wg, wu, wd,
               final_norm_gain.reshape(1, -1), tb=MOE_TOKEN_BLOCK)
    return out.reshape(B, S, D_MODEL)
```

```python
import functools
import math

import jax
import jax.numpy as jnp
from jax import lax
from jax.experimental import pallas as pl
from jax.experimental.pallas import tpu as pltpu

F32 = jnp.float32
BF16 = jnp.bfloat16

D_MODEL = 1024
HEAD_W = 128
N_HEADS = 4
HALF_W = N_HEADS * HEAD_W
QK_DIM = 64
ROT_DIM = 16
ROPE_THETA = 500000.0
QUERY_GROUP = 512
VT_ROWS = HEAD_W + 16
HG_CHUNK = 128
HG_UNROLL = 8
N_GROUPS = 4
EXPERTS_PER_GROUP = 8
N_EXPERTS = 32
D_EXPERT = 512
NORM_EPS = 1e-6
SUBLN_EPS = 1e-5
LAMBDA_INIT = 0.8 - 0.6 * math.exp(-0.3 * 0)
LOG2E = 1.4426950408889634
ROUTER_LANES = 128
MOE_TOKEN_BLOCK = 1024
V7X_VMEM_LIMIT = 56 * 1024 * 1024

NT_DIMS = (((1,), (1,)), ((), ()))
TN_DIMS = (((0,), (0,)), ((), ()))


def _rms(x, eps):
    return x * lax.rsqrt(jnp.mean(x * x, axis=-1, keepdims=True) + eps)


def _inproj_kernel(x_ref, pos_ref, g1_ref, w_ref, lb_ref, invf_ref, sgn_ref, pcos_ref, psin_ref,
                   hq_ref, kf_ref, kb_ref, lff_ref, lfb_ref, hi_ref, hg_ref,
                   dq_ref, dk_ref, dv_ref):
    x = x_ref[...]
    h = (_rms(x, NORM_EPS) * g1_ref[...]).astype(BF16)

    def proj(i):
        return jnp.dot(h, w_ref[:, i * HALF_W:(i + 1) * HALF_W], preferred_element_type=F32)

    hq_ref[...] = proj(0)
    for i, (k_ref, lf_ref) in enumerate(((kf_ref, lff_ref), (kb_ref, lfb_ref))):
        z = proj(1 + i)
        lb = lb_ref[i:i + 1, :]
        sg = jax.nn.sigmoid(z)
        lf_ref[...] = jnp.log(lb + (1.0 - lb) * sg)
        k_ref[...] = (1.0 - lb) * (1.0 - sg)
    hi_ref[...] = proj(3)
    hg_ref[...] = proj(4)

    ang_t = invf_ref[...] * pos_ref[...]

    def spread(t, place_ref):
        hi = t.astype(BF16)
        r1 = t - hi.astype(F32)
        mid = r1.astype(BF16)
        lo = (r1 - mid.astype(F32)).astype(BF16)
        pieces = jnp.concatenate([hi, mid, lo, jnp.zeros_like(hi)], axis=0)
        return lax.dot_general(pieces, place_ref[...], TN_DIMS, preferred_element_type=F32)

    sgn = sgn_ref[...]
    cos = jnp.concatenate([spread(jnp.cos(ang_t), pcos_ref) + jnp.where(sgn == 0.0, 1.0, 0.0)]
                          * N_HEADS, axis=1)
    sin = jnp.concatenate([spread(jnp.sin(ang_t), psin_ref)] * N_HEADS, axis=1)
    take_hi = jnp.concatenate([jnp.broadcast_to(sgn, (x.shape[0], HEAD_W))] * N_HEADS,
                              axis=1) < 0.0

    def rotary(t):
        partner = jnp.where(take_hi, pltpu.roll(t, HALF_W - ROT_DIM // 2, axis=1),
                            pltpu.roll(t, ROT_DIM // 2, axis=1))
        return t * cos + partner * sin

    dq_ref[...] = (rotary(proj(5)) * (QK_DIM ** -0.5 * LOG2E)).astype(BF16)
    dk_ref[...] = rotary(proj(6)).astype(BF16)
    v = proj(7)
    sub = lax.broadcasted_iota(jnp.int32, (VT_ROWS - HEAD_W, v.shape[0]), 0)
    ones_rows = jnp.where(sub == 0, 1.0, 0.0)
    pieces = []
    for hd in range(N_HEADS):
        pieces += [v[:, hd * HEAD_W:(hd + 1) * HEAD_W].T, ones_rows]
    dv_ref[...] = jnp.concatenate(pieces, axis=0).astype(BF16)


def _inproj(x, pos, g1, w_in, lb, invf, sgn, pcos, psin, tm=512):
    S = x.shape[0]
    row = lambda w: pl.BlockSpec((tm, w), lambda i: (i, 0))
    full = lambda a: pl.BlockSpec(a.shape, lambda i: (0, 0))
    f32o = jax.ShapeDtypeStruct((S, HALF_W), F32)
    bf16o = jax.ShapeDtypeStruct((S, HALF_W), BF16)
    return pl.pallas_call(
        _inproj_kernel,
        grid=(S // tm,),
        in_specs=[row(D_MODEL), pl.BlockSpec((1, tm), lambda i: (0, i)), full(g1), full(w_in),
                  full(lb), full(invf), full(sgn), full(pcos), full(psin)],
        out_specs=[row(HALF_W)] * 9 + [pl.BlockSpec((N_HEADS * VT_ROWS, tm), lambda i: (0, i))],
        out_shape=[f32o] * 7 + [bf16o, bf16o,
                                jax.ShapeDtypeStruct((N_HEADS * VT_ROWS, S), BF16)],
        compiler_params=pltpu.CompilerParams(dimension_semantics=("arbitrary",),
                                             vmem_limit_bytes=V7X_VMEM_LIMIT),
        name="inproj",
    )(x, pos, g1, w_in, lb, invf, sgn, pcos, psin)


def _chunk_cumsum(x, rev):
    n = x.shape[0]
    row = lax.broadcasted_iota(jnp.int32, x.shape, 0)
    d = 1
    while d < n:
        if rev:
            x = x + jnp.where(row < n - d, pltpu.roll(x, n - d, axis=0), 0.0)
        else:
            x = x + jnp.where(row >= d, pltpu.roll(x, d, axis=0), 0.0)
        d *= 2
    return x


def _pivot_rows(b, blk, rev):
    n = b.shape[0]
    half = blk // 2
    groups = []
    for g0 in range(0, n, 8):
        def piv(r):
            base = (r // blk) * blk
            return base + half if rev else base + half - 1
        if blk >= 8:
            p = piv(g0)
            groups.append(jnp.broadcast_to(b[p:p + 1, :], (8, b.shape[1])))
        else:
            sub = lax.broadcasted_iota(jnp.int32, (8, b.shape[1]), 0)
            acc = None
            for s0 in range(0, 8, blk):
                p = piv(g0 + s0)
                rowv = jnp.broadcast_to(b[p:p + 1, :], (8, b.shape[1]))
                acc = rowv if acc is None else jnp.where(sub >= s0, rowv, acc)
            groups.append(acc)
    return jnp.concatenate(groups, axis=0)


def _hgrn_kernel(q_ref, k_ref, lf_ref, v_ref, *rest, rev, n_chunks, n_cast):
    w_refs, o_ref, wb_refs, st_ref = (rest[:n_cast], rest[n_cast], rest[n_cast + 1:-1], rest[-1])
    for w_ref, wb_ref in zip(w_refs, wb_refs):
        wb_ref[...] = w_ref[...].astype(BF16)

    @pl.when(pl.program_id(1) == 0)
    def _():
        st_ref[...] = jnp.zeros_like(st_ref)

    C = HG_CHUNK
    rt = lax.broadcasted_iota(jnp.int32, (C, C), 0)
    ct = lax.broadcasted_iota(jnp.int32, (C, C), 1)
    rowi = lax.broadcasted_iota(jnp.int32, (C, HEAD_W), 0)
    eye = jnp.where(rt == ct, 1.0, 0.0)
    levels = []
    blk = C
    while blk >= 2:
        half = blk // 2
        q_side = (lambda r: (r % blk) < half) if rev else (lambda r: (r % blk) >= half)
        pair = ((rt // blk) == (ct // blk)) & q_side(rt) & jnp.logical_not(q_side(ct))
        levels.append((blk, jnp.where(q_side(rowi), 1.0, 0.0), jnp.where(pair, 1.0, 0.0)))
        blk = half

    def chunk(ci, st):
        c = (n_chunks - 1 - ci) if rev else ci
        sl = pl.ds(pl.multiple_of(c * C, C), C)
        q = q_ref[sl, :]
        k = k_ref[sl, :]
        v = v_ref[sl, :]
        vb = v.astype(BF16)
        b = _chunk_cumsum(lf_ref[sl, :], rev)
        tot = b[0:1, :] if rev else b[C - 1:C, :]

        o = lax.dot_general((q * jnp.exp(b)).astype(BF16), st.astype(BF16), NT_DIMS,
                            preferred_element_type=F32)
        kdec = (k * jnp.exp(tot - b)).astype(BF16)
        st_next = st * jnp.exp(tot) + lax.dot_general(vb, kdec, TN_DIMS,
                                                      preferred_element_type=F32)

        scores = eye * jnp.sum(q * k, axis=1, keepdims=True)
        for blk, q_side, pair in levels:
            w = jnp.exp2(jnp.abs(b - _pivot_rows(b, blk, rev)) * (-LOG2E))
            z = (jnp.where(q_side > 0.5, q, k) * w).astype(BF16)
            g = lax.dot_general(z, z, NT_DIMS, preferred_element_type=F32)
            scores = scores + pair * g
        o_ref[sl, :] = o + jnp.dot(scores.astype(BF16), vb, preferred_element_type=F32)
        return st_next

    st_ref[...] = lax.fori_loop(0, n_chunks, chunk, st_ref[...], unroll=HG_UNROLL)


def _hgrn(q, k, lf, v, rev, weights, rb=2048):
    S = q.shape[0]
    nb = S // rb
    steps = N_HEADS * nb
    idx = (lambda h, j: (nb - 1 - j, h)) if rev else (lambda h, j: (j, h))
    spec = pl.BlockSpec((rb, HEAD_W), idx)
    w_specs = []
    for w in weights:
        n_exp, rows, cols = w.shape
        per_exp = steps // n_exp
        assert per_exp * n_exp == steps and rows % per_exp == 0
        w_specs.append(pl.BlockSpec(
            (1, rows // per_exp, cols),
            lambda h, j, per_exp=per_exp: ((h * nb + j) // per_exp, (h * nb + j) % per_exp, 0)))
    return pl.pallas_call(
        functools.partial(_hgrn_kernel, rev=rev, n_chunks=rb // HG_CHUNK, n_cast=len(weights)),
        grid=(N_HEADS, nb),
        in_specs=[spec] * 4 + w_specs,
        out_specs=[spec] + w_specs,
        out_shape=[jax.ShapeDtypeStruct((S, HALF_W), F32)]
        + [jax.ShapeDtypeStruct(w.shape, BF16) for w in weights],
        scratch_shapes=[pltpu.VMEM((HEAD_W, HEAD_W), F32)],
        compiler_params=pltpu.CompilerParams(dimension_semantics=("arbitrary", "arbitrary"),
                                             vmem_limit_bytes=V7X_VMEM_LIMIT),
        name="hgrn_bwd" if rev else "hgrn_fwd",
    )(q, k, lf, v, *weights)


def _attn_kernel(lam_ref, q_ref, k_ref, vt_ref, gain_ref, o_ref, sa_ref, sb_ref, m_ref, acc_ref,
                 *, tk):
    n_kv = k_ref.shape[0] // tk
    tq = q_ref.shape[0]
    assert n_kv % 2 == 0 and tq % QUERY_GROUP == 0
    q = q_ref[...]
    lane = lax.broadcasted_iota(jnp.int32, q.shape, 1)
    qm = (jnp.where(lane < QK_DIM, q, jnp.zeros_like(q)),
          jnp.where(lane >= QK_DIM, q, jnp.zeros_like(q)))
    m_ref[...] = jnp.full(m_ref.shape, -jnp.inf, F32)
    acc_ref[...] = jnp.zeros_like(acc_ref)

    def chunk(j):
        return pl.ds(pl.multiple_of(j * tk, tk), tk)

    def softmax_pv(c, cols, s, vt):
        m_old = m_ref[c, :, cols]
        m_new = jnp.maximum(m_old, jnp.max(s, axis=0, keepdims=True))
        p = jnp.exp2(s - m_new).astype(BF16)
        acc_ref[c, :, cols] = jnp.exp2(m_old - m_new) * acc_ref[c, :, cols] + jnp.dot(
            vt, p, preferred_element_type=F32)
        m_ref[c, :, cols] = m_new

    def step(js, dst_ref, jp, src_ref):
        kc = None if dst_ref is None else k_ref[chunk(js), :]
        vt = None if src_ref is None else vt_ref[:, chunk(jp)]
        for c in range(2):
            for g in range(0, tq, QUERY_GROUP):
                cols = slice(g, g + QUERY_GROUP)
                if dst_ref is not None:
                    dst_ref[c, :, cols] = lax.dot_general(kc, qm[c][cols, :], NT_DIMS,
                                                          preferred_element_type=F32)
                if src_ref is not None:
                    softmax_pv(c, cols, src_ref[c, :, cols], vt)

    step(0, sa_ref, None, None)

    def two_steps(i, carry):
        step(2 * i + 1, sb_ref, 2 * i, sa_ref)
        step(2 * i + 2, sa_ref, 2 * i + 1, sb_ref)
        return carry

    lax.fori_loop(0, n_kv // 2 - 1, two_steps, 0)
    step(n_kv - 1, sb_ref, n_kv - 2, sa_ref)
    step(None, None, n_kv - 1, sb_ref)

    outs = []
    for c in range(2):
        a = acc_ref[c]
        outs.append(a[:HEAD_W, :] / a[HEAD_W:HEAD_W + 1, :])
    o = (outs[0] - lam_ref[0] * outs[1]).T
    o_ref[...] = _rms(o, SUBLN_EPS) * gain_ref[...] * (1.0 - LAMBDA_INIT)


def _attn(lam, dq, dk, dvt, gain, tq=2048, tk=512):
    S = dq.shape[0]
    return pl.pallas_call(
        functools.partial(_attn_kernel, tk=tk),
        grid=(N_HEADS, S // tq),
        in_specs=[pl.BlockSpec(memory_space=pltpu.SMEM),
                  pl.BlockSpec((tq, HEAD_W), lambda h, i: (i, h)),
                  pl.BlockSpec((S, HEAD_W), lambda h, i: (0, h)),
                  pl.BlockSpec((VT_ROWS, S), lambda h, i: (h, 0)),
                  pl.BlockSpec((1, HEAD_W), lambda h, i: (0, 0))],
        out_specs=pl.BlockSpec((tq, HEAD_W), lambda h, i: (i, h)),
        out_shape=jax.ShapeDtypeStruct((S, HALF_W), F32),
        scratch_shapes=[pltpu.VMEM((2, tk, tq), F32), pltpu.VMEM((2, tk, tq), F32),
                        pltpu.VMEM((2, 1, tq), F32), pltpu.VMEM((2, VT_ROWS, tq), F32)],
        compiler_params=pltpu.CompilerParams(dimension_semantics=("arbitrary", "arbitrary"),
                                             vmem_limit_bytes=V7X_VMEM_LIMIT),
        name="diff_attn",
    )(lam, dq, dk, dvt, gain)


def _outproj_kernel(x_ref, of_ref, ob_ref, hg_ref, oda_ref, hgain_ref, wo_ref, g2_ref,
                    wr_ref, br_ref, x2_ref, rank_t_ref, gates_t_ref, cnt_ref,
                    carry_ref, *, sub_blocks):
    i = pl.program_id(0)
    o = of_ref[...] + ob_ref[...]
    gate = hg_ref[...]
    gate = gate * jax.nn.sigmoid(gate)
    parts = []
    for hd in range(N_HEADS):
        sl = slice(hd * HEAD_W, (hd + 1) * HEAD_W)
        parts.append(_rms(o[:, sl], NORM_EPS) * hgain_ref[...] * gate[:, sl])
    parts.append(oda_ref[...])
    mixed = jnp.concatenate(parts, axis=1).astype(BF16)
    x2 = x_ref[...] + jnp.dot(mixed, wo_ref[...], preferred_element_type=F32)
    x2_ref[...] = x2

    h2 = _rms(x2, NORM_EPS) * g2_ref[...]
    h_hi = h2.astype(BF16)
    h_lo = (h2 - h_hi.astype(F32)).astype(BF16)
    t = jnp.dot(h_hi, wr_ref[...], preferred_element_type=F32)
    logits = (t[:, :ROUTER_LANES] + t[:, ROUTER_LANES:] + br_ref[...]
              + jnp.dot(h_lo, wr_ref[:, :ROUTER_LANES], preferred_element_type=F32))
    lane = lax.broadcasted_iota(jnp.int32, logits.shape, 1)
    neg = -jnp.inf
    big = ROUTER_LANES

    def first_max(vals):
        mx = jnp.max(vals, axis=1, keepdims=True)
        idx = jnp.min(jnp.where(vals == mx, lane, big), axis=1, keepdims=True)
        return mx, idx

    gl = jnp.where((lane >= N_EXPERTS) & (lane < N_EXPERTS + N_GROUPS), logits, neg)
    gm, gidx = first_max(gl)
    g_p = 1.0 / jnp.sum(jnp.exp(gl - gm), axis=1, keepdims=True)
    el = jnp.where((lane < N_EXPERTS) & ((lane // EXPERTS_PER_GROUP) == (gidx - N_EXPERTS)),
                   logits, neg)
    em1, idx1 = first_max(el)
    em2, idx2 = first_max(jnp.where(lane == idx1, neg, el))
    e2 = jnp.exp(em2 - em1)
    w1 = g_p / (1.0 + e2)
    w2 = g_p * e2 / (1.0 + e2)
    is1 = lane == idx1
    is2 = lane == idx2
    gates = jnp.where(is1, w1, 0.0) + jnp.where(is2, w2, 0.0)

    @pl.when(i % sub_blocks == 0)
    def _():
        carry_ref[...] = jnp.zeros_like(carry_ref)

    assigned = jnp.where(is1 | is2, 1.0, 0.0)
    tm = assigned.shape[0]
    tri = jnp.where(lax.broadcasted_iota(jnp.int32, (tm, tm), 0)
                    >= lax.broadcasted_iota(jnp.int32, (tm, tm), 1), 1.0, 0.0).astype(BF16)
    cum = jnp.dot(tri, assigned.astype(BF16), preferred_element_type=F32)
    carry = carry_ref[...]
    rank = jnp.where(assigned > 0.0, carry + cum - 1.0, -1.0)
    carry = carry + cum[tm - 1:tm, :]
    carry_ref[...] = carry
    cnt_ref[pl.ds(i // sub_blocks, 1), :] = carry.astype(jnp.int32)
    rank_t_ref[...] = rank.T
    gates_t_ref[...] = gates.T


def _outproj(x, o_f, o_b, hg, o_da, hgain, w_out, g2, wr, br, tb, tm=512):
    S = x.shape[0]
    row = lambda w: pl.BlockSpec((tm, w), lambda i: (i, 0))
    col = pl.BlockSpec((ROUTER_LANES, tm), lambda i: (0, i))
    full = lambda a: pl.BlockSpec(a.shape, lambda i: (0, 0))
    return pl.pallas_call(
        functools.partial(_outproj_kernel, sub_blocks=tb // tm),
        grid=(S // tm,),
        in_specs=[row(D_MODEL), row(HALF_W), row(HALF_W), row(HALF_W), row(HALF_W),
                  full(hgain), full(w_out), full(g2), full(wr), full(br)],
        out_specs=[row(D_MODEL), col, col,
                   pl.BlockSpec((S // tb, ROUTER_LANES), lambda i: (0, 0))],
        out_shape=[jax.ShapeDtypeStruct((S, D_MODEL), F32),
                   jax.ShapeDtypeStruct((ROUTER_LANES, S), F32),
                   jax.ShapeDtypeStruct((ROUTER_LANES, S), F32),
                   jax.ShapeDtypeStruct((S // tb, ROUTER_LANES), jnp.int32)],
        scratch_shapes=[pltpu.VMEM((1, ROUTER_LANES), F32)],
        compiler_params=pltpu.CompilerParams(dimension_semantics=("arbitrary",),
                                             vmem_limit_bytes=V7X_VMEM_LIMIT),
        name="outproj_router",
    )(x, o_f, o_b, hg, o_da, hgain, w_out, g2, wr, br)


def _moe_kernel(cnt_ref, x2_ref, rank_t_ref, gates_t_ref, g2_ref, wg_ref, wu_ref,
                wd_ref, gf_ref, out_ref, h2_ref, *, tile_rows, tb):
    b = pl.program_id(0)
    p = pl.program_id(1)
    R = tile_rows
    n_exp = wg_ref.shape[0]
    n_sub = x2_ref.shape[0] // tb

    @pl.when(p == 0)
    def _():
        x2 = x2_ref[...]
        h2_ref[...] = (_rms(x2, NORM_EPS) * g2_ref[...]).astype(BF16)
        out_ref[...] = x2

    experts = [n_exp * p + i for i in range(n_exp)]
    subs = [slice(s * tb, (s + 1) * tb) for s in range(n_sub)]
    rank_row = [[rank_t_ref[pl.ds(e, 1), sub] for e in experts] for sub in subs]
    gate_row = [[gates_t_ref[pl.ds(e, 1), sub] for e in experts] for sub in subs]
    n_max = jnp.int32(0)
    for s in range(n_sub):
        for e in experts:
            n_max = jnp.maximum(n_max, cnt_ref[n_sub * b + s, e])

    def tile(t, carry):
        base = (t * R).astype(F32)
        slot_col = lax.broadcasted_iota(jnp.int32, (R, 1), 0).astype(F32) + base
        sel, gather, xg = [], [], []
        for s, sub in enumerate(subs):
            sel.append([rank_row[s][i] == slot_col for i in range(n_exp)])
            gather.append(jnp.concatenate(
                [jnp.where(m, 1.0, 0.0).astype(BF16) for m in sel[s]], axis=0))
            xg.append(jnp.dot(gather[s], h2_ref[sub, :],
                              preferred_element_type=F32).astype(BF16))
        ys = [[] for _ in subs]
        for i in range(n_exp):
            xi = jnp.concatenate([xg[s][i * R:(i + 1) * R] for s in range(n_sub)], axis=0)
            a = jnp.dot(xi, wg_ref[i], preferred_element_type=F32)
            a = a * jax.nn.sigmoid(a) * jnp.dot(xi, wu_ref[i], preferred_element_type=F32)
            y = jnp.dot(a.astype(BF16), wd_ref[i], preferred_element_type=F32)
            for s in range(n_sub):
                g = jnp.sum(jnp.where(sel[s][i], gate_row[s][i], 0.0), axis=1, keepdims=True)
                ys[s].append((y[s * R:(s + 1) * R] * g).astype(BF16))
        for s, sub in enumerate(subs):
            out_ref[sub, :] += lax.dot_general(gather[s], jnp.concatenate(ys[s], axis=0),
                                               TN_DIMS, preferred_element_type=F32)
        return carry

    lax.fori_loop(0, (n_max + R - 1) // R, tile, 0)

    @pl.when(p == pl.num_programs(1) - 1)
    def _():
        out_ref[...] = _rms(out_ref[...], NORM_EPS) * gf_ref[...]


def _moe(cnt, x2, rank_t, gates_t, g2, wg, wu, wd, gf, tb, tile_rows=96,
         experts_per_step=2, blocks_per_step=2):
    S = x2.shape[0]
    eps = experts_per_step
    rows = tb * blocks_per_step
    grid_spec = pltpu.PrefetchScalarGridSpec(
        num_scalar_prefetch=1,
        grid=(S // rows, N_EXPERTS // eps),
        in_specs=[pl.BlockSpec((rows, D_MODEL), lambda b, p, c: (b, 0)),
                  pl.BlockSpec((N_EXPERTS, rows), lambda b, p, c: (0, b)),
                  pl.BlockSpec((N_EXPERTS, rows), lambda b, p, c: (0, b)),
                  pl.BlockSpec((1, D_MODEL), lambda b, p, c: (0, 0)),
                  pl.BlockSpec((eps, D_MODEL, D_EXPERT), lambda b, p, c: (p, 0, 0)),
                  pl.BlockSpec((eps, D_MODEL, D_EXPERT), lambda b, p, c: (p, 0, 0)),
                  pl.BlockSpec((eps, D_EXPERT, D_MODEL), lambda b, p, c: (p, 0, 0)),
                  pl.BlockSpec((1, D_MODEL), lambda b, p, c: (0, 0))],
        out_specs=pl.BlockSpec((rows, D_MODEL), lambda b, p, c: (b, 0)),
        scratch_shapes=[pltpu.VMEM((rows, D_MODEL), BF16)])
    return pl.pallas_call(
        functools.partial(_moe_kernel, tile_rows=tile_rows, tb=tb),
        grid_spec=grid_spec,
        out_shape=jax.ShapeDtypeStruct((S, D_MODEL), F32),
        compiler_params=pltpu.CompilerParams(dimension_semantics=("arbitrary", "arbitrary"),
                                             vmem_limit_bytes=V7X_VMEM_LIMIT),
        name="moe",
    )(cnt, x2, rank_t, gates_t, g2, wg, wu, wd, gf)


def kernel(x, positions, norm1_gain, w_in, hg_lower_bounds, hg_norm_gain, diff_lambda,
           diff_subln_gain, w_out, norm2_gain, router_group_w, router_group_b,
           router_expert_w, router_expert_b, moe_w_gate, moe_w_up, moe_w_down,
           final_norm_gain):
    B, S, _ = x.shape
    assert B == 1 and norm1_gain.shape[0] == 1
    layer = 0
    xs = x.reshape(S, D_MODEL)
    pos = positions.reshape(1, S).astype(F32)

    lb_cum = jnp.cumsum(jax.nn.softmax(hg_lower_bounds.astype(F32), axis=1), axis=1)
    lb = lb_cum[:, layer + 1] - lb_cum[:, 0]
    lam_p = diff_lambda[layer].astype(F32)
    lam = (jnp.exp(jnp.sum(lam_p[0] * lam_p[1])) - jnp.exp(jnp.sum(lam_p[2] * lam_p[3]))
           + LAMBDA_INIT).reshape(1)
    lane = jnp.arange(HEAD_W)
    inv_freq = jnp.float32(ROPE_THETA) ** (-jnp.arange(0, ROT_DIM, 2, dtype=F32) / ROT_DIM)
    in_rot = (lane % QK_DIM) < ROT_DIM
    n_freq = ROT_DIM // 2
    invf = inv_freq.reshape(n_freq, 1)
    sgn = jnp.where(in_rot, jnp.where((lane % QK_DIM) < n_freq, -1.0, 1.0),
                    0.0).astype(F32).reshape(1, HEAD_W)
    place = (in_rot[None, :] & (lane[None, :] % n_freq == jnp.arange(n_freq)[:, None])).astype(F32)
    pcos = jnp.tile(place, (4, 1)).astype(BF16)
    psin = jnp.tile(place * sgn, (4, 1)).astype(BF16)
    wr = jnp.zeros((D_MODEL, ROUTER_LANES), F32)
    wr = wr.at[:, :N_EXPERTS].set(router_expert_w[layer])
    wr = wr.at[:, N_EXPERTS:N_EXPERTS + N_GROUPS].set(router_group_w[layer])
    wr_hi = wr.astype(BF16)
    wr = jnp.concatenate([wr_hi, (wr - wr_hi.astype(F32)).astype(BF16)], axis=1)
    br = jnp.zeros((1, ROUTER_LANES), F32)
    br = br.at[0, :N_EXPERTS].set(router_expert_b[layer])
    br = br.at[0, N_EXPERTS:N_EXPERTS + N_GROUPS].set(router_group_b[layer])

    hq, kf, kb, lff, lfb, hi, hg, dq, dk, dv = _inproj(
        xs, pos, norm1_gain[layer].reshape(1, -1), w_in[layer].astype(BF16), lb, invf, sgn,
        pcos, psin)
    o_f, wg, wu = _hgrn(hq, kf, lff, hi, False, (moe_w_gate[layer], moe_w_up[layer]))
    o_b, wd = _hgrn(hq, kb, lfb, hi, True, (moe_w_down[layer],))
    o_da = _attn(lam, dq, dk, dv, diff_subln_gain[layer].reshape(1, -1))
    x2, rank_t, gates_t, cnt = _outproj(
        xs, o_f, o_b, hg, o_da, hg_norm_gain[layer].reshape(1, -1), w_out[layer].astype(BF16),
        norm2_gain[layer].reshape(1, -1), wr, br, tb=MOE_TOKEN_BLOCK)
    out = _moe(cnt, x2, rank_t, gates_t, norm2_gain[layer].reshape(1, -1), wg, wu, wd,
               final_norm_gain.reshape(1, -1), tb=MOE_TOKEN_BLOCK)
    return out.reshape(B, S, D_MODEL)
```

```python
import functools
import math

import jax
import jax.numpy as jnp
from jax import lax
from jax.experimental import pallas as pl
from jax.experimental.pallas import tpu as pltpu

F32 = jnp.float32
BF16 = jnp.bfloat16

D_MODEL = 1024
HEAD_W = 128
N_HEADS = 4
HALF_W = N_HEADS * HEAD_W
QK_DIM = 64
ROT_DIM = 16
ROPE_THETA = 500000.0
QUERY_GROUP = 512
VT_ROWS = HEAD_W + 16
HG_CHUNK = 128
HG_UNROLL = 8
N_GROUPS = 4
EXPERTS_PER_GROUP = 8
N_EXPERTS = 32
D_EXPERT = 512
NORM_EPS = 1e-6
SUBLN_EPS = 1e-5
LAMBDA_INIT = 0.8 - 0.6 * math.exp(-0.3 * 0)
LOG2E = 1.4426950408889634
ROUTER_LANES = 128
MOE_TOKEN_BLOCK = 1024
V7X_VMEM_LIMIT = 56 * 1024 * 1024

NT_DIMS = (((1,), (1,)), ((), ()))
TN_DIMS = (((0,), (0,)), ((), ()))


def _rms(x, eps):
    return x * lax.rsqrt(jnp.mean(x * x, axis=-1, keepdims=True) + eps)


def _inproj_kernel(x_ref, pos_ref, g1_ref, w_ref, lb_ref, invf_ref, sgn_ref, pcos_ref, psin_ref,
                   hq_ref, kf_ref, kb_ref, lff_ref, lfb_ref, hi_ref, hg_ref,
                   dq_ref, dk_ref, dv_ref):
    x = x_ref[...]
    h = (_rms(x, NORM_EPS) * g1_ref[...]).astype(BF16)

    def proj(i):
        return jnp.dot(h, w_ref[:, i * HALF_W:(i + 1) * HALF_W], preferred_element_type=F32)

    hq_ref[...] = proj(0)
    for i, (k_ref, lf_ref) in enumerate(((kf_ref, lff_ref), (kb_ref, lfb_ref))):
        z = proj(1 + i)
        lb = lb_ref[i:i + 1, :]
        sg = jax.nn.sigmoid(z)
        lf_ref[...] = jnp.log(lb + (1.0 - lb) * sg)
        k_ref[...] = (1.0 - lb) * (1.0 - sg)
    hi_ref[...] = proj(3)
    hg_ref[...] = proj(4)

    ang_t = invf_ref[...] * pos_ref[...]

    def spread(t, place_ref):
        hi = t.astype(BF16)
        r1 = t - hi.astype(F32)
        mid = r1.astype(BF16)
        lo = (r1 - mid.astype(F32)).astype(BF16)
        pieces = jnp.concatenate([hi, mid, lo, jnp.zeros_like(hi)], axis=0)
        return lax.dot_general(pieces, place_ref[...], TN_DIMS, preferred_element_type=F32)

    sgn = sgn_ref[...]
    cos = jnp.concatenate([spread(jnp.cos(ang_t), pcos_ref) + jnp.where(sgn == 0.0, 1.0, 0.0)]
                          * N_HEADS, axis=1)
    sin = jnp.concatenate([spread(jnp.sin(ang_t), psin_ref)] * N_HEADS, axis=1)
    take_hi = jnp.concatenate([jnp.broadcast_to(sgn, (x.shape[0], HEAD_W))] * N_HEADS,
                              axis=1) < 0.0

    def rotary(t):
        partner = jnp.where(take_hi, pltpu.roll(t, HALF_W - ROT_DIM // 2, axis=1),
                            pltpu.roll(t, ROT_DIM // 2, axis=1))
        return t * cos + partner * sin

    dq_ref[...] = (rotary(proj(5)) * (QK_DIM ** -0.5 * LOG2E)).astype(BF16)
    dk_ref[...] = rotary(proj(6)).astype(BF16)
    v = proj(7)
    sub = lax.broadcasted_iota(jnp.int32, (VT_ROWS - HEAD_W, v.shape[0]), 0)
    ones_rows = jnp.where(sub == 0, 1.0, 0.0)
    pieces = []
    for hd in range(N_HEADS):
        pieces += [v[:, hd * HEAD_W:(hd + 1) * HEAD_W].T, ones_rows]
    dv_ref[...] = jnp.concatenate(pieces, axis=0).astype(BF16)


def _inproj(x, pos, g1, w_in, lb, invf, sgn, pcos, psin, tm=512):
    S = x.shape[0]
    row = lambda w: pl.BlockSpec((tm, w), lambda i: (i, 0))
    full = lambda a: pl.BlockSpec(a.shape, lambda i: (0, 0))
    f32o = jax.ShapeDtypeStruct((S, HALF_W), F32)
    bf16o = jax.ShapeDtypeStruct((S, HALF_W), BF16)
    return pl.pallas_call(
        _inproj_kernel,
        grid=(S // tm,),
        in_specs=[row(D_MODEL), pl.BlockSpec((1, tm), lambda i: (0, i)), full(g1), full(w_in),
                  full(lb), full(invf), full(sgn), full(pcos), full(psin)],
        out_specs=[row(HALF_W)] * 9 + [pl.BlockSpec((N_HEADS * VT_ROWS, tm), lambda i: (0, i))],
        out_shape=[f32o] * 7 + [bf16o, bf16o,
                                jax.ShapeDtypeStruct((N_HEADS * VT_ROWS, S), BF16)],
        compiler_params=pltpu.CompilerParams(dimension_semantics=("arbitrary",),
                                             vmem_limit_bytes=V7X_VMEM_LIMIT),
        name="inproj",
    )(x, pos, g1, w_in, lb, invf, sgn, pcos, psin)


def _chunk_cumsum(x, rev):
    n = x.shape[0]
    row = lax.broadcasted_iota(jnp.int32, x.shape, 0)
    d = 1
    while d < n:
        if rev:
            x = x + jnp.where(row < n - d, pltpu.roll(x, n - d, axis=0), 0.0)
        else:
            x = x + jnp.where(row >= d, pltpu.roll(x, d, axis=0), 0.0)
        d *= 2
    return x


def _pivot_rows(b, blk, rev):
    n = b.shape[0]
    half = blk // 2
    groups = []
    for g0 in range(0, n, 8):
        def piv(r):
            base = (r // blk) * blk
            return base + half if rev else base + half - 1
        if blk >= 8:
            p = piv(g0)
            groups.append(jnp.broadcast_to(b[p:p + 1, :], (8, b.shape[1])))
        else:
            sub = lax.broadcasted_iota(jnp.int32, (8, b.shape[1]), 0)
            acc = None
            for s0 in range(0, 8, blk):
                p = piv(g0 + s0)
                rowv = jnp.broadcast_to(b[p:p + 1, :], (8, b.shape[1]))
                acc = rowv if acc is None else jnp.where(sub >= s0, rowv, acc)
            groups.append(acc)
    return jnp.concatenate(groups, axis=0)


def _hgrn_kernel(q_ref, k_ref, lf_ref, v_ref, *rest, rev, n_chunks, n_cast):
    w_refs, o_ref, wb_refs, st_ref = (rest[:n_cast], rest[n_cast], rest[n_cast + 1:-1], rest[-1])
    for w_ref, wb_ref in zip(w_refs, wb_refs):
        wb_ref[...] = w_ref[...].astype(BF16)

    @pl.when(pl.program_id(1) == 0)
    def _():
        st_ref[...] = jnp.zeros_like(st_ref)

    C, W = HG_CHUNK, HEAD_W
    rt = lax.broadcasted_iota(jnp.int32, (C, C), 0)
    ct = lax.broadcasted_iota(jnp.int32, (C, C), 1)
    rowi = lax.broadcasted_iota(jnp.int32, (C, 2 * W), 0)
    eye = jnp.where(rt == ct, 1.0, 0.0)
    levels = []
    blk = C
    while blk >= 2:
        half = blk // 2
        q_side = (lambda r: (r % blk) < half) if rev else (lambda r: (r % blk) >= half)
        pair = ((rt // blk) == (ct // blk)) & q_side(rt) & jnp.logical_not(q_side(ct))
        levels.append((blk, jnp.where(q_side(rowi), 1.0, 0.0), jnp.where(pair, 1.0, 0.0)))
        blk = half

    def stack(a):
        return jnp.concatenate([a[:, :W], a[:, W:]], axis=0)

    def chunk(ci, st):
        c = (n_chunks - 1 - ci) if rev else ci
        sl = pl.ds(pl.multiple_of(c * C, C), C)
        q = q_ref[sl, :]
        k = k_ref[sl, :]
        vb = v_ref[sl, :].astype(BF16)
        b = _chunk_cumsum(lf_ref[sl, :], rev)
        tot = b[0:1, :] if rev else b[C - 1:C, :]

        o = lax.dot_general(stack((q * jnp.exp(b)).astype(BF16)), st.astype(BF16), NT_DIMS,
                            preferred_element_type=F32)
        kdec = (k * jnp.exp(tot - b)).astype(BF16)
        upd = lax.dot_general(vb, kdec, TN_DIMS, preferred_element_type=F32)
        dec = jnp.exp(tot)
        st_next = jnp.concatenate([st[:W] * dec[:, :W] + upd[:W, :W],
                                   st[W:] * dec[:, W:] + upd[W:, W:]], axis=0)

        qk = q * k
        scores = [eye * jnp.sum(qk[:, :W], axis=1, keepdims=True),
                  eye * jnp.sum(qk[:, W:], axis=1, keepdims=True)]
        for blk, q_side, pair in levels:
            w = jnp.exp2(jnp.abs(b - _pivot_rows(b, blk, rev)) * (-LOG2E))
            z = stack((jnp.where(q_side > 0.5, q, k) * w).astype(BF16))
            g = lax.dot_general(z, z, NT_DIMS, preferred_element_type=F32)
            scores = [scores[0] + pair * g[:C, :C], scores[1] + pair * g[C:, C:]]
        intra = jnp.dot(jnp.concatenate(scores, axis=0).astype(BF16), vb,
                        preferred_element_type=F32)
        o_ref[sl, :] = jnp.concatenate([o[:C, :W] + intra[:C, :W], o[C:, W:] + intra[C:, W:]],
                                       axis=1)
        return st_next

    st_ref[...] = lax.fori_loop(0, n_chunks, chunk, st_ref[...], unroll=HG_UNROLL)


def _hgrn(q, k, lf, v, rev, weights, rb=1024):
    S = q.shape[0]
    nb = S // rb
    n_pairs = N_HEADS // 2
    steps = n_pairs * nb
    idx = (lambda h, j: (nb - 1 - j, h)) if rev else (lambda h, j: (j, h))
    spec = pl.BlockSpec((rb, 2 * HEAD_W), idx)
    w_specs = []
    for w in weights:
        n_exp, rows, cols = w.shape
        per_exp = steps // n_exp
        assert per_exp * n_exp == steps and rows % per_exp == 0
        w_specs.append(pl.BlockSpec(
            (1, rows // per_exp, cols),
            lambda h, j, per_exp=per_exp: ((h * nb + j) // per_exp, (h * nb + j) % per_exp, 0)))
    return pl.pallas_call(
        functools.partial(_hgrn_kernel, rev=rev, n_chunks=rb // HG_CHUNK, n_cast=len(weights)),
        grid=(n_pairs, nb),
        in_specs=[spec] * 4 + w_specs,
        out_specs=[spec] + w_specs,
        out_shape=[jax.ShapeDtypeStruct((S, HALF_W), F32)]
        + [jax.ShapeDtypeStruct(w.shape, BF16) for w in weights],
        scratch_shapes=[pltpu.VMEM((2 * HEAD_W, HEAD_W), F32)],
        compiler_params=pltpu.CompilerParams(dimension_semantics=("arbitrary", "arbitrary"),
                                             vmem_limit_bytes=V7X_VMEM_LIMIT),
        name="hgrn_bwd" if rev else "hgrn_fwd",
    )(q, k, lf, v, *weights)


def _attn_kernel(lam_ref, q_ref, k_ref, vt_ref, gain_ref, o_ref, sa_ref, sb_ref, m_ref, acc_ref,
                 *, tk):
    n_kv = k_ref.shape[0] // tk
    tq = q_ref.shape[0]
    assert n_kv % 2 == 0 and tq % QUERY_GROUP == 0
    q = q_ref[...]
    lane = lax.broadcasted_iota(jnp.int32, q.shape, 1)
    qm = (jnp.where(lane < QK_DIM, q, jnp.zeros_like(q)),
          jnp.where(lane >= QK_DIM, q, jnp.zeros_like(q)))
    m_ref[...] = jnp.full(m_ref.shape, -jnp.inf, F32)
    acc_ref[...] = jnp.zeros_like(acc_ref)

    def chunk(j):
        return pl.ds(pl.multiple_of(j * tk, tk), tk)

    def softmax_pv(c, cols, s, vt):
        m_old = m_ref[c, :, cols]
        m_new = jnp.maximum(m_old, jnp.max(s, axis=0, keepdims=True))
        p = jnp.exp2(s - m_new).astype(BF16)
        acc_ref[c, :, cols] = jnp.exp2(m_old - m_new) * acc_ref[c, :, cols] + jnp.dot(
            vt, p, preferred_element_type=F32)
        m_ref[c, :, cols] = m_new

    def step(js, dst_ref, jp, src_ref):
        kc = None if dst_ref is None else k_ref[chunk(js), :]
        vt = None if src_ref is None else vt_ref[:, chunk(jp)]
        for c in range(2):
            for g in range(0, tq, QUERY_GROUP):
                cols = slice(g, g + QUERY_GROUP)
                if dst_ref is not None:
                    dst_ref[c, :, cols] = lax.dot_general(kc, qm[c][cols, :], NT_DIMS,
                                                          preferred_element_type=F32)
                if src_ref is not None:
                    softmax_pv(c, cols, src_ref[c, :, cols], vt)

    step(0, sa_ref, None, None)

    def two_steps(i, carry):
        step(2 * i + 1, sb_ref, 2 * i, sa_ref)
        step(2 * i + 2, sa_ref, 2 * i + 1, sb_ref)
        return carry

    lax.fori_loop(0, n_kv // 2 - 1, two_steps, 0)
    step(n_kv - 1, sb_ref, n_kv - 2, sa_ref)
    step(None, None, n_kv - 1, sb_ref)

    outs = []
    for c in range(2):
        a = acc_ref[c]
        outs.append(a[:HEAD_W, :] / a[HEAD_W:HEAD_W + 1, :])
    o = (outs[0] - lam_ref[0] * outs[1]).T
    o_ref[...] = _rms(o, SUBLN_EPS) * gain_ref[...] * (1.0 - LAMBDA_INIT)


def _attn(lam, dq, dk, dvt, gain, tq=2048, tk=512):
    S = dq.shape[0]
    return pl.pallas_call(
        functools.partial(_attn_kernel, tk=tk),
        grid=(N_HEADS, S // tq),
        in_specs=[pl.BlockSpec(memory_space=pltpu.SMEM),
                  pl.BlockSpec((tq, HEAD_W), lambda h, i: (i, h)),
                  pl.BlockSpec((S, HEAD_W), lambda h, i: (0, h)),
                  pl.BlockSpec((VT_ROWS, S), lambda h, i: (h, 0)),
                  pl.BlockSpec((1, HEAD_W), lambda h, i: (0, 0))],
        out_specs=pl.BlockSpec((tq, HEAD_W), lambda h, i: (i, h)),
        out_shape=jax.ShapeDtypeStruct((S, HALF_W), F32),
        scratch_shapes=[pltpu.VMEM((2, tk, tq), F32), pltpu.VMEM((2, tk, tq), F32),
                        pltpu.VMEM((2, 1, tq), F32), pltpu.VMEM((2, VT_ROWS, tq), F32)],
        compiler_params=pltpu.CompilerParams(dimension_semantics=("arbitrary", "arbitrary"),
                                             vmem_limit_bytes=V7X_VMEM_LIMIT),
        name="diff_attn",
    )(lam, dq, dk, dvt, gain)


def _outproj_kernel(x_ref, of_ref, ob_ref, hg_ref, oda_ref, hgain_ref, wo_ref, g2_ref,
                    wr_ref, br_ref, x2_ref, rank_t_ref, gates_t_ref, cnt_ref,
                    carry_ref, *, sub_blocks):
    i = pl.program_id(0)
    o = of_ref[...] + ob_ref[...]
    gate = hg_ref[...]
    gate = gate * jax.nn.sigmoid(gate)
    parts = []
    for hd in range(N_HEADS):
        sl = slice(hd * HEAD_W, (hd + 1) * HEAD_W)
        parts.append(_rms(o[:, sl], NORM_EPS) * hgain_ref[...] * gate[:, sl])
    parts.append(oda_ref[...])
    mixed = jnp.concatenate(parts, axis=1).astype(BF16)
    x2 = x_ref[...] + jnp.dot(mixed, wo_ref[...], preferred_element_type=F32)
    x2_ref[...] = x2

    h2 = _rms(x2, NORM_EPS) * g2_ref[...]
    h_hi = h2.astype(BF16)
    h_lo = (h2 - h_hi.astype(F32)).astype(BF16)
    t = jnp.dot(h_hi, wr_ref[...], preferred_element_type=F32)
    logits = (t[:, :ROUTER_LANES] + t[:, ROUTER_LANES:] + br_ref[...]
              + jnp.dot(h_lo, wr_ref[:, :ROUTER_LANES], preferred_element_type=F32))
    lane = lax.broadcasted_iota(jnp.int32, logits.shape, 1)
    neg = -jnp.inf
    big = ROUTER_LANES

    def first_max(vals):
        mx = jnp.max(vals, axis=1, keepdims=True)
        idx = jnp.min(jnp.where(vals == mx, lane, big), axis=1, keepdims=True)
        return mx, idx

    gl = jnp.where((lane >= N_EXPERTS) & (lane < N_EXPERTS + N_GROUPS), logits, neg)
    gm, gidx = first_max(gl)
    g_p = 1.0 / jnp.sum(jnp.exp(gl - gm), axis=1, keepdims=True)
    el = jnp.where((lane < N_EXPERTS) & ((lane // EXPERTS_PER_GROUP) == (gidx - N_EXPERTS)),
                   logits, neg)
    em1, idx1 = first_max(el)
    em2, idx2 = first_max(jnp.where(lane == idx1, neg, el))
    e2 = jnp.exp(em2 - em1)
    w1 = g_p / (1.0 + e2)
    w2 = g_p * e2 / (1.0 + e2)
    is1 = lane == idx1
    is2 = lane == idx2
    gates = jnp.where(is1, w1, 0.0) + jnp.where(is2, w2, 0.0)

    @pl.when(i % sub_blocks == 0)
    def _():
        carry_ref[...] = jnp.zeros_like(carry_ref)

    assigned = jnp.where(is1 | is2, 1.0, 0.0)
    tm = assigned.shape[0]
    tri = jnp.where(lax.broadcasted_iota(jnp.int32, (tm, tm), 0)
                    >= lax.broadcasted_iota(jnp.int32, (tm, tm), 1), 1.0, 0.0).astype(BF16)
    cum = jnp.dot(tri, assigned.astype(BF16), preferred_element_type=F32)
    carry = carry_ref[...]
    rank = jnp.where(assigned > 0.0, carry + cum - 1.0, -1.0)
    carry = carry + cum[tm - 1:tm, :]
    carry_ref[...] = carry
    cnt_ref[pl.ds(i // sub_blocks, 1), :] = carry.astype(jnp.int32)
    rank_t_ref[...] = rank.T
    gates_t_ref[...] = gates.T


def _outproj(x, o_f, o_b, hg, o_da, hgain, w_out, g2, wr, br, tb, tm=512):
    S = x.shape[0]
    row = lambda w: pl.BlockSpec((tm, w), lambda i: (i, 0))
    col = pl.BlockSpec((ROUTER_LANES, tm), lambda i: (0, i))
    full = lambda a: pl.BlockSpec(a.shape, lambda i: (0, 0))
    return pl.pallas_call(
        functools.partial(_outproj_kernel, sub_blocks=tb // tm),
        grid=(S // tm,),
        in_specs=[row(D_MODEL), row(HALF_W), row(HALF_W), row(HALF_W), row(HALF_W),
                  full(hgain), full(w_out), full(g2), full(wr), full(br)],
        out_specs=[row(D_MODEL), col, col,
                   pl.BlockSpec((S // tb, ROUTER_LANES), lambda i: (0, 0))],
        out_shape=[jax.ShapeDtypeStruct((S, D_MODEL), F32),
                   jax.ShapeDtypeStruct((ROUTER_LANES, S), F32),
                   jax.ShapeDtypeStruct((ROUTER_LANES, S), F32),
                   jax.ShapeDtypeStruct((S // tb, ROUTER_LANES), jnp.int32)],
        scratch_shapes=[pltpu.VMEM((1, ROUTER_LANES), F32)],
        compiler_params=pltpu.CompilerParams(dimension_semantics=("arbitrary",),
                                             vmem_limit_bytes=V7X_VMEM_LIMIT),
        name="outproj_router",
    )(x, o_f, o_b, hg, o_da, hgain, w_out, g2, wr, br)


def _moe_kernel(cnt_ref, x2_ref, rank_t_ref, gates_t_ref, g2_ref, wg_ref, wu_ref,
                wd_ref, gf_ref, out_ref, h2_ref, *, tile_rows, tb):
    b = pl.program_id(0)
    p = pl.program_id(1)
    R = tile_rows
    n_exp = wg_ref.shape[0]
    n_sub = x2_ref.shape[0] // tb

    @pl.when(p == 0)
    def _():
        x2 = x2_ref[...]
        h2_ref[...] = (_rms(x2, NORM_EPS) * g2_ref[...]).astype(BF16)
        out_ref[...] = x2

    experts = [n_exp * p + i for i in range(n_exp)]
    subs = [slice(s * tb, (s + 1) * tb) for s in range(n_sub)]
    rank_row = [[rank_t_ref[pl.ds(e, 1), sub] for e in experts] for sub in subs]
    gate_row = [[gates_t_ref[pl.ds(e, 1), sub] for e in experts] for sub in subs]
    n_max = jnp.int32(0)
    for s in range(n_sub):
        for e in experts:
            n_max = jnp.maximum(n_max, cnt_ref[n_sub * b + s, e])

    def tile(t, carry):
        base = (t * R).astype(F32)
        slot_col = lax.broadcasted_iota(jnp.int32, (R, 1), 0).astype(F32) + base
        sel, gather, xg = [], [], []
        for s, sub in enumerate(subs):
            sel.append([rank_row[s][i] == slot_col for i in range(n_exp)])
            gather.append(jnp.concatenate(
                [jnp.where(m, 1.0, 0.0).astype(BF16) for m in sel[s]], axis=0))
            xg.append(jnp.dot(gather[s], h2_ref[sub, :],
                              preferred_element_type=F32).astype(BF16))
        ys = [[] for _ in subs]
        for i in range(n_exp):
            xi = jnp.concatenate([xg[s][i * R:(i + 1) * R] for s in range(n_sub)], axis=0)
            a = jnp.dot(xi, wg_ref[i], preferred_element_type=F32)
            a = a * jax.nn.sigmoid(a) * jnp.dot(xi, wu_ref[i], preferred_element_type=F32)
            y = jnp.dot(a.astype(BF16), wd_ref[i], preferred_element_type=F32)
            for s in range(n_sub):
                g = jnp.sum(jnp.where(sel[s][i], gate_row[s][i], 0.0), axis=1, keepdims=True)
                ys[s].append((y[s * R:(s + 1) * R] * g).astype(BF16))
        for s, sub in enumerate(subs):
            out_ref[sub, :] += lax.dot_general(gather[s], jnp.concatenate(ys[s], axis=0),
                                               TN_DIMS, preferred_element_type=F32)
        return carry

    lax.fori_loop(0, (n_max + R - 1) // R, tile, 0)

    @pl.when(p == pl.num_programs(1) - 1)
    def _():
        out_ref[...] = _rms(out_ref[...], NORM_EPS) * gf_ref[...]


def _moe(cnt, x2, rank_t, gates_t, g2, wg, wu, wd, gf, tb, tile_rows=96,
         experts_per_step=2, blocks_per_step=2):
    S = x2.shape[0]
    eps = experts_per_step
    rows = tb * blocks_per_step
    grid_spec = pltpu.PrefetchScalarGridSpec(
        num_scalar_prefetch=1,
        grid=(S // rows, N_EXPERTS // eps),
        in_specs=[pl.BlockSpec((rows, D_MODEL), lambda b, p, c: (b, 0)),
                  pl.BlockSpec((N_EXPERTS, rows), lambda b, p, c: (0, b)),
                  pl.BlockSpec((N_EXPERTS, rows), lambda b, p, c: (0, b)),
                  pl.BlockSpec((1, D_MODEL), lambda b, p, c: (0, 0)),
                  pl.BlockSpec((eps, D_MODEL, D_EXPERT), lambda b, p, c: (p, 0, 0)),
                  pl.BlockSpec((eps, D_MODEL, D_EXPERT), lambda b, p, c: (p, 0, 0)),
                  pl.BlockSpec((eps, D_EXPERT, D_MODEL), lambda b, p, c: (p, 0, 0)),
                  pl.BlockSpec((1, D_MODEL), lambda b, p, c: (0, 0))],
        out_specs=pl.BlockSpec((rows, D_MODEL), lambda b, p, c: (b, 0)),
        scratch_shapes=[pltpu.VMEM((rows, D_MODEL), BF16)])
    return pl.pallas_call(
        functools.partial(_moe_kernel, tile_rows=tile_rows, tb=tb),
        grid_spec=grid_spec,
        out_shape=jax.ShapeDtypeStruct((S, D_MODEL), F32),
        compiler_params=pltpu.CompilerParams(dimension_semantics=("arbitrary", "arbitrary"),
                                             vmem_limit_bytes=V7X_VMEM_LIMIT),
        name="moe",
    )(cnt, x2, rank_t, gates_t, g2, wg, wu, wd, gf)


def kernel(x, positions, norm1_gain, w_in, hg_lower_bounds, hg_norm_gain, diff_lambda,
           diff_subln_gain, w_out, norm2_gain, router_group_w, router_group_b,
           router_expert_w, router_expert_b, moe_w_gate, moe_w_up, moe_w_down,
           final_norm_gain):
    B, S, _ = x.shape
    assert B == 1 and norm1_gain.shape[0] == 1
    layer = 0
    xs = x.reshape(S, D_MODEL)
    pos = positions.reshape(1, S).astype(F32)

    lb_cum = jnp.cumsum(jax.nn.softmax(hg_lower_bounds.astype(F32), axis=1), axis=1)
    lb = lb_cum[:, layer + 1] - lb_cum[:, 0]
    lam_p = diff_lambda[layer].astype(F32)
    lam = (jnp.exp(jnp.sum(lam_p[0] * lam_p[1])) - jnp.exp(jnp.sum(lam_p[2] * lam_p[3]))
           + LAMBDA_INIT).reshape(1)
    lane = jnp.arange(HEAD_W)
    inv_freq = jnp.float32(ROPE_THETA) ** (-jnp.arange(0, ROT_DIM, 2, dtype=F32) / ROT_DIM)
    in_rot = (lane % QK_DIM) < ROT_DIM
    n_freq = ROT_DIM // 2
    invf = inv_freq.reshape(n_freq, 1)
    sgn = jnp.where(in_rot, jnp.where((lane % QK_DIM) < n_freq, -1.0, 1.0),
                    0.0).astype(F32).reshape(1, HEAD_W)
    place = (in_rot[None, :] & (lane[None, :] % n_freq == jnp.arange(n_freq)[:, None])).astype(F32)
    pcos = jnp.tile(place, (4, 1)).astype(BF16)
    psin = jnp.tile(place * sgn, (4, 1)).astype(BF16)
    wr = jnp.zeros((D_MODEL, ROUTER_LANES), F32)
    wr = wr.at[:, :N_EXPERTS].set(router_expert_w[layer])
    wr = wr.at[:, N_EXPERTS:N_EXPERTS + N_GROUPS].set(router_group_w[layer])
    wr_hi = wr.astype(BF16)
    wr = jnp.concatenate([wr_hi, (wr - wr_hi.astype(F32)).astype(BF16)], axis=1)
    br = jnp.zeros((1, ROUTER_LANES), F32)
    br = br.at[0, :N_EXPERTS].set(router_expert_b[layer])
    br = br.at[0, N_EXPERTS:N_EXPERTS + N_GROUPS].set(router_group_b[layer])

    hq, kf, kb, lff, lfb, hi, hg, dq, dk, dv = _inproj(
        xs, pos, norm1_gain[layer].reshape(1, -1), w_in[layer].astype(BF16), lb, invf, sgn,
        pcos, psin)
    o_f, wg, wu = _hgrn(hq, kf, lff, hi, False, (moe_w_gate[layer], moe_w_up[layer]))
    o_b, wd = _hgrn(hq, kb, lfb, hi, True, (moe_w_down[layer],))
    o_da = _attn(lam, dq, dk, dv, diff_subln_gain[layer].reshape(1, -1))
    x2, rank_t, gates_t, cnt = _outproj(
        xs, o_f, o_b, hg, o_da, hg_norm_gain[layer].reshape(1, -1), w_out[layer].astype(BF16),
        norm2_gain[layer].reshape(1, -1), wr, br, tb=MOE_TOKEN_BLOCK)
    out = _moe(cnt, x2, rank_t, gates_t, norm2_gain[layer].reshape(1, -1), wg, wu, wd,
               final_norm_gain.reshape(1, -1), tb=MOE_TOKEN_BLOCK)
    return out.reshape(B, S, D_MODEL)
```

```python
import functools
import math

import jax
import jax.numpy as jnp
from jax import lax
from jax.experimental import pallas as pl
from jax.experimental.pallas import tpu as pltpu

F32 = jnp.float32
BF16 = jnp.bfloat16

D_MODEL = 1024
HEAD_W = 128
N_HEADS = 4
HALF_W = N_HEADS * HEAD_W
QK_DIM = 64
ROT_DIM = 16
ROPE_THETA = 500000.0
QUERY_GROUP = 512
VT_ROWS = HEAD_W + 16
HG_CHUNK = 128
HG_UNROLL = 8
N_GROUPS = 4
EXPERTS_PER_GROUP = 8
N_EXPERTS = 32
D_EXPERT = 512
NORM_EPS = 1e-6
SUBLN_EPS = 1e-5
LAMBDA_INIT = 0.8 - 0.6 * math.exp(-0.3 * 0)
LOG2E = 1.4426950408889634
ROUTER_LANES = 128
MOE_TOKEN_BLOCK = 512
V7X_VMEM_LIMIT = 58 * 1024 * 1024

NT_DIMS = (((1,), (1,)), ((), ()))
TN_DIMS = (((0,), (0,)), ((), ()))


def _rms(x, eps):
    return x * lax.rsqrt(jnp.mean(x * x, axis=-1, keepdims=True) + eps)


def _inproj_kernel(x_ref, pos_ref, g1_ref, w_ref, lb_ref, invf_ref, sgn_ref, pcos_ref, psin_ref,
                   hq_ref, kf_ref, kb_ref, lff_ref, lfb_ref, hi_ref, hg_ref,
                   dq_ref, dk_ref, dv_ref):
    x = x_ref[...]
    h = (_rms(x, NORM_EPS) * g1_ref[...]).astype(BF16)

    def proj(i):
        return jnp.dot(h, w_ref[:, i * HALF_W:(i + 1) * HALF_W], preferred_element_type=F32)

    hq_ref[...] = proj(0)
    for i, (k_ref, lf_ref) in enumerate(((kf_ref, lff_ref), (kb_ref, lfb_ref))):
        z = proj(1 + i)
        lb = lb_ref[i:i + 1, :]
        sg = jax.nn.sigmoid(z)
        lf_ref[...] = jnp.log(lb + (1.0 - lb) * sg)
        k_ref[...] = (1.0 - lb) * (1.0 - sg)
    hi_ref[...] = proj(3)
    hg_ref[...] = proj(4)

    ang_t = invf_ref[...] * pos_ref[...]

    def spread(t, place_ref):
        hi = t.astype(BF16)
        r1 = t - hi.astype(F32)
        mid = r1.astype(BF16)
        lo = (r1 - mid.astype(F32)).astype(BF16)
        pieces = jnp.concatenate([hi, mid, lo, jnp.zeros_like(hi)], axis=0)
        return lax.dot_general(pieces, place_ref[...], TN_DIMS, preferred_element_type=F32)

    sgn = sgn_ref[...]
    cos = jnp.concatenate([spread(jnp.cos(ang_t), pcos_ref) + jnp.where(sgn == 0.0, 1.0, 0.0)]
                          * N_HEADS, axis=1)
    sin = jnp.concatenate([spread(jnp.sin(ang_t), psin_ref)] * N_HEADS, axis=1)
    take_hi = jnp.concatenate([jnp.broadcast_to(sgn, (x.shape[0], HEAD_W))] * N_HEADS,
                              axis=1) < 0.0

    def rotary(t):
        partner = jnp.where(take_hi, pltpu.roll(t, HALF_W - ROT_DIM // 2, axis=1),
                            pltpu.roll(t, ROT_DIM // 2, axis=1))
        return t * cos + partner * sin

    dq_ref[...] = (rotary(proj(5)) * (QK_DIM ** -0.5 * LOG2E)).astype(BF16)
    dk_ref[...] = rotary(proj(6)).astype(BF16)
    v = proj(7)
    sub = lax.broadcasted_iota(jnp.int32, (VT_ROWS - HEAD_W, v.shape[0]), 0)
    ones_rows = jnp.where(sub == 0, 1.0, 0.0)
    pieces = []
    for hd in range(N_HEADS):
        pieces += [v[:, hd * HEAD_W:(hd + 1) * HEAD_W].T, ones_rows]
    dv_ref[...] = jnp.concatenate(pieces, axis=0).astype(BF16)


def _inproj(x, pos, g1, w_in, lb, invf, sgn, pcos, psin, tm=512):
    S = x.shape[0]
    row = lambda w: pl.BlockSpec((tm, w), lambda i: (i, 0))
    full = lambda a: pl.BlockSpec(a.shape, lambda i: (0, 0))
    f32o = jax.ShapeDtypeStruct((S, HALF_W), F32)
    bf16o = jax.ShapeDtypeStruct((S, HALF_W), BF16)
    return pl.pallas_call(
        _inproj_kernel,
        grid=(S // tm,),
        in_specs=[row(D_MODEL), pl.BlockSpec((1, tm), lambda i: (0, i)), full(g1), full(w_in),
                  full(lb), full(invf), full(sgn), full(pcos), full(psin)],
        out_specs=[row(HALF_W)] * 9 + [pl.BlockSpec((N_HEADS * VT_ROWS, tm), lambda i: (0, i))],
        out_shape=[f32o] * 7 + [bf16o, bf16o,
                                jax.ShapeDtypeStruct((N_HEADS * VT_ROWS, S), BF16)],
        compiler_params=pltpu.CompilerParams(dimension_semantics=("arbitrary",),
                                             vmem_limit_bytes=V7X_VMEM_LIMIT),
        name="inproj",
    )(x, pos, g1, w_in, lb, invf, sgn, pcos, psin)


def _chunk_cumsum(x, rev):
    n = x.shape[0]
    row = lax.broadcasted_iota(jnp.int32, x.shape, 0)
    d = 1
    while d < n:
        if rev:
            x = x + jnp.where(row < n - d, pltpu.roll(x, n - d, axis=0), 0.0)
        else:
            x = x + jnp.where(row >= d, pltpu.roll(x, d, axis=0), 0.0)
        d *= 2
    return x


def _pivot_rows(b, blk, rev):
    n = b.shape[0]
    half = blk // 2
    groups = []
    for g0 in range(0, n, 8):
        def piv(r):
            base = (r // blk) * blk
            return base + half if rev else base + half - 1
        if blk >= 8:
            p = piv(g0)
            groups.append(jnp.broadcast_to(b[p:p + 1, :], (8, b.shape[1])))
        else:
            sub = lax.broadcasted_iota(jnp.int32, (8, b.shape[1]), 0)
            acc = None
            for s0 in range(0, 8, blk):
                p = piv(g0 + s0)
                rowv = jnp.broadcast_to(b[p:p + 1, :], (8, b.shape[1]))
                acc = rowv if acc is None else jnp.where(sub >= s0, rowv, acc)
            groups.append(acc)
    return jnp.concatenate(groups, axis=0)


def _hgrn_kernel(q_ref, k_ref, lf_ref, v_ref, *rest, rev, n_chunks, n_cast):
    w_refs, o_ref, wb_refs, st_ref = (rest[:n_cast], rest[n_cast], rest[n_cast + 1:-1], rest[-1])
    for w_ref, wb_ref in zip(w_refs, wb_refs):
        wb_ref[...] = w_ref[...].astype(BF16)

    @pl.when(pl.program_id(1) == 0)
    def _():
        st_ref[...] = jnp.zeros_like(st_ref)

    C, W = HG_CHUNK, HEAD_W
    rt = lax.broadcasted_iota(jnp.int32, (C, C), 0)
    ct = lax.broadcasted_iota(jnp.int32, (C, C), 1)
    rowi = lax.broadcasted_iota(jnp.int32, (C, 2 * W), 0)
    eye = jnp.where(rt == ct, 1.0, 0.0)
    levels = []
    blk = C
    while blk >= 2:
        half = blk // 2
        q_side = (lambda r: (r % blk) < half) if rev else (lambda r: (r % blk) >= half)
        pair = ((rt // blk) == (ct // blk)) & q_side(rt) & jnp.logical_not(q_side(ct))
        levels.append((blk, jnp.where(q_side(rowi), 1.0, 0.0), jnp.where(pair, 1.0, 0.0)))
        blk = half

    def stack(a):
        return jnp.concatenate([a[:, :W], a[:, W:]], axis=0)

    def chunk(ci, st):
        c = (n_chunks - 1 - ci) if rev else ci
        sl = pl.ds(pl.multiple_of(c * C, C), C)
        q = q_ref[sl, :]
        k = k_ref[sl, :]
        vb = v_ref[sl, :].astype(BF16)
        b = _chunk_cumsum(lf_ref[sl, :], rev)
        tot = b[0:1, :] if rev else b[C - 1:C, :]

        o = lax.dot_general(stack((q * jnp.exp(b)).astype(BF16)), st.astype(BF16), NT_DIMS,
                            preferred_element_type=F32)
        kdec = (k * jnp.exp(tot - b)).astype(BF16)
        upd = lax.dot_general(vb, kdec, TN_DIMS, preferred_element_type=F32)
        dec = jnp.exp(tot)
        st_next = jnp.concatenate([st[:W] * dec[:, :W] + upd[:W, :W],
                                   st[W:] * dec[:, W:] + upd[W:, W:]], axis=0)

        qk = q * k
        scores = [eye * jnp.sum(qk[:, :W], axis=1, keepdims=True),
                  eye * jnp.sum(qk[:, W:], axis=1, keepdims=True)]
        for blk, q_side, pair in levels:
            w = jnp.exp2(jnp.abs(b - _pivot_rows(b, blk, rev)) * (-LOG2E))
            z = stack((jnp.where(q_side > 0.5, q, k) * w).astype(BF16))
            g = lax.dot_general(z, z, NT_DIMS, preferred_element_type=F32)
            scores = [scores[0] + pair * g[:C, :C], scores[1] + pair * g[C:, C:]]
        intra = jnp.dot(jnp.concatenate(scores, axis=0).astype(BF16), vb,
                        preferred_element_type=F32)
        o_ref[sl, :] = jnp.concatenate([o[:C, :W] + intra[:C, :W], o[C:, W:] + intra[C:, W:]],
                                       axis=1)
        return st_next

    st_ref[...] = lax.fori_loop(0, n_chunks, chunk, st_ref[...], unroll=HG_UNROLL)


def _hgrn(q, k, lf, v, rev, weights, rb=1024):
    S = q.shape[0]
    nb = S // rb
    n_pairs = N_HEADS // 2
    steps = n_pairs * nb
    idx = (lambda h, j: (nb - 1 - j, h)) if rev else (lambda h, j: (j, h))
    spec = pl.BlockSpec((rb, 2 * HEAD_W), idx)
    w_specs = []
    for w in weights:
        n_exp, rows, cols = w.shape
        per_exp = steps // n_exp
        assert per_exp * n_exp == steps and rows % per_exp == 0
        w_specs.append(pl.BlockSpec(
            (1, rows // per_exp, cols),
            lambda h, j, per_exp=per_exp: ((h * nb + j) // per_exp, (h * nb + j) % per_exp, 0)))
    return pl.pallas_call(
        functools.partial(_hgrn_kernel, rev=rev, n_chunks=rb // HG_CHUNK, n_cast=len(weights)),
        grid=(n_pairs, nb),
        in_specs=[spec] * 4 + w_specs,
        out_specs=[spec] + w_specs,
        out_shape=[jax.ShapeDtypeStruct((S, HALF_W), F32)]
        + [jax.ShapeDtypeStruct(w.shape, BF16) for w in weights],
        scratch_shapes=[pltpu.VMEM((2 * HEAD_W, HEAD_W), F32)],
        compiler_params=pltpu.CompilerParams(dimension_semantics=("arbitrary", "arbitrary"),
                                             vmem_limit_bytes=V7X_VMEM_LIMIT),
        name="hgrn_bwd" if rev else "hgrn_fwd",
    )(q, k, lf, v, *weights)


def _attn_kernel(lam_ref, q_ref, k_ref, vt_ref, gain_ref, o_ref, sa_ref, sb_ref, m_ref, acc_ref,
                 *, tk):
    n_kv = k_ref.shape[0] // tk
    tq = q_ref.shape[0]
    assert n_kv % 2 == 0 and tq % QUERY_GROUP == 0
    q = q_ref[...]
    lane = lax.broadcasted_iota(jnp.int32, q.shape, 1)
    qm = (jnp.where(lane < QK_DIM, q, jnp.zeros_like(q)),
          jnp.where(lane >= QK_DIM, q, jnp.zeros_like(q)))
    m_ref[...] = jnp.full(m_ref.shape, -jnp.inf, F32)
    acc_ref[...] = jnp.zeros_like(acc_ref)

    def chunk(j):
        return pl.ds(pl.multiple_of(j * tk, tk), tk)

    def softmax_pv(c, cols, s, vt):
        m_old = m_ref[c, :, cols]
        m_new = jnp.maximum(m_old, jnp.max(s, axis=0, keepdims=True))
        p = jnp.exp2(s - m_new).astype(BF16)
        acc_ref[c, :, cols] = jnp.exp2(m_old - m_new) * acc_ref[c, :, cols] + jnp.dot(
            vt, p, preferred_element_type=F32)
        m_ref[c, :, cols] = m_new

    def step(js, dst_ref, jp, src_ref):
        kc = None if dst_ref is None else k_ref[chunk(js), :]
        vt = None if src_ref is None else vt_ref[:, chunk(jp)]
        for c in range(2):
            for g in range(0, tq, QUERY_GROUP):
                cols = slice(g, g + QUERY_GROUP)
                if dst_ref is not None:
                    dst_ref[c, :, cols] = lax.dot_general(kc, qm[c][cols, :], NT_DIMS,
                                                          preferred_element_type=F32)
                if src_ref is not None:
                    softmax_pv(c, cols, src_ref[c, :, cols], vt)

    step(0, sa_ref, None, None)

    def two_steps(i, carry):
        step(2 * i + 1, sb_ref, 2 * i, sa_ref)
        step(2 * i + 2, sa_ref, 2 * i + 1, sb_ref)
        return carry

    lax.fori_loop(0, n_kv // 2 - 1, two_steps, 0)
    step(n_kv - 1, sb_ref, n_kv - 2, sa_ref)
    step(None, None, n_kv - 1, sb_ref)

    outs = []
    for c in range(2):
        a = acc_ref[c]
        outs.append(a[:HEAD_W, :] / a[HEAD_W:HEAD_W + 1, :])
    o = (outs[0] - lam_ref[0] * outs[1]).T
    o_ref[...] = _rms(o, SUBLN_EPS) * gain_ref[...] * (1.0 - LAMBDA_INIT)


def _attn(lam, dq, dk, dvt, gain, tq=2048, tk=512):
    S = dq.shape[0]
    return pl.pallas_call(
        functools.partial(_attn_kernel, tk=tk),
        grid=(N_HEADS, S // tq),
        in_specs=[pl.BlockSpec(memory_space=pltpu.SMEM),
                  pl.BlockSpec((tq, HEAD_W), lambda h, i: (i, h)),
                  pl.BlockSpec((S, HEAD_W), lambda h, i: (0, h)),
                  pl.BlockSpec((VT_ROWS, S), lambda h, i: (h, 0)),
                  pl.BlockSpec((1, HEAD_W), lambda h, i: (0, 0))],
        out_specs=pl.BlockSpec((tq, HEAD_W), lambda h, i: (i, h)),
        out_shape=jax.ShapeDtypeStruct((S, HALF_W), F32),
        scratch_shapes=[pltpu.VMEM((2, tk, tq), F32), pltpu.VMEM((2, tk, tq), F32),
                        pltpu.VMEM((2, 1, tq), F32), pltpu.VMEM((2, VT_ROWS, tq), F32)],
        compiler_params=pltpu.CompilerParams(dimension_semantics=("arbitrary", "arbitrary"),
                                             vmem_limit_bytes=V7X_VMEM_LIMIT),
        name="diff_attn",
    )(lam, dq, dk, dvt, gain)


def _outproj_kernel(x_ref, of_ref, ob_ref, hg_ref, oda_ref, hgain_ref, wo_ref, g2_ref,
                    wr_ref, br_ref, x2_ref, rank_t_ref, gates_t_ref, cnt_ref,
                    carry_ref, *, sub_blocks):
    i = pl.program_id(0)
    part = x_ref.shape[0]

    def project(rows):
        o = of_ref[rows, :] + ob_ref[rows, :]
        gate = hg_ref[rows, :]
        gate = gate * jax.nn.sigmoid(gate)
        parts = []
        for hd in range(N_HEADS):
            sl = slice(hd * HEAD_W, (hd + 1) * HEAD_W)
            parts.append(_rms(o[:, sl], NORM_EPS) * hgain_ref[...] * gate[:, sl])
        parts.append(oda_ref[rows, :])
        mixed = jnp.concatenate(parts, axis=1).astype(BF16)
        x2 = x_ref[rows, :] + jnp.dot(mixed, wo_ref[...], preferred_element_type=F32)
        x2_ref[rows, :] = x2
        return x2

    lane = lax.broadcasted_iota(jnp.int32, (part, ROUTER_LANES), 1)
    neg = -jnp.inf
    tri = jnp.where(lax.broadcasted_iota(jnp.int32, (part, part), 0)
                    >= lax.broadcasted_iota(jnp.int32, (part, part), 1), 1.0, 0.0).astype(BF16)

    def first_max(vals):
        mx = jnp.max(vals, axis=1, keepdims=True)
        idx = jnp.min(jnp.where(vals == mx, lane, ROUTER_LANES), axis=1, keepdims=True)
        return mx, idx

    def route(rows, x2, carry):
        h2 = _rms(x2, NORM_EPS) * g2_ref[...]
        h_hi = h2.astype(BF16)
        h_lo = (h2 - h_hi.astype(F32)).astype(BF16)
        t = jnp.dot(h_hi, wr_ref[...], preferred_element_type=F32)
        logits = (t[:, :ROUTER_LANES] + t[:, ROUTER_LANES:] + br_ref[...]
                  + jnp.dot(h_lo, wr_ref[:, :ROUTER_LANES], preferred_element_type=F32))
        gl = jnp.where((lane >= N_EXPERTS) & (lane < N_EXPERTS + N_GROUPS), logits, neg)
        gm, gidx = first_max(gl)
        g_p = 1.0 / jnp.sum(jnp.exp(gl - gm), axis=1, keepdims=True)
        el = jnp.where((lane < N_EXPERTS) & ((lane // EXPERTS_PER_GROUP) == (gidx - N_EXPERTS)),
                       logits, neg)
        em1, idx1 = first_max(el)
        em2, idx2 = first_max(jnp.where(lane == idx1, neg, el))
        e2 = jnp.exp(em2 - em1)
        w1 = g_p / (1.0 + e2)
        w2 = g_p * e2 / (1.0 + e2)
        is1 = lane == idx1
        is2 = lane == idx2
        gates = jnp.where(is1, w1, 0.0) + jnp.where(is2, w2, 0.0)
        assigned = jnp.where(is1 | is2, 1.0, 0.0)
        cum = jnp.dot(tri, assigned.astype(BF16), preferred_element_type=F32)
        rank = jnp.where(assigned > 0.0, carry + cum - 1.0, -1.0)
        rank_t_ref[:, rows] = rank.T
        gates_t_ref[:, rows] = gates.T
        return carry + cum[part - 1:part, :]

    @pl.when(i % sub_blocks == 0)
    def _():
        carry_ref[...] = jnp.zeros_like(carry_ref)

    rows = slice(0, part)
    carry = route(rows, project(rows), carry_ref[...])
    carry_ref[...] = carry
    cnt_ref[pl.ds(i // sub_blocks, 1), :] = carry.astype(jnp.int32)


def _outproj(x, o_f, o_b, hg, o_da, hgain, w_out, g2, wr, br, tb, tm=512):
    S = x.shape[0]
    row = lambda w: pl.BlockSpec((tm, w), lambda i: (i, 0))
    col = pl.BlockSpec((ROUTER_LANES, tm), lambda i: (0, i))
    full = lambda a: pl.BlockSpec(a.shape, lambda i: (0, 0))
    return pl.pallas_call(
        functools.partial(_outproj_kernel, sub_blocks=tb // tm),
        grid=(S // tm,),
        in_specs=[row(D_MODEL), row(HALF_W), row(HALF_W), row(HALF_W), row(HALF_W),
                  full(hgain), full(w_out), full(g2), full(wr), full(br)],
        out_specs=[row(D_MODEL), col, col,
                   pl.BlockSpec((S // tb, ROUTER_LANES), lambda i: (0, 0))],
        out_shape=[jax.ShapeDtypeStruct((S, D_MODEL), F32),
                   jax.ShapeDtypeStruct((ROUTER_LANES, S), F32),
                   jax.ShapeDtypeStruct((ROUTER_LANES, S), F32),
                   jax.ShapeDtypeStruct((S // tb, ROUTER_LANES), jnp.int32)],
        scratch_shapes=[pltpu.VMEM((1, ROUTER_LANES), F32)],
        compiler_params=pltpu.CompilerParams(dimension_semantics=("arbitrary",),
                                             vmem_limit_bytes=V7X_VMEM_LIMIT),
        name="outproj_router",
    )(x, o_f, o_b, hg, o_da, hgain, w_out, g2, wr, br)


def _moe_kernel(cnt_ref, x2_ref, rank_t_ref, gates_t_ref, g2_ref, wg_ref, wu_ref,
                wd_ref, gf_ref, out_ref, h2_ref, *, tile_rows, tb):
    b = pl.program_id(0)
    p = pl.program_id(1)
    R = tile_rows
    n_exp = wg_ref.shape[0]
    n_sub = x2_ref.shape[0] // tb

    @pl.when(p == 0)
    def _():
        x2 = x2_ref[...]
        h2_ref[...] = (_rms(x2, NORM_EPS) * g2_ref[...]).astype(BF16)
        out_ref[...] = x2

    experts = [n_exp * p + i for i in range(n_exp)]
    subs = [slice(s * tb, (s + 1) * tb) for s in range(n_sub)]
    rank_row = [[rank_t_ref[pl.ds(e, 1), sub] for e in experts] for sub in subs]
    gate_row = [[gates_t_ref[pl.ds(e, 1), sub] for e in experts] for sub in subs]
    n_max = jnp.int32(0)
    for s in range(n_sub):
        for e in experts:
            n_max = jnp.maximum(n_max, cnt_ref[n_sub * b + s, e])

    def tile(t, carry):
        base = (t * R).astype(F32)
        slot_col = lax.broadcasted_iota(jnp.int32, (R, 1), 0).astype(F32) + base
        sel, gather, xg = [], [], []
        for s, sub in enumerate(subs):
            sel.append([rank_row[s][i] == slot_col for i in range(n_exp)])
            gather.append(jnp.concatenate(
                [jnp.where(m, 1.0, 0.0).astype(BF16) for m in sel[s]], axis=0))
            xg.append(jnp.dot(gather[s], h2_ref[sub, :],
                              preferred_element_type=F32).astype(BF16))
        ys = [[] for _ in subs]
        for i in range(n_exp):
            xi = jnp.concatenate([xg[s][i * R:(i + 1) * R] for s in range(n_sub)], axis=0)
            a = jnp.dot(xi, wg_ref[i], preferred_element_type=F32)
            a = a * jax.nn.sigmoid(a) * jnp.dot(xi, wu_ref[i], preferred_element_type=F32)
            y = jnp.dot(a.astype(BF16), wd_ref[i], preferred_element_type=F32)
            for s in range(n_sub):
                g = jnp.sum(jnp.where(sel[s][i], gate_row[s][i], 0.0), axis=1, keepdims=True)
                ys[s].append((y[s * R:(s + 1) * R] * g).astype(BF16))
        for s, sub in enumerate(subs):
            out_ref[sub, :] += lax.dot_general(gather[s], jnp.concatenate(ys[s], axis=0),
                                               TN_DIMS, preferred_element_type=F32)
        return carry

    lax.fori_loop(0, (n_max + R - 1) // R, tile, 0)

    @pl.when(p == pl.num_programs(1) - 1)
    def _():
        out_ref[...] = _rms(out_ref[...], NORM_EPS) * gf_ref[...]


def _moe(cnt, x2, rank_t, gates_t, g2, wg, wu, wd, gf, tb, tile_rows=48,
         experts_per_step=4, blocks_per_step=4):
    S = x2.shape[0]
    eps = experts_per_step
    rows = tb * blocks_per_step
    grid_spec = pltpu.PrefetchScalarGridSpec(
        num_scalar_prefetch=1,
        grid=(S // rows, N_EXPERTS // eps),
        in_specs=[pl.BlockSpec((rows, D_MODEL), lambda b, p, c: (b, 0),
                               pipeline_mode=pl.Buffered(1)),
                  pl.BlockSpec((N_EXPERTS, rows), lambda b, p, c: (0, b)),
                  pl.BlockSpec((N_EXPERTS, rows), lambda b, p, c: (0, b)),
                  pl.BlockSpec((1, D_MODEL), lambda b, p, c: (0, 0)),
                  pl.BlockSpec((eps, D_MODEL, D_EXPERT), lambda b, p, c: (p, 0, 0)),
                  pl.BlockSpec((eps, D_MODEL, D_EXPERT), lambda b, p, c: (p, 0, 0)),
                  pl.BlockSpec((eps, D_EXPERT, D_MODEL), lambda b, p, c: (p, 0, 0)),
                  pl.BlockSpec((1, D_MODEL), lambda b, p, c: (0, 0))],
        out_specs=pl.BlockSpec((rows, D_MODEL), lambda b, p, c: (b, 0)),
        scratch_shapes=[pltpu.VMEM((rows, D_MODEL), BF16)])
    return pl.pallas_call(
        functools.partial(_moe_kernel, tile_rows=tile_rows, tb=tb),
        grid_spec=grid_spec,
        out_shape=jax.ShapeDtypeStruct((S, D_MODEL), F32),
        compiler_params=pltpu.CompilerParams(dimension_semantics=("arbitrary", "arbitrary"),
                                             vmem_limit_bytes=V7X_VMEM_LIMIT),
        name="moe",
    )(cnt, x2, rank_t, gates_t, g2, wg, wu, wd, gf)


def kernel(x, positions, norm1_gain, w_in, hg_lower_bounds, hg_norm_gain, diff_lambda,
           diff_subln_gain, w_out, norm2_gain, router_group_w, router_group_b,
           router_expert_w, router_expert_b, moe_w_gate, moe_w_up, moe_w_down,
           final_norm_gain):
    B, S, _ = x.shape
    assert B == 1 and norm1_gain.shape[0] == 1
    layer = 0
    xs = x.reshape(S, D_MODEL)
    pos = positions.reshape(1, S).astype(F32)

    lb_cum = jnp.cumsum(jax.nn.softmax(hg_lower_bounds.astype(F32), axis=1), axis=1)
    lb = lb_cum[:, layer + 1] - lb_cum[:, 0]
    lam_p = diff_lambda[layer].astype(F32)
    lam = (jnp.exp(jnp.sum(lam_p[0] * lam_p[1])) - jnp.exp(jnp.sum(lam_p[2] * lam_p[3]))
           + LAMBDA_INIT).reshape(1)
    lane = jnp.arange(HEAD_W)
    inv_freq = jnp.float32(ROPE_THETA) ** (-jnp.arange(0, ROT_DIM, 2, dtype=F32) / ROT_DIM)
    in_rot = (lane % QK_DIM) < ROT_DIM
    n_freq = ROT_DIM // 2
    invf = inv_freq.reshape(n_freq, 1)
    sgn = jnp.where(in_rot, jnp.where((lane % QK_DIM) < n_freq, -1.0, 1.0),
                    0.0).astype(F32).reshape(1, HEAD_W)
    place = (in_rot[None, :] & (lane[None, :] % n_freq == jnp.arange(n_freq)[:, None])).astype(F32)
    pcos = jnp.tile(place, (4, 1)).astype(BF16)
    psin = jnp.tile(place * sgn, (4, 1)).astype(BF16)
    wr = jnp.zeros((D_MODEL, ROUTER_LANES), F32)
    wr = wr.at[:, :N_EXPERTS].set(router_expert_w[layer])
    wr = wr.at[:, N_EXPERTS:N_EXPERTS + N_GROUPS].set(router_group_w[layer])
    wr_hi = wr.astype(BF16)
    wr = jnp.concatenate([wr_hi, (wr - wr_hi.astype(F32)).astype(BF16)], axis=1)
    br = jnp.zeros((1, ROUTER_LANES), F32)
    br = br.at[0, :N_EXPERTS].set(router_expert_b[layer])
    br = br.at[0, N_EXPERTS:N_EXPERTS + N_GROUPS].set(router_group_b[layer])

    hq, kf, kb, lff, lfb, hi, hg, dq, dk, dv = _inproj(
        xs, pos, norm1_gain[layer].reshape(1, -1), w_in[layer].astype(BF16), lb, invf, sgn,
        pcos, psin)
    o_f, wg, wu = _hgrn(hq, kf, lff, hi, False, (moe_w_gate[layer], moe_w_up[layer]))
    o_b, wd = _hgrn(hq, kb, lfb, hi, True, (moe_w_down[layer],))
    o_da = _attn(lam, dq, dk, dv, diff_subln_gain[layer].reshape(1, -1))
    x2, rank_t, gates_t, cnt = _outproj(
        xs, o_f, o_b, hg, o_da, hg_norm_gain[layer].reshape(1, -1), w_out[layer].astype(BF16),
        norm2_gain[layer].reshape(1, -1), wr, br, tb=MOE_TOKEN_BLOCK)
    out = _moe(cnt, x2, rank_t, gates_t, norm2_gain[layer].reshape(1, -1), wg, wu, wd,
               final_norm_gain.reshape(1, -1), tb=MOE_TOKEN_BLOCK)
    return out.reshape(B, S, D_MODEL)
```

```python
import functools
import math

import jax
import jax.numpy as jnp
from jax import lax
from jax.experimental import pallas as pl
from jax.experimental.pallas import tpu as pltpu

F32 = jnp.float32
BF16 = jnp.bfloat16

D_MODEL = 1024
HEAD_W = 128
N_HEADS = 4
HALF_W = N_HEADS * HEAD_W
QK_DIM = 64
ROT_DIM = 16
ROPE_THETA = 500000.0
QUERY_GROUP = 512
VT_ROWS = HEAD_W + 16
HG_CHUNK = 128
HG_UNROLL = 8
N_GROUPS = 4
EXPERTS_PER_GROUP = 8
N_EXPERTS = 32
D_EXPERT = 512
NORM_EPS = 1e-6
SUBLN_EPS = 1e-5
LAMBDA_INIT = 0.8 - 0.6 * math.exp(-0.3 * 0)
LOG2E = 1.4426950408889634
ROUTER_LANES = 128
MOE_TOKEN_BLOCK = 512
V7X_VMEM_LIMIT = 58 * 1024 * 1024

NT_DIMS = (((1,), (1,)), ((), ()))
TN_DIMS = (((0,), (0,)), ((), ()))


def _rms(x, eps):
    return x * lax.rsqrt(jnp.mean(x * x, axis=-1, keepdims=True) + eps)


def _inproj_kernel(x_ref, pos_ref, g1_ref, w_ref, lb_ref, invf_ref, sgn_ref, pcos_ref, psin_ref,
                   hq_ref, kf_ref, kb_ref, lff_ref, lfb_ref, hi_ref, hg_ref,
                   dq_ref, dk_ref, dv_ref):
    x = x_ref[...]
    h = (_rms(x, NORM_EPS) * g1_ref[...]).astype(BF16)

    def proj(i):
        return jnp.dot(h, w_ref[:, i * HALF_W:(i + 1) * HALF_W], preferred_element_type=F32)

    hq_ref[...] = proj(0)
    for i, (k_ref, lf_ref) in enumerate(((kf_ref, lff_ref), (kb_ref, lfb_ref))):
        z = proj(1 + i)
        lb = lb_ref[i:i + 1, :]
        sg = jax.nn.sigmoid(z)
        lf_ref[...] = jnp.log(lb + (1.0 - lb) * sg)
        k_ref[...] = (1.0 - lb) * (1.0 - sg)
    hi_ref[...] = proj(3)
    hg_ref[...] = proj(4)

    ang_t = invf_ref[...] * pos_ref[...]

    def spread(t, place_ref):
        hi = t.astype(BF16)
        r1 = t - hi.astype(F32)
        mid = r1.astype(BF16)
        lo = (r1 - mid.astype(F32)).astype(BF16)
        pieces = jnp.concatenate([hi, mid, lo, jnp.zeros_like(hi)], axis=0)
        return lax.dot_general(pieces, place_ref[...], TN_DIMS, preferred_element_type=F32)

    sgn = sgn_ref[...]
    cos = jnp.concatenate([spread(jnp.cos(ang_t), pcos_ref) + jnp.where(sgn == 0.0, 1.0, 0.0)]
                          * N_HEADS, axis=1)
    sin = jnp.concatenate([spread(jnp.sin(ang_t), psin_ref)] * N_HEADS, axis=1)
    take_hi = jnp.concatenate([jnp.broadcast_to(sgn, (x.shape[0], HEAD_W))] * N_HEADS,
                              axis=1) < 0.0

    def rotary(t):
        partner = jnp.where(take_hi, pltpu.roll(t, HALF_W - ROT_DIM // 2, axis=1),
                            pltpu.roll(t, ROT_DIM // 2, axis=1))
        return t * cos + partner * sin

    dq_ref[...] = (rotary(proj(5)) * (QK_DIM ** -0.5 * LOG2E)).astype(BF16)
    dk_ref[...] = rotary(proj(6)).astype(BF16)
    v = proj(7)
    sub = lax.broadcasted_iota(jnp.int32, (VT_ROWS - HEAD_W, v.shape[0]), 0)
    ones_rows = jnp.where(sub == 0, 1.0, 0.0)
    pieces = []
    for hd in range(N_HEADS):
        pieces += [v[:, hd * HEAD_W:(hd + 1) * HEAD_W].T, ones_rows]
    dv_ref[...] = jnp.concatenate(pieces, axis=0).astype(BF16)


def _inproj(x, pos, g1, w_in, lb, invf, sgn, pcos, psin, tm=512):
    S = x.shape[0]
    row = lambda w: pl.BlockSpec((tm, w), lambda i: (i, 0))
    full = lambda a: pl.BlockSpec(a.shape, lambda i: (0, 0))
    f32o = jax.ShapeDtypeStruct((S, HALF_W), F32)
    bf16o = jax.ShapeDtypeStruct((S, HALF_W), BF16)
    return pl.pallas_call(
        _inproj_kernel,
        grid=(S // tm,),
        in_specs=[row(D_MODEL), pl.BlockSpec((1, tm), lambda i: (0, i)), full(g1), full(w_in),
                  full(lb), full(invf), full(sgn), full(pcos), full(psin)],
        out_specs=[row(HALF_W)] * 9 + [pl.BlockSpec((N_HEADS * VT_ROWS, tm), lambda i: (0, i))],
        out_shape=[f32o] * 7 + [bf16o, bf16o,
                                jax.ShapeDtypeStruct((N_HEADS * VT_ROWS, S), BF16)],
        compiler_params=pltpu.CompilerParams(dimension_semantics=("arbitrary",),
                                             vmem_limit_bytes=V7X_VMEM_LIMIT),
        name="inproj",
    )(x, pos, g1, w_in, lb, invf, sgn, pcos, psin)


def _chunk_cumsum(x, rev):
    n = x.shape[0]
    row = lax.broadcasted_iota(jnp.int32, x.shape, 0)
    d = 1
    while d < n:
        if rev:
            x = x + jnp.where(row < n - d, pltpu.roll(x, n - d, axis=0), 0.0)
        else:
            x = x + jnp.where(row >= d, pltpu.roll(x, d, axis=0), 0.0)
        d *= 2
    return x


def _pivot_rows(b, blk, rev):
    n = b.shape[0]
    half = blk // 2
    groups = []
    for g0 in range(0, n, 8):
        def piv(r):
            base = (r // blk) * blk
            return base + half if rev else base + half - 1
        if blk >= 8:
            p = piv(g0)
            groups.append(jnp.broadcast_to(b[p:p + 1, :], (8, b.shape[1])))
        else:
            sub = lax.broadcasted_iota(jnp.int32, (8, b.shape[1]), 0)
            acc = None
            for s0 in range(0, 8, blk):
                p = piv(g0 + s0)
                rowv = jnp.broadcast_to(b[p:p + 1, :], (8, b.shape[1]))
                acc = rowv if acc is None else jnp.where(sub >= s0, rowv, acc)
            groups.append(acc)
    return jnp.concatenate(groups, axis=0)


def _hgrn_kernel(q_ref, k_ref, lf_ref, v_ref, *rest, rev, n_chunks, n_cast):
    w_refs, o_ref, wb_refs, st_ref = (rest[:n_cast], rest[n_cast], rest[n_cast + 1:-1], rest[-1])
    for w_ref, wb_ref in zip(w_refs, wb_refs):
        wb_ref[...] = w_ref[...].astype(BF16)

    @pl.when(pl.program_id(1) == 0)
    def _():
        st_ref[...] = jnp.zeros_like(st_ref)

    C, W = HG_CHUNK, HEAD_W
    rt = lax.broadcasted_iota(jnp.int32, (C, C), 0)
    ct = lax.broadcasted_iota(jnp.int32, (C, C), 1)
    rowi = lax.broadcasted_iota(jnp.int32, (C, 2 * W), 0)
    eye = jnp.where(rt == ct, 1.0, 0.0)
    levels = []
    blk = C
    while blk >= 2:
        half = blk // 2
        q_side = (lambda r: (r % blk) < half) if rev else (lambda r: (r % blk) >= half)
        pair = ((rt // blk) == (ct // blk)) & q_side(rt) & jnp.logical_not(q_side(ct))
        levels.append((blk, jnp.where(q_side(rowi), 1.0, 0.0), jnp.where(pair, 1.0, 0.0)))
        blk = half

    def stack(a):
        return jnp.concatenate([a[:, :W], a[:, W:]], axis=0)

    def chunk(ci, st):
        c = (n_chunks - 1 - ci) if rev else ci
        sl = pl.ds(pl.multiple_of(c * C, C), C)
        q = q_ref[sl, :]
        k = k_ref[sl, :]
        vb = v_ref[sl, :].astype(BF16)
        b = _chunk_cumsum(lf_ref[sl, :], rev)
        tot = b[0:1, :] if rev else b[C - 1:C, :]

        o = lax.dot_general(stack((q * jnp.exp(b)).astype(BF16)), st.astype(BF16), NT_DIMS,
                            preferred_element_type=F32)
        kdec = (k * jnp.exp(tot - b)).astype(BF16)
        upd = lax.dot_general(vb, kdec, TN_DIMS, preferred_element_type=F32)
        dec = jnp.exp(tot)
        st_next = jnp.concatenate([st[:W] * dec[:, :W] + upd[:W, :W],
                                   st[W:] * dec[:, W:] + upd[W:, W:]], axis=0)

        qk = q * k
        scores = [eye * jnp.sum(qk[:, :W], axis=1, keepdims=True),
                  eye * jnp.sum(qk[:, W:], axis=1, keepdims=True)]
        for blk, q_side, pair in levels:
            w = jnp.exp2(jnp.abs(b - _pivot_rows(b, blk, rev)) * (-LOG2E))
            z = stack((jnp.where(q_side > 0.5, q, k) * w).astype(BF16))
            g = lax.dot_general(z, z, NT_DIMS, preferred_element_type=F32)
            scores = [scores[0] + pair * g[:C, :C], scores[1] + pair * g[C:, C:]]
        intra = jnp.dot(jnp.concatenate(scores, axis=0).astype(BF16), vb,
                        preferred_element_type=F32)
        o_ref[sl, :] = jnp.concatenate([o[:C, :W] + intra[:C, :W], o[C:, W:] + intra[C:, W:]],
                                       axis=1)
        return st_next

    st_ref[...] = lax.fori_loop(0, n_chunks, chunk, st_ref[...], unroll=HG_UNROLL)


def _hgrn(q, k, lf, v, rev, weights, rb=1024):
    S = q.shape[0]
    nb = S // rb
    n_pairs = N_HEADS // 2
    steps = n_pairs * nb
    idx = (lambda h, j: (nb - 1 - j, h)) if rev else (lambda h, j: (j, h))
    spec = pl.BlockSpec((rb, 2 * HEAD_W), idx)
    w_specs = []
    for w in weights:
        n_exp, rows, cols = w.shape
        per_exp = steps // n_exp
        assert per_exp * n_exp == steps and rows % per_exp == 0
        w_specs.append(pl.BlockSpec(
            (1, rows // per_exp, cols),
            lambda h, j, per_exp=per_exp: ((h * nb + j) // per_exp, (h * nb + j) % per_exp, 0)))
    return pl.pallas_call(
        functools.partial(_hgrn_kernel, rev=rev, n_chunks=rb // HG_CHUNK, n_cast=len(weights)),
        grid=(n_pairs, nb),
        in_specs=[spec] * 4 + w_specs,
        out_specs=[spec] + w_specs,
        out_shape=[jax.ShapeDtypeStruct((S, HALF_W), F32)]
        + [jax.ShapeDtypeStruct(w.shape, BF16) for w in weights],
        scratch_shapes=[pltpu.VMEM((2 * HEAD_W, HEAD_W), F32)],
        compiler_params=pltpu.CompilerParams(dimension_semantics=("arbitrary", "arbitrary"),
                                             vmem_limit_bytes=V7X_VMEM_LIMIT),
        name="hgrn_bwd" if rev else "hgrn_fwd",
    )(q, k, lf, v, *weights)


def _attn_kernel(lam_ref, q_ref, k_ref, vt_ref, gain_ref, o_ref, sa_ref, sb_ref, m_ref, acc_ref,
                 *, tk):
    n_kv = k_ref.shape[0] // tk
    tq = q_ref.shape[0]
    assert n_kv % 2 == 0 and tq % QUERY_GROUP == 0
    q = q_ref[...]
    lane = lax.broadcasted_iota(jnp.int32, q.shape, 1)
    qm = (jnp.where(lane < QK_DIM, q, jnp.zeros_like(q)),
          jnp.where(lane >= QK_DIM, q, jnp.zeros_like(q)))
    m_ref[...] = jnp.full(m_ref.shape, -jnp.inf, F32)
    acc_ref[...] = jnp.zeros_like(acc_ref)

    def chunk(j):
        return pl.ds(pl.multiple_of(j * tk, tk), tk)

    def softmax_pv(c, cols, s, vt):
        m_old = m_ref[c, :, cols]
        m_new = jnp.maximum(m_old, jnp.max(s, axis=0, keepdims=True))
        p = jnp.exp2(s - m_new).astype(BF16)
        acc_ref[c, :, cols] = jnp.exp2(m_old - m_new) * acc_ref[c, :, cols] + jnp.dot(
            vt, p, preferred_element_type=F32)
        m_ref[c, :, cols] = m_new

    def step(js, dst_ref, jp, src_ref):
        kc = None if dst_ref is None else k_ref[chunk(js), :]
        vt = None if src_ref is None else vt_ref[:, chunk(jp)]
        for c in range(2):
            for g in range(0, tq, QUERY_GROUP):
                cols = slice(g, g + QUERY_GROUP)
                if dst_ref is not None:
                    dst_ref[c, :, cols] = lax.dot_general(kc, qm[c][cols, :], NT_DIMS,
                                                          preferred_element_type=F32)
                if src_ref is not None:
                    softmax_pv(c, cols, src_ref[c, :, cols], vt)

    step(0, sa_ref, None, None)

    def two_steps(i, carry):
        step(2 * i + 1, sb_ref, 2 * i, sa_ref)
        step(2 * i + 2, sa_ref, 2 * i + 1, sb_ref)
        return carry

    lax.fori_loop(0, n_kv // 2 - 1, two_steps, 0)
    step(n_kv - 1, sb_ref, n_kv - 2, sa_ref)
    step(None, None, n_kv - 1, sb_ref)

    outs = []
    for c in range(2):
        a = acc_ref[c]
        outs.append(a[:HEAD_W, :] / a[HEAD_W:HEAD_W + 1, :])
    o = (outs[0] - lam_ref[0] * outs[1]).T
    o_ref[...] = _rms(o, SUBLN_EPS) * gain_ref[...] * (1.0 - LAMBDA_INIT)


def _attn(lam, dq, dk, dvt, gain, tq=2048, tk=512):
    S = dq.shape[0]
    return pl.pallas_call(
        functools.partial(_attn_kernel, tk=tk),
        grid=(N_HEADS, S // tq),
        in_specs=[pl.BlockSpec(memory_space=pltpu.SMEM),
                  pl.BlockSpec((tq, HEAD_W), lambda h, i: (i, h)),
                  pl.BlockSpec((S, HEAD_W), lambda h, i: (0, h)),
                  pl.BlockSpec((VT_ROWS, S), lambda h, i: (h, 0)),
                  pl.BlockSpec((1, HEAD_W), lambda h, i: (0, 0))],
        out_specs=pl.BlockSpec((tq, HEAD_W), lambda h, i: (i, h)),
        out_shape=jax.ShapeDtypeStruct((S, HALF_W), F32),
        scratch_shapes=[pltpu.VMEM((2, tk, tq), F32), pltpu.VMEM((2, tk, tq), F32),
                        pltpu.VMEM((2, 1, tq), F32), pltpu.VMEM((2, VT_ROWS, tq), F32)],
        compiler_params=pltpu.CompilerParams(dimension_semantics=("arbitrary", "arbitrary"),
                                             vmem_limit_bytes=V7X_VMEM_LIMIT),
        name="diff_attn",
    )(lam, dq, dk, dvt, gain)


def _outproj_kernel(x_ref, of_ref, ob_ref, hg_ref, oda_ref, hgain_ref, wo_ref, g2_ref,
                    wr_ref, br_ref, x2_ref, rank_t_ref, gates_t_ref, cnt_ref,
                    carry_ref, *, sub_blocks):
    i = pl.program_id(0)
    part = x_ref.shape[0]

    def project(rows):
        o = of_ref[rows, :] + ob_ref[rows, :]
        gate = hg_ref[rows, :]
        gate = gate * jax.nn.sigmoid(gate)
        parts = []
        for hd in range(N_HEADS):
            sl = slice(hd * HEAD_W, (hd + 1) * HEAD_W)
            parts.append(_rms(o[:, sl], NORM_EPS) * hgain_ref[...] * gate[:, sl])
        parts.append(oda_ref[rows, :])
        mixed = jnp.concatenate(parts, axis=1).astype(BF16)
        x2 = x_ref[rows, :] + jnp.dot(mixed, wo_ref[...], preferred_element_type=F32)
        x2_ref[rows, :] = x2
        return x2

    lane = lax.broadcasted_iota(jnp.int32, (part, ROUTER_LANES), 1)
    neg = -jnp.inf
    tri = jnp.where(lax.broadcasted_iota(jnp.int32, (part, part), 0)
                    >= lax.broadcasted_iota(jnp.int32, (part, part), 1), 1.0, 0.0).astype(BF16)

    def first_max(vals):
        mx = jnp.max(vals, axis=1, keepdims=True)
        idx = jnp.min(jnp.where(vals == mx, lane, ROUTER_LANES), axis=1, keepdims=True)
        return mx, idx

    def route(rows, x2, carry):
        h2 = _rms(x2, NORM_EPS) * g2_ref[...]
        h_hi = h2.astype(BF16)
        h_lo = (h2 - h_hi.astype(F32)).astype(BF16)
        t = jnp.dot(h_hi, wr_ref[...], preferred_element_type=F32)
        logits = (t[:, :ROUTER_LANES] + t[:, ROUTER_LANES:] + br_ref[...]
                  + jnp.dot(h_lo, wr_ref[:, :ROUTER_LANES], preferred_element_type=F32))
        gl = jnp.where((lane >= N_EXPERTS) & (lane < N_EXPERTS + N_GROUPS), logits, neg)
        gm, gidx = first_max(gl)
        g_p = 1.0 / jnp.sum(jnp.exp(gl - gm), axis=1, keepdims=True)
        el = jnp.where((lane < N_EXPERTS) & ((lane // EXPERTS_PER_GROUP) == (gidx - N_EXPERTS)),
                       logits, neg)
        em1, idx1 = first_max(el)
        em2, idx2 = first_max(jnp.where(lane == idx1, neg, el))
        e2 = jnp.exp(em2 - em1)
        w1 = g_p / (1.0 + e2)
        w2 = g_p * e2 / (1.0 + e2)
        is1 = lane == idx1
        is2 = lane == idx2
        gates = jnp.where(is1, w1, 0.0) + jnp.where(is2, w2, 0.0)
        assigned = jnp.where(is1 | is2, 1.0, 0.0)
        cum = jnp.dot(tri, assigned.astype(BF16), preferred_element_type=F32)
        rank = jnp.where(assigned > 0.0, carry + cum - 1.0, -1.0)
        rank_t_ref[:, rows] = rank.T
        gates_t_ref[:, rows] = gates.T
        return carry + cum[part - 1:part, :]

    @pl.when(i % sub_blocks == 0)
    def _():
        carry_ref[...] = jnp.zeros_like(carry_ref)

    rows = slice(0, part)
    carry = route(rows, project(rows), carry_ref[...])
    carry_ref[...] = carry
    cnt_ref[pl.ds(i // sub_blocks, 1), :] = carry.astype(jnp.int32)


def _outproj(x, o_f, o_b, hg, o_da, hgain, w_out, g2, wr, br, tb, tm=512):
    S = x.shape[0]
    row = lambda w: pl.BlockSpec((tm, w), lambda i: (i, 0))
    col = pl.BlockSpec((ROUTER_LANES, tm), lambda i: (0, i))
    full = lambda a: pl.BlockSpec(a.shape, lambda i: (0, 0))
    return pl.pallas_call(
        functools.partial(_outproj_kernel, sub_blocks=tb // tm),
        grid=(S // tm,),
        in_specs=[row(D_MODEL), row(HALF_W), row(HALF_W), row(HALF_W), row(HALF_W),
                  full(hgain), full(w_out), full(g2), full(wr), full(br)],
        out_specs=[row(D_MODEL), col, col,
                   pl.BlockSpec((S // tb, ROUTER_LANES), lambda i: (0, 0))],
        out_shape=[jax.ShapeDtypeStruct((S, D_MODEL), F32),
                   jax.ShapeDtypeStruct((ROUTER_LANES, S), F32),
                   jax.ShapeDtypeStruct((ROUTER_LANES, S), F32),
                   jax.ShapeDtypeStruct((S // tb, ROUTER_LANES), jnp.int32)],
        scratch_shapes=[pltpu.VMEM((1, ROUTER_LANES), F32)],
        compiler_params=pltpu.CompilerParams(dimension_semantics=("arbitrary",),
                                             vmem_limit_bytes=V7X_VMEM_LIMIT),
        name="outproj_router",
    )(x, o_f, o_b, hg, o_da, hgain, w_out, g2, wr, br)


def _moe_kernel(cnt_ref, x2_ref, rank_t_ref, gates_t_ref, g2_ref, *rest, tile_rows, tb, n_exp):
    b = pl.program_id(0)
    p = pl.program_id(1)
    R = tile_rows
    wg_ref, wu_ref, wd_ref = (rest[:n_exp], rest[n_exp:2 * n_exp], rest[2 * n_exp:3 * n_exp])
    gf_ref, out_ref, h2_ref = rest[3 * n_exp:]
    n_sub = x2_ref.shape[0] // tb

    @pl.when(p == 0)
    def _():
        x2 = x2_ref[...]
        h2_ref[...] = (_rms(x2, NORM_EPS) * g2_ref[...]).astype(BF16)
        out_ref[...] = x2

    experts = [n_exp * p + i for i in range(n_exp)]
    subs = [slice(s * tb, (s + 1) * tb) for s in range(n_sub)]
    rank_row = [[rank_t_ref[pl.ds(e, 1), sub] for e in experts] for sub in subs]
    gate_row = [[gates_t_ref[pl.ds(e, 1), sub] for e in experts] for sub in subs]
    n_max = jnp.int32(0)
    for s in range(n_sub):
        for e in experts:
            n_max = jnp.maximum(n_max, cnt_ref[n_sub * b + s, e])

    def tile(t, carry):
        base = (t * R).astype(F32)
        slot_col = lax.broadcasted_iota(jnp.int32, (R, 1), 0).astype(F32) + base
        sel, gather, xg = [], [], []
        for s, sub in enumerate(subs):
            sel.append([rank_row[s][i] == slot_col for i in range(n_exp)])
            gather.append(jnp.concatenate(
                [jnp.where(m, 1.0, 0.0).astype(BF16) for m in sel[s]], axis=0))
            xg.append(jnp.dot(gather[s], h2_ref[sub, :],
                              preferred_element_type=F32).astype(BF16))
        ys = [[] for _ in subs]
        for i in range(n_exp):
            xi = jnp.concatenate([xg[s][i * R:(i + 1) * R] for s in range(n_sub)], axis=0)
            a = jnp.dot(xi, wg_ref[i][0], preferred_element_type=F32)
            a = a * jax.nn.sigmoid(a) * jnp.dot(xi, wu_ref[i][0], preferred_element_type=F32)
            y = jnp.dot(a.astype(BF16), wd_ref[i][0], preferred_element_type=F32)
            for s in range(n_sub):
                g = jnp.sum(jnp.where(sel[s][i], gate_row[s][i], 0.0), axis=1, keepdims=True)
                ys[s].append((y[s * R:(s + 1) * R] * g).astype(BF16))
        for s, sub in enumerate(subs):
            out_ref[sub, :] += lax.dot_general(gather[s], jnp.concatenate(ys[s], axis=0),
                                               TN_DIMS, preferred_element_type=F32)
        return carry

    lax.fori_loop(0, (n_max + R - 1) // R, tile, 0)

    @pl.when(p == pl.num_programs(1) - 1)
    def _():
        out_ref[...] = _rms(out_ref[...], NORM_EPS) * gf_ref[...]


def _moe(cnt, x2, rank_t, gates_t, g2, wg, wu, wd, gf, tb, tile_rows=48,
         experts_per_step=4, blocks_per_step=4):
    S = x2.shape[0]
    eps = experts_per_step
    rows = tb * blocks_per_step
    grid_spec = pltpu.PrefetchScalarGridSpec(
        num_scalar_prefetch=1,
        grid=(S // rows, N_EXPERTS // eps),
        in_specs=[pl.BlockSpec((rows, D_MODEL), lambda b, p, c: (b, 0),
                               pipeline_mode=pl.Buffered(1)),
                  pl.BlockSpec((N_EXPERTS, rows), lambda b, p, c: (0, b)),
                  pl.BlockSpec((N_EXPERTS, rows), lambda b, p, c: (0, b)),
                  pl.BlockSpec((1, D_MODEL), lambda b, p, c: (0, 0))]
        + [pl.BlockSpec((1,) + w.shape[1:], lambda b, p, c, i=i: (eps * p + i, 0, 0))
           for w in (wg, wu, wd) for i in range(eps)]
        + [pl.BlockSpec((1, D_MODEL), lambda b, p, c: (0, 0))],
        out_specs=pl.BlockSpec((rows, D_MODEL), lambda b, p, c: (b, 0)),
        scratch_shapes=[pltpu.VMEM((rows, D_MODEL), BF16)])
    return pl.pallas_call(
        functools.partial(_moe_kernel, tile_rows=tile_rows, tb=tb, n_exp=eps),
        grid_spec=grid_spec,
        out_shape=jax.ShapeDtypeStruct((S, D_MODEL), F32),
        compiler_params=pltpu.CompilerParams(dimension_semantics=("arbitrary", "arbitrary"),
                                             vmem_limit_bytes=V7X_VMEM_LIMIT),
        name="moe",
    )(cnt, x2, rank_t, gates_t, g2, *([wg] * eps + [wu] * eps + [wd] * eps), gf)


def kernel(x, positions, norm1_gain, w_in, hg_lower_bounds, hg_norm_gain, diff_lambda,
           diff_subln_gain, w_out, norm2_gain, router_group_w, router_group_b,
           router_expert_w, router_expert_b, moe_w_gate, moe_w_up, moe_w_down,
           final_norm_gain):
    B, S, _ = x.shape
    assert B == 1 and norm1_gain.shape[0] == 1
    layer = 0
    xs = x.reshape(S, D_MODEL)
    pos = positions.reshape(1, S).astype(F32)

    lb_cum = jnp.cumsum(jax.nn.softmax(hg_lower_bounds.astype(F32), axis=1), axis=1)
    lb = lb_cum[:, layer + 1] - lb_cum[:, 0]
    lam_p = diff_lambda[layer].astype(F32)
    lam = (jnp.exp(jnp.sum(lam_p[0] * lam_p[1])) - jnp.exp(jnp.sum(lam_p[2] * lam_p[3]))
           + LAMBDA_INIT).reshape(1)
    lane = jnp.arange(HEAD_W)
    inv_freq = jnp.float32(ROPE_THETA) ** (-jnp.arange(0, ROT_DIM, 2, dtype=F32) / ROT_DIM)
    in_rot = (lane % QK_DIM) < ROT_DIM
    n_freq = ROT_DIM // 2
    invf = inv_freq.reshape(n_freq, 1)
    sgn = jnp.where(in_rot, jnp.where((lane % QK_DIM) < n_freq, -1.0, 1.0),
                    0.0).astype(F32).reshape(1, HEAD_W)
    place = (in_rot[None, :] & (lane[None, :] % n_freq == jnp.arange(n_freq)[:, None])).astype(F32)
    pcos = jnp.tile(place, (4, 1)).astype(BF16)
    psin = jnp.tile(place * sgn, (4, 1)).astype(BF16)
    wr = jnp.zeros((D_MODEL, ROUTER_LANES), F32)
    wr = wr.at[:, :N_EXPERTS].set(router_expert_w[layer])
    wr = wr.at[:, N_EXPERTS:N_EXPERTS + N_GROUPS].set(router_group_w[layer])
    wr_hi = wr.astype(BF16)
    wr = jnp.concatenate([wr_hi, (wr - wr_hi.astype(F32)).astype(BF16)], axis=1)
    br = jnp.zeros((1, ROUTER_LANES), F32)
    br = br.at[0, :N_EXPERTS].set(router_expert_b[layer])
    br = br.at[0, N_EXPERTS:N_EXPERTS + N_GROUPS].set(router_group_b[layer])

    hq, kf, kb, lff, lfb, hi, hg, dq, dk, dv = _inproj(
        xs, pos, norm1_gain[layer].reshape(1, -1), w_in[layer].astype(BF16), lb, invf, sgn,
        pcos, psin)
    o_f, wg, wu = _hgrn(hq, kf, lff, hi, False, (moe_w_gate[layer], moe_w_up[layer]))
    o_b, wd = _hgrn(hq, kb, lfb, hi, True, (moe_w_down[layer],))
    o_da = _attn(lam, dq, dk, dv, diff_subln_gain[layer].reshape(1, -1))
    x2, rank_t, gates_t, cnt = _outproj(
        xs, o_f, o_b, hg, o_da, hg_norm_gain[layer].reshape(1, -1), w_out[layer].astype(BF16),
        norm2_gain[layer].reshape(1, -1), wr, br, tb=MOE_TOKEN_BLOCK)
    out = _moe(cnt, x2, rank_t, gates_t, norm2_gain[layer].reshape(1, -1), wg, wu, wd,
               final_norm_gain.reshape(1, -1), tb=MOE_TOKEN_BLOCK)
    return out.reshape(B, S, D_MODEL)
```

```python
import functools
import math

import jax
import jax.numpy as jnp
from jax import lax
from jax.experimental import pallas as pl
from jax.experimental.pallas import tpu as pltpu

F32 = jnp.float32
BF16 = jnp.bfloat16

D_MODEL = 1024
HEAD_W = 128
N_HEADS = 4
HALF_W = N_HEADS * HEAD_W
QK_DIM = 64
ROT_DIM = 16
ROPE_THETA = 500000.0
QUERY_GROUP = 512
VT_ROWS = HEAD_W + 16
HG_CHUNK = 128
HG_UNROLL = 8
N_GROUPS = 4
EXPERTS_PER_GROUP = 8
N_EXPERTS = 32
D_EXPERT = 512
NORM_EPS = 1e-6
SUBLN_EPS = 1e-5
LAMBDA_INIT = 0.8 - 0.6 * math.exp(-0.3 * 0)
LOG2E = 1.4426950408889634
ROUTER_LANES = 128
MOE_TOKEN_BLOCK = 512
V7X_VMEM_LIMIT = 58 * 1024 * 1024

NT_DIMS = (((1,), (1,)), ((), ()))
TN_DIMS = (((0,), (0,)), ((), ()))


def _rms(x, eps):
    return x * lax.rsqrt(jnp.mean(x * x, axis=-1, keepdims=True) + eps)


def _inproj_kernel(x_ref, pos_ref, g1_ref, w_ref, lb_ref, invf_ref, sgn_ref, pcos_ref, psin_ref,
                   hq_ref, kf_ref, kb_ref, lff_ref, lfb_ref, hi_ref, hg_ref,
                   dq_ref, dk_ref, dv_ref):
    x = x_ref[...]
    h = (_rms(x, NORM_EPS) * g1_ref[...]).astype(BF16)

    def proj(i):
        return jnp.dot(h, w_ref[:, i * HALF_W:(i + 1) * HALF_W], preferred_element_type=F32)

    hq_ref[...] = proj(0)
    for i, (k_ref, lf_ref) in enumerate(((kf_ref, lff_ref), (kb_ref, lfb_ref))):
        z = proj(1 + i)
        lb = lb_ref[i:i + 1, :]
        sg = jax.nn.sigmoid(z)
        lf_ref[...] = jnp.log(lb + (1.0 - lb) * sg)
        k_ref[...] = (1.0 - lb) * (1.0 - sg)
    hi_ref[...] = proj(3)
    hg_ref[...] = proj(4)

    ang_t = invf_ref[...] * pos_ref[...]

    def spread(t, place_ref):
        hi = t.astype(BF16)
        r1 = t - hi.astype(F32)
        mid = r1.astype(BF16)
        lo = (r1 - mid.astype(F32)).astype(BF16)
        pieces = jnp.concatenate([hi, mid, lo, jnp.zeros_like(hi)], axis=0)
        return lax.dot_general(pieces, place_ref[...], TN_DIMS, preferred_element_type=F32)

    sgn = sgn_ref[...]
    cos = jnp.concatenate([spread(jnp.cos(ang_t), pcos_ref) + jnp.where(sgn == 0.0, 1.0, 0.0)]
                          * N_HEADS, axis=1)
    sin = jnp.concatenate([spread(jnp.sin(ang_t), psin_ref)] * N_HEADS, axis=1)
    take_hi = jnp.concatenate([jnp.broadcast_to(sgn, (x.shape[0], HEAD_W))] * N_HEADS,
                              axis=1) < 0.0

    def rotary(t):
        partner = jnp.where(take_hi, pltpu.roll(t, HALF_W - ROT_DIM // 2, axis=1),
                            pltpu.roll(t, ROT_DIM // 2, axis=1))
        return t * cos + partner * sin

    dq_ref[...] = (rotary(proj(5)) * (QK_DIM ** -0.5 * LOG2E)).astype(BF16)
    dk_ref[...] = rotary(proj(6)).astype(BF16)
    v = proj(7)
    sub = lax.broadcasted_iota(jnp.int32, (VT_ROWS - HEAD_W, v.shape[0]), 0)
    ones_rows = jnp.where(sub == 0, 1.0, 0.0)
    pieces = []
    for hd in range(N_HEADS):
        pieces += [v[:, hd * HEAD_W:(hd + 1) * HEAD_W].T, ones_rows]
    dv_ref[...] = jnp.concatenate(pieces, axis=0).astype(BF16)


def _inproj(x, pos, g1, w_in, lb, invf, sgn, pcos, psin, tm=512):
    S = x.shape[0]
    row = lambda w: pl.BlockSpec((tm, w), lambda i: (i, 0))
    full = lambda a: pl.BlockSpec(a.shape, lambda i: (0, 0))
    f32o = jax.ShapeDtypeStruct((S, HALF_W), F32)
    bf16o = jax.ShapeDtypeStruct((S, HALF_W), BF16)
    return pl.pallas_call(
        _inproj_kernel,
        grid=(S // tm,),
        in_specs=[row(D_MODEL), pl.BlockSpec((1, tm), lambda i: (0, i)), full(g1), full(w_in),
                  full(lb), full(invf), full(sgn), full(pcos), full(psin)],
        out_specs=[row(HALF_W)] * 9 + [pl.BlockSpec((N_HEADS * VT_ROWS, tm), lambda i: (0, i))],
        out_shape=[f32o] * 7 + [bf16o, bf16o,
                                jax.ShapeDtypeStruct((N_HEADS * VT_ROWS, S), BF16)],
        compiler_params=pltpu.CompilerParams(dimension_semantics=("arbitrary",),
                                             vmem_limit_bytes=V7X_VMEM_LIMIT),
        name="inproj",
    )(x, pos, g1, w_in, lb, invf, sgn, pcos, psin)


def _chunk_cumsum(x, rev):
    n = x.shape[0]
    row = lax.broadcasted_iota(jnp.int32, x.shape, 0)
    d = 1
    while d < n:
        if rev:
            x = x + jnp.where(row < n - d, pltpu.roll(x, n - d, axis=0), 0.0)
        else:
            x = x + jnp.where(row >= d, pltpu.roll(x, d, axis=0), 0.0)
        d *= 2
    return x


def _pivot_rows(b, blk, rev):
    n = b.shape[0]
    half = blk // 2
    groups = []
    for g0 in range(0, n, 8):
        def piv(r):
            base = (r // blk) * blk
            return base + half if rev else base + half - 1
        if blk >= 8:
            p = piv(g0)
            groups.append(jnp.broadcast_to(b[p:p + 1, :], (8, b.shape[1])))
        else:
            sub = lax.broadcasted_iota(jnp.int32, (8, b.shape[1]), 0)
            acc = None
            for s0 in range(0, 8, blk):
                p = piv(g0 + s0)
                rowv = jnp.broadcast_to(b[p:p + 1, :], (8, b.shape[1]))
                acc = rowv if acc is None else jnp.where(sub >= s0, rowv, acc)
            groups.append(acc)
    return jnp.concatenate(groups, axis=0)


def _hgrn_kernel(q_ref, k_ref, lf_ref, v_ref, *rest, rev, n_chunks, n_cast):
    w_refs, o_ref, wb_refs, st_ref = (rest[:n_cast], rest[n_cast], rest[n_cast + 1:-1], rest[-1])
    for w_ref, wb_ref in zip(w_refs, wb_refs):
        wb_ref[...] = w_ref[...].astype(BF16)

    @pl.when(pl.program_id(1) == 0)
    def _():
        st_ref[...] = jnp.zeros_like(st_ref)

    C, W = HG_CHUNK, HEAD_W
    rt = lax.broadcasted_iota(jnp.int32, (C, C), 0)
    ct = lax.broadcasted_iota(jnp.int32, (C, C), 1)
    rowi = lax.broadcasted_iota(jnp.int32, (C, 2 * W), 0)
    eye = jnp.where(rt == ct, 1.0, 0.0)
    levels = []
    blk = C
    while blk >= 2:
        half = blk // 2
        q_side = (lambda r: (r % blk) < half) if rev else (lambda r: (r % blk) >= half)
        pair = ((rt // blk) == (ct // blk)) & q_side(rt) & jnp.logical_not(q_side(ct))
        levels.append((blk, jnp.where(q_side(rowi), 1.0, 0.0), jnp.where(pair, 1.0, 0.0)))
        blk = half

    def stack(a):
        return jnp.concatenate([a[:, :W], a[:, W:]], axis=0)

    def chunk(ci, st):
        c = (n_chunks - 1 - ci) if rev else ci
        sl = pl.ds(pl.multiple_of(c * C, C), C)
        q = q_ref[sl, :]
        k = k_ref[sl, :]
        vb = v_ref[sl, :].astype(BF16)
        b = _chunk_cumsum(lf_ref[sl, :], rev)
        tot = b[0:1, :] if rev else b[C - 1:C, :]

        o = lax.dot_general(stack((q * jnp.exp(b)).astype(BF16)), st.astype(BF16), NT_DIMS,
                            preferred_element_type=F32)
        kdec = (k * jnp.exp(tot - b)).astype(BF16)
        upd = lax.dot_general(vb, kdec, TN_DIMS, preferred_element_type=F32)
        dec = jnp.exp(tot)
        st_next = jnp.concatenate([st[:W] * dec[:, :W] + upd[:W, :W],
                                   st[W:] * dec[:, W:] + upd[W:, W:]], axis=0)

        qk = q * k
        scores = [eye * jnp.sum(qk[:, :W], axis=1, keepdims=True),
                  eye * jnp.sum(qk[:, W:], axis=1, keepdims=True)]
        for blk, q_side, pair in levels:
            w = jnp.exp2(jnp.abs(b - _pivot_rows(b, blk, rev)) * (-LOG2E))
            z = stack((jnp.where(q_side > 0.5, q, k) * w).astype(BF16))
            g = lax.dot_general(z, z, NT_DIMS, preferred_element_type=F32)
            scores = [scores[0] + pair * g[:C, :C], scores[1] + pair * g[C:, C:]]
        intra = jnp.dot(jnp.concatenate(scores, axis=0).astype(BF16), vb,
                        preferred_element_type=F32)
        o_ref[sl, :] = jnp.concatenate([o[:C, :W] + intra[:C, :W], o[C:, W:] + intra[C:, W:]],
                                       axis=1)
        return st_next

    st_ref[...] = lax.fori_loop(0, n_chunks, chunk, st_ref[...], unroll=HG_UNROLL)


def _hgrn(q, k, lf, v, rev, weights, rb=1024):
    S = q.shape[0]
    nb = S // rb
    n_pairs = N_HEADS // 2
    steps = n_pairs * nb
    idx = (lambda h, j: (nb - 1 - j, h)) if rev else (lambda h, j: (j, h))
    spec = pl.BlockSpec((rb, 2 * HEAD_W), idx)
    w_specs = []
    for w in weights:
        n_exp, rows, cols = w.shape
        per_exp = steps // n_exp
        assert per_exp * n_exp == steps and rows % per_exp == 0
        w_specs.append(pl.BlockSpec(
            (1, rows // per_exp, cols),
            lambda h, j, per_exp=per_exp: ((h * nb + j) // per_exp, (h * nb + j) % per_exp, 0)))
    return pl.pallas_call(
        functools.partial(_hgrn_kernel, rev=rev, n_chunks=rb // HG_CHUNK, n_cast=len(weights)),
        grid=(n_pairs, nb),
        in_specs=[spec] * 4 + w_specs,
        out_specs=[spec] + w_specs,
        out_shape=[jax.ShapeDtypeStruct((S, HALF_W), F32)]
        + [jax.ShapeDtypeStruct(w.shape, BF16) for w in weights],
        scratch_shapes=[pltpu.VMEM((2 * HEAD_W, HEAD_W), F32)],
        compiler_params=pltpu.CompilerParams(dimension_semantics=("arbitrary", "arbitrary"),
                                             vmem_limit_bytes=V7X_VMEM_LIMIT),
        name="hgrn_bwd" if rev else "hgrn_fwd",
    )(q, k, lf, v, *weights)


def _attn_kernel(lam_ref, q_ref, k_ref, vt_ref, gain_ref, o_ref, sa_ref, sb_ref, m_ref, acc_ref,
                 *, tk):
    n_kv = k_ref.shape[0] // tk
    tq = q_ref.shape[0]
    assert n_kv % 2 == 0 and tq % QUERY_GROUP == 0
    q = q_ref[...]
    lane = lax.broadcasted_iota(jnp.int32, q.shape, 1)
    qm = (jnp.where(lane < QK_DIM, q, jnp.zeros_like(q)),
          jnp.where(lane >= QK_DIM, q, jnp.zeros_like(q)))
    m_ref[...] = jnp.full(m_ref.shape, -jnp.inf, F32)
    acc_ref[...] = jnp.zeros_like(acc_ref)

    def chunk(j):
        return pl.ds(pl.multiple_of(j * tk, tk), tk)

    def softmax_pv(c, cols, s, vt):
        m_old = m_ref[c, :, cols]
        m_new = jnp.maximum(m_old, jnp.max(s, axis=0, keepdims=True))
        p = jnp.exp2(s - m_new).astype(BF16)
        acc_ref[c, :, cols] = jnp.exp2(m_old - m_new) * acc_ref[c, :, cols] + jnp.dot(
            vt, p, preferred_element_type=F32)
        m_ref[c, :, cols] = m_new

    def step(js, dst_ref, jp, src_ref):
        kc = None if dst_ref is None else k_ref[chunk(js), :]
        vt = None if src_ref is None else vt_ref[:, chunk(jp)]
        for c in range(2):
            for g in range(0, tq, QUERY_GROUP):
                cols = slice(g, g + QUERY_GROUP)
                if dst_ref is not None:
                    dst_ref[c, :, cols] = lax.dot_general(kc, qm[c][cols, :], NT_DIMS,
                                                          preferred_element_type=F32)
                if src_ref is not None:
                    softmax_pv(c, cols, src_ref[c, :, cols], vt)

    step(0, sa_ref, None, None)

    def two_steps(i, carry):
        step(2 * i + 1, sb_ref, 2 * i, sa_ref)
        step(2 * i + 2, sa_ref, 2 * i + 1, sb_ref)
        return carry

    lax.fori_loop(0, n_kv // 2 - 1, two_steps, 0)
    step(n_kv - 1, sb_ref, n_kv - 2, sa_ref)
    step(None, None, n_kv - 1, sb_ref)

    outs = []
    for c in range(2):
        a = acc_ref[c]
        outs.append(a[:HEAD_W, :] / a[HEAD_W:HEAD_W + 1, :])
    o = (outs[0] - lam_ref[0] * outs[1]).T
    o_ref[...] = _rms(o, SUBLN_EPS) * gain_ref[...] * (1.0 - LAMBDA_INIT)


def _attn(lam, dq, dk, dvt, gain, tq=2048, tk=512):
    S = dq.shape[0]
    return pl.pallas_call(
        functools.partial(_attn_kernel, tk=tk),
        grid=(N_HEADS, S // tq),
        in_specs=[pl.BlockSpec(memory_space=pltpu.SMEM),
                  pl.BlockSpec((tq, HEAD_W), lambda h, i: (i, h)),
                  pl.BlockSpec((S, HEAD_W), lambda h, i: (0, h)),
                  pl.BlockSpec((VT_ROWS, S), lambda h, i: (h, 0)),
                  pl.BlockSpec((1, HEAD_W), lambda h, i: (0, 0))],
        out_specs=pl.BlockSpec((tq, HEAD_W), lambda h, i: (i, h)),
        out_shape=jax.ShapeDtypeStruct((S, HALF_W), F32),
        scratch_shapes=[pltpu.VMEM((2, tk, tq), F32), pltpu.VMEM((2, tk, tq), F32),
                        pltpu.VMEM((2, 1, tq), F32), pltpu.VMEM((2, VT_ROWS, tq), F32)],
        compiler_params=pltpu.CompilerParams(dimension_semantics=("arbitrary", "arbitrary"),
                                             vmem_limit_bytes=V7X_VMEM_LIMIT),
        name="diff_attn",
    )(lam, dq, dk, dvt, gain)


def _outproj_kernel(x_ref, of_ref, ob_ref, hg_ref, oda_ref, hgain_ref, wo_ref, g2_ref,
                    wr_ref, br_ref, x2_ref, rank_t_ref, gates_t_ref, cnt_ref,
                    carry_ref, *, sub_blocks):
    i = pl.program_id(0)
    part = x_ref.shape[0]

    def project(rows):
        o = of_ref[rows, :] + ob_ref[rows, :]
        gate = hg_ref[rows, :]
        gate = gate * jax.nn.sigmoid(gate)
        parts = []
        for hd in range(N_HEADS):
            sl = slice(hd * HEAD_W, (hd + 1) * HEAD_W)
            parts.append(_rms(o[:, sl], NORM_EPS) * hgain_ref[...] * gate[:, sl])
        parts.append(oda_ref[rows, :])
        mixed = jnp.concatenate(parts, axis=1).astype(BF16)
        x2 = x_ref[rows, :] + jnp.dot(mixed, wo_ref[...], preferred_element_type=F32)
        x2_ref[rows, :] = x2
        return x2

    lane = lax.broadcasted_iota(jnp.int32, (part, ROUTER_LANES), 1)
    neg = -jnp.inf
    tri = jnp.where(lax.broadcasted_iota(jnp.int32, (part, part), 0)
                    >= lax.broadcasted_iota(jnp.int32, (part, part), 1), 1.0, 0.0).astype(BF16)

    def first_max(vals):
        mx = jnp.max(vals, axis=1, keepdims=True)
        idx = jnp.min(jnp.where(vals == mx, lane, ROUTER_LANES), axis=1, keepdims=True)
        return mx, idx

    def route(rows, x2, carry):
        h2 = _rms(x2, NORM_EPS) * g2_ref[...]
        h_hi = h2.astype(BF16)
        h_lo = (h2 - h_hi.astype(F32)).astype(BF16)
        t = jnp.dot(h_hi, wr_ref[...], preferred_element_type=F32)
        logits = (t[:, :ROUTER_LANES] + t[:, ROUTER_LANES:] + br_ref[...]
                  + jnp.dot(h_lo, wr_ref[:, :ROUTER_LANES], preferred_element_type=F32))
        gl = jnp.where((lane >= N_EXPERTS) & (lane < N_EXPERTS + N_GROUPS), logits, neg)
        gm, gidx = first_max(gl)
        g_p = 1.0 / jnp.sum(jnp.exp(gl - gm), axis=1, keepdims=True)
        el = jnp.where((lane < N_EXPERTS) & ((lane // EXPERTS_PER_GROUP) == (gidx - N_EXPERTS)),
                       logits, neg)
        em1, idx1 = first_max(el)
        em2, idx2 = first_max(jnp.where(lane == idx1, neg, el))
        e2 = jnp.exp(em2 - em1)
        w1 = g_p / (1.0 + e2)
        w2 = g_p * e2 / (1.0 + e2)
        is1 = lane == idx1
        is2 = lane == idx2
        gates = jnp.where(is1, w1, 0.0) + jnp.where(is2, w2, 0.0)
        assigned = jnp.where(is1 | is2, 1.0, 0.0)
        cum = jnp.dot(tri, assigned.astype(BF16), preferred_element_type=F32)
        rank = jnp.where(assigned > 0.0, carry + cum - 1.0, -1.0)
        rank_t_ref[:, rows] = rank.T
        gates_t_ref[:, rows] = gates.T
        return carry + cum[part - 1:part, :]

    @pl.when(i % sub_blocks == 0)
    def _():
        carry_ref[...] = jnp.zeros_like(carry_ref)

    rows = slice(0, part)
    carry = route(rows, project(rows), carry_ref[...])
    carry_ref[...] = carry
    cnt_ref[pl.ds(i // sub_blocks, 1), :] = carry.astype(jnp.int32)


def _outproj(x, o_f, o_b, hg, o_da, hgain, w_out, g2, wr, br, tb, tm=512):
    S = x.shape[0]
    row = lambda w: pl.BlockSpec((tm, w), lambda i: (i, 0))
    col = pl.BlockSpec((ROUTER_LANES, tm), lambda i: (0, i))
    full = lambda a: pl.BlockSpec(a.shape, lambda i: (0, 0))
    return pl.pallas_call(
        functools.partial(_outproj_kernel, sub_blocks=tb // tm),
        grid=(S // tm,),
        in_specs=[row(D_MODEL), row(HALF_W), row(HALF_W), row(HALF_W), row(HALF_W),
                  full(hgain), full(w_out), full(g2), full(wr), full(br)],
        out_specs=[row(D_MODEL), col, col,
                   pl.BlockSpec((S // tb, ROUTER_LANES), lambda i: (0, 0))],
        out_shape=[jax.ShapeDtypeStruct((S, D_MODEL), F32),
                   jax.ShapeDtypeStruct((ROUTER_LANES, S), F32),
                   jax.ShapeDtypeStruct((ROUTER_LANES, S), F32),
                   jax.ShapeDtypeStruct((S // tb, ROUTER_LANES), jnp.int32)],
        scratch_shapes=[pltpu.VMEM((1, ROUTER_LANES), F32)],
        compiler_params=pltpu.CompilerParams(dimension_semantics=("arbitrary",),
                                             vmem_limit_bytes=V7X_VMEM_LIMIT),
        name="outproj_router",
    )(x, o_f, o_b, hg, o_da, hgain, w_out, g2, wr, br)


def _moe_kernel(cnt_ref, x2_hbm, rank_t_ref, gates_t_ref, g2_ref, wg_ref, wu_ref,
                wd_ref, gf_ref, out_ref, h2_ref, x2_sem, *, tile_rows, tb):
    b = pl.program_id(0)
    p = pl.program_id(1)
    R = tile_rows
    n_exp = wg_ref.shape[0]
    rows = out_ref.shape[0]
    n_sub = rows // tb

    @pl.when(p == 0)
    def _():
        copy = pltpu.make_async_copy(x2_hbm.at[pl.ds(pl.multiple_of(b * rows, rows), rows), :],
                                     out_ref, x2_sem)
        copy.start()
        copy.wait()
        h2_ref[...] = (_rms(out_ref[...], NORM_EPS) * g2_ref[...]).astype(BF16)

    experts = [n_exp * p + i for i in range(n_exp)]
    subs = [slice(s * tb, (s + 1) * tb) for s in range(n_sub)]
    rank_row = [[rank_t_ref[pl.ds(e, 1), sub] for e in experts] for sub in subs]
    gate_row = [[gates_t_ref[pl.ds(e, 1), sub] for e in experts] for sub in subs]
    n_max = jnp.int32(0)
    for s in range(n_sub):
        for e in experts:
            n_max = jnp.maximum(n_max, cnt_ref[n_sub * b + s, e])

    def tile(t, carry):
        base = (t * R).astype(F32)
        slot_col = lax.broadcasted_iota(jnp.int32, (R, 1), 0).astype(F32) + base
        sel, gather, xg = [], [], []
        for s, sub in enumerate(subs):
            sel.append([rank_row[s][i] == slot_col for i in range(n_exp)])
            gather.append(jnp.concatenate(
                [jnp.where(m, 1.0, 0.0).astype(BF16) for m in sel[s]], axis=0))
            xg.append(jnp.dot(gather[s], h2_ref[sub, :],
                              preferred_element_type=F32).astype(BF16))
        ys = [[] for _ in subs]
        for i in range(n_exp):
            xi = jnp.concatenate([xg[s][i * R:(i + 1) * R] for s in range(n_sub)], axis=0)
            a = jnp.dot(xi, wg_ref[i], preferred_element_type=F32)
            a = a * jax.nn.sigmoid(a) * jnp.dot(xi, wu_ref[i], preferred_element_type=F32)
            y = jnp.dot(a.astype(BF16), wd_ref[i], preferred_element_type=F32)
            for s in range(n_sub):
                g = jnp.sum(jnp.where(sel[s][i], gate_row[s][i], 0.0), axis=1, keepdims=True)
                ys[s].append((y[s * R:(s + 1) * R] * g).astype(BF16))
        for s, sub in enumerate(subs):
            out_ref[sub, :] += lax.dot_general(gather[s], jnp.concatenate(ys[s], axis=0),
                                               TN_DIMS, preferred_element_type=F32)
        return carry

    lax.fori_loop(0, (n_max + R - 1) // R, tile, 0)

    @pl.when(p == pl.num_programs(1) - 1)
    def _():
        out_ref[...] = _rms(out_ref[...], NORM_EPS) * gf_ref[...]


def _moe(cnt, x2, rank_t, gates_t, g2, wg, wu, wd, gf, tb, tile_rows=48,
         experts_per_step=4, blocks_per_step=8):
    S = x2.shape[0]
    eps = experts_per_step
    rows = tb * blocks_per_step
    grid_spec = pltpu.PrefetchScalarGridSpec(
        num_scalar_prefetch=1,
        grid=(S // rows, N_EXPERTS // eps),
        in_specs=[pl.BlockSpec(memory_space=pl.ANY),
                  pl.BlockSpec((N_EXPERTS, rows), lambda b, p, c: (0, b)),
                  pl.BlockSpec((N_EXPERTS, rows), lambda b, p, c: (0, b)),
                  pl.BlockSpec((1, D_MODEL), lambda b, p, c: (0, 0)),
                  pl.BlockSpec((eps, D_MODEL, D_EXPERT), lambda b, p, c: (p, 0, 0)),
                  pl.BlockSpec((eps, D_MODEL, D_EXPERT), lambda b, p, c: (p, 0, 0)),
                  pl.BlockSpec((eps, D_EXPERT, D_MODEL), lambda b, p, c: (p, 0, 0)),
                  pl.BlockSpec((1, D_MODEL), lambda b, p, c: (0, 0))],
        out_specs=pl.BlockSpec((rows, D_MODEL), lambda b, p, c: (b, 0),
                               pipeline_mode=pl.Buffered(1)),
        scratch_shapes=[pltpu.VMEM((rows, D_MODEL), BF16), pltpu.SemaphoreType.DMA(())])
    return pl.pallas_call(
        functools.partial(_moe_kernel, tile_rows=tile_rows, tb=tb),
        grid_spec=grid_spec,
        out_shape=jax.ShapeDtypeStruct((S, D_MODEL), F32),
        compiler_params=pltpu.CompilerParams(dimension_semantics=("arbitrary", "arbitrary"),
                                             vmem_limit_bytes=V7X_VMEM_LIMIT),
        name="moe",
    )(cnt, x2, rank_t, gates_t, g2, wg, wu, wd, gf)


def kernel(x, positions, norm1_gain, w_in, hg_lower_bounds, hg_norm_gain, diff_lambda,
           diff_subln_gain, w_out, norm2_gain, router_group_w, router_group_b,
           router_expert_w, router_expert_b, moe_w_gate, moe_w_up, moe_w_down,
           final_norm_gain):
    B, S, _ = x.shape
    assert B == 1 and norm1_gain.shape[0] == 1
    layer = 0
    xs = x.reshape(S, D_MODEL)
    pos = positions.reshape(1, S).astype(F32)

    lb_cum = jnp.cumsum(jax.nn.softmax(hg_lower_bounds.astype(F32), axis=1), axis=1)
    lb = lb_cum[:, layer + 1] - lb_cum[:, 0]
    lam_p = diff_lambda[layer].astype(F32)
    lam = (jnp.exp(jnp.sum(lam_p[0] * lam_p[1])) - jnp.exp(jnp.sum(lam_p[2] * lam_p[3]))
           + LAMBDA_INIT).reshape(1)
    lane = jnp.arange(HEAD_W)
    inv_freq = jnp.float32(ROPE_THETA) ** (-jnp.arange(0, ROT_DIM, 2, dtype=F32) / ROT_DIM)
    in_rot = (lane % QK_DIM) < ROT_DIM
    n_freq = ROT_DIM // 2
    invf = inv_freq.reshape(n_freq, 1)
    sgn = jnp.where(in_rot, jnp.where((lane % QK_DIM) < n_freq, -1.0, 1.0),
                    0.0).astype(F32).reshape(1, HEAD_W)
    place = (in_rot[None, :] & (lane[None, :] % n_freq == jnp.arange(n_freq)[:, None])).astype(F32)
    pcos = jnp.tile(place, (4, 1)).astype(BF16)
    psin = jnp.tile(place * sgn, (4, 1)).astype(BF16)
    wr = jnp.zeros((D_MODEL, ROUTER_LANES), F32)
    wr = wr.at[:, :N_EXPERTS].set(router_expert_w[layer])
    wr = wr.at[:, N_EXPERTS:N_EXPERTS + N_GROUPS].set(router_group_w[layer])
    wr_hi = wr.astype(BF16)
    wr = jnp.concatenate([wr_hi, (wr - wr_hi.astype(F32)).astype(BF16)], axis=1)
    br = jnp.zeros((1, ROUTER_LANES), F32)
    br = br.at[0, :N_EXPERTS].set(router_expert_b[layer])
    br = br.at[0, N_EXPERTS:N_EXPERTS + N_GROUPS].set(router_group_b[layer])

    hq, kf, kb, lff, lfb, hi, hg, dq, dk, dv = _inproj(
        xs, pos, norm1_gain[layer].reshape(1, -1), w_in[layer].astype(BF16), lb, invf, sgn,
        pcos, psin)
    o_f, wg, wu = _hgrn(hq, kf, lff, hi, False, (moe_w_gate[layer], moe_w_up[layer]))
    o_b, wd = _hgrn(hq, kb, lfb, hi, True, (moe_w_down[layer],))
    o_da = _attn(lam, dq, dk, dv, diff_subln_gain[layer].reshape(1, -1))
    x2, rank_t, gates_t, cnt = _outproj(
        xs, o_f, o_b, hg, o_da, hg_norm_gain[layer].reshape(1, -1), w_out[layer].astype(BF16),
        norm2_gain[layer].reshape(1, -1), wr, br, tb=MOE_TOKEN_BLOCK)
    out = _moe(cnt, x2, rank_t, gates_t, norm2_gain[layer].reshape(1, -1), wg, wu, wd,
               final_norm_gain.reshape(1, -1), tb=MOE_TOKEN_BLOCK)
    return out.reshape(B, S, D_MODEL)
```

```python
import functools
import math

import jax
import jax.numpy as jnp
from jax import lax
from jax.experimental import pallas as pl
from jax.experimental.pallas import tpu as pltpu

F32 = jnp.float32
BF16 = jnp.bfloat16

D_MODEL = 1024
HEAD_W = 128
N_HEADS = 4
HALF_W = N_HEADS * HEAD_W
QK_DIM = 64
ROT_DIM = 16
ROPE_THETA = 500000.0
QUERY_GROUP = 512
VT_ROWS = HEAD_W + 16
HG_CHUNK = 128
HG_UNROLL = 8
N_GROUPS = 4
EXPERTS_PER_GROUP = 8
N_EXPERTS = 32
D_EXPERT = 512
NORM_EPS = 1e-6
SUBLN_EPS = 1e-5
LAMBDA_INIT = 0.8 - 0.6 * math.exp(-0.3 * 0)
LOG2E = 1.4426950408889634
ROUTER_LANES = 128
MOE_TOKEN_BLOCK = 512
V7X_VMEM_LIMIT = 58 * 1024 * 1024

NT_DIMS = (((1,), (1,)), ((), ()))
TN_DIMS = (((0,), (0,)), ((), ()))


def _rms(x, eps):
    return x * lax.rsqrt(jnp.mean(x * x, axis=-1, keepdims=True) + eps)


def _inproj_kernel(x_ref, pos_ref, g1_ref, w_ref, lb_ref, invf_ref, sgn_ref, pcos_ref, psin_ref,
                   hq_ref, kf_ref, kb_ref, lff_ref, lfb_ref, hi_ref, hg_ref,
                   dq_ref, dk_ref, dv_ref):
    x = x_ref[...]
    h = (_rms(x, NORM_EPS) * g1_ref[...]).astype(BF16)

    def proj(i):
        return jnp.dot(h, w_ref[:, i * HALF_W:(i + 1) * HALF_W], preferred_element_type=F32)

    hq_ref[...] = proj(0)
    for i, (k_ref, lf_ref) in enumerate(((kf_ref, lff_ref), (kb_ref, lfb_ref))):
        z = proj(1 + i)
        lb = lb_ref[i:i + 1, :]
        sg = jax.nn.sigmoid(z)
        lf_ref[...] = jnp.log(lb + (1.0 - lb) * sg)
        k_ref[...] = (1.0 - lb) * (1.0 - sg)
    hi_ref[...] = proj(3)
    hg_ref[...] = proj(4)

    ang_t = invf_ref[...] * pos_ref[...]

    def spread(t, place_ref):
        hi = t.astype(BF16)
        r1 = t - hi.astype(F32)
        mid = r1.astype(BF16)
        lo = (r1 - mid.astype(F32)).astype(BF16)
        pieces = jnp.concatenate([hi, mid, lo, jnp.zeros_like(hi)], axis=0)
        return lax.dot_general(pieces, place_ref[...], TN_DIMS, preferred_element_type=F32)

    sgn = sgn_ref[...]
    cos = jnp.concatenate([spread(jnp.cos(ang_t), pcos_ref) + jnp.where(sgn == 0.0, 1.0, 0.0)]
                          * N_HEADS, axis=1)
    sin = jnp.concatenate([spread(jnp.sin(ang_t), psin_ref)] * N_HEADS, axis=1)
    take_hi = jnp.concatenate([jnp.broadcast_to(sgn, (x.shape[0], HEAD_W))] * N_HEADS,
                              axis=1) < 0.0

    def rotary(t):
        partner = jnp.where(take_hi, pltpu.roll(t, HALF_W - ROT_DIM // 2, axis=1),
                            pltpu.roll(t, ROT_DIM // 2, axis=1))
        return t * cos + partner * sin

    dq_ref[...] = (rotary(proj(5)) * (QK_DIM ** -0.5 * LOG2E)).astype(BF16)
    dk_ref[...] = rotary(proj(6)).astype(BF16)
    v = proj(7)
    sub = lax.broadcasted_iota(jnp.int32, (VT_ROWS - HEAD_W, v.shape[0]), 0)
    ones_rows = jnp.where(sub == 0, 1.0, 0.0)
    pieces = []
    for hd in range(N_HEADS):
        pieces += [v[:, hd * HEAD_W:(hd + 1) * HEAD_W].T, ones_rows]
    dv_ref[...] = jnp.concatenate(pieces, axis=0).astype(BF16)


def _inproj(x, pos, g1, w_in, lb, invf, sgn, pcos, psin, tm=512):
    S = x.shape[0]
    row = lambda w: pl.BlockSpec((tm, w), lambda i: (i, 0))
    full = lambda a: pl.BlockSpec(a.shape, lambda i: (0, 0))
    f32o = jax.ShapeDtypeStruct((S, HALF_W), F32)
    bf16o = jax.ShapeDtypeStruct((S, HALF_W), BF16)
    return pl.pallas_call(
        _inproj_kernel,
        grid=(S // tm,),
        in_specs=[row(D_MODEL), pl.BlockSpec((1, tm), lambda i: (0, i)), full(g1), full(w_in),
                  full(lb), full(invf), full(sgn), full(pcos), full(psin)],
        out_specs=[row(HALF_W)] * 9 + [pl.BlockSpec((N_HEADS * VT_ROWS, tm), lambda i: (0, i))],
        out_shape=[f32o] * 7 + [bf16o, bf16o,
                                jax.ShapeDtypeStruct((N_HEADS * VT_ROWS, S), BF16)],
        compiler_params=pltpu.CompilerParams(dimension_semantics=("arbitrary",),
                                             vmem_limit_bytes=V7X_VMEM_LIMIT),
        name="inproj",
    )(x, pos, g1, w_in, lb, invf, sgn, pcos, psin)


def _chunk_cumsum(x, rev):
    n = x.shape[0]
    row = lax.broadcasted_iota(jnp.int32, x.shape, 0)
    d = 1
    while d < n:
        if rev:
            x = x + jnp.where(row < n - d, pltpu.roll(x, n - d, axis=0), 0.0)
        else:
            x = x + jnp.where(row >= d, pltpu.roll(x, d, axis=0), 0.0)
        d *= 2
    return x


def _pivot_rows(b, blk, rev):
    n = b.shape[0]
    half = blk // 2
    groups = []
    for g0 in range(0, n, 8):
        def piv(r):
            base = (r // blk) * blk
            return base + half if rev else base + half - 1
        if blk >= 8:
            p = piv(g0)
            groups.append(jnp.broadcast_to(b[p:p + 1, :], (8, b.shape[1])))
        else:
            sub = lax.broadcasted_iota(jnp.int32, (8, b.shape[1]), 0)
            acc = None
            for s0 in range(0, 8, blk):
                p = piv(g0 + s0)
                rowv = jnp.broadcast_to(b[p:p + 1, :], (8, b.shape[1]))
                acc = rowv if acc is None else jnp.where(sub >= s0, rowv, acc)
            groups.append(acc)
    return jnp.concatenate(groups, axis=0)


def _hgrn_kernel(q_ref, k_ref, lf_ref, v_ref, *rest, rev, n_chunks, n_cast):
    w_refs, o_ref, wb_refs, st_ref = (rest[:n_cast], rest[n_cast], rest[n_cast + 1:-1], rest[-1])
    for w_ref, wb_ref in zip(w_refs, wb_refs):
        wb_ref[...] = w_ref[...].astype(BF16)

    @pl.when(pl.program_id(1) == 0)
    def _():
        st_ref[...] = jnp.zeros_like(st_ref)

    C, W = HG_CHUNK, HEAD_W
    rt = lax.broadcasted_iota(jnp.int32, (C, C), 0)
    ct = lax.broadcasted_iota(jnp.int32, (C, C), 1)
    rowi = lax.broadcasted_iota(jnp.int32, (C, 2 * W), 0)
    eye = jnp.where(rt == ct, 1.0, 0.0)
    levels = []
    blk = C
    while blk >= 2:
        half = blk // 2
        q_side = (lambda r: (r % blk) < half) if rev else (lambda r: (r % blk) >= half)
        pair = ((rt // blk) == (ct // blk)) & q_side(rt) & jnp.logical_not(q_side(ct))
        levels.append((blk, jnp.where(q_side(rowi), 1.0, 0.0), jnp.where(pair, 1.0, 0.0)))
        blk = half

    def stack(a):
        return jnp.concatenate([a[:, :W], a[:, W:]], axis=0)

    def chunk(ci, st):
        c = (n_chunks - 1 - ci) if rev else ci
        sl = pl.ds(pl.multiple_of(c * C, C), C)
        q = q_ref[sl, :]
        k = k_ref[sl, :]
        vb = v_ref[sl, :].astype(BF16)
        b = _chunk_cumsum(lf_ref[sl, :], rev)
        tot = b[0:1, :] if rev else b[C - 1:C, :]

        o = lax.dot_general(stack((q * jnp.exp(b)).astype(BF16)), st.astype(BF16), NT_DIMS,
                            preferred_element_type=F32)
        kdec = (k * jnp.exp(tot - b)).astype(BF16)
        upd = lax.dot_general(vb, kdec, TN_DIMS, preferred_element_type=F32)
        dec = jnp.exp(tot)
        st_next = jnp.concatenate([st[:W] * dec[:, :W] + upd[:W, :W],
                                   st[W:] * dec[:, W:] + upd[W:, W:]], axis=0)

        qk = q * k
        scores = [eye * jnp.sum(qk[:, :W], axis=1, keepdims=True),
                  eye * jnp.sum(qk[:, W:], axis=1, keepdims=True)]
        for blk, q_side, pair in levels:
            w = jnp.exp2(jnp.abs(b - _pivot_rows(b, blk, rev)) * (-LOG2E))
            z = stack((jnp.where(q_side > 0.5, q, k) * w).astype(BF16))
            g = lax.dot_general(z, z, NT_DIMS, preferred_element_type=F32)
            scores = [scores[0] + pair * g[:C, :C], scores[1] + pair * g[C:, C:]]
        intra = jnp.dot(jnp.concatenate(scores, axis=0).astype(BF16), vb,
                        preferred_element_type=F32)
        o_ref[sl, :] = jnp.concatenate([o[:C, :W] + intra[:C, :W], o[C:, W:] + intra[C:, W:]],
                                       axis=1)
        return st_next

    st_ref[...] = lax.fori_loop(0, n_chunks, chunk, st_ref[...], unroll=HG_UNROLL)


def _hgrn(q, k, lf, v, rev, weights, rb=1024):
    S = q.shape[0]
    nb = S // rb
    n_pairs = N_HEADS // 2
    steps = n_pairs * nb
    idx = (lambda h, j: (nb - 1 - j, h)) if rev else (lambda h, j: (j, h))
    spec = pl.BlockSpec((rb, 2 * HEAD_W), idx)
    w_specs = []
    for w in weights:
        n_exp, rows, cols = w.shape
        per_exp = steps // n_exp
        assert per_exp * n_exp == steps and rows % per_exp == 0
        w_specs.append(pl.BlockSpec(
            (1, rows // per_exp, cols),
            lambda h, j, per_exp=per_exp: ((h * nb + j) // per_exp, (h * nb + j) % per_exp, 0)))
    return pl.pallas_call(
        functools.partial(_hgrn_kernel, rev=rev, n_chunks=rb // HG_CHUNK, n_cast=len(weights)),
        grid=(n_pairs, nb),
        in_specs=[spec] * 4 + w_specs,
        out_specs=[spec] + w_specs,
        out_shape=[jax.ShapeDtypeStruct((S, HALF_W), F32)]
        + [jax.ShapeDtypeStruct(w.shape, BF16) for w in weights],
        scratch_shapes=[pltpu.VMEM((2 * HEAD_W, HEAD_W), F32)],
        compiler_params=pltpu.CompilerParams(dimension_semantics=("arbitrary", "arbitrary"),
                                             vmem_limit_bytes=V7X_VMEM_LIMIT),
        name="hgrn_bwd" if rev else "hgrn_fwd",
    )(q, k, lf, v, *weights)


def _attn_kernel(lam_ref, q_ref, k_ref, vt_ref, gain_ref, o_ref, sa_ref, sb_ref, m_ref, acc_ref,
                 *, tk):
    n_kv = k_ref.shape[0] // tk
    tq = q_ref.shape[0]
    assert n_kv % 2 == 0 and tq % QUERY_GROUP == 0
    q = q_ref[...]
    lane = lax.broadcasted_iota(jnp.int32, q.shape, 1)
    qm = (jnp.where(lane < QK_DIM, q, jnp.zeros_like(q)),
          jnp.where(lane >= QK_DIM, q, jnp.zeros_like(q)))
    m_ref[...] = jnp.full(m_ref.shape, -jnp.inf, F32)
    acc_ref[...] = jnp.zeros_like(acc_ref)

    def chunk(j):
        return pl.ds(pl.multiple_of(j * tk, tk), tk)

    def softmax_pv(c, cols, s, vt):
        m_old = m_ref[c, :, cols]
        m_new = jnp.maximum(m_old, jnp.max(s, axis=0, keepdims=True))
        p = jnp.exp2(s - m_new).astype(BF16)
        acc_ref[c, :, cols] = jnp.exp2(m_old - m_new) * acc_ref[c, :, cols] + jnp.dot(
            vt, p, preferred_element_type=F32)
        m_ref[c, :, cols] = m_new

    def step(js, dst_ref, jp, src_ref):
        kc = None if dst_ref is None else k_ref[chunk(js), :]
        vt = None if src_ref is None else vt_ref[:, chunk(jp)]
        for c in range(2):
            for g in range(0, tq, QUERY_GROUP):
                cols = slice(g, g + QUERY_GROUP)
                if dst_ref is not None:
                    dst_ref[c, :, cols] = lax.dot_general(kc, qm[c][cols, :], NT_DIMS,
                                                          preferred_element_type=F32)
                if src_ref is not None:
                    softmax_pv(c, cols, src_ref[c, :, cols], vt)

    step(0, sa_ref, None, None)

    def two_steps(i, carry):
        step(2 * i + 1, sb_ref, 2 * i, sa_ref)
        step(2 * i + 2, sa_ref, 2 * i + 1, sb_ref)
        return carry

    lax.fori_loop(0, n_kv // 2 - 1, two_steps, 0)
    step(n_kv - 1, sb_ref, n_kv - 2, sa_ref)
    step(None, None, n_kv - 1, sb_ref)

    outs = []
    for c in range(2):
        a = acc_ref[c]
        outs.append(a[:HEAD_W, :] / a[HEAD_W:HEAD_W + 1, :])
    o = (outs[0] - lam_ref[0] * outs[1]).T
    o_ref[...] = _rms(o, SUBLN_EPS) * gain_ref[...] * (1.0 - LAMBDA_INIT)


def _attn(lam, dq, dk, dvt, gain, tq=2048, tk=512):
    S = dq.shape[0]
    return pl.pallas_call(
        functools.partial(_attn_kernel, tk=tk),
        grid=(N_HEADS, S // tq),
        in_specs=[pl.BlockSpec(memory_space=pltpu.SMEM),
                  pl.BlockSpec((tq, HEAD_W), lambda h, i: (i, h)),
                  pl.BlockSpec((S, HEAD_W), lambda h, i: (0, h)),
                  pl.BlockSpec((VT_ROWS, S), lambda h, i: (h, 0)),
                  pl.BlockSpec((1, HEAD_W), lambda h, i: (0, 0))],
        out_specs=pl.BlockSpec((tq, HEAD_W), lambda h, i: (i, h)),
        out_shape=jax.ShapeDtypeStruct((S, HALF_W), F32),
        scratch_shapes=[pltpu.VMEM((2, tk, tq), F32), pltpu.VMEM((2, tk, tq), F32),
                        pltpu.VMEM((2, 1, tq), F32), pltpu.VMEM((2, VT_ROWS, tq), F32)],
        compiler_params=pltpu.CompilerParams(dimension_semantics=("arbitrary", "arbitrary"),
                                             vmem_limit_bytes=V7X_VMEM_LIMIT),
        name="diff_attn",
    )(lam, dq, dk, dvt, gain)


def _outproj_kernel(x_ref, of_ref, ob_ref, hg_ref, oda_ref, hgain_ref, wo_ref, g2_ref,
                    wr_ref, br_ref, x2_ref, rank_t_ref, gates_t_ref, cnt_ref,
                    carry_ref, *, sub_blocks):
    i = pl.program_id(0)
    part = x_ref.shape[0]

    def project(rows):
        o = of_ref[rows, :] + ob_ref[rows, :]
        gate = hg_ref[rows, :]
        gate = gate * jax.nn.sigmoid(gate)
        parts = []
        for hd in range(N_HEADS):
            sl = slice(hd * HEAD_W, (hd + 1) * HEAD_W)
            parts.append(_rms(o[:, sl], NORM_EPS) * hgain_ref[...] * gate[:, sl])
        parts.append(oda_ref[rows, :])
        mixed = jnp.concatenate(parts, axis=1).astype(BF16)
        x2 = x_ref[rows, :] + jnp.dot(mixed, wo_ref[...], preferred_element_type=F32)
        x2_ref[rows, :] = x2
        return x2

    lane = lax.broadcasted_iota(jnp.int32, (part, ROUTER_LANES), 1)
    neg = -jnp.inf
    tri = jnp.where(lax.broadcasted_iota(jnp.int32, (part, part), 0)
                    >= lax.broadcasted_iota(jnp.int32, (part, part), 1), 1.0, 0.0).astype(BF16)

    def first_max(vals):
        mx = jnp.max(vals, axis=1, keepdims=True)
        idx = jnp.min(jnp.where(vals == mx, lane, ROUTER_LANES), axis=1, keepdims=True)
        return mx, idx

    def route(rows, x2, carry):
        h2 = _rms(x2, NORM_EPS) * g2_ref[...]
        h_hi = h2.astype(BF16)
        h_lo = (h2 - h_hi.astype(F32)).astype(BF16)
        t = jnp.dot(h_hi, wr_ref[...], preferred_element_type=F32)
        logits = (t[:, :ROUTER_LANES] + t[:, ROUTER_LANES:] + br_ref[...]
                  + jnp.dot(h_lo, wr_ref[:, :ROUTER_LANES], preferred_element_type=F32))
        gl = jnp.where((lane >= N_EXPERTS) & (lane < N_EXPERTS + N_GROUPS), logits, neg)
        gm, gidx = first_max(gl)
        g_p = 1.0 / jnp.sum(jnp.exp(gl - gm), axis=1, keepdims=True)
        el = jnp.where((lane < N_EXPERTS) & ((lane // EXPERTS_PER_GROUP) == (gidx - N_EXPERTS)),
                       logits, neg)
        em1, idx1 = first_max(el)
        em2, idx2 = first_max(jnp.where(lane == idx1, neg, el))
        e2 = jnp.exp(em2 - em1)
        w1 = g_p / (1.0 + e2)
        w2 = g_p * e2 / (1.0 + e2)
        is1 = lane == idx1
        is2 = lane == idx2
        gates = jnp.where(is1, w1, 0.0) + jnp.where(is2, w2, 0.0)
        assigned = jnp.where(is1 | is2, 1.0, 0.0)
        cum = jnp.dot(tri, assigned.astype(BF16), preferred_element_type=F32)
        rank = jnp.where(assigned > 0.0, carry + cum - 1.0, -1.0)
        rank_t_ref[:, rows] = rank.T
        gates_t_ref[:, rows] = gates.T
        return carry + cum[part - 1:part, :]

    @pl.when(i % sub_blocks == 0)
    def _():
        carry_ref[...] = jnp.zeros_like(carry_ref)

    rows = slice(0, part)
    carry = route(rows, project(rows), carry_ref[...])
    carry_ref[...] = carry
    cnt_ref[pl.ds(i // sub_blocks, 1), :] = carry.astype(jnp.int32)


def _outproj(x, o_f, o_b, hg, o_da, hgain, w_out, g2, wr, br, tb, tm=512):
    S = x.shape[0]
    row = lambda w: pl.BlockSpec((tm, w), lambda i: (i, 0))
    col = pl.BlockSpec((ROUTER_LANES, tm), lambda i: (0, i))
    full = lambda a: pl.BlockSpec(a.shape, lambda i: (0, 0))
    return pl.pallas_call(
        functools.partial(_outproj_kernel, sub_blocks=tb // tm),
        grid=(S // tm,),
        in_specs=[row(D_MODEL), row(HALF_W), row(HALF_W), row(HALF_W), row(HALF_W),
                  full(hgain), full(w_out), full(g2), full(wr), full(br)],
        out_specs=[row(D_MODEL), col, col,
                   pl.BlockSpec((S // tb, ROUTER_LANES), lambda i: (0, 0))],
        out_shape=[jax.ShapeDtypeStruct((S, D_MODEL), F32),
                   jax.ShapeDtypeStruct((ROUTER_LANES, S), F32),
                   jax.ShapeDtypeStruct((ROUTER_LANES, S), F32),
                   jax.ShapeDtypeStruct((S // tb, ROUTER_LANES), jnp.int32)],
        scratch_shapes=[pltpu.VMEM((1, ROUTER_LANES), F32)],
        compiler_params=pltpu.CompilerParams(dimension_semantics=("arbitrary",),
                                             vmem_limit_bytes=V7X_VMEM_LIMIT),
        name="outproj_router",
    )(x, o_f, o_b, hg, o_da, hgain, w_out, g2, wr, br)


def _moe_kernel(cnt_ref, x2_ref, rank_t_ref, gates_t_ref, g2_ref, wg_ref, wu_ref,
                wd_ref, gf_ref, out_ref, h2_ref, *, tile_rows, tb):
    b = pl.program_id(0)
    p = pl.program_id(1)
    R = tile_rows
    n_exp = wg_ref.shape[0]
    n_sub = x2_ref.shape[0] // tb

    @pl.when(p == 0)
    def _():
        x2 = x2_ref[...]
        h2_ref[...] = (_rms(x2, NORM_EPS) * g2_ref[...]).astype(BF16)
        out_ref[...] = x2

    experts = [n_exp * p + i for i in range(n_exp)]
    subs = [slice(s * tb, (s + 1) * tb) for s in range(n_sub)]
    rank_row = [[rank_t_ref[pl.ds(e, 1), sub] for e in experts] for sub in subs]
    gate_row = [[gates_t_ref[pl.ds(e, 1), sub] for e in experts] for sub in subs]
    n_max = jnp.int32(0)
    for s in range(n_sub):
        for e in experts:
            n_max = jnp.maximum(n_max, cnt_ref[n_sub * b + s, e])

    def tile(t, carry):
        base = (t * R).astype(F32)
        slot_col = lax.broadcasted_iota(jnp.int32, (R, 1), 0).astype(F32) + base
        sel, gather, xg = [], [], []
        for s, sub in enumerate(subs):
            sel.append([rank_row[s][i] == slot_col for i in range(n_exp)])
            gather.append(jnp.concatenate(
                [jnp.where(m, 1.0, 0.0).astype(BF16) for m in sel[s]], axis=0))
            xg.append(jnp.dot(gather[s], h2_ref[sub, :],
                              preferred_element_type=F32).astype(BF16))
        ys = [[] for _ in subs]
        for i in range(n_exp):
            xi = jnp.concatenate([xg[s][i * R:(i + 1) * R] for s in range(n_sub)], axis=0)
            a = jnp.dot(xi, wg_ref[i], preferred_element_type=F32)
            a = a * jax.nn.sigmoid(a) * jnp.dot(xi, wu_ref[i], preferred_element_type=F32)
            y = jnp.dot(a.astype(BF16), wd_ref[i], preferred_element_type=F32)
            for s in range(n_sub):
                g = jnp.sum(jnp.where(sel[s][i], gate_row[s][i], 0.0), axis=1, keepdims=True)
                ys[s].append((y[s * R:(s + 1) * R] * g).astype(BF16))
        for s, sub in enumerate(subs):
            out_ref[sub, :] += lax.dot_general(gather[s], jnp.concatenate(ys[s], axis=0),
                                               TN_DIMS, preferred_element_type=F32)
        return carry

    lax.fori_loop(0, (n_max + R - 1) // R, tile, 0)

    @pl.when(p == pl.num_programs(1) - 1)
    def _():
        out_ref[...] = _rms(out_ref[...], NORM_EPS) * gf_ref[...]


def _moe(cnt, x2, rank_t, gates_t, g2, wg, wu, wd, gf, tb, tile_rows=48,
         experts_per_step=4, blocks_per_step=4):
    S = x2.shape[0]
    eps = experts_per_step
    rows = tb * blocks_per_step
    grid_spec = pltpu.PrefetchScalarGridSpec(
        num_scalar_prefetch=1,
        grid=(S // rows, N_EXPERTS // eps),
        in_specs=[pl.BlockSpec((rows, D_MODEL), lambda b, p, c: (b, 0),
                               pipeline_mode=pl.Buffered(1)),
                  pl.BlockSpec((N_EXPERTS, rows), lambda b, p, c: (0, b)),
                  pl.BlockSpec((N_EXPERTS, rows), lambda b, p, c: (0, b)),
                  pl.BlockSpec((1, D_MODEL), lambda b, p, c: (0, 0)),
                  pl.BlockSpec((eps, D_MODEL, D_EXPERT), lambda b, p, c: (p, 0, 0)),
                  pl.BlockSpec((eps, D_MODEL, D_EXPERT), lambda b, p, c: (p, 0, 0)),
                  pl.BlockSpec((eps, D_EXPERT, D_MODEL), lambda b, p, c: (p, 0, 0)),
                  pl.BlockSpec((1, D_MODEL), lambda b, p, c: (0, 0))],
        out_specs=pl.BlockSpec((rows, D_MODEL), lambda b, p, c: (b, 0)),
        scratch_shapes=[pltpu.VMEM((rows, D_MODEL), BF16)])
    return pl.pallas_call(
        functools.partial(_moe_kernel, tile_rows=tile_rows, tb=tb),
        grid_spec=grid_spec,
        out_shape=jax.ShapeDtypeStruct((S, D_MODEL), F32),
        compiler_params=pltpu.CompilerParams(dimension_semantics=("arbitrary", "arbitrary"),
                                             vmem_limit_bytes=V7X_VMEM_LIMIT),
        name="moe",
    )(cnt, x2, rank_t, gates_t, g2, wg, wu, wd, gf)


def kernel(x, positions, norm1_gain, w_in, hg_lower_bounds, hg_norm_gain, diff_lambda,
           diff_subln_gain, w_out, norm2_gain, router_group_w, router_group_b,
           router_expert_w, router_expert_b, moe_w_gate, moe_w_up, moe_w_down,
           final_norm_gain):
    B, S, _ = x.shape
    assert B == 1 and norm1_gain.shape[0] == 1
    layer = 0
    xs = x.reshape(S, D_MODEL)
    pos = positions.reshape(1, S).astype(F32)

    lb_cum = jnp.cumsum(jax.nn.softmax(hg_lower_bounds.astype(F32), axis=1), axis=1)
    lb = lb_cum[:, layer + 1] - lb_cum[:, 0]
    lam_p = diff_lambda[layer].astype(F32)
    lam = (jnp.exp(jnp.sum(lam_p[0] * lam_p[1])) - jnp.exp(jnp.sum(lam_p[2] * lam_p[3]))
           + LAMBDA_INIT).reshape(1)
    lane = jnp.arange(HEAD_W)
    inv_freq = jnp.float32(ROPE_THETA) ** (-jnp.arange(0, ROT_DIM, 2, dtype=F32) / ROT_DIM)
    in_rot = (lane % QK_DIM) < ROT_DIM
    n_freq = ROT_DIM // 2
    invf = inv_freq.reshape(n_freq, 1)
    sgn = jnp.where(in_rot, jnp.where((lane % QK_DIM) < n_freq, -1.0, 1.0),
                    0.0).astype(F32).reshape(1, HEAD_W)
    place = (in_rot[None, :] & (lane[None, :] % n_freq == jnp.arange(n_freq)[:, None])).astype(F32)
    pcos = jnp.tile(place, (4, 1)).astype(BF16)
    psin = jnp.tile(place * sgn, (4, 1)).astype(BF16)
    wr = jnp.zeros((D_MODEL, ROUTER_LANES), F32)
    wr = wr.at[:, :N_EXPERTS].set(router_expert_w[layer])
    wr = wr.at[:, N_EXPERTS:N_EXPERTS + N_GROUPS].set(router_group_w[layer])
    wr_hi = wr.astype(BF16)
    wr = jnp.concatenate([wr_hi, (wr - wr_hi.astype(F32)).astype(BF16)], axis=1)
    br = jnp.zeros((1, ROUTER_LANES), F32)
    br = br.at[0, :N_EXPERTS].set(router_expert_b[layer])
    br = br.at[0, N_EXPERTS:N_EXPERTS + N_GROUPS].set(router_group_b[layer])

    hq, kf, kb, lff, lfb, hi, hg, dq, dk, dv = _inproj(
        xs, pos, norm1_gain[layer].reshape(1, -1), w_in[layer].astype(BF16), lb, invf, sgn,
        pcos, psin)
    o_f, wg, wu = _hgrn(hq, kf, lff, hi, False, (moe_w_gate[layer], moe_w_up[layer]))
    o_b, wd = _hgrn(hq, kb, lfb, hi, True, (moe_w_down[layer],))
    o_da = _attn(lam, dq, dk, dv, diff_subln_gain[layer].reshape(1, -1))
    x2, rank_t, gates_t, cnt = _outproj(
        xs, o_f, o_b, hg, o_da, hg_norm_gain[layer].reshape(1, -1), w_out[layer].astype(BF16),
        norm2_gain[layer].reshape(1, -1), wr, br, tb=MOE_TOKEN_BLOCK)
    out = _moe(cnt, x2, rank_t, gates_t, norm2_gain[layer].reshape(1, -1), wg, wu, wd,
               final_norm_gain.reshape(1, -1), tb=MOE_TOKEN_BLOCK)
    return out.reshape(B, S, D_MODEL)
```

```python
import functools
import math

import jax
import jax.numpy as jnp
from jax import lax
from jax.experimental import pallas as pl
from jax.experimental.pallas import tpu as pltpu

F32 = jnp.float32
BF16 = jnp.bfloat16

D_MODEL = 1024
HEAD_W = 128
N_HEADS = 4
HALF_W = N_HEADS * HEAD_W
QK_DIM = 64
ROT_DIM = 16
ROPE_THETA = 500000.0
QUERY_GROUP = 512
KV_UNROLL = 4
VT_ROWS = HEAD_W + 16
HG_CHUNK = 128
HG_UNROLL = 8
N_GROUPS = 4
EXPERTS_PER_GROUP = 8
N_EXPERTS = 32
D_EXPERT = 512
NORM_EPS = 1e-6
SUBLN_EPS = 1e-5
LAMBDA_INIT = 0.8 - 0.6 * math.exp(-0.3 * 0)
LOG2E = 1.4426950408889634
ROUTER_LANES = 128
MOE_TOKEN_BLOCK = 512
V7X_VMEM_LIMIT = 58 * 1024 * 1024

NT_DIMS = (((1,), (1,)), ((), ()))
TN_DIMS = (((0,), (0,)), ((), ()))


def _rms(x, eps):
    return x * lax.rsqrt(jnp.mean(x * x, axis=-1, keepdims=True) + eps)


def _inproj_kernel(x_ref, pos_ref, g1_ref, w_ref, lb_ref, invf_ref, sgn_ref, pcos_ref, psin_ref,
                   hq_ref, kf_ref, kb_ref, lff_ref, lfb_ref, hi_ref, hg_ref,
                   dq_ref, dk_ref, dv_ref):
    x = x_ref[...]
    h = (_rms(x, NORM_EPS) * g1_ref[...]).astype(BF16)

    def proj(i):
        return jnp.dot(h, w_ref[:, i * HALF_W:(i + 1) * HALF_W], preferred_element_type=F32)

    hq_ref[...] = proj(0)
    for i, (k_ref, lf_ref) in enumerate(((kf_ref, lff_ref), (kb_ref, lfb_ref))):
        z = proj(1 + i)
        lb = lb_ref[i:i + 1, :]
        sg = jax.nn.sigmoid(z)
        lf_ref[...] = jnp.log(lb + (1.0 - lb) * sg)
        k_ref[...] = (1.0 - lb) * (1.0 - sg)
    hi_ref[...] = proj(3)
    hg_ref[...] = proj(4)

    ang_t = invf_ref[...] * pos_ref[...]

    def spread(t, place_ref):
        hi = t.astype(BF16)
        r1 = t - hi.astype(F32)
        mid = r1.astype(BF16)
        lo = (r1 - mid.astype(F32)).astype(BF16)
        pieces = jnp.concatenate([hi, mid, lo, jnp.zeros_like(hi)], axis=0)
        return lax.dot_general(pieces, place_ref[...], TN_DIMS, preferred_element_type=F32)

    sgn = sgn_ref[...]
    cos = jnp.concatenate([spread(jnp.cos(ang_t), pcos_ref) + jnp.where(sgn == 0.0, 1.0, 0.0)]
                          * N_HEADS, axis=1)
    sin = jnp.concatenate([spread(jnp.sin(ang_t), psin_ref)] * N_HEADS, axis=1)
    take_hi = jnp.concatenate([jnp.broadcast_to(sgn, (x.shape[0], HEAD_W))] * N_HEADS,
                              axis=1) < 0.0

    def rotary(t):
        partner = jnp.where(take_hi, pltpu.roll(t, HALF_W - ROT_DIM // 2, axis=1),
                            pltpu.roll(t, ROT_DIM // 2, axis=1))
        return t * cos + partner * sin

    dq_ref[...] = (rotary(proj(5)) * (QK_DIM ** -0.5 * LOG2E)).astype(BF16)
    dk_ref[...] = rotary(proj(6)).astype(BF16)
    v = proj(7)
    sub = lax.broadcasted_iota(jnp.int32, (VT_ROWS - HEAD_W, v.shape[0]), 0)
    ones_rows = jnp.where(sub == 0, 1.0, 0.0)
    pieces = []
    for hd in range(N_HEADS):
        pieces += [v[:, hd * HEAD_W:(hd + 1) * HEAD_W].T, ones_rows]
    dv_ref[...] = jnp.concatenate(pieces, axis=0).astype(BF16)


def _inproj(x, pos, g1, w_in, lb, invf, sgn, pcos, psin, tm=512):
    S = x.shape[0]
    row = lambda w: pl.BlockSpec((tm, w), lambda i: (i, 0))
    full = lambda a: pl.BlockSpec(a.shape, lambda i: (0, 0))
    f32o = jax.ShapeDtypeStruct((S, HALF_W), F32)
    bf16o = jax.ShapeDtypeStruct((S, HALF_W), BF16)
    return pl.pallas_call(
        _inproj_kernel,
        grid=(S // tm,),
        in_specs=[row(D_MODEL), pl.BlockSpec((1, tm), lambda i: (0, i)), full(g1), full(w_in),
                  full(lb), full(invf), full(sgn), full(pcos), full(psin)],
        out_specs=[row(HALF_W)] * 9 + [pl.BlockSpec((N_HEADS * VT_ROWS, tm), lambda i: (0, i))],
        out_shape=[f32o] * 7 + [bf16o, bf16o,
                                jax.ShapeDtypeStruct((N_HEADS * VT_ROWS, S), BF16)],
        compiler_params=pltpu.CompilerParams(dimension_semantics=("arbitrary",),
                                             vmem_limit_bytes=V7X_VMEM_LIMIT),
        name="inproj",
    )(x, pos, g1, w_in, lb, invf, sgn, pcos, psin)


def _chunk_cumsum(x, rev):
    n = x.shape[0]
    row = lax.broadcasted_iota(jnp.int32, x.shape, 0)
    d = 1
    while d < n:
        if rev:
            x = x + jnp.where(row < n - d, pltpu.roll(x, n - d, axis=0), 0.0)
        else:
            x = x + jnp.where(row >= d, pltpu.roll(x, d, axis=0), 0.0)
        d *= 2
    return x


def _pivot_rows(b, blk, rev):
    n = b.shape[0]
    half = blk // 2
    groups = []
    for g0 in range(0, n, 8):
        def piv(r):
            base = (r // blk) * blk
            return base + half if rev else base + half - 1
        if blk >= 8:
            p = piv(g0)
            groups.append(jnp.broadcast_to(b[p:p + 1, :], (8, b.shape[1])))
        else:
            sub = lax.broadcasted_iota(jnp.int32, (8, b.shape[1]), 0)
            acc = None
            for s0 in range(0, 8, blk):
                p = piv(g0 + s0)
                rowv = jnp.broadcast_to(b[p:p + 1, :], (8, b.shape[1]))
                acc = rowv if acc is None else jnp.where(sub >= s0, rowv, acc)
            groups.append(acc)
    return jnp.concatenate(groups, axis=0)


def _hgrn_kernel(q_ref, k_ref, lf_ref, v_ref, *rest, rev, n_chunks, n_cast):
    w_refs, o_ref, wb_refs, st_ref = (rest[:n_cast], rest[n_cast], rest[n_cast + 1:-1], rest[-1])
    for w_ref, wb_ref in zip(w_refs, wb_refs):
        wb_ref[...] = w_ref[...].astype(BF16)

    @pl.when(pl.program_id(1) == 0)
    def _():
        st_ref[...] = jnp.zeros_like(st_ref)

    C, W = HG_CHUNK, HEAD_W
    rt = lax.broadcasted_iota(jnp.int32, (C, C), 0)
    ct = lax.broadcasted_iota(jnp.int32, (C, C), 1)
    rowi = lax.broadcasted_iota(jnp.int32, (C, 2 * W), 0)
    eye = jnp.where(rt == ct, 1.0, 0.0)
    levels = []
    blk = C
    while blk >= 2:
        half = blk // 2
        q_side = (lambda r: (r % blk) < half) if rev else (lambda r: (r % blk) >= half)
        pair = ((rt // blk) == (ct // blk)) & q_side(rt) & jnp.logical_not(q_side(ct))
        levels.append((blk, jnp.where(q_side(rowi), 1.0, 0.0), jnp.where(pair, 1.0, 0.0)))
        blk = half

    def stack(a):
        return jnp.concatenate([a[:, :W], a[:, W:]], axis=0)

    def chunk(ci, st):
        c = (n_chunks - 1 - ci) if rev else ci
        sl = pl.ds(pl.multiple_of(c * C, C), C)
        q = q_ref[sl, :]
        k = k_ref[sl, :]
        vb = v_ref[sl, :].astype(BF16)
        b = _chunk_cumsum(lf_ref[sl, :], rev)
        tot = b[0:1, :] if rev else b[C - 1:C, :]

        o = lax.dot_general(stack((q * jnp.exp(b)).astype(BF16)), st.astype(BF16), NT_DIMS,
                            preferred_element_type=F32)
        kdec = (k * jnp.exp(tot - b)).astype(BF16)
        upd = lax.dot_general(vb, kdec, TN_DIMS, preferred_element_type=F32)
        dec = jnp.exp(tot)
        st_next = jnp.concatenate([st[:W] * dec[:, :W] + upd[:W, :W],
                                   st[W:] * dec[:, W:] + upd[W:, W:]], axis=0)

        qk = q * k
        scores = [eye * jnp.sum(qk[:, :W], axis=1, keepdims=True),
                  eye * jnp.sum(qk[:, W:], axis=1, keepdims=True)]
        for blk, q_side, pair in levels:
            w = jnp.exp2(jnp.abs(b - _pivot_rows(b, blk, rev)) * (-LOG2E))
            z = stack((jnp.where(q_side > 0.5, q, k) * w).astype(BF16))
            g = lax.dot_general(z, z, NT_DIMS, preferred_element_type=F32)
            scores = [scores[0] + pair * g[:C, :C], scores[1] + pair * g[C:, C:]]
        intra = jnp.dot(jnp.concatenate(scores, axis=0).astype(BF16), vb,
                        preferred_element_type=F32)
        o_ref[sl, :] = jnp.concatenate([o[:C, :W] + intra[:C, :W], o[C:, W:] + intra[C:, W:]],
                                       axis=1)
        return st_next

    st_ref[...] = lax.fori_loop(0, n_chunks, chunk, st_ref[...], unroll=HG_UNROLL)


def _hgrn(q, k, lf, v, rev, weights, rb=1024):
    S = q.shape[0]
    nb = S // rb
    n_pairs = N_HEADS // 2
    steps = n_pairs * nb
    idx = (lambda h, j: (nb - 1 - j, h)) if rev else (lambda h, j: (j, h))
    spec = pl.BlockSpec((rb, 2 * HEAD_W), idx)
    w_specs = []
    for w in weights:
        n_exp, rows, cols = w.shape
        per_exp = steps // n_exp
        assert per_exp * n_exp == steps and rows % per_exp == 0
        w_specs.append(pl.BlockSpec(
            (1, rows // per_exp, cols),
            lambda h, j, per_exp=per_exp: ((h * nb + j) // per_exp, (h * nb + j) % per_exp, 0)))
    return pl.pallas_call(
        functools.partial(_hgrn_kernel, rev=rev, n_chunks=rb // HG_CHUNK, n_cast=len(weights)),
        grid=(n_pairs, nb),
        in_specs=[spec] * 4 + w_specs,
        out_specs=[spec] + w_specs,
        out_shape=[jax.ShapeDtypeStruct((S, HALF_W), F32)]
        + [jax.ShapeDtypeStruct(w.shape, BF16) for w in weights],
        scratch_shapes=[pltpu.VMEM((2 * HEAD_W, HEAD_W), F32)],
        compiler_params=pltpu.CompilerParams(dimension_semantics=("arbitrary", "arbitrary"),
                                             vmem_limit_bytes=V7X_VMEM_LIMIT),
        name="hgrn_bwd" if rev else "hgrn_fwd",
    )(q, k, lf, v, *weights)


def _attn_kernel(lam_ref, q_ref, k_ref, vt_ref, gain_ref, o_ref, sa_ref, sb_ref, m_ref, acc_ref,
                 *, tk):
    n_kv = k_ref.shape[0] // tk
    tq = q_ref.shape[0]
    assert n_kv % KV_UNROLL == 0 and KV_UNROLL % 2 == 0 and tq % QUERY_GROUP == 0
    q = q_ref[...]
    lane = lax.broadcasted_iota(jnp.int32, q.shape, 1)
    qm = (jnp.where(lane < QK_DIM, q, jnp.zeros_like(q)),
          jnp.where(lane >= QK_DIM, q, jnp.zeros_like(q)))
    m_ref[...] = jnp.full(m_ref.shape, -jnp.inf, F32)
    acc_ref[...] = jnp.zeros_like(acc_ref)

    def chunk(j):
        return pl.ds(pl.multiple_of(j * tk, tk), tk)

    def softmax_pv(c, cols, s, vt):
        m_old = m_ref[c, :, cols]
        m_new = jnp.maximum(m_old, jnp.max(s, axis=0, keepdims=True))
        p = jnp.exp2(s - m_new).astype(BF16)
        acc_ref[c, :, cols] = jnp.exp2(m_old - m_new) * acc_ref[c, :, cols] + jnp.dot(
            vt, p, preferred_element_type=F32)
        m_ref[c, :, cols] = m_new

    def step(js, dst_ref, jp, src_ref):
        kc = None if dst_ref is None else k_ref[chunk(js), :]
        vt = None if src_ref is None else vt_ref[:, chunk(jp)]
        for c in range(2):
            for g in range(0, tq, QUERY_GROUP):
                cols = slice(g, g + QUERY_GROUP)
                if dst_ref is not None:
                    dst_ref[c, :, cols] = lax.dot_general(kc, qm[c][cols, :], NT_DIMS,
                                                          preferred_element_type=F32)
                if src_ref is not None:
                    softmax_pv(c, cols, src_ref[c, :, cols], vt)

    bufs = (sa_ref, sb_ref)
    step(0, sa_ref, None, None)

    def body(i, carry):
        for u in range(KV_UNROLL):
            j = KV_UNROLL * i + u
            step(j + 1, bufs[(u + 1) % 2], j, bufs[u % 2])
        return carry

    lax.fori_loop(0, n_kv // KV_UNROLL - 1, body, 0)
    for j in range(n_kv - KV_UNROLL, n_kv - 1):
        step(j + 1, bufs[(j + 1) % 2], j, bufs[j % 2])
    step(None, None, n_kv - 1, bufs[(n_kv - 1) % 2])

    outs = []
    for c in range(2):
        a = acc_ref[c]
        outs.append(a[:HEAD_W, :] / a[HEAD_W:HEAD_W + 1, :])
    o = (outs[0] - lam_ref[0] * outs[1]).T
    o_ref[...] = _rms(o, SUBLN_EPS) * gain_ref[...] * (1.0 - LAMBDA_INIT)


def _attn(lam, dq, dk, dvt, gain, tq=2048, tk=512):
    S = dq.shape[0]
    return pl.pallas_call(
        functools.partial(_attn_kernel, tk=tk),
        grid=(N_HEADS, S // tq),
        in_specs=[pl.BlockSpec(memory_space=pltpu.SMEM),
                  pl.BlockSpec((tq, HEAD_W), lambda h, i: (i, h)),
                  pl.BlockSpec((S, HEAD_W), lambda h, i: (0, h)),
                  pl.BlockSpec((VT_ROWS, S), lambda h, i: (h, 0)),
                  pl.BlockSpec((1, HEAD_W), lambda h, i: (0, 0))],
        out_specs=pl.BlockSpec((tq, HEAD_W), lambda h, i: (i, h)),
        out_shape=jax.ShapeDtypeStruct((S, HALF_W), F32),
        scratch_shapes=[pltpu.VMEM((2, tk, tq), F32), pltpu.VMEM((2, tk, tq), F32),
                        pltpu.VMEM((2, 1, tq), F32), pltpu.VMEM((2, VT_ROWS, tq), F32)],
        compiler_params=pltpu.CompilerParams(dimension_semantics=("arbitrary", "arbitrary"),
                                             vmem_limit_bytes=V7X_VMEM_LIMIT),
        name="diff_attn",
    )(lam, dq, dk, dvt, gain)


def _outproj_kernel(x_ref, of_ref, ob_ref, hg_ref, oda_ref, hgain_ref, wo_ref, g2_ref,
                    wr_ref, br_ref, x2_ref, rank_t_ref, gates_t_ref, cnt_ref,
                    carry_ref, *, sub_blocks):
    i = pl.program_id(0)
    part = x_ref.shape[0]

    def project(rows):
        o = of_ref[rows, :] + ob_ref[rows, :]
        gate = hg_ref[rows, :]
        gate = gate * jax.nn.sigmoid(gate)
        parts = []
        for hd in range(N_HEADS):
            sl = slice(hd * HEAD_W, (hd + 1) * HEAD_W)
            parts.append(_rms(o[:, sl], NORM_EPS) * hgain_ref[...] * gate[:, sl])
        parts.append(oda_ref[rows, :])
        mixed = jnp.concatenate(parts, axis=1).astype(BF16)
        x2 = x_ref[rows, :] + jnp.dot(mixed, wo_ref[...], preferred_element_type=F32)
        x2_ref[rows, :] = x2
        return x2

    lane = lax.broadcasted_iota(jnp.int32, (part, ROUTER_LANES), 1)
    neg = -jnp.inf
    tri = jnp.where(lax.broadcasted_iota(jnp.int32, (part, part), 0)
                    >= lax.broadcasted_iota(jnp.int32, (part, part), 1), 1.0, 0.0).astype(BF16)

    def first_max(vals):
        mx = jnp.max(vals, axis=1, keepdims=True)
        idx = jnp.min(jnp.where(vals == mx, lane, ROUTER_LANES), axis=1, keepdims=True)
        return mx, idx

    def route(rows, x2, carry):
        h2 = _rms(x2, NORM_EPS) * g2_ref[...]
        h_hi = h2.astype(BF16)
        h_lo = (h2 - h_hi.astype(F32)).astype(BF16)
        t = jnp.dot(h_hi, wr_ref[...], preferred_element_type=F32)
        logits = (t[:, :ROUTER_LANES] + t[:, ROUTER_LANES:] + br_ref[...]
                  + jnp.dot(h_lo, wr_ref[:, :ROUTER_LANES], preferred_element_type=F32))
        gl = jnp.where((lane >= N_EXPERTS) & (lane < N_EXPERTS + N_GROUPS), logits, neg)
        gm, gidx = first_max(gl)
        g_p = 1.0 / jnp.sum(jnp.exp(gl - gm), axis=1, keepdims=True)
        el = jnp.where((lane < N_EXPERTS) & ((lane // EXPERTS_PER_GROUP) == (gidx - N_EXPERTS)),
                       logits, neg)
        em1, idx1 = first_max(el)
        em2, idx2 = first_max(jnp.where(lane == idx1, neg, el))
        e2 = jnp.exp(em2 - em1)
        w1 = g_p / (1.0 + e2)
        w2 = g_p * e2 / (1.0 + e2)
        is1 = lane == idx1
        is2 = lane == idx2
        gates = jnp.where(is1, w1, 0.0) + jnp.where(is2, w2, 0.0)
        assigned = jnp.where(is1 | is2, 1.0, 0.0)
        cum = jnp.dot(tri, assigned.astype(BF16), preferred_element_type=F32)
        rank = jnp.where(assigned > 0.0, carry + cum - 1.0, -1.0)
        rank_t_ref[:, rows] = rank.T
        gates_t_ref[:, rows] = gates.T
        return carry + cum[part - 1:part, :]

    @pl.when(i % sub_blocks == 0)
    def _():
        carry_ref[...] = jnp.zeros_like(carry_ref)

    rows = slice(0, part)
    carry = route(rows, project(rows), carry_ref[...])
    carry_ref[...] = carry
    cnt_ref[pl.ds(i // sub_blocks, 1), :] = carry.astype(jnp.int32)


def _outproj(x, o_f, o_b, hg, o_da, hgain, w_out, g2, wr, br, tb, tm=512):
    S = x.shape[0]
    row = lambda w: pl.BlockSpec((tm, w), lambda i: (i, 0))
    col = pl.BlockSpec((ROUTER_LANES, tm), lambda i: (0, i))
    full = lambda a: pl.BlockSpec(a.shape, lambda i: (0, 0))
    return pl.pallas_call(
        functools.partial(_outproj_kernel, sub_blocks=tb // tm),
        grid=(S // tm,),
        in_specs=[row(D_MODEL), row(HALF_W), row(HALF_W), row(HALF_W), row(HALF_W),
                  full(hgain), full(w_out), full(g2), full(wr), full(br)],
        out_specs=[row(D_MODEL), col, col,
                   pl.BlockSpec((S // tb, ROUTER_LANES), lambda i: (0, 0))],
        out_shape=[jax.ShapeDtypeStruct((S, D_MODEL), F32),
                   jax.ShapeDtypeStruct((ROUTER_LANES, S), F32),
                   jax.ShapeDtypeStruct((ROUTER_LANES, S), F32),
                   jax.ShapeDtypeStruct((S // tb, ROUTER_LANES), jnp.int32)],
        scratch_shapes=[pltpu.VMEM((1, ROUTER_LANES), F32)],
        compiler_params=pltpu.CompilerParams(dimension_semantics=("arbitrary",),
                                             vmem_limit_bytes=V7X_VMEM_LIMIT),
        name="outproj_router",
    )(x, o_f, o_b, hg, o_da, hgain, w_out, g2, wr, br)


def _moe_kernel(cnt_ref, x2_ref, rank_t_ref, gates_t_ref, g2_ref, wg_ref, wu_ref,
                wd_ref, gf_ref, out_ref, h2_ref, *, tile_rows, tb):
    b = pl.program_id(0)
    p = pl.program_id(1)
    R = tile_rows
    n_exp = wg_ref.shape[0]
    n_sub = x2_ref.shape[0] // tb

    @pl.when(p == 0)
    def _():
        x2 = x2_ref[...]
        h2_ref[...] = (_rms(x2, NORM_EPS) * g2_ref[...]).astype(BF16)
        out_ref[...] = x2

    experts = [n_exp * p + i for i in range(n_exp)]
    subs = [slice(s * tb, (s + 1) * tb) for s in range(n_sub)]
    rank_row = [[rank_t_ref[pl.ds(e, 1), sub] for e in experts] for sub in subs]
    gate_row = [[gates_t_ref[pl.ds(e, 1), sub] for e in experts] for sub in subs]
    n_max = jnp.int32(0)
    for s in range(n_sub):
        for e in experts:
            n_max = jnp.maximum(n_max, cnt_ref[n_sub * b + s, e])

    def tile(t, carry):
        base = (t * R).astype(F32)
        slot_col = lax.broadcasted_iota(jnp.int32, (R, 1), 0).astype(F32) + base
        sel, gather, xg = [], [], []
        for s, sub in enumerate(subs):
            sel.append([rank_row[s][i] == slot_col for i in range(n_exp)])
            gather.append(jnp.concatenate(
                [jnp.where(m, 1.0, 0.0).astype(BF16) for m in sel[s]], axis=0))
            xg.append(jnp.dot(gather[s], h2_ref[sub, :],
                              preferred_element_type=F32).astype(BF16))
        ys = [[] for _ in subs]
        for i in range(n_exp):
            xi = jnp.concatenate([xg[s][i * R:(i + 1) * R] for s in range(n_sub)], axis=0)
            a = jnp.dot(xi, wg_ref[i], preferred_element_type=F32)
            a = a * jax.nn.sigmoid(a) * jnp.dot(xi, wu_ref[i], preferred_element_type=F32)
            y = jnp.dot(a.astype(BF16), wd_ref[i], preferred_element_type=F32)
            for s in range(n_sub):
                g = jnp.sum(jnp.where(sel[s][i], gate_row[s][i], 0.0), axis=1, keepdims=True)
                ys[s].append((y[s * R:(s + 1) * R] * g).astype(BF16))
        for s, sub in enumerate(subs):
            out_ref[sub, :] += lax.dot_general(gather[s], jnp.concatenate(ys[s], axis=0),
                                               TN_DIMS, preferred_element_type=F32)
        return carry

    lax.fori_loop(0, (n_max + R - 1) // R, tile, 0)

    @pl.when(p == pl.num_programs(1) - 1)
    def _():
        out_ref[...] = _rms(out_ref[...], NORM_EPS) * gf_ref[...]


def _moe(cnt, x2, rank_t, gates_t, g2, wg, wu, wd, gf, tb, tile_rows=48,
         experts_per_step=4, blocks_per_step=4):
    S = x2.shape[0]
    eps = experts_per_step
    rows = tb * blocks_per_step
    grid_spec = pltpu.PrefetchScalarGridSpec(
        num_scalar_prefetch=1,
        grid=(S // rows, N_EXPERTS // eps),
        in_specs=[pl.BlockSpec((rows, D_MODEL), lambda b, p, c: (b, 0),
                               pipeline_mode=pl.Buffered(1)),
                  pl.BlockSpec((N_EXPERTS, rows), lambda b, p, c: (0, b)),
                  pl.BlockSpec((N_EXPERTS, rows), lambda b, p, c: (0, b)),
                  pl.BlockSpec((1, D_MODEL), lambda b, p, c: (0, 0)),
                  pl.BlockSpec((eps, D_MODEL, D_EXPERT), lambda b, p, c: (p, 0, 0)),
                  pl.BlockSpec((eps, D_MODEL, D_EXPERT), lambda b, p, c: (p, 0, 0)),
                  pl.BlockSpec((eps, D_EXPERT, D_MODEL), lambda b, p, c: (p, 0, 0)),
                  pl.BlockSpec((1, D_MODEL), lambda b, p, c: (0, 0))],
        out_specs=pl.BlockSpec((rows, D_MODEL), lambda b, p, c: (b, 0)),
        scratch_shapes=[pltpu.VMEM((rows, D_MODEL), BF16)])
    return pl.pallas_call(
        functools.partial(_moe_kernel, tile_rows=tile_rows, tb=tb),
        grid_spec=grid_spec,
        out_shape=jax.ShapeDtypeStruct((S, D_MODEL), F32),
        compiler_params=pltpu.CompilerParams(dimension_semantics=("arbitrary", "arbitrary"),
                                             vmem_limit_bytes=V7X_VMEM_LIMIT),
        name="moe",
    )(cnt, x2, rank_t, gates_t, g2, wg, wu, wd, gf)


def kernel(x, positions, norm1_gain, w_in, hg_lower_bounds, hg_norm_gain, diff_lambda,
           diff_subln_gain, w_out, norm2_gain, router_group_w, router_group_b,
           router_expert_w, router_expert_b, moe_w_gate, moe_w_up, moe_w_down,
           final_norm_gain):
    B, S, _ = x.shape
    assert B == 1 and norm1_gain.shape[0] == 1
    layer = 0
    xs = x.reshape(S, D_MODEL)
    pos = positions.reshape(1, S).astype(F32)

    lb_cum = jnp.cumsum(jax.nn.softmax(hg_lower_bounds.astype(F32), axis=1), axis=1)
    lb = lb_cum[:, layer + 1] - lb_cum[:, 0]
    lam_p = diff_lambda[layer].astype(F32)
    lam = (jnp.exp(jnp.sum(lam_p[0] * lam_p[1])) - jnp.exp(jnp.sum(lam_p[2] * lam_p[3]))
           + LAMBDA_INIT).reshape(1)
    lane = jnp.arange(HEAD_W)
    inv_freq = jnp.float32(ROPE_THETA) ** (-jnp.arange(0, ROT_DIM, 2, dtype=F32) / ROT_DIM)
    in_rot = (lane % QK_DIM) < ROT_DIM
    n_freq = ROT_DIM // 2
    invf = inv_freq.reshape(n_freq, 1)
    sgn = jnp.where(in_rot, jnp.where((lane % QK_DIM) < n_freq, -1.0, 1.0),
                    0.0).astype(F32).reshape(1, HEAD_W)
    place = (in_rot[None, :] & (lane[None, :] % n_freq == jnp.arange(n_freq)[:, None])).astype(F32)
    pcos = jnp.tile(place, (4, 1)).astype(BF16)
    psin = jnp.tile(place * sgn, (4, 1)).astype(BF16)
    wr = jnp.zeros((D_MODEL, ROUTER_LANES), F32)
    wr = wr.at[:, :N_EXPERTS].set(router_expert_w[layer])
    wr = wr.at[:, N_EXPERTS:N_EXPERTS + N_GROUPS].set(router_group_w[layer])
    wr_hi = wr.astype(BF16)
    wr = jnp.concatenate([wr_hi, (wr - wr_hi.astype(F32)).astype(BF16)], axis=1)
    br = jnp.zeros((1, ROUTER_LANES), F32)
    br = br.at[0, :N_EXPERTS].set(router_expert_b[layer])
    br = br.at[0, N_EXPERTS:N_EXPERTS + N_GROUPS].set(router_group_b[layer])

    hq, kf, kb, lff, lfb, hi, hg, dq, dk, dv = _inproj(
        xs, pos, norm1_gain[layer].reshape(1, -1), w_in[layer].astype(BF16), lb, invf, sgn,
        pcos, psin)
    o_f, wg, wu = _hgrn(hq, kf, lff, hi, False, (moe_w_gate[layer], moe_w_up[layer]))
    o_b, wd = _hgrn(hq, kb, lfb, hi, True, (moe_w_down[layer],))
    o_da = _attn(lam, dq, dk, dv, diff_subln_gain[layer].reshape(1, -1))
    x2, rank_t, gates_t, cnt = _outproj(
        xs, o_f, o_b, hg, o_da, hg_norm_gain[layer].reshape(1, -1), w_out[layer].astype(BF16),
        norm2_gain[layer].reshape(1, -1), wr, br, tb=MOE_TOKEN_BLOCK)
    out = _moe(cnt, x2, rank_t, gates_t, norm2_gain[layer].reshape(1, -1), wg, wu, wd,
               final_norm_gain.reshape(1, -1), tb=MOE_TOKEN_BLOCK)
    return out.reshape(B, S, D_MODEL)
```

```python
import functools
import math

import jax
import jax.numpy as jnp
from jax import lax
from jax.experimental import pallas as pl
from jax.experimental.pallas import tpu as pltpu

F32 = jnp.float32
BF16 = jnp.bfloat16

D_MODEL = 1024
HEAD_W = 128
N_HEADS = 4
HALF_W = N_HEADS * HEAD_W
QK_DIM = 64
ROT_DIM = 16
ROPE_THETA = 500000.0
QUERY_GROUP = 512
KV_UNROLL = 4
VT_ROWS = HEAD_W + 16
HG_CHUNK = 128
HG_UNROLL = 8
N_GROUPS = 4
EXPERTS_PER_GROUP = 8
N_EXPERTS = 32
D_EXPERT = 512
NORM_EPS = 1e-6
SUBLN_EPS = 1e-5
LAMBDA_INIT = 0.8 - 0.6 * math.exp(-0.3 * 0)
LOG2E = 1.4426950408889634
ROUTER_LANES = 128
MOE_TOKEN_BLOCK = 512
V7X_VMEM_LIMIT = 58 * 1024 * 1024

NT_DIMS = (((1,), (1,)), ((), ()))
TN_DIMS = (((0,), (0,)), ((), ()))


def _rms(x, eps):
    return x * lax.rsqrt(jnp.mean(x * x, axis=-1, keepdims=True) + eps)


def _inproj_kernel(x_ref, pos_ref, g1_ref, w_ref, lb_ref, invf_ref, sgn_ref, pcos_ref, psin_ref,
                   hq_ref, kf_ref, kb_ref, lff_ref, lfb_ref, hi_ref, hg_ref,
                   dq_ref, dk_ref, dv_ref):
    x = x_ref[...]
    h = (_rms(x, NORM_EPS) * g1_ref[...]).astype(BF16)

    def proj(i):
        return jnp.dot(h, w_ref[:, i * HALF_W:(i + 1) * HALF_W], preferred_element_type=F32)

    hq_ref[...] = proj(0)
    for i, (k_ref, lf_ref) in enumerate(((kf_ref, lff_ref), (kb_ref, lfb_ref))):
        z = proj(1 + i)
        lb = lb_ref[i:i + 1, :]
        sg = jax.nn.sigmoid(z)
        lf_ref[...] = jnp.log(lb + (1.0 - lb) * sg)
        k_ref[...] = (1.0 - lb) * (1.0 - sg)
    hi_ref[...] = proj(3)
    hg_ref[...] = proj(4)

    ang_t = invf_ref[...] * pos_ref[...]

    def spread(t, place_ref):
        hi = t.astype(BF16)
        r1 = t - hi.astype(F32)
        mid = r1.astype(BF16)
        lo = (r1 - mid.astype(F32)).astype(BF16)
        pieces = jnp.concatenate([hi, mid, lo, jnp.zeros_like(hi)], axis=0)
        return lax.dot_general(pieces, place_ref[...], TN_DIMS, preferred_element_type=F32)

    sgn = sgn_ref[...]
    cos = jnp.concatenate([spread(jnp.cos(ang_t), pcos_ref) + jnp.where(sgn == 0.0, 1.0, 0.0)]
                          * N_HEADS, axis=1)
    sin = jnp.concatenate([spread(jnp.sin(ang_t), psin_ref)] * N_HEADS, axis=1)
    take_hi = jnp.concatenate([jnp.broadcast_to(sgn, (x.shape[0], HEAD_W))] * N_HEADS,
                              axis=1) < 0.0

    def rotary(t):
        partner = jnp.where(take_hi, pltpu.roll(t, HALF_W - ROT_DIM // 2, axis=1),
                            pltpu.roll(t, ROT_DIM // 2, axis=1))
        return t * cos + partner * sin

    dq_ref[...] = (rotary(proj(5)) * (QK_DIM ** -0.5 * LOG2E)).astype(BF16)
    dk_ref[...] = rotary(proj(6)).astype(BF16)
    v = proj(7)
    sub = lax.broadcasted_iota(jnp.int32, (VT_ROWS - HEAD_W, v.shape[0]), 0)
    ones_rows = jnp.where(sub == 0, 1.0, 0.0)
    pieces = []
    for hd in range(N_HEADS):
        pieces += [v[:, hd * HEAD_W:(hd + 1) * HEAD_W].T, ones_rows]
    dv_ref[...] = jnp.concatenate(pieces, axis=0).astype(BF16)


def _inproj(x, pos, g1, w_in, lb, invf, sgn, pcos, psin, tm=512):
    S = x.shape[0]
    row = lambda w: pl.BlockSpec((tm, w), lambda i: (i, 0))
    full = lambda a: pl.BlockSpec(a.shape, lambda i: (0, 0))
    f32o = jax.ShapeDtypeStruct((S, HALF_W), F32)
    bf16o = jax.ShapeDtypeStruct((S, HALF_W), BF16)
    return pl.pallas_call(
        _inproj_kernel,
        grid=(S // tm,),
        in_specs=[row(D_MODEL), pl.BlockSpec((1, tm), lambda i: (0, i)), full(g1), full(w_in),
                  full(lb), full(invf), full(sgn), full(pcos), full(psin)],
        out_specs=[row(HALF_W)] * 9 + [pl.BlockSpec((N_HEADS * VT_ROWS, tm), lambda i: (0, i))],
        out_shape=[f32o] * 7 + [bf16o, bf16o,
                                jax.ShapeDtypeStruct((N_HEADS * VT_ROWS, S), BF16)],
        compiler_params=pltpu.CompilerParams(dimension_semantics=("arbitrary",),
                                             vmem_limit_bytes=V7X_VMEM_LIMIT),
        name="inproj",
    )(x, pos, g1, w_in, lb, invf, sgn, pcos, psin)


def _chunk_cumsum(x, rev):
    n = x.shape[0]
    row = lax.broadcasted_iota(jnp.int32, x.shape, 0)
    d = 1
    while d < n:
        if rev:
            x = x + jnp.where(row < n - d, pltpu.roll(x, n - d, axis=0), 0.0)
        else:
            x = x + jnp.where(row >= d, pltpu.roll(x, d, axis=0), 0.0)
        d *= 2
    return x


def _pivot_rows(b, blk, rev):
    n = b.shape[0]
    half = blk // 2
    groups = []
    for g0 in range(0, n, 8):
        def piv(r):
            base = (r // blk) * blk
            return base + half if rev else base + half - 1
        if blk >= 8:
            p = piv(g0)
            groups.append(jnp.broadcast_to(b[p:p + 1, :], (8, b.shape[1])))
        else:
            sub = lax.broadcasted_iota(jnp.int32, (8, b.shape[1]), 0)
            acc = None
            for s0 in range(0, 8, blk):
                p = piv(g0 + s0)
                rowv = jnp.broadcast_to(b[p:p + 1, :], (8, b.shape[1]))
                acc = rowv if acc is None else jnp.where(sub >= s0, rowv, acc)
            groups.append(acc)
    return jnp.concatenate(groups, axis=0)


def _hgrn_kernel(q_ref, k_ref, lf_ref, v_ref, *rest, rev, n_chunks, n_cast):
    w_refs, o_ref, wb_refs, st_ref = (rest[:n_cast], rest[n_cast], rest[n_cast + 1:-1], rest[-1])
    for w_ref, wb_ref in zip(w_refs, wb_refs):
        wb_ref[...] = w_ref[...].astype(BF16)

    @pl.when(pl.program_id(1) == 0)
    def _():
        st_ref[...] = jnp.zeros_like(st_ref)

    C, W = HG_CHUNK, HEAD_W
    rt = lax.broadcasted_iota(jnp.int32, (C, C), 0)
    ct = lax.broadcasted_iota(jnp.int32, (C, C), 1)
    rowi = lax.broadcasted_iota(jnp.int32, (C, 2 * W), 0)
    eye = jnp.where(rt == ct, 1.0, 0.0)
    levels = []
    blk = C
    while blk >= 2:
        half = blk // 2
        q_side = (lambda r: (r % blk) < half) if rev else (lambda r: (r % blk) >= half)
        pair = ((rt // blk) == (ct // blk)) & q_side(rt) & jnp.logical_not(q_side(ct))
        levels.append((blk, jnp.where(q_side(rowi), LOG2E, -LOG2E), jnp.where(pair, 1.0, 0.0)))
        blk = half

    def stack(a):
        return jnp.concatenate([a[:, :W], a[:, W:]], axis=0)

    def chunk(ci, st):
        c = (n_chunks - 1 - ci) if rev else ci
        sl = pl.ds(pl.multiple_of(c * C, C), C)
        q = q_ref[sl, :]
        k = k_ref[sl, :]
        vb = v_ref[sl, :].astype(BF16)
        b = _chunk_cumsum(lf_ref[sl, :], rev)
        tot = b[0:1, :] if rev else b[C - 1:C, :]

        o = lax.dot_general(stack((q * jnp.exp(b)).astype(BF16)), st.astype(BF16), NT_DIMS,
                            preferred_element_type=F32)
        kdec = (k * jnp.exp(tot - b)).astype(BF16)
        upd = lax.dot_general(vb, kdec, TN_DIMS, preferred_element_type=F32)
        dec = jnp.exp(tot)
        st_next = jnp.concatenate([st[:W] * dec[:, :W] + upd[:W, :W],
                                   st[W:] * dec[:, W:] + upd[W:, W:]], axis=0)

        qk = q * k
        scores = [eye * jnp.sum(qk[:, :W], axis=1, keepdims=True),
                  eye * jnp.sum(qk[:, W:], axis=1, keepdims=True)]
        for blk, side, pair in levels:
            w = jnp.exp2((b - _pivot_rows(b, blk, rev)) * side)
            z = stack((jnp.where(side > 0.0, q, k) * w).astype(BF16))
            g = lax.dot_general(z, z, NT_DIMS, preferred_element_type=F32)
            scores = [scores[0] + pair * g[:C, :C], scores[1] + pair * g[C:, C:]]
        intra = jnp.dot(jnp.concatenate(scores, axis=0).astype(BF16), vb,
                        preferred_element_type=F32)
        o_ref[sl, :] = jnp.concatenate([o[:C, :W] + intra[:C, :W], o[C:, W:] + intra[C:, W:]],
                                       axis=1)
        return st_next

    st_ref[...] = lax.fori_loop(0, n_chunks, chunk, st_ref[...], unroll=HG_UNROLL)


def _hgrn(q, k, lf, v, rev, weights, rb=2048):
    S = q.shape[0]
    nb = S // rb
    n_pairs = N_HEADS // 2
    steps = n_pairs * nb
    idx = (lambda h, j: (nb - 1 - j, h)) if rev else (lambda h, j: (j, h))
    spec = pl.BlockSpec((rb, 2 * HEAD_W), idx)
    w_specs = []
    for w in weights:
        n_exp, rows, cols = w.shape
        per_step = n_exp // steps
        assert per_step * steps == n_exp
        w_specs.append(pl.BlockSpec((per_step, rows, cols), lambda h, j: (h * nb + j, 0, 0)))
    return pl.pallas_call(
        functools.partial(_hgrn_kernel, rev=rev, n_chunks=rb // HG_CHUNK, n_cast=len(weights)),
        grid=(n_pairs, nb),
        in_specs=[spec] * 4 + w_specs,
        out_specs=[spec] + w_specs,
        out_shape=[jax.ShapeDtypeStruct((S, HALF_W), F32)]
        + [jax.ShapeDtypeStruct(w.shape, BF16) for w in weights],
        scratch_shapes=[pltpu.VMEM((2 * HEAD_W, HEAD_W), F32)],
        compiler_params=pltpu.CompilerParams(dimension_semantics=("arbitrary", "arbitrary"),
                                             vmem_limit_bytes=V7X_VMEM_LIMIT),
        name="hgrn_bwd" if rev else "hgrn_fwd",
    )(q, k, lf, v, *weights)


def _attn_kernel(lam_ref, q_ref, k_ref, vt_ref, gain_ref, o_ref, sa_ref, sb_ref, m_ref, acc_ref,
                 *, tk):
    n_kv = k_ref.shape[0] // tk
    tq = q_ref.shape[0]
    assert n_kv % KV_UNROLL == 0 and KV_UNROLL % 2 == 0 and tq % QUERY_GROUP == 0
    q = q_ref[...]
    lane = lax.broadcasted_iota(jnp.int32, q.shape, 1)
    qm = (jnp.where(lane < QK_DIM, q, jnp.zeros_like(q)),
          jnp.where(lane >= QK_DIM, q, jnp.zeros_like(q)))
    m_ref[...] = jnp.full(m_ref.shape, -jnp.inf, F32)
    acc_ref[...] = jnp.zeros_like(acc_ref)

    def chunk(j):
        return pl.ds(pl.multiple_of(j * tk, tk), tk)

    def softmax_pv(c, cols, s, vt):
        m_old = m_ref[c, :, cols]
        m_new = jnp.maximum(m_old, jnp.max(s, axis=0, keepdims=True))
        p = jnp.exp2(s - m_new).astype(BF16)
        acc_ref[c, :, cols] = jnp.exp2(m_old - m_new) * acc_ref[c, :, cols] + jnp.dot(
            vt, p, preferred_element_type=F32)
        m_ref[c, :, cols] = m_new

    def step(js, dst_ref, jp, src_ref):
        kc = None if dst_ref is None else k_ref[chunk(js), :]
        vt = None if src_ref is None else vt_ref[:, chunk(jp)]
        for c in range(2):
            for g in range(0, tq, QUERY_GROUP):
                cols = slice(g, g + QUERY_GROUP)
                if dst_ref is not None:
                    dst_ref[c, :, cols] = lax.dot_general(kc, qm[c][cols, :], NT_DIMS,
                                                          preferred_element_type=F32)
                if src_ref is not None:
                    softmax_pv(c, cols, src_ref[c, :, cols], vt)

    bufs = (sa_ref, sb_ref)
    step(0, sa_ref, None, None)

    def body(i, carry):
        for u in range(KV_UNROLL):
            j = KV_UNROLL * i + u
            step(j + 1, bufs[(u + 1) % 2], j, bufs[u % 2])
        return carry

    lax.fori_loop(0, n_kv // KV_UNROLL - 1, body, 0)
    for j in range(n_kv - KV_UNROLL, n_kv - 1):
        step(j + 1, bufs[(j + 1) % 2], j, bufs[j % 2])
    step(None, None, n_kv - 1, bufs[(n_kv - 1) % 2])

    outs = []
    for c in range(2):
        a = acc_ref[c]
        outs.append(a[:HEAD_W, :] / a[HEAD_W:HEAD_W + 1, :])
    o = (outs[0] - lam_ref[0] * outs[1]).T
    o_ref[...] = _rms(o, SUBLN_EPS) * gain_ref[...] * (1.0 - LAMBDA_INIT)


def _attn(lam, dq, dk, dvt, gain, tq=2048, tk=512):
    S = dq.shape[0]
    return pl.pallas_call(
        functools.partial(_attn_kernel, tk=tk),
        grid=(N_HEADS, S // tq),
        in_specs=[pl.BlockSpec(memory_space=pltpu.SMEM),
                  pl.BlockSpec((tq, HEAD_W), lambda h, i: (i, h)),
                  pl.BlockSpec((S, HEAD_W), lambda h, i: (0, h)),
                  pl.BlockSpec((VT_ROWS, S), lambda h, i: (h, 0)),
                  pl.BlockSpec((1, HEAD_W), lambda h, i: (0, 0))],
        out_specs=pl.BlockSpec((tq, HEAD_W), lambda h, i: (i, h)),
        out_shape=jax.ShapeDtypeStruct((S, HALF_W), F32),
        scratch_shapes=[pltpu.VMEM((2, tk, tq), F32), pltpu.VMEM((2, tk, tq), F32),
                        pltpu.VMEM((2, 1, tq), F32), pltpu.VMEM((2, VT_ROWS, tq), F32)],
        compiler_params=pltpu.CompilerParams(dimension_semantics=("arbitrary", "arbitrary"),
                                             vmem_limit_bytes=V7X_VMEM_LIMIT),
        name="diff_attn",
    )(lam, dq, dk, dvt, gain)


def _outproj_kernel(x_ref, of_ref, ob_ref, hg_ref, oda_ref, hgain_ref, wo_ref, g2_ref,
                    wr_ref, br_ref, x2_ref, rank_t_ref, gates_t_ref, cnt_ref,
                    carry_ref, *, sub_blocks):
    i = pl.program_id(0)
    part = x_ref.shape[0]

    def project(rows):
        o = of_ref[rows, :] + ob_ref[rows, :]
        gate = hg_ref[rows, :]
        gate = gate * jax.nn.sigmoid(gate)
        parts = []
        for hd in range(N_HEADS):
            sl = slice(hd * HEAD_W, (hd + 1) * HEAD_W)
            parts.append(_rms(o[:, sl], NORM_EPS) * hgain_ref[...] * gate[:, sl])
        parts.append(oda_ref[rows, :])
        mixed = jnp.concatenate(parts, axis=1).astype(BF16)
        x2 = x_ref[rows, :] + jnp.dot(mixed, wo_ref[...], preferred_element_type=F32)
        x2_ref[rows, :] = x2
        return x2

    lane = lax.broadcasted_iota(jnp.int32, (part, ROUTER_LANES), 1)
    neg = -jnp.inf
    tri = jnp.where(lax.broadcasted_iota(jnp.int32, (part, part), 0)
                    >= lax.broadcasted_iota(jnp.int32, (part, part), 1), 1.0, 0.0).astype(BF16)

    def first_max(vals):
        mx = jnp.max(vals, axis=1, keepdims=True)
        idx = jnp.min(jnp.where(vals == mx, lane, ROUTER_LANES), axis=1, keepdims=True)
        return mx, idx

    def route(rows, x2, carry):
        h2 = _rms(x2, NORM_EPS) * g2_ref[...]
        h_hi = h2.astype(BF16)
        h_lo = (h2 - h_hi.astype(F32)).astype(BF16)
        t = jnp.dot(h_hi, wr_ref[...], preferred_element_type=F32)
        logits = (t[:, :ROUTER_LANES] + t[:, ROUTER_LANES:] + br_ref[...]
                  + jnp.dot(h_lo, wr_ref[:, :ROUTER_LANES], preferred_element_type=F32))
        gl = jnp.where((lane >= N_EXPERTS) & (lane < N_EXPERTS + N_GROUPS), logits, neg)
        gm, gidx = first_max(gl)
        g_p = 1.0 / jnp.sum(jnp.exp(gl - gm), axis=1, keepdims=True)
        el = jnp.where((lane < N_EXPERTS) & ((lane // EXPERTS_PER_GROUP) == (gidx - N_EXPERTS)),
                       logits, neg)
        em1, idx1 = first_max(el)
        em2, idx2 = first_max(jnp.where(lane == idx1, neg, el))
        e2 = jnp.exp(em2 - em1)
        w1 = g_p / (1.0 + e2)
        w2 = g_p * e2 / (1.0 + e2)
        is1 = lane == idx1
        is2 = lane == idx2
        gates = jnp.where(is1, w1, 0.0) + jnp.where(is2, w2, 0.0)
        assigned = jnp.where(is1 | is2, 1.0, 0.0)
        cum = jnp.dot(tri, assigned.astype(BF16), preferred_element_type=F32)
        rank = jnp.where(assigned > 0.0, carry + cum - 1.0, -1.0)
        rank_t_ref[:, rows] = rank.T
        gates_t_ref[:, rows] = gates.T
        return carry + cum[part - 1:part, :]

    @pl.when(i % sub_blocks == 0)
    def _():
        carry_ref[...] = jnp.zeros_like(carry_ref)

    rows = slice(0, part)
    carry = route(rows, project(rows), carry_ref[...])
    carry_ref[...] = carry
    cnt_ref[pl.ds(i // sub_blocks, 1), :] = carry.astype(jnp.int32)


def _outproj(x, o_f, o_b, hg, o_da, hgain, w_out, g2, wr, br, tb, tm=512):
    S = x.shape[0]
    row = lambda w: pl.BlockSpec((tm, w), lambda i: (i, 0))
    col = pl.BlockSpec((ROUTER_LANES, tm), lambda i: (0, i))
    full = lambda a: pl.BlockSpec(a.shape, lambda i: (0, 0))
    return pl.pallas_call(
        functools.partial(_outproj_kernel, sub_blocks=tb // tm),
        grid=(S // tm,),
        in_specs=[row(D_MODEL), row(HALF_W), row(HALF_W), row(HALF_W), row(HALF_W),
                  full(hgain), full(w_out), full(g2), full(wr), full(br)],
        out_specs=[row(D_MODEL), col, col,
                   pl.BlockSpec((S // tb, ROUTER_LANES), lambda i: (0, 0))],
        out_shape=[jax.ShapeDtypeStruct((S, D_MODEL), F32),
                   jax.ShapeDtypeStruct((ROUTER_LANES, S), F32),
                   jax.ShapeDtypeStruct((ROUTER_LANES, S), F32),
                   jax.ShapeDtypeStruct((S // tb, ROUTER_LANES), jnp.int32)],
        scratch_shapes=[pltpu.VMEM((1, ROUTER_LANES), F32)],
        compiler_params=pltpu.CompilerParams(dimension_semantics=("arbitrary",),
                                             vmem_limit_bytes=V7X_VMEM_LIMIT),
        name="outproj_router",
    )(x, o_f, o_b, hg, o_da, hgain, w_out, g2, wr, br)


def _moe_kernel(cnt_ref, x2_ref, rank_t_ref, gates_t_ref, g2_ref, wg_ref, wu_ref,
                wd_ref, gf_ref, out_ref, h2_ref, *, tile_rows, tb):
    b = pl.program_id(0)
    p = pl.program_id(1)
    R = tile_rows
    n_exp = wg_ref.shape[0]
    n_sub = x2_ref.shape[0] // tb

    @pl.when(p == 0)
    def _():
        x2 = x2_ref[...]
        h2_ref[...] = (_rms(x2, NORM_EPS) * g2_ref[...]).astype(BF16)
        out_ref[...] = x2

    experts = [n_exp * p + i for i in range(n_exp)]
    subs = [slice(s * tb, (s + 1) * tb) for s in range(n_sub)]
    rank_row = [[rank_t_ref[pl.ds(e, 1), sub] for e in experts] for sub in subs]
    gate_row = [[gates_t_ref[pl.ds(e, 1), sub] for e in experts] for sub in subs]
    n_max = jnp.int32(0)
    for s in range(n_sub):
        for e in experts:
            n_max = jnp.maximum(n_max, cnt_ref[n_sub * b + s, e])

    def tile(t, carry):
        base = (t * R).astype(F32)
        slot_col = lax.broadcasted_iota(jnp.int32, (R, 1), 0).astype(F32) + base
        sel, gather, xg = [], [], []
        for s, sub in enumerate(subs):
            sel.append([rank_row[s][i] == slot_col for i in range(n_exp)])
            gather.append(jnp.concatenate(
                [jnp.where(m, 1.0, 0.0).astype(BF16) for m in sel[s]], axis=0))
            xg.append(jnp.dot(gather[s], h2_ref[sub, :],
                              preferred_element_type=F32).astype(BF16))
        ys = [[] for _ in subs]
        for i in range(n_exp):
            xi = jnp.concatenate([xg[s][i * R:(i + 1) * R] for s in range(n_sub)], axis=0)
            a = jnp.dot(xi, wg_ref[i], preferred_element_type=F32)
            a = a * jax.nn.sigmoid(a) * jnp.dot(xi, wu_ref[i], preferred_element_type=F32)
            y = jnp.dot(a.astype(BF16), wd_ref[i], preferred_element_type=F32)
            for s in range(n_sub):
                g = jnp.sum(jnp.where(sel[s][i], gate_row[s][i], 0.0), axis=1, keepdims=True)
                ys[s].append((y[s * R:(s + 1) * R] * g).astype(BF16))
        for s, sub in enumerate(subs):
            out_ref[sub, :] += lax.dot_general(gather[s], jnp.concatenate(ys[s], axis=0),
                                               TN_DIMS, preferred_element_type=F32)
        return carry

    lax.fori_loop(0, (n_max + R - 1) // R, tile, 0)

    @pl.when(p == pl.num_programs(1) - 1)
    def _():
        out_ref[...] = _rms(out_ref[...], NORM_EPS) * gf_ref[...]


def _moe(cnt, x2, rank_t, gates_t, g2, wg, wu, wd, gf, tb, tile_rows=48,
         experts_per_step=4, blocks_per_step=4):
    S = x2.shape[0]
    eps = experts_per_step
    rows = tb * blocks_per_step
    grid_spec = pltpu.PrefetchScalarGridSpec(
        num_scalar_prefetch=1,
        grid=(S // rows, N_EXPERTS // eps),
        in_specs=[pl.BlockSpec((rows, D_MODEL), lambda b, p, c: (b, 0),
                               pipeline_mode=pl.Buffered(1)),
                  pl.BlockSpec((N_EXPERTS, rows), lambda b, p, c: (0, b)),
                  pl.BlockSpec((N_EXPERTS, rows), lambda b, p, c: (0, b)),
                  pl.BlockSpec((1, D_MODEL), lambda b, p, c: (0, 0)),
                  pl.BlockSpec((eps, D_MODEL, D_EXPERT), lambda b, p, c: (p, 0, 0)),
                  pl.BlockSpec((eps, D_MODEL, D_EXPERT), lambda b, p, c: (p, 0, 0)),
                  pl.BlockSpec((eps, D_EXPERT, D_MODEL), lambda b, p, c: (p, 0, 0)),
                  pl.BlockSpec((1, D_MODEL), lambda b, p, c: (0, 0))],
        out_specs=pl.BlockSpec((rows, D_MODEL), lambda b, p, c: (b, 0)),
        scratch_shapes=[pltpu.VMEM((rows, D_MODEL), BF16)])
    return pl.pallas_call(
        functools.partial(_moe_kernel, tile_rows=tile_rows, tb=tb),
        grid_spec=grid_spec,
        out_shape=jax.ShapeDtypeStruct((S, D_MODEL), F32),
        compiler_params=pltpu.CompilerParams(dimension_semantics=("arbitrary", "arbitrary"),
                                             vmem_limit_bytes=V7X_VMEM_LIMIT),
        name="moe",
    )(cnt, x2, rank_t, gates_t, g2, wg, wu, wd, gf)


def kernel(x, positions, norm1_gain, w_in, hg_lower_bounds, hg_norm_gain, diff_lambda,
           diff_subln_gain, w_out, norm2_gain, router_group_w, router_group_b,
           router_expert_w, router_expert_b, moe_w_gate, moe_w_up, moe_w_down,
           final_norm_gain):
    B, S, _ = x.shape
    assert B == 1 and norm1_gain.shape[0] == 1
    layer = 0
    xs = x.reshape(S, D_MODEL)
    pos = positions.reshape(1, S).astype(F32)

    lb_cum = jnp.cumsum(jax.nn.softmax(hg_lower_bounds.astype(F32), axis=1), axis=1)
    lb = lb_cum[:, layer + 1] - lb_cum[:, 0]
    lam_p = diff_lambda[layer].astype(F32)
    lam = (jnp.exp(jnp.sum(lam_p[0] * lam_p[1])) - jnp.exp(jnp.sum(lam_p[2] * lam_p[3]))
           + LAMBDA_INIT).reshape(1)
    lane = jnp.arange(HEAD_W)
    inv_freq = jnp.float32(ROPE_THETA) ** (-jnp.arange(0, ROT_DIM, 2, dtype=F32) / ROT_DIM)
    in_rot = (lane % QK_DIM) < ROT_DIM
    n_freq = ROT_DIM // 2
    invf = inv_freq.reshape(n_freq, 1)
    sgn = jnp.where(in_rot, jnp.where((lane % QK_DIM) < n_freq, -1.0, 1.0),
                    0.0).astype(F32).reshape(1, HEAD_W)
    place = (in_rot[None, :] & (lane[None, :] % n_freq == jnp.arange(n_freq)[:, None])).astype(F32)
    pcos = jnp.tile(place, (4, 1)).astype(BF16)
    psin = jnp.tile(place * sgn, (4, 1)).astype(BF16)
    wr = jnp.zeros((D_MODEL, ROUTER_LANES), F32)
    wr = wr.at[:, :N_EXPERTS].set(router_expert_w[layer])
    wr = wr.at[:, N_EXPERTS:N_EXPERTS + N_GROUPS].set(router_group_w[layer])
    wr_hi = wr.astype(BF16)
    wr = jnp.concatenate([wr_hi, (wr - wr_hi.astype(F32)).astype(BF16)], axis=1)
    br = jnp.zeros((1, ROUTER_LANES), F32)
    br = br.at[0, :N_EXPERTS].set(router_expert_b[layer])
    br = br.at[0, N_EXPERTS:N_EXPERTS + N_GROUPS].set(router_group_b[layer])

    hq, kf, kb, lff, lfb, hi, hg, dq, dk, dv = _inproj(
        xs, pos, norm1_gain[layer].reshape(1, -1), w_in[layer].astype(BF16), lb, invf, sgn,
        pcos, psin)
    o_f, wg, wu = _hgrn(hq, kf, lff, hi, False, (moe_w_gate[layer], moe_w_up[layer]))
    o_b, wd = _hgrn(hq, kb, lfb, hi, True, (moe_w_down[layer],))
    o_da = _attn(lam, dq, dk, dv, diff_subln_gain[layer].reshape(1, -1))
    x2, rank_t, gates_t, cnt = _outproj(
        xs, o_f, o_b, hg, o_da, hg_norm_gain[layer].reshape(1, -1), w_out[layer].astype(BF16),
        norm2_gain[layer].reshape(1, -1), wr, br, tb=MOE_TOKEN_BLOCK)
    out = _moe(cnt, x2, rank_t, gates_t, norm2_gain[layer].reshape(1, -1), wg, wu, wd,
               final_norm_gain.reshape(1, -1), tb=MOE_TOKEN_BLOCK)
    return out.reshape(B, S, D_MODEL)
```

```python
import functools
import math

import jax
import jax.numpy as jnp
from jax import lax
from jax.experimental import pallas as pl
from jax.experimental.pallas import tpu as pltpu

F32 = jnp.float32
BF16 = jnp.bfloat16

D_MODEL = 1024
HEAD_W = 128
N_HEADS = 4
HALF_W = N_HEADS * HEAD_W
QK_DIM = 64
ROT_DIM = 16
ROPE_THETA = 500000.0
QUERY_GROUP = 512
KV_UNROLL = 4
VT_ROWS = HEAD_W + 16
HG_CHUNK = 128
HG_UNROLL = 8
N_GROUPS = 4
EXPERTS_PER_GROUP = 8
N_EXPERTS = 32
D_EXPERT = 512
NORM_EPS = 1e-6
SUBLN_EPS = 1e-5
LAMBDA_INIT = 0.8 - 0.6 * math.exp(-0.3 * 0)
LOG2E = 1.4426950408889634
ROUTER_LANES = 128
MOE_TOKEN_BLOCK = 512
V7X_VMEM_LIMIT = 58 * 1024 * 1024

NT_DIMS = (((1,), (1,)), ((), ()))
TN_DIMS = (((0,), (0,)), ((), ()))


def _rms(x, eps):
    return x * lax.rsqrt(jnp.mean(x * x, axis=-1, keepdims=True) + eps)


def _inproj_kernel(x_ref, pos_ref, g1_ref, w_ref, lb_ref, invf_ref, sgn_ref, pcos_ref, psin_ref,
                   hq_ref, kf_ref, kb_ref, lff_ref, lfb_ref, hi_ref, hg_ref,
                   dq_ref, dk_ref, dv_ref):
    x = x_ref[...]
    h = (_rms(x, NORM_EPS) * g1_ref[...]).astype(BF16)

    def proj(i):
        return jnp.dot(h, w_ref[:, i * HALF_W:(i + 1) * HALF_W], preferred_element_type=F32)

    hq_ref[...] = proj(0)
    for i, (k_ref, lf_ref) in enumerate(((kf_ref, lff_ref), (kb_ref, lfb_ref))):
        z = proj(1 + i)
        lb = lb_ref[i:i + 1, :]
        sg = jax.nn.sigmoid(z)
        lf_ref[...] = jnp.log(lb + (1.0 - lb) * sg)
        k_ref[...] = (1.0 - lb) * (1.0 - sg)
    hi_ref[...] = proj(3)
    hg_ref[...] = proj(4)

    ang_t = invf_ref[...] * pos_ref[...]

    def spread(t, place_ref):
        hi = t.astype(BF16)
        r1 = t - hi.astype(F32)
        mid = r1.astype(BF16)
        lo = (r1 - mid.astype(F32)).astype(BF16)
        pieces = jnp.concatenate([hi, mid, lo, jnp.zeros_like(hi)], axis=0)
        return lax.dot_general(pieces, place_ref[...], TN_DIMS, preferred_element_type=F32)

    sgn = sgn_ref[...]
    cos = jnp.concatenate([spread(jnp.cos(ang_t), pcos_ref) + jnp.where(sgn == 0.0, 1.0, 0.0)]
                          * N_HEADS, axis=1)
    sin = jnp.concatenate([spread(jnp.sin(ang_t), psin_ref)] * N_HEADS, axis=1)
    take_hi = jnp.concatenate([jnp.broadcast_to(sgn, (x.shape[0], HEAD_W))] * N_HEADS,
                              axis=1) < 0.0

    def rotary(t):
        partner = jnp.where(take_hi, pltpu.roll(t, HALF_W - ROT_DIM // 2, axis=1),
                            pltpu.roll(t, ROT_DIM // 2, axis=1))
        return t * cos + partner * sin

    dq_ref[...] = (rotary(proj(5)) * (QK_DIM ** -0.5 * LOG2E)).astype(BF16)
    dk_ref[...] = rotary(proj(6)).astype(BF16)
    v = proj(7)
    sub = lax.broadcasted_iota(jnp.int32, (VT_ROWS - HEAD_W, v.shape[0]), 0)
    ones_rows = jnp.where(sub == 0, 1.0, 0.0)
    pieces = []
    for hd in range(N_HEADS):
        pieces += [v[:, hd * HEAD_W:(hd + 1) * HEAD_W].T, ones_rows]
    dv_ref[...] = jnp.concatenate(pieces, axis=0).astype(BF16)


def _inproj(x, pos, g1, w_in, lb, invf, sgn, pcos, psin, tm=512):
    S = x.shape[0]
    row = lambda w: pl.BlockSpec((tm, w), lambda i: (i, 0))
    full = lambda a: pl.BlockSpec(a.shape, lambda i: (0, 0))
    f32o = jax.ShapeDtypeStruct((S, HALF_W), F32)
    bf16o = jax.ShapeDtypeStruct((S, HALF_W), BF16)
    return pl.pallas_call(
        _inproj_kernel,
        grid=(S // tm,),
        in_specs=[row(D_MODEL), pl.BlockSpec((1, tm), lambda i: (0, i)), full(g1), full(w_in),
                  full(lb), full(invf), full(sgn), full(pcos), full(psin)],
        out_specs=[row(HALF_W)] * 9 + [pl.BlockSpec((N_HEADS * VT_ROWS, tm), lambda i: (0, i))],
        out_shape=[f32o] * 7 + [bf16o, bf16o,
                                jax.ShapeDtypeStruct((N_HEADS * VT_ROWS, S), BF16)],
        compiler_params=pltpu.CompilerParams(dimension_semantics=("arbitrary",),
                                             vmem_limit_bytes=V7X_VMEM_LIMIT),
        name="inproj",
    )(x, pos, g1, w_in, lb, invf, sgn, pcos, psin)


def _chunk_cumsum(x, rev):
    n = x.shape[0]
    row = lax.broadcasted_iota(jnp.int32, x.shape, 0)
    d = 1
    while d < n:
        if rev:
            x = x + jnp.where(row < n - d, pltpu.roll(x, n - d, axis=0), 0.0)
        else:
            x = x + jnp.where(row >= d, pltpu.roll(x, d, axis=0), 0.0)
        d *= 2
    return x


def _pivot_rows(b, blk, rev):
    n = b.shape[0]
    half = blk // 2
    groups = []
    for g0 in range(0, n, 8):
        def piv(r):
            base = (r // blk) * blk
            return base + half if rev else base + half - 1
        if blk >= 8:
            p = piv(g0)
            groups.append(jnp.broadcast_to(b[p:p + 1, :], (8, b.shape[1])))
        else:
            sub = lax.broadcasted_iota(jnp.int32, (8, b.shape[1]), 0)
            acc = None
            for s0 in range(0, 8, blk):
                p = piv(g0 + s0)
                rowv = jnp.broadcast_to(b[p:p + 1, :], (8, b.shape[1]))
                acc = rowv if acc is None else jnp.where(sub >= s0, rowv, acc)
            groups.append(acc)
    return jnp.concatenate(groups, axis=0)


def _hgrn_kernel(q_ref, k_ref, lf_ref, v_ref, *rest, rev, n_chunks, n_cast):
    w_refs, o_ref, wb_refs, st_ref = (rest[:n_cast], rest[n_cast], rest[n_cast + 1:-1], rest[-1])
    for w_ref, wb_ref in zip(w_refs, wb_refs):
        wb_ref[...] = w_ref[...].astype(BF16)

    @pl.when(pl.program_id(1) == 0)
    def _():
        st_ref[...] = jnp.zeros_like(st_ref)

    C, W = HG_CHUNK, HEAD_W
    rt = lax.broadcasted_iota(jnp.int32, (C, C), 0)
    ct = lax.broadcasted_iota(jnp.int32, (C, C), 1)
    rowi = lax.broadcasted_iota(jnp.int32, (C, 2 * W), 0)
    eye = jnp.where(rt == ct, 1.0, 0.0)
    levels = []
    blk = C
    while blk >= 2:
        half = blk // 2
        q_side = (lambda r: (r % blk) < half) if rev else (lambda r: (r % blk) >= half)
        pair = ((rt // blk) == (ct // blk)) & q_side(rt) & jnp.logical_not(q_side(ct))
        levels.append((blk, jnp.where(q_side(rowi), LOG2E, -LOG2E), jnp.where(pair, 1.0, 0.0)))
        blk = half

    def stack(a):
        return jnp.concatenate([a[:, :W], a[:, W:]], axis=0)

    def chunk(ci, st):
        c = (n_chunks - 1 - ci) if rev else ci
        sl = pl.ds(pl.multiple_of(c * C, C), C)
        q = q_ref[sl, :]
        k = k_ref[sl, :]
        vb = v_ref[sl, :].astype(BF16)
        b = _chunk_cumsum(lf_ref[sl, :], rev)
        tot = b[0:1, :] if rev else b[C - 1:C, :]

        o = lax.dot_general(stack((q * jnp.exp(b)).astype(BF16)), st.astype(BF16), NT_DIMS,
                            preferred_element_type=F32)
        kdec = (k * jnp.exp(tot - b)).astype(BF16)
        upd = lax.dot_general(vb, kdec, TN_DIMS, preferred_element_type=F32)
        dec = jnp.exp(tot)
        st_next = jnp.concatenate([st[:W] * dec[:, :W] + upd[:W, :W],
                                   st[W:] * dec[:, W:] + upd[W:, W:]], axis=0)

        qk = q * k
        scores = [eye * jnp.sum(qk[:, :W], axis=1, keepdims=True),
                  eye * jnp.sum(qk[:, W:], axis=1, keepdims=True)]
        for blk, side, pair in levels:
            w = jnp.exp2((b - _pivot_rows(b, blk, rev)) * side)
            z = stack((jnp.where(side > 0.0, q, k) * w).astype(BF16))
            g = lax.dot_general(z, z, NT_DIMS, preferred_element_type=F32)
            scores = [scores[0] + pair * g[:C, :C], scores[1] + pair * g[C:, C:]]
        intra = jnp.dot(jnp.concatenate(scores, axis=0).astype(BF16), vb,
                        preferred_element_type=F32)
        o_ref[sl, :] = jnp.concatenate([o[:C, :W] + intra[:C, :W], o[C:, W:] + intra[C:, W:]],
                                       axis=1)
        return st_next

    st_ref[...] = lax.fori_loop(0, n_chunks, chunk, st_ref[...], unroll=HG_UNROLL)


def _hgrn(q, k, lf, v, rev, weights, rb=2048):
    S = q.shape[0]
    nb = S // rb
    n_pairs = N_HEADS // 2
    steps = n_pairs * nb
    idx = (lambda h, j: (nb - 1 - j, h)) if rev else (lambda h, j: (j, h))
    spec = pl.BlockSpec((rb, 2 * HEAD_W), idx)
    w_specs = []
    for w in weights:
        n_exp, rows, cols = w.shape
        per_step = n_exp // steps
        assert per_step * steps == n_exp
        w_specs.append(pl.BlockSpec((per_step, rows, cols), lambda h, j: (h * nb + j, 0, 0)))
    return pl.pallas_call(
        functools.partial(_hgrn_kernel, rev=rev, n_chunks=rb // HG_CHUNK, n_cast=len(weights)),
        grid=(n_pairs, nb),
        in_specs=[spec] * 4 + w_specs,
        out_specs=[spec] + w_specs,
        out_shape=[jax.ShapeDtypeStruct((S, HALF_W), F32)]
        + [jax.ShapeDtypeStruct(w.shape, BF16) for w in weights],
        scratch_shapes=[pltpu.VMEM((2 * HEAD_W, HEAD_W), F32)],
        compiler_params=pltpu.CompilerParams(dimension_semantics=("arbitrary", "arbitrary"),
                                             vmem_limit_bytes=V7X_VMEM_LIMIT),
        name="hgrn_bwd" if rev else "hgrn_fwd",
    )(q, k, lf, v, *weights)


def _attn_kernel(lam_ref, q_ref, k_ref, vt_ref, gain_ref, o_ref, sa_ref, sb_ref, m_ref, acc_ref,
                 *, tk):
    n_kv = k_ref.shape[0] // tk
    tq = q_ref.shape[0]
    assert n_kv % KV_UNROLL == 0 and KV_UNROLL % 2 == 0 and tq % QUERY_GROUP == 0
    q = q_ref[...]
    lane = lax.broadcasted_iota(jnp.int32, q.shape, 1)
    qm = (jnp.where(lane < QK_DIM, q, jnp.zeros_like(q)),
          jnp.where(lane >= QK_DIM, q, jnp.zeros_like(q)))
    m_ref[...] = jnp.full(m_ref.shape, -jnp.inf, F32)
    acc_ref[...] = jnp.zeros_like(acc_ref)

    def chunk(j):
        return pl.ds(pl.multiple_of(j * tk, tk), tk)

    def softmax_pv(c, cols, s, vt):
        m_old = m_ref[c, :, cols]
        m_new = jnp.maximum(m_old, jnp.max(s, axis=0, keepdims=True))
        p = jnp.exp2(s - m_new).astype(BF16)
        acc_ref[c, :, cols] = jnp.exp2(m_old - m_new) * acc_ref[c, :, cols] + jnp.dot(
            vt, p, preferred_element_type=F32)
        m_ref[c, :, cols] = m_new

    def step(js, dst_ref, jp, src_ref):
        kc = None if dst_ref is None else k_ref[chunk(js), :]
        vt = None if src_ref is None else vt_ref[:, chunk(jp)]
        for c in range(2):
            for g in range(0, tq, QUERY_GROUP):
                cols = slice(g, g + QUERY_GROUP)
                if dst_ref is not None:
                    dst_ref[c, :, cols] = lax.dot_general(kc, qm[c][cols, :], NT_DIMS,
                                                          preferred_element_type=F32)
                if src_ref is not None:
                    softmax_pv(c, cols, src_ref[c, :, cols], vt)

    bufs = (sa_ref, sb_ref)
    step(0, sa_ref, None, None)

    def body(i, carry):
        for u in range(KV_UNROLL):
            j = KV_UNROLL * i + u
            step(j + 1, bufs[(u + 1) % 2], j, bufs[u % 2])
        return carry

    lax.fori_loop(0, n_kv // KV_UNROLL - 1, body, 0)
    for j in range(n_kv - KV_UNROLL, n_kv - 1):
        step(j + 1, bufs[(j + 1) % 2], j, bufs[j % 2])
    step(None, None, n_kv - 1, bufs[(n_kv - 1) % 2])

    outs = []
    for c in range(2):
        a = acc_ref[c]
        outs.append(a[:HEAD_W, :] / a[HEAD_W:HEAD_W + 1, :])
    o = (outs[0] - lam_ref[0] * outs[1]).T
    o_ref[...] = _rms(o, SUBLN_EPS) * gain_ref[...] * (1.0 - LAMBDA_INIT)


def _attn(lam, dq, dk, dvt, gain, tq=2048, tk=512):
    S = dq.shape[0]
    return pl.pallas_call(
        functools.partial(_attn_kernel, tk=tk),
        grid=(N_HEADS, S // tq),
        in_specs=[pl.BlockSpec(memory_space=pltpu.SMEM),
                  pl.BlockSpec((tq, HEAD_W), lambda h, i: (i, h)),
                  pl.BlockSpec((S, HEAD_W), lambda h, i: (0, h)),
                  pl.BlockSpec((VT_ROWS, S), lambda h, i: (h, 0)),
                  pl.BlockSpec((1, HEAD_W), lambda h, i: (0, 0))],
        out_specs=pl.BlockSpec((tq, HEAD_W), lambda h, i: (i, h)),
        out_shape=jax.ShapeDtypeStruct((S, HALF_W), F32),
        scratch_shapes=[pltpu.VMEM((2, tk, tq), F32), pltpu.VMEM((2, tk, tq), F32),
                        pltpu.VMEM((2, 1, tq), F32), pltpu.VMEM((2, VT_ROWS, tq), F32)],
        compiler_params=pltpu.CompilerParams(dimension_semantics=("arbitrary", "arbitrary"),
                                             vmem_limit_bytes=V7X_VMEM_LIMIT),
        name="diff_attn",
    )(lam, dq, dk, dvt, gain)


def _outproj_kernel(x_ref, of_ref, ob_ref, hg_ref, oda_ref, hgain_ref, wo_ref, g2_ref,
                    wr_ref, br_ref, x2_ref, rank_t_ref, gates_t_ref, cnt_ref,
                    carry_ref, *, sub_blocks):
    i = pl.program_id(0)
    part = x_ref.shape[0]

    def project(rows):
        o = of_ref[rows, :] + ob_ref[rows, :]
        gate = hg_ref[rows, :]
        gate = gate * jax.nn.sigmoid(gate)
        parts = []
        for hd in range(N_HEADS):
            sl = slice(hd * HEAD_W, (hd + 1) * HEAD_W)
            parts.append(_rms(o[:, sl], NORM_EPS) * hgain_ref[...] * gate[:, sl])
        parts.append(oda_ref[rows, :])
        mixed = jnp.concatenate(parts, axis=1).astype(BF16)
        x2 = x_ref[rows, :] + jnp.dot(mixed, wo_ref[...], preferred_element_type=F32)
        x2_ref[rows, :] = x2
        return x2

    lane = lax.broadcasted_iota(jnp.int32, (part, ROUTER_LANES), 1)
    neg = -jnp.inf
    tri = jnp.where(lax.broadcasted_iota(jnp.int32, (part, part), 0)
                    >= lax.broadcasted_iota(jnp.int32, (part, part), 1), 1.0, 0.0).astype(BF16)

    def first_max(vals):
        mx = jnp.max(vals, axis=1, keepdims=True)
        idx = jnp.min(jnp.where(vals == mx, lane, ROUTER_LANES), axis=1, keepdims=True)
        return mx, idx

    def route(rows, x2, carry):
        h2 = _rms(x2, NORM_EPS) * g2_ref[...]
        h_hi = h2.astype(BF16)
        h_lo = (h2 - h_hi.astype(F32)).astype(BF16)
        t = jnp.dot(h_hi, wr_ref[...], preferred_element_type=F32)
        logits = (t[:, :ROUTER_LANES] + t[:, ROUTER_LANES:] + br_ref[...]
                  + jnp.dot(h_lo, wr_ref[:, :ROUTER_LANES], preferred_element_type=F32))
        gl = jnp.where((lane >= N_EXPERTS) & (lane < N_EXPERTS + N_GROUPS), logits, neg)
        gm, gidx = first_max(gl)
        g_p = 1.0 / jnp.sum(jnp.exp(gl - gm), axis=1, keepdims=True)
        el = jnp.where((lane < N_EXPERTS) & ((lane // EXPERTS_PER_GROUP) == (gidx - N_EXPERTS)),
                       logits, neg)
        em1, idx1 = first_max(el)
        em2, idx2 = first_max(jnp.where(lane == idx1, neg, el))
        e2 = jnp.exp(em2 - em1)
        w1 = g_p / (1.0 + e2)
        w2 = g_p * e2 / (1.0 + e2)
        is1 = lane == idx1
        is2 = lane == idx2
        gates = jnp.where(is1, w1, 0.0) + jnp.where(is2, w2, 0.0)
        assigned = jnp.where(is1 | is2, 1.0, 0.0)
        cum = jnp.dot(tri, assigned.astype(BF16), preferred_element_type=F32)
        rank = jnp.where(assigned > 0.0, carry + cum - 1.0, -1.0)
        rank_t_ref[:, rows] = rank.T
        gates_t_ref[:, rows] = gates.T
        return carry + cum[part - 1:part, :]

    @pl.when(i % sub_blocks == 0)
    def _():
        carry_ref[...] = jnp.zeros_like(carry_ref)

    rows = slice(0, part)
    carry = route(rows, project(rows), carry_ref[...])
    carry_ref[...] = carry
    cnt_ref[pl.ds(i // sub_blocks, 1), :] = carry.astype(jnp.int32)


def _outproj(x, o_f, o_b, hg, o_da, hgain, w_out, g2, wr, br, tb, tm=512):
    S = x.shape[0]
    row = lambda w: pl.BlockSpec((tm, w), lambda i: (i, 0))
    col = pl.BlockSpec((ROUTER_LANES, tm), lambda i: (0, i))
    full = lambda a: pl.BlockSpec(a.shape, lambda i: (0, 0))
    return pl.pallas_call(
        functools.partial(_outproj_kernel, sub_blocks=tb // tm),
        grid=(S // tm,),
        in_specs=[row(D_MODEL), row(HALF_W), row(HALF_W), row(HALF_W), row(HALF_W),
                  full(hgain), full(w_out), full(g2), full(wr), full(br)],
        out_specs=[row(D_MODEL), col, col,
                   pl.BlockSpec((S // tb, ROUTER_LANES), lambda i: (0, 0))],
        out_shape=[jax.ShapeDtypeStruct((S, D_MODEL), F32),
                   jax.ShapeDtypeStruct((ROUTER_LANES, S), F32),
                   jax.ShapeDtypeStruct((ROUTER_LANES, S), F32),
                   jax.ShapeDtypeStruct((S // tb, ROUTER_LANES), jnp.int32)],
        scratch_shapes=[pltpu.VMEM((1, ROUTER_LANES), F32)],
        compiler_params=pltpu.CompilerParams(dimension_semantics=("arbitrary",),
                                             vmem_limit_bytes=V7X_VMEM_LIMIT),
        name="outproj_router",
    )(x, o_f, o_b, hg, o_da, hgain, w_out, g2, wr, br)


def _moe_kernel(cnt_ref, x2_ref, rank_t_ref, gates_t_ref, g2_ref, wg_ref, wu_ref,
                wd_ref, gf_ref, out_ref, h2_ref, *, tile_rows, tb):
    b = pl.program_id(0)
    p = pl.program_id(1)
    R = tile_rows
    n_exp = wg_ref.shape[0]
    n_sub = x2_ref.shape[0] // tb

    @pl.when(p == 0)
    def _():
        x2 = x2_ref[...]
        h2_ref[...] = (_rms(x2, NORM_EPS) * g2_ref[...]).astype(BF16)
        out_ref[...] = x2

    experts = [n_exp * p + i for i in range(n_exp)]
    subs = [slice(s * tb, (s + 1) * tb) for s in range(n_sub)]
    rank_row = [[rank_t_ref[pl.ds(e, 1), sub] for e in experts] for sub in subs]
    gate_row = [[gates_t_ref[pl.ds(e, 1), sub] for e in experts] for sub in subs]
    n_max = jnp.int32(0)
    for s in range(n_sub):
        for e in experts:
            n_max = jnp.maximum(n_max, cnt_ref[n_sub * b + s, e])

    def tile(t, carry):
        base = (t * R).astype(F32)
        slot_col = lax.broadcasted_iota(jnp.int32, (R, 1), 0).astype(F32) + base
        sel, gather, xg = [], [], []
        for s, sub in enumerate(subs):
            sel.append([rank_row[s][i] == slot_col for i in range(n_exp)])
            gather.append(jnp.concatenate(
                [jnp.where(m, 1.0, 0.0).astype(BF16) for m in sel[s]], axis=0))
            xg.append(jnp.dot(gather[s], h2_ref[sub, :],
                              preferred_element_type=F32).astype(BF16))
        ys = [[] for _ in subs]
        for i in range(n_exp):
            xi = jnp.concatenate([xg[s][i * R:(i + 1) * R] for s in range(n_sub)], axis=0)
            a = jnp.dot(xi, wg_ref[i], preferred_element_type=F32)
            a = a * jax.nn.sigmoid(a) * jnp.dot(xi, wu_ref[i], preferred_element_type=F32)
            y = jnp.dot(a.astype(BF16), wd_ref[i], preferred_element_type=F32)
            for s in range(n_sub):
                g = jnp.sum(jnp.where(sel[s][i], gate_row[s][i], 0.0), axis=1, keepdims=True)
                ys[s].append((y[s * R:(s + 1) * R] * g).astype(BF16))
        for s, sub in enumerate(subs):
            out_ref[sub, :] += lax.dot_general(gather[s], jnp.concatenate(ys[s], axis=0),
                                               TN_DIMS, preferred_element_type=F32)
        return carry

    lax.fori_loop(0, (n_max + R - 1) // R, tile, 0)

    @pl.when(p == pl.num_programs(1) - 1)
    def _():
        out_ref[...] = _rms(out_ref[...], NORM_EPS) * gf_ref[...]


def _moe(cnt, x2, rank_t, gates_t, g2, wg, wu, wd, gf, tb, tile_rows=64,
         experts_per_step=4, blocks_per_step=4):
    S = x2.shape[0]
    eps = experts_per_step
    rows = tb * blocks_per_step
    grid_spec = pltpu.PrefetchScalarGridSpec(
        num_scalar_prefetch=1,
        grid=(S // rows, N_EXPERTS // eps),
        in_specs=[pl.BlockSpec((rows, D_MODEL), lambda b, p, c: (b, 0),
                               pipeline_mode=pl.Buffered(1)),
                  pl.BlockSpec((N_EXPERTS, rows), lambda b, p, c: (0, b)),
                  pl.BlockSpec((N_EXPERTS, rows), lambda b, p, c: (0, b)),
                  pl.BlockSpec((1, D_MODEL), lambda b, p, c: (0, 0)),
                  pl.BlockSpec((eps, D_MODEL, D_EXPERT), lambda b, p, c: (p, 0, 0)),
                  pl.BlockSpec((eps, D_MODEL, D_EXPERT), lambda b, p, c: (p, 0, 0)),
                  pl.BlockSpec((eps, D_EXPERT, D_MODEL), lambda b, p, c: (p, 0, 0)),
                  pl.BlockSpec((1, D_MODEL), lambda b, p, c: (0, 0))],
        out_specs=pl.BlockSpec((rows, D_MODEL), lambda b, p, c: (b, 0)),
        scratch_shapes=[pltpu.VMEM((rows, D_MODEL), BF16)])
    return pl.pallas_call(
        functools.partial(_moe_kernel, tile_rows=tile_rows, tb=tb),
        grid_spec=grid_spec,
        out_shape=jax.ShapeDtypeStruct((S, D_MODEL), F32),
        compiler_params=pltpu.CompilerParams(dimension_semantics=("arbitrary", "arbitrary"),
                                             vmem_limit_bytes=V7X_VMEM_LIMIT),
        name="moe",
    )(cnt, x2, rank_t, gates_t, g2, wg, wu, wd, gf)


def kernel(x, positions, norm1_gain, w_in, hg_lower_bounds, hg_norm_gain, diff_lambda,
           diff_subln_gain, w_out, norm2_gain, router_group_w, router_group_b,
           router_expert_w, router_expert_b, moe_w_gate, moe_w_up, moe_w_down,
           final_norm_gain):
    B, S, _ = x.shape
    assert B == 1 and norm1_gain.shape[0] == 1
    layer = 0
    xs = x.reshape(S, D_MODEL)
    pos = positions.reshape(1, S).astype(F32)

    lb_cum = jnp.cumsum(jax.nn.softmax(hg_lower_bounds.astype(F32), axis=1), axis=1)
    lb = lb_cum[:, layer + 1] - lb_cum[:, 0]
    lam_p = diff_lambda[layer].astype(F32)
    lam = (jnp.exp(jnp.sum(lam_p[0] * lam_p[1])) - jnp.exp(jnp.sum(lam_p[2] * lam_p[3]))
           + LAMBDA_INIT).reshape(1)
    lane = jnp.arange(HEAD_W)
    inv_freq = jnp.float32(ROPE_THETA) ** (-jnp.arange(0, ROT_DIM, 2, dtype=F32) / ROT_DIM)
    in_rot = (lane % QK_DIM) < ROT_DIM
    n_freq = ROT_DIM // 2
    invf = inv_freq.reshape(n_freq, 1)
    sgn = jnp.where(in_rot, jnp.where((lane % QK_DIM) < n_freq, -1.0, 1.0),
                    0.0).astype(F32).reshape(1, HEAD_W)
    place = (in_rot[None, :] & (lane[None, :] % n_freq == jnp.arange(n_freq)[:, None])).astype(F32)
    pcos = jnp.tile(place, (4, 1)).astype(BF16)
    psin = jnp.tile(place * sgn, (4, 1)).astype(BF16)
    wr = jnp.zeros((D_MODEL, ROUTER_LANES), F32)
    wr = wr.at[:, :N_EXPERTS].set(router_expert_w[layer])
    wr = wr.at[:, N_EXPERTS:N_EXPERTS + N_GROUPS].set(router_group_w[layer])
    wr_hi = wr.astype(BF16)
    wr = jnp.concatenate([wr_hi, (wr - wr_hi.astype(F32)).astype(BF16)], axis=1)
    br = jnp.zeros((1, ROUTER_LANES), F32)
    br = br.at[0, :N_EXPERTS].set(router_expert_b[layer])
    br = br.at[0, N_EXPERTS:N_EXPERTS + N_GROUPS].set(router_group_b[layer])

    hq, kf, kb, lff, lfb, hi, hg, dq, dk, dv = _inproj(
        xs, pos, norm1_gain[layer].reshape(1, -1), w_in[layer].astype(BF16), lb, invf, sgn,
        pcos, psin)
    o_f, wg, wu = _hgrn(hq, kf, lff, hi, False, (moe_w_gate[layer], moe_w_up[layer]))
    o_b, wd = _hgrn(hq, kb, lfb, hi, True, (moe_w_down[layer],))
    o_da = _attn(lam, dq, dk, dv, diff_subln_gain[layer].reshape(1, -1))
    x2, rank_t, gates_t, cnt = _outproj(
        xs, o_f, o_b, hg, o_da, hg_norm_gain[layer].reshape(1, -1), w_out[layer].astype(BF16),
        norm2_gain[layer].reshape(1, -1), wr, br, tb=MOE_TOKEN_BLOCK)
    out = _moe(cnt, x2, rank_t, gates_t, norm2_gain[layer].reshape(1, -1), wg, wu, wd,
               final_norm_gain.reshape(1, -1), tb=MOE_TOKEN_BLOCK)
    return out.reshape(B, S, D_MODEL)
```

```python
import functools
import math

import jax
import jax.numpy as jnp
from jax import lax
from jax.experimental import pallas as pl
from jax.experimental.pallas import tpu as pltpu

F32 = jnp.float32
BF16 = jnp.bfloat16

D_MODEL = 1024
HEAD_W = 128
N_HEADS = 4
HALF_W = N_HEADS * HEAD_W
QK_DIM = 64
ROT_DIM = 16
ROPE_THETA = 500000.0
QUERY_GROUP = 512
KV_UNROLL = 4
VT_ROWS = HEAD_W + 16
HG_CHUNK = 128
HG_UNROLL = 8
N_GROUPS = 4
EXPERTS_PER_GROUP = 8
N_EXPERTS = 32
D_EXPERT = 512
NORM_EPS = 1e-6
SUBLN_EPS = 1e-5
LAMBDA_INIT = 0.8 - 0.6 * math.exp(-0.3 * 0)
LOG2E = 1.4426950408889634
ROUTER_LANES = 128
MOE_TOKEN_BLOCK = 256
V7X_VMEM_LIMIT = 58 * 1024 * 1024

NT_DIMS = (((1,), (1,)), ((), ()))
TN_DIMS = (((0,), (0,)), ((), ()))


def _rms(x, eps):
    return x * lax.rsqrt(jnp.mean(x * x, axis=-1, keepdims=True) + eps)


def _inproj_kernel(x_ref, pos_ref, g1_ref, w_ref, lb_ref, invf_ref, sgn_ref, pcos_ref, psin_ref,
                   hq_ref, kf_ref, kb_ref, lff_ref, lfb_ref, hi_ref, hg_ref,
                   dq_ref, dk_ref, dv_ref):
    x = x_ref[...]
    h = (_rms(x, NORM_EPS) * g1_ref[...]).astype(BF16)

    def proj(i):
        return jnp.dot(h, w_ref[:, i * HALF_W:(i + 1) * HALF_W], preferred_element_type=F32)

    hq_ref[...] = proj(0)
    for i, (k_ref, lf_ref) in enumerate(((kf_ref, lff_ref), (kb_ref, lfb_ref))):
        z = proj(1 + i)
        lb = lb_ref[i:i + 1, :]
        sg = jax.nn.sigmoid(z)
        lf_ref[...] = jnp.log(lb + (1.0 - lb) * sg)
        k_ref[...] = (1.0 - lb) * (1.0 - sg)
    hi_ref[...] = proj(3)
    hg_ref[...] = proj(4)

    ang_t = invf_ref[...] * pos_ref[...]

    def spread(t, place_ref):
        hi = t.astype(BF16)
        r1 = t - hi.astype(F32)
        mid = r1.astype(BF16)
        lo = (r1 - mid.astype(F32)).astype(BF16)
        pieces = jnp.concatenate([hi, mid, lo, jnp.zeros_like(hi)], axis=0)
        return lax.dot_general(pieces, place_ref[...], TN_DIMS, preferred_element_type=F32)

    sgn = sgn_ref[...]
    cos = jnp.concatenate([spread(jnp.cos(ang_t), pcos_ref) + jnp.where(sgn == 0.0, 1.0, 0.0)]
                          * N_HEADS, axis=1)
    sin = jnp.concatenate([spread(jnp.sin(ang_t), psin_ref)] * N_HEADS, axis=1)
    take_hi = jnp.concatenate([jnp.broadcast_to(sgn, (x.shape[0], HEAD_W))] * N_HEADS,
                              axis=1) < 0.0

    def rotary(t):
        partner = jnp.where(take_hi, pltpu.roll(t, HALF_W - ROT_DIM // 2, axis=1),
                            pltpu.roll(t, ROT_DIM // 2, axis=1))
        return t * cos + partner * sin

    dq_ref[...] = (rotary(proj(5)) * (QK_DIM ** -0.5 * LOG2E)).astype(BF16)
    dk_ref[...] = rotary(proj(6)).astype(BF16)
    v = proj(7)
    sub = lax.broadcasted_iota(jnp.int32, (VT_ROWS - HEAD_W, v.shape[0]), 0)
    ones_rows = jnp.where(sub == 0, 1.0, 0.0)
    pieces = []
    for hd in range(N_HEADS):
        pieces += [v[:, hd * HEAD_W:(hd + 1) * HEAD_W].T, ones_rows]
    dv_ref[...] = jnp.concatenate(pieces, axis=0).astype(BF16)


def _inproj(x, pos, g1, w_in, lb, invf, sgn, pcos, psin, tm=512):
    S = x.shape[0]
    row = lambda w: pl.BlockSpec((tm, w), lambda i: (i, 0))
    full = lambda a: pl.BlockSpec(a.shape, lambda i: (0, 0))
    f32o = jax.ShapeDtypeStruct((S, HALF_W), F32)
    bf16o = jax.ShapeDtypeStruct((S, HALF_W), BF16)
    return pl.pallas_call(
        _inproj_kernel,
        grid=(S // tm,),
        in_specs=[row(D_MODEL), pl.BlockSpec((1, tm), lambda i: (0, i)), full(g1), full(w_in),
                  full(lb), full(invf), full(sgn), full(pcos), full(psin)],
        out_specs=[row(HALF_W)] * 9 + [pl.BlockSpec((N_HEADS * VT_ROWS, tm), lambda i: (0, i))],
        out_shape=[f32o] * 7 + [bf16o, bf16o,
                                jax.ShapeDtypeStruct((N_HEADS * VT_ROWS, S), BF16)],
        compiler_params=pltpu.CompilerParams(dimension_semantics=("arbitrary",),
                                             vmem_limit_bytes=V7X_VMEM_LIMIT),
        name="inproj",
    )(x, pos, g1, w_in, lb, invf, sgn, pcos, psin)


def _chunk_cumsum(x, rev):
    n = x.shape[0]
    row = lax.broadcasted_iota(jnp.int32, x.shape, 0)
    d = 1
    while d < n:
        if rev:
            x = x + jnp.where(row < n - d, pltpu.roll(x, n - d, axis=0), 0.0)
        else:
            x = x + jnp.where(row >= d, pltpu.roll(x, d, axis=0), 0.0)
        d *= 2
    return x


def _pivot_rows(b, blk, rev):
    n = b.shape[0]
    half = blk // 2
    groups = []
    for g0 in range(0, n, 8):
        def piv(r):
            base = (r // blk) * blk
            return base + half if rev else base + half - 1
        if blk >= 8:
            p = piv(g0)
            groups.append(jnp.broadcast_to(b[p:p + 1, :], (8, b.shape[1])))
        else:
            sub = lax.broadcasted_iota(jnp.int32, (8, b.shape[1]), 0)
            acc = None
            for s0 in range(0, 8, blk):
                p = piv(g0 + s0)
                rowv = jnp.broadcast_to(b[p:p + 1, :], (8, b.shape[1]))
                acc = rowv if acc is None else jnp.where(sub >= s0, rowv, acc)
            groups.append(acc)
    return jnp.concatenate(groups, axis=0)


def _hgrn_kernel(q_ref, k_ref, lf_ref, v_ref, *rest, rev, n_chunks, n_cast):
    w_refs, o_ref, wb_refs, st_ref = (rest[:n_cast], rest[n_cast], rest[n_cast + 1:-1], rest[-1])
    for w_ref, wb_ref in zip(w_refs, wb_refs):
        wb_ref[...] = w_ref[...].astype(BF16)

    @pl.when(pl.program_id(1) == 0)
    def _():
        st_ref[...] = jnp.zeros_like(st_ref)

    C, W = HG_CHUNK, HEAD_W
    rt = lax.broadcasted_iota(jnp.int32, (C, C), 0)
    ct = lax.broadcasted_iota(jnp.int32, (C, C), 1)
    rowi = lax.broadcasted_iota(jnp.int32, (C, 2 * W), 0)
    eye = jnp.where(rt == ct, 1.0, 0.0)
    levels = []
    blk = C
    while blk >= 2:
        half = blk // 2
        q_side = (lambda r: (r % blk) < half) if rev else (lambda r: (r % blk) >= half)
        pair = ((rt // blk) == (ct // blk)) & q_side(rt) & jnp.logical_not(q_side(ct))
        levels.append((blk, jnp.where(q_side(rowi), LOG2E, -LOG2E), jnp.where(pair, 1.0, 0.0)))
        blk = half

    def stack(a):
        return jnp.concatenate([a[:, :W], a[:, W:]], axis=0)

    def chunk(ci, st):
        c = (n_chunks - 1 - ci) if rev else ci
        sl = pl.ds(pl.multiple_of(c * C, C), C)
        q = q_ref[sl, :]
        k = k_ref[sl, :]
        vb = v_ref[sl, :].astype(BF16)
        b = _chunk_cumsum(lf_ref[sl, :], rev)
        tot = b[0:1, :] if rev else b[C - 1:C, :]

        o = lax.dot_general(stack((q * jnp.exp(b)).astype(BF16)), st.astype(BF16), NT_DIMS,
                            preferred_element_type=F32)
        kdec = (k * jnp.exp(tot - b)).astype(BF16)
        upd = lax.dot_general(vb, kdec, TN_DIMS, preferred_element_type=F32)
        dec = jnp.exp(tot)
        st_next = jnp.concatenate([st[:W] * dec[:, :W] + upd[:W, :W],
                                   st[W:] * dec[:, W:] + upd[W:, W:]], axis=0)

        qk = q * k
        scores = [eye * jnp.sum(qk[:, :W], axis=1, keepdims=True),
                  eye * jnp.sum(qk[:, W:], axis=1, keepdims=True)]
        for blk, side, pair in levels:
            w = jnp.exp2((b - _pivot_rows(b, blk, rev)) * side)
            z = stack((jnp.where(side > 0.0, q, k) * w).astype(BF16))
            g = lax.dot_general(z, z, NT_DIMS, preferred_element_type=F32)
            scores = [scores[0] + pair * g[:C, :C], scores[1] + pair * g[C:, C:]]
        intra = jnp.dot(jnp.concatenate(scores, axis=0).astype(BF16), vb,
                        preferred_element_type=F32)
        o_ref[sl, :] = jnp.concatenate([o[:C, :W] + intra[:C, :W], o[C:, W:] + intra[C:, W:]],
                                       axis=1)
        return st_next

    st_ref[...] = lax.fori_loop(0, n_chunks, chunk, st_ref[...], unroll=HG_UNROLL)


def _hgrn(q, k, lf, v, rev, weights, rb=2048):
    S = q.shape[0]
    nb = S // rb
    n_pairs = N_HEADS // 2
    steps = n_pairs * nb
    idx = (lambda h, j: (nb - 1 - j, h)) if rev else (lambda h, j: (j, h))
    spec = pl.BlockSpec((rb, 2 * HEAD_W), idx)
    w_specs = []
    for w in weights:
        n_exp, rows, cols = w.shape
        per_step = n_exp // steps
        assert per_step * steps == n_exp
        w_specs.append(pl.BlockSpec((per_step, rows, cols), lambda h, j: (h * nb + j, 0, 0)))
    return pl.pallas_call(
        functools.partial(_hgrn_kernel, rev=rev, n_chunks=rb // HG_CHUNK, n_cast=len(weights)),
        grid=(n_pairs, nb),
        in_specs=[spec] * 4 + w_specs,
        out_specs=[spec] + w_specs,
        out_shape=[jax.ShapeDtypeStruct((S, HALF_W), F32)]
        + [jax.ShapeDtypeStruct(w.shape, BF16) for w in weights],
        scratch_shapes=[pltpu.VMEM((2 * HEAD_W, HEAD_W), F32)],
        compiler_params=pltpu.CompilerParams(dimension_semantics=("arbitrary", "arbitrary"),
                                             vmem_limit_bytes=V7X_VMEM_LIMIT),
        name="hgrn_bwd" if rev else "hgrn_fwd",
    )(q, k, lf, v, *weights)


def _attn_kernel(lam_ref, q_ref, k_ref, vt_ref, gain_ref, o_ref, sa_ref, sb_ref, m_ref, acc_ref,
                 *, tk):
    n_kv = k_ref.shape[0] // tk
    tq = q_ref.shape[0]
    assert n_kv % KV_UNROLL == 0 and KV_UNROLL % 2 == 0 and tq % QUERY_GROUP == 0
    q = q_ref[...]
    lane = lax.broadcasted_iota(jnp.int32, q.shape, 1)
    qm = (jnp.where(lane < QK_DIM, q, jnp.zeros_like(q)),
          jnp.where(lane >= QK_DIM, q, jnp.zeros_like(q)))
    m_ref[...] = jnp.full(m_ref.shape, -jnp.inf, F32)
    acc_ref[...] = jnp.zeros_like(acc_ref)

    def chunk(j):
        return pl.ds(pl.multiple_of(j * tk, tk), tk)

    def softmax_pv(c, cols, s, vt):
        m_old = m_ref[c, :, cols]
        m_new = jnp.maximum(m_old, jnp.max(s, axis=0, keepdims=True))
        p = jnp.exp2(s - m_new).astype(BF16)
        acc_ref[c, :, cols] = jnp.exp2(m_old - m_new) * acc_ref[c, :, cols] + jnp.dot(
            vt, p, preferred_element_type=F32)
        m_ref[c, :, cols] = m_new

    def step(js, dst_ref, jp, src_ref):
        kc = None if dst_ref is None else k_ref[chunk(js), :]
        vt = None if src_ref is None else vt_ref[:, chunk(jp)]
        for c in range(2):
            for g in range(0, tq, QUERY_GROUP):
                cols = slice(g, g + QUERY_GROUP)
                if dst_ref is not None:
                    dst_ref[c, :, cols] = lax.dot_general(kc, qm[c][cols, :], NT_DIMS,
                                                          preferred_element_type=F32)
                if src_ref is not None:
                    softmax_pv(c, cols, src_ref[c, :, cols], vt)

    bufs = (sa_ref, sb_ref)
    step(0, sa_ref, None, None)

    def body(i, carry):
        for u in range(KV_UNROLL):
            j = KV_UNROLL * i + u
            step(j + 1, bufs[(u + 1) % 2], j, bufs[u % 2])
        return carry

    lax.fori_loop(0, n_kv // KV_UNROLL - 1, body, 0)
    for j in range(n_kv - KV_UNROLL, n_kv - 1):
        step(j + 1, bufs[(j + 1) % 2], j, bufs[j % 2])
    step(None, None, n_kv - 1, bufs[(n_kv - 1) % 2])

    outs = []
    for c in range(2):
        a = acc_ref[c]
        outs.append(a[:HEAD_W, :] / a[HEAD_W:HEAD_W + 1, :])
    o = (outs[0] - lam_ref[0] * outs[1]).T
    o_ref[...] = _rms(o, SUBLN_EPS) * gain_ref[...] * (1.0 - LAMBDA_INIT)


def _attn(lam, dq, dk, dvt, gain, tq=2048, tk=512):
    S = dq.shape[0]
    return pl.pallas_call(
        functools.partial(_attn_kernel, tk=tk),
        grid=(N_HEADS, S // tq),
        in_specs=[pl.BlockSpec(memory_space=pltpu.SMEM),
                  pl.BlockSpec((tq, HEAD_W), lambda h, i: (i, h)),
                  pl.BlockSpec((S, HEAD_W), lambda h, i: (0, h)),
                  pl.BlockSpec((VT_ROWS, S), lambda h, i: (h, 0)),
                  pl.BlockSpec((1, HEAD_W), lambda h, i: (0, 0))],
        out_specs=pl.BlockSpec((tq, HEAD_W), lambda h, i: (i, h)),
        out_shape=jax.ShapeDtypeStruct((S, HALF_W), F32),
        scratch_shapes=[pltpu.VMEM((2, tk, tq), F32), pltpu.VMEM((2, tk, tq), F32),
                        pltpu.VMEM((2, 1, tq), F32), pltpu.VMEM((2, VT_ROWS, tq), F32)],
        compiler_params=pltpu.CompilerParams(dimension_semantics=("arbitrary", "arbitrary"),
                                             vmem_limit_bytes=V7X_VMEM_LIMIT),
        name="diff_attn",
    )(lam, dq, dk, dvt, gain)


def _outproj_kernel(x_ref, of_ref, ob_ref, hg_ref, oda_ref, hgain_ref, wo_ref, g2_ref,
                    wr_ref, br_ref, x2_ref, rank_t_ref, gates_t_ref, cnt_ref,
                    *, tb):
    i = pl.program_id(0)
    part = x_ref.shape[0]
    n_routing = part // tb

    def project(rows):
        o = of_ref[rows, :] + ob_ref[rows, :]
        gate = hg_ref[rows, :]
        gate = gate * jax.nn.sigmoid(gate)
        parts = []
        for hd in range(N_HEADS):
            sl = slice(hd * HEAD_W, (hd + 1) * HEAD_W)
            parts.append(_rms(o[:, sl], NORM_EPS) * hgain_ref[...] * gate[:, sl])
        parts.append(oda_ref[rows, :])
        mixed = jnp.concatenate(parts, axis=1).astype(BF16)
        x2 = x_ref[rows, :] + jnp.dot(mixed, wo_ref[...], preferred_element_type=F32)
        x2_ref[rows, :] = x2
        return x2

    lane = lax.broadcasted_iota(jnp.int32, (part, ROUTER_LANES), 1)
    neg = -jnp.inf
    rt = lax.broadcasted_iota(jnp.int32, (part, part), 0)
    ct = lax.broadcasted_iota(jnp.int32, (part, part), 1)
    tri = jnp.where((rt >= ct) & (rt // tb == ct // tb), 1.0, 0.0).astype(BF16)

    def first_max(vals):
        mx = jnp.max(vals, axis=1, keepdims=True)
        idx = jnp.min(jnp.where(vals == mx, lane, ROUTER_LANES), axis=1, keepdims=True)
        return mx, idx

    def route(rows, x2):
        h2 = _rms(x2, NORM_EPS) * g2_ref[...]
        h_hi = h2.astype(BF16)
        h_lo = (h2 - h_hi.astype(F32)).astype(BF16)
        t = jnp.dot(h_hi, wr_ref[...], preferred_element_type=F32)
        logits = (t[:, :ROUTER_LANES] + t[:, ROUTER_LANES:] + br_ref[...]
                  + jnp.dot(h_lo, wr_ref[:, :ROUTER_LANES], preferred_element_type=F32))
        gl = jnp.where((lane >= N_EXPERTS) & (lane < N_EXPERTS + N_GROUPS), logits, neg)
        gm, gidx = first_max(gl)
        g_p = 1.0 / jnp.sum(jnp.exp(gl - gm), axis=1, keepdims=True)
        el = jnp.where((lane < N_EXPERTS) & ((lane // EXPERTS_PER_GROUP) == (gidx - N_EXPERTS)),
                       logits, neg)
        em1, idx1 = first_max(el)
        em2, idx2 = first_max(jnp.where(lane == idx1, neg, el))
        e2 = jnp.exp(em2 - em1)
        w1 = g_p / (1.0 + e2)
        w2 = g_p * e2 / (1.0 + e2)
        is1 = lane == idx1
        is2 = lane == idx2
        gates = jnp.where(is1, w1, 0.0) + jnp.where(is2, w2, 0.0)
        assigned = jnp.where(is1 | is2, 1.0, 0.0)
        cum = jnp.dot(tri, assigned.astype(BF16), preferred_element_type=F32)
        rank_t_ref[:, rows] = jnp.where(assigned > 0.0, cum - 1.0, -1.0).T
        gates_t_ref[:, rows] = gates.T
        for s in range(n_routing):
            last = (s + 1) * tb - 1
            cnt_ref[pl.ds(n_routing * i + s, 1), :] = cum[last:last + 1, :].astype(jnp.int32)

    rows = slice(0, part)
    route(rows, project(rows))


def _outproj(x, o_f, o_b, hg, o_da, hgain, w_out, g2, wr, br, tb, tm=512):
    S = x.shape[0]
    row = lambda w: pl.BlockSpec((tm, w), lambda i: (i, 0))
    col = pl.BlockSpec((ROUTER_LANES, tm), lambda i: (0, i))
    full = lambda a: pl.BlockSpec(a.shape, lambda i: (0, 0))
    return pl.pallas_call(
        functools.partial(_outproj_kernel, tb=tb),
        grid=(S // tm,),
        in_specs=[row(D_MODEL), row(HALF_W), row(HALF_W), row(HALF_W), row(HALF_W),
                  full(hgain), full(w_out), full(g2), full(wr), full(br)],
        out_specs=[row(D_MODEL), col, col,
                   pl.BlockSpec((S // tb, ROUTER_LANES), lambda i: (0, 0))],
        out_shape=[jax.ShapeDtypeStruct((S, D_MODEL), F32),
                   jax.ShapeDtypeStruct((ROUTER_LANES, S), F32),
                   jax.ShapeDtypeStruct((ROUTER_LANES, S), F32),
                   jax.ShapeDtypeStruct((S // tb, ROUTER_LANES), jnp.int32)],
        compiler_params=pltpu.CompilerParams(dimension_semantics=("arbitrary",),
                                             vmem_limit_bytes=V7X_VMEM_LIMIT),
        name="outproj_router",
    )(x, o_f, o_b, hg, o_da, hgain, w_out, g2, wr, br)


def _moe_kernel(cnt_ref, x2_ref, rank_t_ref, gates_t_ref, g2_ref, wg_ref, wu_ref,
                wd_ref, gf_ref, out_ref, h2_ref, *, tile_rows, tb):
    b = pl.program_id(0)
    p = pl.program_id(1)
    R = tile_rows
    n_exp = wg_ref.shape[0]
    n_sub = x2_ref.shape[0] // tb

    @pl.when(p == 0)
    def _():
        x2 = x2_ref[...]
        h2_ref[...] = (_rms(x2, NORM_EPS) * g2_ref[...]).astype(BF16)
        out_ref[...] = x2

    experts = [n_exp * p + i for i in range(n_exp)]
    subs = [slice(s * tb, (s + 1) * tb) for s in range(n_sub)]
    rank_row = [[rank_t_ref[pl.ds(e, 1), sub] for e in experts] for sub in subs]
    gate_row = [[gates_t_ref[pl.ds(e, 1), sub] for e in experts] for sub in subs]
    n_max = jnp.int32(0)
    for s in range(n_sub):
        for e in experts:
            n_max = jnp.maximum(n_max, cnt_ref[n_sub * b + s, e])

    def tile(t, carry):
        base = (t * R).astype(F32)
        slot_col = lax.broadcasted_iota(jnp.int32, (R, 1), 0).astype(F32) + base
        sel, gather, xg = [], [], []
        for s, sub in enumerate(subs):
            sel.append([rank_row[s][i] == slot_col for i in range(n_exp)])
            gather.append(jnp.concatenate(
                [jnp.where(m, 1.0, 0.0).astype(BF16) for m in sel[s]], axis=0))
            xg.append(jnp.dot(gather[s], h2_ref[sub, :],
                              preferred_element_type=F32).astype(BF16))
        ys = [[] for _ in subs]
        for i in range(n_exp):
            xi = jnp.concatenate([xg[s][i * R:(i + 1) * R] for s in range(n_sub)], axis=0)
            a = jnp.dot(xi, wg_ref[i], preferred_element_type=F32)
            a = a * jax.nn.sigmoid(a) * jnp.dot(xi, wu_ref[i], preferred_element_type=F32)
            y = jnp.dot(a.astype(BF16), wd_ref[i], preferred_element_type=F32)
            for s in range(n_sub):
                g = jnp.sum(jnp.where(sel[s][i], gate_row[s][i], 0.0), axis=1, keepdims=True)
                ys[s].append((y[s * R:(s + 1) * R] * g).astype(BF16))
        for s, sub in enumerate(subs):
            out_ref[sub, :] += lax.dot_general(gather[s], jnp.concatenate(ys[s], axis=0),
                                               TN_DIMS, preferred_element_type=F32)
        return carry

    lax.fori_loop(0, (n_max + R - 1) // R, tile, 0)

    @pl.when(p == pl.num_programs(1) - 1)
    def _():
        out_ref[...] = _rms(out_ref[...], NORM_EPS) * gf_ref[...]


def _moe(cnt, x2, rank_t, gates_t, g2, wg, wu, wd, gf, tb, tile_rows=32,
         experts_per_step=4, blocks_per_step=8):
    S = x2.shape[0]
    eps = experts_per_step
    rows = tb * blocks_per_step
    grid_spec = pltpu.PrefetchScalarGridSpec(
        num_scalar_prefetch=1,
        grid=(S // rows, N_EXPERTS // eps),
        in_specs=[pl.BlockSpec((rows, D_MODEL), lambda b, p, c: (b, 0),
                               pipeline_mode=pl.Buffered(1)),
                  pl.BlockSpec((N_EXPERTS, rows), lambda b, p, c: (0, b)),
                  pl.BlockSpec((N_EXPERTS, rows), lambda b, p, c: (0, b)),
                  pl.BlockSpec((1, D_MODEL), lambda b, p, c: (0, 0)),
                  pl.BlockSpec((eps, D_MODEL, D_EXPERT), lambda b, p, c: (p, 0, 0)),
                  pl.BlockSpec((eps, D_MODEL, D_EXPERT), lambda b, p, c: (p, 0, 0)),
                  pl.BlockSpec((eps, D_EXPERT, D_MODEL), lambda b, p, c: (p, 0, 0)),
                  pl.BlockSpec((1, D_MODEL), lambda b, p, c: (0, 0))],
        out_specs=pl.BlockSpec((rows, D_MODEL), lambda b, p, c: (b, 0)),
        scratch_shapes=[pltpu.VMEM((rows, D_MODEL), BF16)])
    return pl.pallas_call(
        functools.partial(_moe_kernel, tile_rows=tile_rows, tb=tb),
        grid_spec=grid_spec,
        out_shape=jax.ShapeDtypeStruct((S, D_MODEL), F32),
        compiler_params=pltpu.CompilerParams(dimension_semantics=("arbitrary", "arbitrary"),
                                             vmem_limit_bytes=V7X_VMEM_LIMIT),
        name="moe",
    )(cnt, x2, rank_t, gates_t, g2, wg, wu, wd, gf)


def kernel(x, positions, norm1_gain, w_in, hg_lower_bounds, hg_norm_gain, diff_lambda,
           diff_subln_gain, w_out, norm2_gain, router_group_w, router_group_b,
           router_expert_w, router_expert_b, moe_w_gate, moe_w_up, moe_w_down,
           final_norm_gain):
    B, S, _ = x.shape
    assert B == 1 and norm1_gain.shape[0] == 1
    layer = 0
    xs = x.reshape(S, D_MODEL)
    pos = positions.reshape(1, S).astype(F32)

    lb_cum = jnp.cumsum(jax.nn.softmax(hg_lower_bounds.astype(F32), axis=1), axis=1)
    lb = lb_cum[:, layer + 1] - lb_cum[:, 0]
    lam_p = diff_lambda[layer].astype(F32)
    lam = (jnp.exp(jnp.sum(lam_p[0] * lam_p[1])) - jnp.exp(jnp.sum(lam_p[2] * lam_p[3]))
           + LAMBDA_INIT).reshape(1)
    lane = jnp.arange(HEAD_W)
    inv_freq = jnp.float32(ROPE_THETA) ** (-jnp.arange(0, ROT_DIM, 2, dtype=F32) / ROT_DIM)
    in_rot = (lane % QK_DIM) < ROT_DIM
    n_freq = ROT_DIM // 2
    invf = inv_freq.reshape(n_freq, 1)
    sgn = jnp.where(in_rot, jnp.where((lane % QK_DIM) < n_freq, -1.0, 1.0),
                    0.0).astype(F32).reshape(1, HEAD_W)
    place = (in_rot[None, :] & (lane[None, :] % n_freq == jnp.arange(n_freq)[:, None])).astype(F32)
    pcos = jnp.tile(place, (4, 1)).astype(BF16)
    psin = jnp.tile(place * sgn, (4, 1)).astype(BF16)
    wr = jnp.zeros((D_MODEL, ROUTER_LANES), F32)
    wr = wr.at[:, :N_EXPERTS].set(router_expert_w[layer])
    wr = wr.at[:, N_EXPERTS:N_EXPERTS + N_GROUPS].set(router_group_w[layer])
    wr_hi = wr.astype(BF16)
    wr = jnp.concatenate([wr_hi, (wr - wr_hi.astype(F32)).astype(BF16)], axis=1)
    br = jnp.zeros((1, ROUTER_LANES), F32)
    br = br.at[0, :N_EXPERTS].set(router_expert_b[layer])
    br = br.at[0, N_EXPERTS:N_EXPERTS + N_GROUPS].set(router_group_b[layer])

    hq, kf, kb, lff, lfb, hi, hg, dq, dk, dv = _inproj(
        xs, pos, norm1_gain[layer].reshape(1, -1), w_in[layer].astype(BF16), lb, invf, sgn,
        pcos, psin)
    o_f, wg, wu = _hgrn(hq, kf, lff, hi, False, (moe_w_gate[layer], moe_w_up[layer]))
    o_b, wd = _hgrn(hq, kb, lfb, hi, True, (moe_w_down[layer],))
    o_da = _attn(lam, dq, dk, dv, diff_subln_gain[layer].reshape(1, -1))
    x2, rank_t, gates_t, cnt = _outproj(
        xs, o_f, o_b, hg, o_da, hg_norm_gain[layer].reshape(1, -1), w_out[layer].astype(BF16),
        norm2_gain[layer].reshape(1, -1), wr, br, tb=MOE_TOKEN_BLOCK)
    out = _moe(cnt, x2, rank_t, gates_t, norm2_gain[layer].reshape(1, -1), wg, wu, wd,
               final_norm_gain.reshape(1, -1), tb=MOE_TOKEN_BLOCK)
    return out.reshape(B, S, D_MODEL)
```

```python
import functools
import math

import jax
import jax.numpy as jnp
from jax import lax
from jax.experimental import pallas as pl
from jax.experimental.pallas import tpu as pltpu

F32 = jnp.float32
BF16 = jnp.bfloat16

D_MODEL = 1024
HEAD_W = 128
N_HEADS = 4
HALF_W = N_HEADS * HEAD_W
QK_DIM = 64
ROT_DIM = 16
ROPE_THETA = 500000.0
QUERY_GROUP = 512
KV_UNROLL = 4
VT_ROWS = HEAD_W + 16
HG_CHUNK = 128
HG_UNROLL = 8
N_GROUPS = 4
EXPERTS_PER_GROUP = 8
N_EXPERTS = 32
D_EXPERT = 512
NORM_EPS = 1e-6
SUBLN_EPS = 1e-5
LAMBDA_INIT = 0.8 - 0.6 * math.exp(-0.3 * 0)
LOG2E = 1.4426950408889634
ROUTER_LANES = 128
MOE_TOKEN_BLOCK = 256
V7X_VMEM_LIMIT = 58 * 1024 * 1024

NT_DIMS = (((1,), (1,)), ((), ()))
TN_DIMS = (((0,), (0,)), ((), ()))


def _rms(x, eps):
    return x * lax.rsqrt(jnp.mean(x * x, axis=-1, keepdims=True) + eps)


def _inproj_kernel(x_ref, pos_ref, g1_ref, w_ref, lb_ref, invf_ref, sgn_ref, pcos_ref, psin_ref,
                   hq_ref, kf_ref, kb_ref, lff_ref, lfb_ref, hi_ref, hg_ref,
                   dq_ref, dk_ref, dv_ref):
    x = x_ref[...]
    h = (_rms(x, NORM_EPS) * g1_ref[...]).astype(BF16)

    def proj(i):
        return jnp.dot(h, w_ref[:, i * HALF_W:(i + 1) * HALF_W], preferred_element_type=F32)

    hq_ref[...] = proj(0)
    for i, (k_ref, lf_ref) in enumerate(((kf_ref, lff_ref), (kb_ref, lfb_ref))):
        z = proj(1 + i)
        lb = lb_ref[i:i + 1, :]
        sg = jax.nn.sigmoid(z)
        lf_ref[...] = jnp.log(lb + (1.0 - lb) * sg)
        k_ref[...] = (1.0 - lb) * (1.0 - sg)
    hi_ref[...] = proj(3)
    hg_ref[...] = proj(4)

    ang_t = invf_ref[...] * pos_ref[...]

    def spread(t, place_ref):
        hi = t.astype(BF16)
        r1 = t - hi.astype(F32)
        mid = r1.astype(BF16)
        lo = (r1 - mid.astype(F32)).astype(BF16)
        pieces = jnp.concatenate([hi, mid, lo, jnp.zeros_like(hi)], axis=0)
        return lax.dot_general(pieces, place_ref[...], TN_DIMS, preferred_element_type=F32)

    sgn = sgn_ref[...]
    cos = jnp.concatenate([spread(jnp.cos(ang_t), pcos_ref) + jnp.where(sgn == 0.0, 1.0, 0.0)]
                          * N_HEADS, axis=1)
    sin = jnp.concatenate([spread(jnp.sin(ang_t), psin_ref)] * N_HEADS, axis=1)
    take_hi = jnp.concatenate([jnp.broadcast_to(sgn, (x.shape[0], HEAD_W))] * N_HEADS,
                              axis=1) < 0.0

    def rotary(t):
        partner = jnp.where(take_hi, pltpu.roll(t, HALF_W - ROT_DIM // 2, axis=1),
                            pltpu.roll(t, ROT_DIM // 2, axis=1))
        return t * cos + partner * sin

    dq_ref[...] = (rotary(proj(5)) * (QK_DIM ** -0.5 * LOG2E)).astype(BF16)
    dk_ref[...] = rotary(proj(6)).astype(BF16)
    v = proj(7)
    sub = lax.broadcasted_iota(jnp.int32, (VT_ROWS - HEAD_W, v.shape[0]), 0)
    ones_rows = jnp.where(sub == 0, 1.0, 0.0)
    pieces = []
    for hd in range(N_HEADS):
        pieces += [v[:, hd * HEAD_W:(hd + 1) * HEAD_W].T, ones_rows]
    dv_ref[...] = jnp.concatenate(pieces, axis=0).astype(BF16)


def _inproj(x, pos, g1, w_in, lb, invf, sgn, pcos, psin, tm=512):
    S = x.shape[0]
    row = lambda w: pl.BlockSpec((tm, w), lambda i: (i, 0))
    full = lambda a: pl.BlockSpec(a.shape, lambda i: (0, 0))
    f32o = jax.ShapeDtypeStruct((S, HALF_W), F32)
    bf16o = jax.ShapeDtypeStruct((S, HALF_W), BF16)
    return pl.pallas_call(
        _inproj_kernel,
        grid=(S // tm,),
        in_specs=[row(D_MODEL), pl.BlockSpec((1, tm), lambda i: (0, i)), full(g1), full(w_in),
                  full(lb), full(invf), full(sgn), full(pcos), full(psin)],
        out_specs=[row(HALF_W)] * 9 + [pl.BlockSpec((N_HEADS * VT_ROWS, tm), lambda i: (0, i))],
        out_shape=[f32o] * 7 + [bf16o, bf16o,
                                jax.ShapeDtypeStruct((N_HEADS * VT_ROWS, S), BF16)],
        compiler_params=pltpu.CompilerParams(dimension_semantics=("arbitrary",),
                                             vmem_limit_bytes=V7X_VMEM_LIMIT),
        name="inproj",
    )(x, pos, g1, w_in, lb, invf, sgn, pcos, psin)


def _chunk_cumsum(x, rev):
    n = x.shape[0]
    row = lax.broadcasted_iota(jnp.int32, x.shape, 0)
    d = 1
    while d < n:
        if rev:
            x = x + jnp.where(row < n - d, pltpu.roll(x, n - d, axis=0), 0.0)
        else:
            x = x + jnp.where(row >= d, pltpu.roll(x, d, axis=0), 0.0)
        d *= 2
    return x


def _pivot_rows(b, blk, rev):
    n = b.shape[0]
    half = blk // 2
    groups = []
    for g0 in range(0, n, 8):
        def piv(r):
            base = (r // blk) * blk
            return base + half if rev else base + half - 1
        if blk >= 8:
            p = piv(g0)
            groups.append(jnp.broadcast_to(b[p:p + 1, :], (8, b.shape[1])))
        else:
            sub = lax.broadcasted_iota(jnp.int32, (8, b.shape[1]), 0)
            acc = None
            for s0 in range(0, 8, blk):
                p = piv(g0 + s0)
                rowv = jnp.broadcast_to(b[p:p + 1, :], (8, b.shape[1]))
                acc = rowv if acc is None else jnp.where(sub >= s0, rowv, acc)
            groups.append(acc)
    return jnp.concatenate(groups, axis=0)


def _hgrn_kernel(q_ref, k_ref, lf_ref, v_ref, *rest, rev, n_chunks, n_cast):
    w_refs, o_ref, wb_refs, st_ref = (rest[:n_cast], rest[n_cast], rest[n_cast + 1:-1], rest[-1])
    for w_ref, wb_ref in zip(w_refs, wb_refs):
        wb_ref[...] = w_ref[...].astype(BF16)

    @pl.when(pl.program_id(1) == 0)
    def _():
        st_ref[...] = jnp.zeros_like(st_ref)

    C, W = HG_CHUNK, HEAD_W
    rt = lax.broadcasted_iota(jnp.int32, (C, C), 0)
    ct = lax.broadcasted_iota(jnp.int32, (C, C), 1)
    rowi = lax.broadcasted_iota(jnp.int32, (C, 2 * W), 0)
    eye = jnp.where(rt == ct, 1.0, 0.0)
    levels = []
    blk = C
    while blk >= 2:
        half = blk // 2
        q_side = (lambda r: (r % blk) < half) if rev else (lambda r: (r % blk) >= half)
        pair = ((rt // blk) == (ct // blk)) & q_side(rt) & jnp.logical_not(q_side(ct))
        levels.append((blk, jnp.where(q_side(rowi), LOG2E, -LOG2E), jnp.where(pair, 1.0, 0.0)))
        blk = half

    def stack(a):
        return jnp.concatenate([a[:, :W], a[:, W:]], axis=0)

    def chunk(ci, st):
        c = (n_chunks - 1 - ci) if rev else ci
        sl = pl.ds(pl.multiple_of(c * C, C), C)
        q = q_ref[sl, :]
        k = k_ref[sl, :]
        vb = v_ref[sl, :].astype(BF16)
        b = _chunk_cumsum(lf_ref[sl, :], rev)
        tot = b[0:1, :] if rev else b[C - 1:C, :]

        o = lax.dot_general(stack((q * jnp.exp(b)).astype(BF16)), st.astype(BF16), NT_DIMS,
                            preferred_element_type=F32)
        kdec = (k * jnp.exp(tot - b)).astype(BF16)
        upd = lax.dot_general(vb, kdec, TN_DIMS, preferred_element_type=F32)
        dec = jnp.exp(tot)
        st_next = jnp.concatenate([st[:W] * dec[:, :W] + upd[:W, :W],
                                   st[W:] * dec[:, W:] + upd[W:, W:]], axis=0)

        qk = q * k
        scores = [eye * jnp.sum(qk[:, :W], axis=1, keepdims=True),
                  eye * jnp.sum(qk[:, W:], axis=1, keepdims=True)]
        for blk, side, pair in levels:
            w = jnp.exp2((b - _pivot_rows(b, blk, rev)) * side)
            z = stack((jnp.where(side > 0.0, q, k) * w).astype(BF16))
            g = lax.dot_general(z, z, NT_DIMS, preferred_element_type=F32)
            scores = [scores[0] + pair * g[:C, :C], scores[1] + pair * g[C:, C:]]
        intra = jnp.dot(jnp.concatenate(scores, axis=0).astype(BF16), vb,
                        preferred_element_type=F32)
        o_ref[sl, :] = jnp.concatenate([o[:C, :W] + intra[:C, :W], o[C:, W:] + intra[C:, W:]],
                                       axis=1)
        return st_next

    st_ref[...] = lax.fori_loop(0, n_chunks, chunk, st_ref[...], unroll=HG_UNROLL)


def _hgrn(q, k, lf, v, rev, weights, rb=2048):
    S = q.shape[0]
    nb = S // rb
    n_pairs = N_HEADS // 2
    steps = n_pairs * nb
    idx = (lambda h, j: (nb - 1 - j, h)) if rev else (lambda h, j: (j, h))
    spec = pl.BlockSpec((rb, 2 * HEAD_W), idx)
    w_specs = []
    for w in weights:
        n_exp, rows, cols = w.shape
        per_step = n_exp // steps
        assert per_step * steps == n_exp
        w_specs.append(pl.BlockSpec((per_step, rows, cols), lambda h, j: (h * nb + j, 0, 0)))
    return pl.pallas_call(
        functools.partial(_hgrn_kernel, rev=rev, n_chunks=rb // HG_CHUNK, n_cast=len(weights)),
        grid=(n_pairs, nb),
        in_specs=[spec] * 4 + w_specs,
        out_specs=[spec] + w_specs,
        out_shape=[jax.ShapeDtypeStruct((S, HALF_W), F32)]
        + [jax.ShapeDtypeStruct(w.shape, BF16) for w in weights],
        scratch_shapes=[pltpu.VMEM((2 * HEAD_W, HEAD_W), F32)],
        compiler_params=pltpu.CompilerParams(dimension_semantics=("arbitrary", "arbitrary"),
                                             vmem_limit_bytes=V7X_VMEM_LIMIT),
        name="hgrn_bwd" if rev else "hgrn_fwd",
    )(q, k, lf, v, *weights)


def _attn_kernel(lam_ref, q_ref, k_ref, vt_ref, gain_ref, o_ref, sa_ref, sb_ref, m_ref, acc_ref,
                 *, tk):
    n_kv = k_ref.shape[0] // tk
    tq = q_ref.shape[0]
    assert n_kv % KV_UNROLL == 0 and KV_UNROLL % 2 == 0 and tq % QUERY_GROUP == 0
    q = q_ref[...]
    lane = lax.broadcasted_iota(jnp.int32, q.shape, 1)
    qm = (jnp.where(lane < QK_DIM, q, jnp.zeros_like(q)),
          jnp.where(lane >= QK_DIM, q, jnp.zeros_like(q)))
    m_ref[...] = jnp.full(m_ref.shape, -jnp.inf, F32)
    acc_ref[...] = jnp.zeros_like(acc_ref)

    def chunk(j):
        return pl.ds(pl.multiple_of(j * tk, tk), tk)

    def softmax_pv(c, cols, s, vt):
        m_old = m_ref[c, :, cols]
        m_new = jnp.maximum(m_old, jnp.max(s, axis=0, keepdims=True))
        p = jnp.exp2(s - m_new).astype(BF16)
        acc_ref[c, :, cols] = jnp.exp2(m_old - m_new) * acc_ref[c, :, cols] + jnp.dot(
            vt, p, preferred_element_type=F32)
        m_ref[c, :, cols] = m_new

    def step(js, dst_ref, jp, src_ref):
        kc = None if dst_ref is None else k_ref[chunk(js), :]
        vt = None if src_ref is None else vt_ref[:, chunk(jp)]
        for c in range(2):
            for g in range(0, tq, QUERY_GROUP):
                cols = slice(g, g + QUERY_GROUP)
                if dst_ref is not None:
                    dst_ref[c, :, cols] = lax.dot_general(kc, qm[c][cols, :], NT_DIMS,
                                                          preferred_element_type=F32)
                if src_ref is not None:
                    softmax_pv(c, cols, src_ref[c, :, cols], vt)

    bufs = (sa_ref, sb_ref)
    step(0, sa_ref, None, None)

    def body(i, carry):
        for u in range(KV_UNROLL):
            j = KV_UNROLL * i + u
            step(j + 1, bufs[(u + 1) % 2], j, bufs[u % 2])
        return carry

    lax.fori_loop(0, n_kv // KV_UNROLL - 1, body, 0)
    for j in range(n_kv - KV_UNROLL, n_kv - 1):
        step(j + 1, bufs[(j + 1) % 2], j, bufs[j % 2])
    step(None, None, n_kv - 1, bufs[(n_kv - 1) % 2])

    outs = []
    for c in range(2):
        a = acc_ref[c]
        outs.append(a[:HEAD_W, :] / a[HEAD_W:HEAD_W + 1, :])
    o = (outs[0] - lam_ref[0] * outs[1]).T
    o_ref[...] = _rms(o, SUBLN_EPS) * gain_ref[...] * (1.0 - LAMBDA_INIT)


def _attn(lam, dq, dk, dvt, gain, tq=2048, tk=512):
    S = dq.shape[0]
    return pl.pallas_call(
        functools.partial(_attn_kernel, tk=tk),
        grid=(N_HEADS, S // tq),
        in_specs=[pl.BlockSpec(memory_space=pltpu.SMEM),
                  pl.BlockSpec((tq, HEAD_W), lambda h, i: (i, h)),
                  pl.BlockSpec((S, HEAD_W), lambda h, i: (0, h)),
                  pl.BlockSpec((VT_ROWS, S), lambda h, i: (h, 0)),
                  pl.BlockSpec((1, HEAD_W), lambda h, i: (0, 0))],
        out_specs=pl.BlockSpec((tq, HEAD_W), lambda h, i: (i, h)),
        out_shape=jax.ShapeDtypeStruct((S, HALF_W), F32),
        scratch_shapes=[pltpu.VMEM((2, tk, tq), F32), pltpu.VMEM((2, tk, tq), F32),
                        pltpu.VMEM((2, 1, tq), F32), pltpu.VMEM((2, VT_ROWS, tq), F32)],
        compiler_params=pltpu.CompilerParams(dimension_semantics=("arbitrary", "arbitrary"),
                                             vmem_limit_bytes=V7X_VMEM_LIMIT),
        name="diff_attn",
    )(lam, dq, dk, dvt, gain)


def _outproj_kernel(x_ref, of_ref, ob_ref, hg_ref, oda_ref, hgain_ref, wo_ref, g2_ref,
                    wr_ref, br_ref, x2_ref, rank_t_ref, gates_t_ref, cnt_ref,
                    *, tb):
    i = pl.program_id(0)
    part = x_ref.shape[0]
    n_routing = part // tb

    def project(rows):
        o = of_ref[rows, :] + ob_ref[rows, :]
        gate = hg_ref[rows, :]
        gate = gate * jax.nn.sigmoid(gate)
        parts = []
        for hd in range(N_HEADS):
            sl = slice(hd * HEAD_W, (hd + 1) * HEAD_W)
            parts.append(_rms(o[:, sl], NORM_EPS) * hgain_ref[...] * gate[:, sl])
        parts.append(oda_ref[rows, :])
        mixed = jnp.concatenate(parts, axis=1).astype(BF16)
        x2 = x_ref[rows, :] + jnp.dot(mixed, wo_ref[...], preferred_element_type=F32)
        x2_ref[rows, :] = x2
        return x2

    lane = lax.broadcasted_iota(jnp.int32, (part, ROUTER_LANES), 1)
    neg = -jnp.inf
    rt = lax.broadcasted_iota(jnp.int32, (part, part), 0)
    ct = lax.broadcasted_iota(jnp.int32, (part, part), 1)
    tri = jnp.where((rt >= ct) & (rt // tb == ct // tb), 1.0, 0.0).astype(BF16)

    def first_max(vals):
        mx = jnp.max(vals, axis=1, keepdims=True)
        idx = jnp.min(jnp.where(vals == mx, lane, ROUTER_LANES), axis=1, keepdims=True)
        return mx, idx

    def route(rows, x2):
        h2 = _rms(x2, NORM_EPS) * g2_ref[...]
        h_hi = h2.astype(BF16)
        h_lo = (h2 - h_hi.astype(F32)).astype(BF16)
        t = jnp.dot(h_hi, wr_ref[...], preferred_element_type=F32)
        logits = (t[:, :ROUTER_LANES] + t[:, ROUTER_LANES:] + br_ref[...]
                  + jnp.dot(h_lo, wr_ref[:, :ROUTER_LANES], preferred_element_type=F32))
        gl = jnp.where((lane >= N_EXPERTS) & (lane < N_EXPERTS + N_GROUPS), logits, neg)
        gm, gidx = first_max(gl)
        g_p = 1.0 / jnp.sum(jnp.exp(gl - gm), axis=1, keepdims=True)
        el = jnp.where((lane < N_EXPERTS) & ((lane // EXPERTS_PER_GROUP) == (gidx - N_EXPERTS)),
                       logits, neg)
        em1, idx1 = first_max(el)
        em2, idx2 = first_max(jnp.where(lane == idx1, neg, el))
        e2 = jnp.exp(em2 - em1)
        w1 = g_p / (1.0 + e2)
        w2 = g_p * e2 / (1.0 + e2)
        is1 = lane == idx1
        is2 = lane == idx2
        gates = jnp.where(is1, w1, 0.0) + jnp.where(is2, w2, 0.0)
        assigned = jnp.where(is1 | is2, 1.0, 0.0)
        cum = jnp.dot(tri, assigned.astype(BF16), preferred_element_type=F32)
        rank_t_ref[:, rows] = jnp.where(assigned > 0.0, cum - 1.0, -1.0).T
        gates_t_ref[:, rows] = gates.T
        for s in range(n_routing):
            last = (s + 1) * tb - 1
            cnt_ref[pl.ds(n_routing * i + s, 1), :] = cum[last:last + 1, :].astype(jnp.int32)

    rows = slice(0, part)
    route(rows, project(rows))


def _outproj(x, o_f, o_b, hg, o_da, hgain, w_out, g2, wr, br, tb, tm=512):
    S = x.shape[0]
    row = lambda w: pl.BlockSpec((tm, w), lambda i: (i, 0))
    col = pl.BlockSpec((ROUTER_LANES, tm), lambda i: (0, i))
    full = lambda a: pl.BlockSpec(a.shape, lambda i: (0, 0))
    return pl.pallas_call(
        functools.partial(_outproj_kernel, tb=tb),
        grid=(S // tm,),
        in_specs=[row(D_MODEL), row(HALF_W), row(HALF_W), row(HALF_W), row(HALF_W),
                  full(hgain), full(w_out), full(g2), full(wr), full(br)],
        out_specs=[row(D_MODEL), col, col,
                   pl.BlockSpec((S // tb, ROUTER_LANES), lambda i: (0, 0))],
        out_shape=[jax.ShapeDtypeStruct((S, D_MODEL), F32),
                   jax.ShapeDtypeStruct((ROUTER_LANES, S), F32),
                   jax.ShapeDtypeStruct((ROUTER_LANES, S), F32),
                   jax.ShapeDtypeStruct((S // tb, ROUTER_LANES), jnp.int32)],
        compiler_params=pltpu.CompilerParams(dimension_semantics=("arbitrary",),
                                             vmem_limit_bytes=V7X_VMEM_LIMIT),
        name="outproj_router",
    )(x, o_f, o_b, hg, o_da, hgain, w_out, g2, wr, br)


def _moe_kernel(cnt_ref, x2_hbm, rank_t_ref, gates_t_ref, g2_ref, wg_ref, wu_ref,
                wd_ref, gf_ref, out_ref, h2_ref, x2_buf, x2_sem, *, tile_rows, tb):
    b = pl.program_id(0)
    p = pl.program_id(1)
    R = tile_rows
    n_exp = wg_ref.shape[0]
    rows = out_ref.shape[0]
    n_sub = rows // tb

    def fetch(blk):
        return pltpu.make_async_copy(
            x2_hbm.at[pl.ds(pl.multiple_of(blk * rows, rows), rows), :], x2_buf, x2_sem)

    @pl.when((b == 0) & (p == 0))
    def _():
        fetch(0).start()

    @pl.when(p == 0)
    def _():
        fetch(b).wait()
        x2 = x2_buf[...]
        h2_ref[...] = (_rms(x2, NORM_EPS) * g2_ref[...]).astype(BF16)
        out_ref[...] = x2

    @pl.when((p == 1) & (b + 1 < pl.num_programs(0)))
    def _():
        fetch(b + 1).start()

    experts = [n_exp * p + i for i in range(n_exp)]
    subs = [slice(s * tb, (s + 1) * tb) for s in range(n_sub)]
    rank_row = [[rank_t_ref[pl.ds(e, 1), sub] for e in experts] for sub in subs]
    gate_row = [[gates_t_ref[pl.ds(e, 1), sub] for e in experts] for sub in subs]
    n_max = jnp.int32(0)
    for s in range(n_sub):
        for e in experts:
            n_max = jnp.maximum(n_max, cnt_ref[n_sub * b + s, e])

    def tile(t, carry):
        base = (t * R).astype(F32)
        slot_col = lax.broadcasted_iota(jnp.int32, (R, 1), 0).astype(F32) + base
        sel, gather, xg = [], [], []
        for s, sub in enumerate(subs):
            sel.append([rank_row[s][i] == slot_col for i in range(n_exp)])
            gather.append(jnp.concatenate(
                [jnp.where(m, 1.0, 0.0).astype(BF16) for m in sel[s]], axis=0))
            xg.append(jnp.dot(gather[s], h2_ref[sub, :],
                              preferred_element_type=F32).astype(BF16))
        ys = [[] for _ in subs]
        for i in range(n_exp):
            xi = jnp.concatenate([xg[s][i * R:(i + 1) * R] for s in range(n_sub)], axis=0)
            a = jnp.dot(xi, wg_ref[i], preferred_element_type=F32)
            a = a * jax.nn.sigmoid(a) * jnp.dot(xi, wu_ref[i], preferred_element_type=F32)
            y = jnp.dot(a.astype(BF16), wd_ref[i], preferred_element_type=F32)
            for s in range(n_sub):
                g = jnp.sum(jnp.where(sel[s][i], gate_row[s][i], 0.0), axis=1, keepdims=True)
                ys[s].append((y[s * R:(s + 1) * R] * g).astype(BF16))
        for s, sub in enumerate(subs):
            out_ref[sub, :] += lax.dot_general(gather[s], jnp.concatenate(ys[s], axis=0),
                                               TN_DIMS, preferred_element_type=F32)
        return carry

    lax.fori_loop(0, (n_max + R - 1) // R, tile, 0)

    @pl.when(p == pl.num_programs(1) - 1)
    def _():
        out_ref[...] = _rms(out_ref[...], NORM_EPS) * gf_ref[...]


def _moe(cnt, x2, rank_t, gates_t, g2, wg, wu, wd, gf, tb, tile_rows=32,
         experts_per_step=4, blocks_per_step=8):
    S = x2.shape[0]
    eps = experts_per_step
    rows = tb * blocks_per_step
    grid_spec = pltpu.PrefetchScalarGridSpec(
        num_scalar_prefetch=1,
        grid=(S // rows, N_EXPERTS // eps),
        in_specs=[pl.BlockSpec(memory_space=pl.ANY),
                  pl.BlockSpec((N_EXPERTS, rows), lambda b, p, c: (0, b)),
                  pl.BlockSpec((N_EXPERTS, rows), lambda b, p, c: (0, b)),
                  pl.BlockSpec((1, D_MODEL), lambda b, p, c: (0, 0)),
                  pl.BlockSpec((eps, D_MODEL, D_EXPERT), lambda b, p, c: (p, 0, 0)),
                  pl.BlockSpec((eps, D_MODEL, D_EXPERT), lambda b, p, c: (p, 0, 0)),
                  pl.BlockSpec((eps, D_EXPERT, D_MODEL), lambda b, p, c: (p, 0, 0)),
                  pl.BlockSpec((1, D_MODEL), lambda b, p, c: (0, 0))],
        out_specs=pl.BlockSpec((rows, D_MODEL), lambda b, p, c: (b, 0)),
        scratch_shapes=[pltpu.VMEM((rows, D_MODEL), BF16), pltpu.VMEM((rows, D_MODEL), F32),
                        pltpu.SemaphoreType.DMA(())])
    return pl.pallas_call(
        functools.partial(_moe_kernel, tile_rows=tile_rows, tb=tb),
        grid_spec=grid_spec,
        out_shape=jax.ShapeDtypeStruct((S, D_MODEL), F32),
        compiler_params=pltpu.CompilerParams(dimension_semantics=("arbitrary", "arbitrary"),
                                             vmem_limit_bytes=V7X_VMEM_LIMIT),
        name="moe",
    )(cnt, x2, rank_t, gates_t, g2, wg, wu, wd, gf)


def kernel(x, positions, norm1_gain, w_in, hg_lower_bounds, hg_norm_gain, diff_lambda,
           diff_subln_gain, w_out, norm2_gain, router_group_w, router_group_b,
           router_expert_w, router_expert_b, moe_w_gate, moe_w_up, moe_w_down,
           final_norm_gain):
    B, S, _ = x.shape
    assert B == 1 and norm1_gain.shape[0] == 1
    layer = 0
    xs = x.reshape(S, D_MODEL)
    pos = positions.reshape(1, S).astype(F32)

    lb_cum = jnp.cumsum(jax.nn.softmax(hg_lower_bounds.astype(F32), axis=1), axis=1)
    lb = lb_cum[:, layer + 1] - lb_cum[:, 0]
    lam_p = diff_lambda[layer].astype(F32)
    lam = (jnp.exp(jnp.sum(lam_p[0] * lam_p[1])) - jnp.exp(jnp.sum(lam_p[2] * lam_p[3]))
           + LAMBDA_INIT).reshape(1)
    lane = jnp.arange(HEAD_W)
    inv_freq = jnp.float32(ROPE_THETA) ** (-jnp.arange(0, ROT_DIM, 2, dtype=F32) / ROT_DIM)
    in_rot = (lane % QK_DIM) < ROT_DIM
    n_freq = ROT_DIM // 2
    invf = inv_freq.reshape(n_freq, 1)
    sgn = jnp.where(in_rot, jnp.where((lane % QK_DIM) < n_freq, -1.0, 1.0),
                    0.0).astype(F32).reshape(1, HEAD_W)
    place = (in_rot[None, :] & (lane[None, :] % n_freq == jnp.arange(n_freq)[:, None])).astype(F32)
    pcos = jnp.tile(place, (4, 1)).astype(BF16)
    psin = jnp.tile(place * sgn, (4, 1)).astype(BF16)
    wr = jnp.zeros((D_MODEL, ROUTER_LANES), F32)
    wr = wr.at[:, :N_EXPERTS].set(router_expert_w[layer])
    wr = wr.at[:, N_EXPERTS:N_EXPERTS + N_GROUPS].set(router_group_w[layer])
    wr_hi = wr.astype(BF16)
    wr = jnp.concatenate([wr_hi, (wr - wr_hi.astype(F32)).astype(BF16)], axis=1)
    br = jnp.zeros((1, ROUTER_LANES), F32)
    br = br.at[0, :N_EXPERTS].set(router_expert_b[layer])
    br = br.at[0, N_EXPERTS:N_EXPERTS + N_GROUPS].set(router_group_b[layer])

    hq, kf, kb, lff, lfb, hi, hg, dq, dk, dv = _inproj(
        xs, pos, norm1_gain[layer].reshape(1, -1), w_in[layer].astype(BF16), lb, invf, sgn,
        pcos, psin)
    o_f, wg, wu = _hgrn(hq, kf, lff, hi, False, (moe_w_gate[layer], moe_w_up[layer]))
    o_b, wd = _hgrn(hq, kb, lfb, hi, True, (moe_w_down[layer],))
    o_da = _attn(lam, dq, dk, dv, diff_subln_gain[layer].reshape(1, -1))
    x2, rank_t, gates_t, cnt = _outproj(
        xs, o_f, o_b, hg, o_da, hg_norm_gain[layer].reshape(1, -1), w_out[layer].astype(BF16),
        norm2_gain[layer].reshape(1, -1), wr, br, tb=MOE_TOKEN_BLOCK)
    out = _moe(cnt, x2, rank_t, gates_t, norm2_gain[layer].reshape(1, -1), wg, wu, wd,
               final_norm_gain.reshape(1, -1), tb=MOE_TOKEN_BLOCK)
    return out.reshape(B, S, D_MODEL)
```

```python
import functools
import math

import jax
import jax.numpy as jnp
from jax import lax
from jax.experimental import pallas as pl
from jax.experimental.pallas import tpu as pltpu

F32 = jnp.float32
BF16 = jnp.bfloat16

D_MODEL = 1024
HEAD_W = 128
N_HEADS = 4
HALF_W = N_HEADS * HEAD_W
QK_DIM = 64
ROT_DIM = 16
ROPE_THETA = 500000.0
QUERY_GROUP = 512
KV_UNROLL = 4
VT_ROWS = HEAD_W + 16
HG_CHUNK = 128
HG_UNROLL = 8
N_GROUPS = 4
EXPERTS_PER_GROUP = 8
N_EXPERTS = 32
D_EXPERT = 512
NORM_EPS = 1e-6
SUBLN_EPS = 1e-5
LAMBDA_INIT = 0.8 - 0.6 * math.exp(-0.3 * 0)
LOG2E = 1.4426950408889634
ROUTER_LANES = 128
MOE_TOKEN_BLOCK = 256
V7X_VMEM_LIMIT = 58 * 1024 * 1024

NT_DIMS = (((1,), (1,)), ((), ()))
TN_DIMS = (((0,), (0,)), ((), ()))


def _rms(x, eps):
    return x * lax.rsqrt(jnp.mean(x * x, axis=-1, keepdims=True) + eps)


def _inproj_kernel(x_ref, pos_ref, g1_ref, w_ref, lb_ref, invf_ref, sgn_ref, pcos_ref, psin_ref,
                   hq_ref, kf_ref, kb_ref, lff_ref, lfb_ref, hi_ref, hg_ref,
                   dq_ref, dk_ref, dv_ref):
    x = x_ref[...]
    h = (_rms(x, NORM_EPS) * g1_ref[...]).astype(BF16)

    def proj(i):
        return jnp.dot(h, w_ref[:, i * HALF_W:(i + 1) * HALF_W], preferred_element_type=F32)

    hq_ref[...] = proj(0)
    for i, (k_ref, lf_ref) in enumerate(((kf_ref, lff_ref), (kb_ref, lfb_ref))):
        z = proj(1 + i)
        lb = lb_ref[i:i + 1, :]
        sg = jax.nn.sigmoid(z)
        lf_ref[...] = jnp.log(lb + (1.0 - lb) * sg)
        k_ref[...] = (1.0 - lb) * (1.0 - sg)
    hi_ref[...] = proj(3)
    hg_ref[...] = proj(4)

    ang_t = invf_ref[...] * pos_ref[...]

    def spread(t, place_ref):
        hi = t.astype(BF16)
        r1 = t - hi.astype(F32)
        mid = r1.astype(BF16)
        lo = (r1 - mid.astype(F32)).astype(BF16)
        pieces = jnp.concatenate([hi, mid, lo, jnp.zeros_like(hi)], axis=0)
        return lax.dot_general(pieces, place_ref[...], TN_DIMS, preferred_element_type=F32)

    sgn = sgn_ref[...]
    cos = jnp.concatenate([spread(jnp.cos(ang_t), pcos_ref) + jnp.where(sgn == 0.0, 1.0, 0.0)]
                          * N_HEADS, axis=1)
    sin = jnp.concatenate([spread(jnp.sin(ang_t), psin_ref)] * N_HEADS, axis=1)
    take_hi = jnp.concatenate([jnp.broadcast_to(sgn, (x.shape[0], HEAD_W))] * N_HEADS,
                              axis=1) < 0.0

    def rotary(t):
        partner = jnp.where(take_hi, pltpu.roll(t, HALF_W - ROT_DIM // 2, axis=1),
                            pltpu.roll(t, ROT_DIM // 2, axis=1))
        return t * cos + partner * sin

    dq_ref[...] = (rotary(proj(5)) * (QK_DIM ** -0.5 * LOG2E)).astype(BF16)
    dk_ref[...] = rotary(proj(6)).astype(BF16)
    v = proj(7)
    sub = lax.broadcasted_iota(jnp.int32, (VT_ROWS - HEAD_W, v.shape[0]), 0)
    ones_rows = jnp.where(sub == 0, 1.0, 0.0)
    pieces = []
    for hd in range(N_HEADS):
        pieces += [v[:, hd * HEAD_W:(hd + 1) * HEAD_W].T, ones_rows]
    dv_ref[...] = jnp.concatenate(pieces, axis=0).astype(BF16)


def _inproj(x, pos, g1, w_in, lb, invf, sgn, pcos, psin, tm=1024):
    S = x.shape[0]
    row = lambda w: pl.BlockSpec((tm, w), lambda i: (i, 0))
    full = lambda a: pl.BlockSpec(a.shape, lambda i: (0, 0))
    f32o = jax.ShapeDtypeStruct((S, HALF_W), F32)
    bf16o = jax.ShapeDtypeStruct((S, HALF_W), BF16)
    return pl.pallas_call(
        _inproj_kernel,
        grid=(S // tm,),
        in_specs=[row(D_MODEL), pl.BlockSpec((1, tm), lambda i: (0, i)), full(g1),
                  pl.BlockSpec(w_in.shape, lambda i: (0, 0), pipeline_mode=pl.Buffered(1)),
                  full(lb), full(invf), full(sgn), full(pcos), full(psin)],
        out_specs=[row(HALF_W)] * 9 + [pl.BlockSpec((N_HEADS * VT_ROWS, tm), lambda i: (0, i))],
        out_shape=[f32o] * 7 + [bf16o, bf16o,
                                jax.ShapeDtypeStruct((N_HEADS * VT_ROWS, S), BF16)],
        compiler_params=pltpu.CompilerParams(dimension_semantics=("arbitrary",),
                                             vmem_limit_bytes=V7X_VMEM_LIMIT),
        name="inproj",
    )(x, pos, g1, w_in, lb, invf, sgn, pcos, psin)


def _chunk_cumsum(x, rev):
    n = x.shape[0]
    row = lax.broadcasted_iota(jnp.int32, x.shape, 0)
    d = 1
    while d < n:
        if rev:
            x = x + jnp.where(row < n - d, pltpu.roll(x, n - d, axis=0), 0.0)
        else:
            x = x + jnp.where(row >= d, pltpu.roll(x, d, axis=0), 0.0)
        d *= 2
    return x


def _pivot_rows(b, blk, rev):
    n = b.shape[0]
    half = blk // 2
    groups = []
    for g0 in range(0, n, 8):
        def piv(r):
            base = (r // blk) * blk
            return base + half if rev else base + half - 1
        if blk >= 8:
            p = piv(g0)
            groups.append(jnp.broadcast_to(b[p:p + 1, :], (8, b.shape[1])))
        else:
            sub = lax.broadcasted_iota(jnp.int32, (8, b.shape[1]), 0)
            acc = None
            for s0 in range(0, 8, blk):
                p = piv(g0 + s0)
                rowv = jnp.broadcast_to(b[p:p + 1, :], (8, b.shape[1]))
                acc = rowv if acc is None else jnp.where(sub >= s0, rowv, acc)
            groups.append(acc)
    return jnp.concatenate(groups, axis=0)


def _hgrn_kernel(q_ref, k_ref, lf_ref, v_ref, *rest, rev, n_chunks, n_cast):
    w_refs, o_ref, wb_refs, st_ref = (rest[:n_cast], rest[n_cast], rest[n_cast + 1:-1], rest[-1])
    for w_ref, wb_ref in zip(w_refs, wb_refs):
        wb_ref[...] = w_ref[...].astype(BF16)

    @pl.when(pl.program_id(1) == 0)
    def _():
        st_ref[...] = jnp.zeros_like(st_ref)

    C, W = HG_CHUNK, HEAD_W
    rt = lax.broadcasted_iota(jnp.int32, (C, C), 0)
    ct = lax.broadcasted_iota(jnp.int32, (C, C), 1)
    rowi = lax.broadcasted_iota(jnp.int32, (C, 2 * W), 0)
    eye = jnp.where(rt == ct, 1.0, 0.0)
    levels = []
    blk = C
    while blk >= 2:
        half = blk // 2
        q_side = (lambda r: (r % blk) < half) if rev else (lambda r: (r % blk) >= half)
        pair = ((rt // blk) == (ct // blk)) & q_side(rt) & jnp.logical_not(q_side(ct))
        levels.append((blk, jnp.where(q_side(rowi), LOG2E, -LOG2E), jnp.where(pair, 1.0, 0.0)))
        blk = half

    def stack(a):
        return jnp.concatenate([a[:, :W], a[:, W:]], axis=0)

    def chunk(ci, st):
        c = (n_chunks - 1 - ci) if rev else ci
        sl = pl.ds(pl.multiple_of(c * C, C), C)
        q = q_ref[sl, :]
        k = k_ref[sl, :]
        vb = v_ref[sl, :].astype(BF16)
        b = _chunk_cumsum(lf_ref[sl, :], rev)
        tot = b[0:1, :] if rev else b[C - 1:C, :]

        o = lax.dot_general(stack((q * jnp.exp(b)).astype(BF16)), st.astype(BF16), NT_DIMS,
                            preferred_element_type=F32)
        kdec = (k * jnp.exp(tot - b)).astype(BF16)
        upd = lax.dot_general(vb, kdec, TN_DIMS, preferred_element_type=F32)
        dec = jnp.exp(tot)
        st_next = jnp.concatenate([st[:W] * dec[:, :W] + upd[:W, :W],
                                   st[W:] * dec[:, W:] + upd[W:, W:]], axis=0)

        qk = q * k
        scores = [eye * jnp.sum(qk[:, :W], axis=1, keepdims=True),
                  eye * jnp.sum(qk[:, W:], axis=1, keepdims=True)]
        for blk, side, pair in levels:
            w = jnp.exp2((b - _pivot_rows(b, blk, rev)) * side)
            z = stack((jnp.where(side > 0.0, q, k) * w).astype(BF16))
            g = lax.dot_general(z, z, NT_DIMS, preferred_element_type=F32)
            scores = [scores[0] + pair * g[:C, :C], scores[1] + pair * g[C:, C:]]
        intra = jnp.dot(jnp.concatenate(scores, axis=0).astype(BF16), vb,
                        preferred_element_type=F32)
        o_ref[sl, :] = jnp.concatenate([o[:C, :W] + intra[:C, :W], o[C:, W:] + intra[C:, W:]],
                                       axis=1)
        return st_next

    st_ref[...] = lax.fori_loop(0, n_chunks, chunk, st_ref[...], unroll=HG_UNROLL)


def _hgrn(q, k, lf, v, rev, weights, rb=2048):
    S = q.shape[0]
    nb = S // rb
    n_pairs = N_HEADS // 2
    steps = n_pairs * nb
    idx = (lambda h, j: (nb - 1 - j, h)) if rev else (lambda h, j: (j, h))
    spec = pl.BlockSpec((rb, 2 * HEAD_W), idx)
    w_specs = []
    for w in weights:
        n_exp, rows, cols = w.shape
        per_step = n_exp // steps
        assert per_step * steps == n_exp
        w_specs.append(pl.BlockSpec((per_step, rows, cols), lambda h, j: (h * nb + j, 0, 0)))
    return pl.pallas_call(
        functools.partial(_hgrn_kernel, rev=rev, n_chunks=rb // HG_CHUNK, n_cast=len(weights)),
        grid=(n_pairs, nb),
        in_specs=[spec] * 4 + w_specs,
        out_specs=[spec] + w_specs,
        out_shape=[jax.ShapeDtypeStruct((S, HALF_W), F32)]
        + [jax.ShapeDtypeStruct(w.shape, BF16) for w in weights],
        scratch_shapes=[pltpu.VMEM((2 * HEAD_W, HEAD_W), F32)],
        compiler_params=pltpu.CompilerParams(dimension_semantics=("arbitrary", "arbitrary"),
                                             vmem_limit_bytes=V7X_VMEM_LIMIT),
        name="hgrn_bwd" if rev else "hgrn_fwd",
    )(q, k, lf, v, *weights)


def _attn_kernel(lam_ref, q_ref, k_ref, vt_ref, gain_ref, o_ref, sa_ref, sb_ref, m_ref, acc_ref,
                 *, tk):
    n_kv = k_ref.shape[0] // tk
    tq = q_ref.shape[0]
    assert n_kv % KV_UNROLL == 0 and KV_UNROLL % 2 == 0 and tq % QUERY_GROUP == 0
    q = q_ref[...]
    lane = lax.broadcasted_iota(jnp.int32, q.shape, 1)
    qm = (jnp.where(lane < QK_DIM, q, jnp.zeros_like(q)),
          jnp.where(lane >= QK_DIM, q, jnp.zeros_like(q)))
    m_ref[...] = jnp.full(m_ref.shape, -jnp.inf, F32)
    acc_ref[...] = jnp.zeros_like(acc_ref)

    def chunk(j):
        return pl.ds(pl.multiple_of(j * tk, tk), tk)

    def softmax_pv(c, cols, s, vt):
        m_old = m_ref[c, :, cols]
        m_new = jnp.maximum(m_old, jnp.max(s, axis=0, keepdims=True))
        p = jnp.exp2(s - m_new).astype(BF16)
        acc_ref[c, :, cols] = jnp.exp2(m_old - m_new) * acc_ref[c, :, cols] + jnp.dot(
            vt, p, preferred_element_type=F32)
        m_ref[c, :, cols] = m_new

    def step(js, dst_ref, jp, src_ref):
        kc = None if dst_ref is None else k_ref[chunk(js), :]
        vt = None if src_ref is None else vt_ref[:, chunk(jp)]
        for c in range(2):
            for g in range(0, tq, QUERY_GROUP):
                cols = slice(g, g + QUERY_GROUP)
                if dst_ref is not None:
                    dst_ref[c, :, cols] = lax.dot_general(kc, qm[c][cols, :], NT_DIMS,
                                                          preferred_element_type=F32)
                if src_ref is not None:
                    softmax_pv(c, cols, src_ref[c, :, cols], vt)

    bufs = (sa_ref, sb_ref)
    step(0, sa_ref, None, None)

    def body(i, carry):
        for u in range(KV_UNROLL):
            j = KV_UNROLL * i + u
            step(j + 1, bufs[(u + 1) % 2], j, bufs[u % 2])
        return carry

    lax.fori_loop(0, n_kv // KV_UNROLL - 1, body, 0)
    for j in range(n_kv - KV_UNROLL, n_kv - 1):
        step(j + 1, bufs[(j + 1) % 2], j, bufs[j % 2])
    step(None, None, n_kv - 1, bufs[(n_kv - 1) % 2])

    outs = []
    for c in range(2):
        a = acc_ref[c]
        outs.append(a[:HEAD_W, :] / a[HEAD_W:HEAD_W + 1, :])
    o = (outs[0] - lam_ref[0] * outs[1]).T
    o_ref[...] = _rms(o, SUBLN_EPS) * gain_ref[...] * (1.0 - LAMBDA_INIT)


def _attn(lam, dq, dk, dvt, gain, tq=2048, tk=512):
    S = dq.shape[0]
    return pl.pallas_call(
        functools.partial(_attn_kernel, tk=tk),
        grid=(N_HEADS, S // tq),
        in_specs=[pl.BlockSpec(memory_space=pltpu.SMEM),
                  pl.BlockSpec((tq, HEAD_W), lambda h, i: (i, h)),
                  pl.BlockSpec((S, HEAD_W), lambda h, i: (0, h)),
                  pl.BlockSpec((VT_ROWS, S), lambda h, i: (h, 0)),
                  pl.BlockSpec((1, HEAD_W), lambda h, i: (0, 0))],
        out_specs=pl.BlockSpec((tq, HEAD_W), lambda h, i: (i, h)),
        out_shape=jax.ShapeDtypeStruct((S, HALF_W), F32),
        scratch_shapes=[pltpu.VMEM((2, tk, tq), F32), pltpu.VMEM((2, tk, tq), F32),
                        pltpu.VMEM((2, 1, tq), F32), pltpu.VMEM((2, VT_ROWS, tq), F32)],
        compiler_params=pltpu.CompilerParams(dimension_semantics=("arbitrary", "arbitrary"),
                                             vmem_limit_bytes=V7X_VMEM_LIMIT),
        name="diff_attn",
    )(lam, dq, dk, dvt, gain)


def _outproj_kernel(x_ref, of_ref, ob_ref, hg_ref, oda_ref, hgain_ref, wo_ref, g2_ref,
                    wr_ref, br_ref, x2_ref, rank_t_ref, gates_t_ref, cnt_ref,
                    *, tb):
    i = pl.program_id(0)
    part = x_ref.shape[0]
    n_routing = part // tb

    def project(rows):
        o = of_ref[rows, :] + ob_ref[rows, :]
        gate = hg_ref[rows, :]
        gate = gate * jax.nn.sigmoid(gate)
        parts = []
        for hd in range(N_HEADS):
            sl = slice(hd * HEAD_W, (hd + 1) * HEAD_W)
            parts.append(_rms(o[:, sl], NORM_EPS) * hgain_ref[...] * gate[:, sl])
        parts.append(oda_ref[rows, :])
        mixed = jnp.concatenate(parts, axis=1).astype(BF16)
        x2 = x_ref[rows, :] + jnp.dot(mixed, wo_ref[...], preferred_element_type=F32)
        x2_ref[rows, :] = x2
        return x2

    lane = lax.broadcasted_iota(jnp.int32, (part, ROUTER_LANES), 1)
    neg = -jnp.inf
    rt = lax.broadcasted_iota(jnp.int32, (part, part), 0)
    ct = lax.broadcasted_iota(jnp.int32, (part, part), 1)
    tri = jnp.where((rt >= ct) & (rt // tb == ct // tb), 1.0, 0.0).astype(BF16)

    def first_max(vals):
        mx = jnp.max(vals, axis=1, keepdims=True)
        idx = jnp.min(jnp.where(vals == mx, lane, ROUTER_LANES), axis=1, keepdims=True)
        return mx, idx

    def route(rows, x2):
        h2 = _rms(x2, NORM_EPS) * g2_ref[...]
        h_hi = h2.astype(BF16)
        h_lo = (h2 - h_hi.astype(F32)).astype(BF16)
        t = jnp.dot(h_hi, wr_ref[...], preferred_element_type=F32)
        logits = (t[:, :ROUTER_LANES] + t[:, ROUTER_LANES:] + br_ref[...]
                  + jnp.dot(h_lo, wr_ref[:, :ROUTER_LANES], preferred_element_type=F32))
        gl = jnp.where((lane >= N_EXPERTS) & (lane < N_EXPERTS + N_GROUPS), logits, neg)
        gm, gidx = first_max(gl)
        g_p = 1.0 / jnp.sum(jnp.exp(gl - gm), axis=1, keepdims=True)
        el = jnp.where((lane < N_EXPERTS) & ((lane // EXPERTS_PER_GROUP) == (gidx - N_EXPERTS)),
                       logits, neg)
        em1, idx1 = first_max(el)
        em2, idx2 = first_max(jnp.where(lane == idx1, neg, el))
        e2 = jnp.exp(em2 - em1)
        w1 = g_p / (1.0 + e2)
        w2 = g_p * e2 / (1.0 + e2)
        is1 = lane == idx1
        is2 = lane == idx2
        gates = jnp.where(is1, w1, 0.0) + jnp.where(is2, w2, 0.0)
        assigned = jnp.where(is1 | is2, 1.0, 0.0)
        cum = jnp.dot(tri, assigned.astype(BF16), preferred_element_type=F32)
        rank_t_ref[:, rows] = jnp.where(assigned > 0.0, cum - 1.0, -1.0).T
        gates_t_ref[:, rows] = gates.T
        for s in range(n_routing):
            last = (s + 1) * tb - 1
            cnt_ref[pl.ds(n_routing * i + s, 1), :] = cum[last:last + 1, :].astype(jnp.int32)

    rows = slice(0, part)
    route(rows, project(rows))


def _outproj(x, o_f, o_b, hg, o_da, hgain, w_out, g2, wr, br, tb, tm=1024):
    S = x.shape[0]
    row = lambda w: pl.BlockSpec((tm, w), lambda i: (i, 0))
    col = pl.BlockSpec((ROUTER_LANES, tm), lambda i: (0, i))
    full = lambda a: pl.BlockSpec(a.shape, lambda i: (0, 0))
    return pl.pallas_call(
        functools.partial(_outproj_kernel, tb=tb),
        grid=(S // tm,),
        in_specs=[row(D_MODEL), row(HALF_W), row(HALF_W), row(HALF_W), row(HALF_W),
                  full(hgain), full(w_out), full(g2), full(wr), full(br)],
        out_specs=[row(D_MODEL), col, col,
                   pl.BlockSpec((S // tb, ROUTER_LANES), lambda i: (0, 0))],
        out_shape=[jax.ShapeDtypeStruct((S, D_MODEL), F32),
                   jax.ShapeDtypeStruct((ROUTER_LANES, S), F32),
                   jax.ShapeDtypeStruct((ROUTER_LANES, S), F32),
                   jax.ShapeDtypeStruct((S // tb, ROUTER_LANES), jnp.int32)],
        compiler_params=pltpu.CompilerParams(dimension_semantics=("arbitrary",),
                                             vmem_limit_bytes=V7X_VMEM_LIMIT),
        name="outproj_router",
    )(x, o_f, o_b, hg, o_da, hgain, w_out, g2, wr, br)


def _moe_kernel(cnt_ref, x2_hbm, rank_t_ref, gates_t_ref, g2_ref, wg_ref, wu_ref,
                wd_ref, gf_ref, out_ref, h2_ref, x2_buf, x2_sem, *, tile_rows, tb):
    b = pl.program_id(0)
    p = pl.program_id(1)
    R = tile_rows
    n_exp = wg_ref.shape[0]
    rows = out_ref.shape[0]
    n_sub = rows // tb

    def fetch(blk):
        return pltpu.make_async_copy(
            x2_hbm.at[pl.ds(pl.multiple_of(blk * rows, rows), rows), :], x2_buf, x2_sem)

    @pl.when((b == 0) & (p == 0))
    def _():
        fetch(0).start()

    @pl.when(p == 0)
    def _():
        fetch(b).wait()
        x2 = x2_buf[...]
        h2_ref[...] = (_rms(x2, NORM_EPS) * g2_ref[...]).astype(BF16)
        out_ref[...] = x2

    @pl.when((p == 1) & (b + 1 < pl.num_programs(0)))
    def _():
        fetch(b + 1).start()

    experts = [n_exp * p + i for i in range(n_exp)]
    subs = [slice(s * tb, (s + 1) * tb) for s in range(n_sub)]
    rank_row = [[rank_t_ref[pl.ds(e, 1), sub] for e in experts] for sub in subs]
    gate_row = [[gates_t_ref[pl.ds(e, 1), sub] for e in experts] for sub in subs]
    n_max = jnp.int32(0)
    for s in range(n_sub):
        for e in experts:
            n_max = jnp.maximum(n_max, cnt_ref[n_sub * b + s, e])

    def tile(t, carry):
        base = (t * R).astype(F32)
        slot_col = lax.broadcasted_iota(jnp.int32, (R, 1), 0).astype(F32) + base
        sel, gather, xg = [], [], []
        for s, sub in enumerate(subs):
            sel.append([rank_row[s][i] == slot_col for i in range(n_exp)])
            gather.append(jnp.concatenate(
                [jnp.where(m, 1.0, 0.0).astype(BF16) for m in sel[s]], axis=0))
            xg.append(jnp.dot(gather[s], h2_ref[sub, :],
                              preferred_element_type=F32).astype(BF16))
        ys = [[] for _ in subs]
        for i in range(n_exp):
            xi = jnp.concatenate([xg[s][i * R:(i + 1) * R] for s in range(n_sub)], axis=0)
            a = jnp.dot(xi, wg_ref[i], preferred_element_type=F32)
            a = a * jax.nn.sigmoid(a) * jnp.dot(xi, wu_ref[i], preferred_element_type=F32)
            y = jnp.dot(a.astype(BF16), wd_ref[i], preferred_element_type=F32)
            for s in range(n_sub):
                g = jnp.sum(jnp.where(sel[s][i], gate_row[s][i], 0.0), axis=1, keepdims=True)
                ys[s].append((y[s * R:(s + 1) * R] * g).astype(BF16))
        for s, sub in enumerate(subs):
            out_ref[sub, :] += lax.dot_general(gather[s], jnp.concatenate(ys[s], axis=0),
                                               TN_DIMS, preferred_element_type=F32)
        return carry

    lax.fori_loop(0, (n_max + R - 1) // R, tile, 0)

    @pl.when(p == pl.num_programs(1) - 1)
    def _():
        out_ref[...] = _rms(out_ref[...], NORM_EPS) * gf_ref[...]


def _moe(cnt, x2, rank_t, gates_t, g2, wg, wu, wd, gf, tb, tile_rows=32,
         experts_per_step=4, blocks_per_step=8):
    S = x2.shape[0]
    eps = experts_per_step
    rows = tb * blocks_per_step
    grid_spec = pltpu.PrefetchScalarGridSpec(
        num_scalar_prefetch=1,
        grid=(S // rows, N_EXPERTS // eps),
        in_specs=[pl.BlockSpec(memory_space=pl.ANY),
                  pl.BlockSpec((N_EXPERTS, rows), lambda b, p, c: (0, b)),
                  pl.BlockSpec((N_EXPERTS, rows), lambda b, p, c: (0, b)),
                  pl.BlockSpec((1, D_MODEL), lambda b, p, c: (0, 0)),
                  pl.BlockSpec((eps, D_MODEL, D_EXPERT), lambda b, p, c: (p, 0, 0)),
                  pl.BlockSpec((eps, D_MODEL, D_EXPERT), lambda b, p, c: (p, 0, 0)),
                  pl.BlockSpec((eps, D_EXPERT, D_MODEL), lambda b, p, c: (p, 0, 0)),
                  pl.BlockSpec((1, D_MODEL), lambda b, p, c: (0, 0))],
        out_specs=pl.BlockSpec((rows, D_MODEL), lambda b, p, c: (b, 0)),
        scratch_shapes=[pltpu.VMEM((rows, D_MODEL), BF16), pltpu.VMEM((rows, D_MODEL), F32),
                        pltpu.SemaphoreType.DMA(())])
    return pl.pallas_call(
        functools.partial(_moe_kernel, tile_rows=tile_rows, tb=tb),
        grid_spec=grid_spec,
        out_shape=jax.ShapeDtypeStruct((S, D_MODEL), F32),
        compiler_params=pltpu.CompilerParams(dimension_semantics=("arbitrary", "arbitrary"),
                                             vmem_limit_bytes=V7X_VMEM_LIMIT),
        name="moe",
    )(cnt, x2, rank_t, gates_t, g2, wg, wu, wd, gf)


def kernel(x, positions, norm1_gain, w_in, hg_lower_bounds, hg_norm_gain, diff_lambda,
           diff_subln_gain, w_out, norm2_gain, router_group_w, router_group_b,
           router_expert_w, router_expert_b, moe_w_gate, moe_w_up, moe_w_down,
           final_norm_gain):
    B, S, _ = x.shape
    assert B == 1 and norm1_gain.shape[0] == 1
    layer = 0
    xs = x.reshape(S, D_MODEL)
    pos = positions.reshape(1, S).astype(F32)

    lb_cum = jnp.cumsum(jax.nn.softmax(hg_lower_bounds.astype(F32), axis=1), axis=1)
    lb = lb_cum[:, layer + 1] - lb_cum[:, 0]
    lam_p = diff_lambda[layer].astype(F32)
    lam = (jnp.exp(jnp.sum(lam_p[0] * lam_p[1])) - jnp.exp(jnp.sum(lam_p[2] * lam_p[3]))
           + LAMBDA_INIT).reshape(1)
    lane = jnp.arange(HEAD_W)
    inv_freq = jnp.float32(ROPE_THETA) ** (-jnp.arange(0, ROT_DIM, 2, dtype=F32) / ROT_DIM)
    in_rot = (lane % QK_DIM) < ROT_DIM
    n_freq = ROT_DIM // 2
    invf = inv_freq.reshape(n_freq, 1)
    sgn = jnp.where(in_rot, jnp.where((lane % QK_DIM) < n_freq, -1.0, 1.0),
                    0.0).astype(F32).reshape(1, HEAD_W)
    place = (in_rot[None, :] & (lane[None, :] % n_freq == jnp.arange(n_freq)[:, None])).astype(F32)
    pcos = jnp.tile(place, (4, 1)).astype(BF16)
    psin = jnp.tile(place * sgn, (4, 1)).astype(BF16)
    wr = jnp.zeros((D_MODEL, ROUTER_LANES), F32)
    wr = wr.at[:, :N_EXPERTS].set(router_expert_w[layer])
    wr = wr.at[:, N_EXPERTS:N_EXPERTS + N_GROUPS].set(router_group_w[layer])
    wr_hi = wr.astype(BF16)
    wr = jnp.concatenate([wr_hi, (wr - wr_hi.astype(F32)).astype(BF16)], axis=1)
    br = jnp.zeros((1, ROUTER_LANES), F32)
    br = br.at[0, :N_EXPERTS].set(router_expert_b[layer])
    br = br.at[0, N_EXPERTS:N_EXPERTS + N_GROUPS].set(router_group_b[layer])

    hq, kf, kb, lff, lfb, hi, hg, dq, dk, dv = _inproj(
        xs, pos, norm1_gain[layer].reshape(1, -1), w_in[layer].astype(BF16), lb, invf, sgn,
        pcos, psin)
    o_f, wg, wu = _hgrn(hq, kf, lff, hi, False, (moe_w_gate[layer], moe_w_up[layer]))
    o_b, wd = _hgrn(hq, kb, lfb, hi, True, (moe_w_down[layer],))
    o_da = _attn(lam, dq, dk, dv, diff_subln_gain[layer].reshape(1, -1))
    x2, rank_t, gates_t, cnt = _outproj(
        xs, o_f, o_b, hg, o_da, hg_norm_gain[layer].reshape(1, -1), w_out[layer].astype(BF16),
        norm2_gain[layer].reshape(1, -1), wr, br, tb=MOE_TOKEN_BLOCK)
    out = _moe(cnt, x2, rank_t, gates_t, norm2_gain[layer].reshape(1, -1), wg, wu, wd,
               final_norm_gain.reshape(1, -1), tb=MOE_TOKEN_BLOCK)
    return out.reshape(B, S, D_MODEL)
```

```python
import functools
import math

import jax
import jax.numpy as jnp
from jax import lax
from jax.experimental import pallas as pl
from jax.experimental.pallas import tpu as pltpu

F32 = jnp.float32
BF16 = jnp.bfloat16

D_MODEL = 1024
HEAD_W = 128
N_HEADS = 4
HALF_W = N_HEADS * HEAD_W
QK_DIM = 64
ROT_DIM = 16
ROPE_THETA = 500000.0
QUERY_GROUP = 512
KV_UNROLL = 4
VT_ROWS = HEAD_W + 16
HG_CHUNK = 128
HG_UNROLL = 8
N_GROUPS = 4
EXPERTS_PER_GROUP = 8
N_EXPERTS = 32
D_EXPERT = 512
NORM_EPS = 1e-6
SUBLN_EPS = 1e-5
LAMBDA_INIT = 0.8 - 0.6 * math.exp(-0.3 * 0)
LOG2E = 1.4426950408889634
ROUTER_LANES = 128
MOE_TOKEN_BLOCK = 256
V7X_VMEM_LIMIT = 58 * 1024 * 1024

NT_DIMS = (((1,), (1,)), ((), ()))
TN_DIMS = (((0,), (0,)), ((), ()))


def _rms(x, eps):
    return x * lax.rsqrt(jnp.mean(x * x, axis=-1, keepdims=True) + eps)


def _inproj_kernel(x_ref, pos_ref, g1_ref, w_ref, lb_ref, invf_ref, sgn_ref, pcos_ref, psin_ref,
                   hq_ref, kf_ref, kb_ref, lff_ref, lfb_ref, hi_ref, hg_ref,
                   dq_ref, dk_ref, dv_ref):
    x = x_ref[...]
    h = (_rms(x, NORM_EPS) * g1_ref[...]).astype(BF16)

    def proj(i):
        return jnp.dot(h, w_ref[:, i * HALF_W:(i + 1) * HALF_W], preferred_element_type=F32)

    hq_ref[...] = proj(0)
    for i, (k_ref, lf_ref) in enumerate(((kf_ref, lff_ref), (kb_ref, lfb_ref))):
        z = proj(1 + i)
        lb = lb_ref[i:i + 1, :]
        sg = jax.nn.sigmoid(z)
        lf_ref[...] = jnp.log(lb + (1.0 - lb) * sg)
        k_ref[...] = (1.0 - lb) * (1.0 - sg)
    hi_ref[...] = proj(3)
    hg_ref[...] = proj(4)

    ang_t = invf_ref[...] * pos_ref[...]

    def spread(t, place_ref):
        hi = t.astype(BF16)
        r1 = t - hi.astype(F32)
        mid = r1.astype(BF16)
        lo = (r1 - mid.astype(F32)).astype(BF16)
        pieces = jnp.concatenate([hi, mid, lo, jnp.zeros_like(hi)], axis=0)
        return lax.dot_general(pieces, place_ref[...], TN_DIMS, preferred_element_type=F32)

    sgn = sgn_ref[...]
    cos = jnp.concatenate([spread(jnp.cos(ang_t), pcos_ref) + jnp.where(sgn == 0.0, 1.0, 0.0)]
                          * N_HEADS, axis=1)
    sin = jnp.concatenate([spread(jnp.sin(ang_t), psin_ref)] * N_HEADS, axis=1)
    take_hi = jnp.concatenate([jnp.broadcast_to(sgn, (x.shape[0], HEAD_W))] * N_HEADS,
                              axis=1) < 0.0

    def rotary(t):
        partner = jnp.where(take_hi, pltpu.roll(t, HALF_W - ROT_DIM // 2, axis=1),
                            pltpu.roll(t, ROT_DIM // 2, axis=1))
        return t * cos + partner * sin

    dq_ref[...] = (rotary(proj(5)) * (QK_DIM ** -0.5 * LOG2E)).astype(BF16)
    dk_ref[...] = rotary(proj(6)).astype(BF16)
    v = proj(7)
    sub = lax.broadcasted_iota(jnp.int32, (VT_ROWS - HEAD_W, v.shape[0]), 0)
    ones_rows = jnp.where(sub == 0, 1.0, 0.0)
    pieces = []
    for hd in range(N_HEADS):
        pieces += [v[:, hd * HEAD_W:(hd + 1) * HEAD_W].T, ones_rows]
    dv_ref[...] = jnp.concatenate(pieces, axis=0).astype(BF16)


def _inproj(x, pos, g1, w_in, lb, invf, sgn, pcos, psin, tm=1024):
    S = x.shape[0]
    row = lambda w: pl.BlockSpec((tm, w), lambda i: (i, 0))
    full = lambda a: pl.BlockSpec(a.shape, lambda i: (0, 0))
    f32o = jax.ShapeDtypeStruct((S, HALF_W), F32)
    bf16o = jax.ShapeDtypeStruct((S, HALF_W), BF16)
    return pl.pallas_call(
        _inproj_kernel,
        grid=(S // tm,),
        in_specs=[row(D_MODEL), pl.BlockSpec((1, tm), lambda i: (0, i)), full(g1),
                  pl.BlockSpec(w_in.shape, lambda i: (0, 0), pipeline_mode=pl.Buffered(1)),
                  full(lb), full(invf), full(sgn), full(pcos), full(psin)],
        out_specs=[row(HALF_W)] * 9 + [pl.BlockSpec((N_HEADS * VT_ROWS, tm), lambda i: (0, i))],
        out_shape=[f32o] * 7 + [bf16o, bf16o,
                                jax.ShapeDtypeStruct((N_HEADS * VT_ROWS, S), BF16)],
        compiler_params=pltpu.CompilerParams(
            dimension_semantics=("arbitrary",), vmem_limit_bytes=V7X_VMEM_LIMIT,
            allow_input_fusion=[False, True, False, True, False, False, False, False, False]),
        name="inproj",
    )(x, pos, g1, w_in, lb, invf, sgn, pcos, psin)


def _chunk_cumsum(x, rev):
    n = x.shape[0]
    row = lax.broadcasted_iota(jnp.int32, x.shape, 0)
    d = 1
    while d < n:
        if rev:
            x = x + jnp.where(row < n - d, pltpu.roll(x, n - d, axis=0), 0.0)
        else:
            x = x + jnp.where(row >= d, pltpu.roll(x, d, axis=0), 0.0)
        d *= 2
    return x


def _pivot_rows(b, blk, rev):
    n = b.shape[0]
    half = blk // 2
    groups = []
    for g0 in range(0, n, 8):
        def piv(r):
            base = (r // blk) * blk
            return base + half if rev else base + half - 1
        if blk >= 8:
            p = piv(g0)
            groups.append(jnp.broadcast_to(b[p:p + 1, :], (8, b.shape[1])))
        else:
            sub = lax.broadcasted_iota(jnp.int32, (8, b.shape[1]), 0)
            acc = None
            for s0 in range(0, 8, blk):
                p = piv(g0 + s0)
                rowv = jnp.broadcast_to(b[p:p + 1, :], (8, b.shape[1]))
                acc = rowv if acc is None else jnp.where(sub >= s0, rowv, acc)
            groups.append(acc)
    return jnp.concatenate(groups, axis=0)


def _hgrn_kernel(q_ref, k_ref, lf_ref, v_ref, *rest, rev, n_chunks, n_cast):
    w_refs, o_ref, wb_refs, st_ref = (rest[:n_cast], rest[n_cast], rest[n_cast + 1:-1], rest[-1])
    for w_ref, wb_ref in zip(w_refs, wb_refs):
        wb_ref[...] = w_ref[...].astype(BF16)

    @pl.when(pl.program_id(1) == 0)
    def _():
        st_ref[...] = jnp.zeros_like(st_ref)

    C, W = HG_CHUNK, HEAD_W
    rt = lax.broadcasted_iota(jnp.int32, (C, C), 0)
    ct = lax.broadcasted_iota(jnp.int32, (C, C), 1)
    rowi = lax.broadcasted_iota(jnp.int32, (C, 2 * W), 0)
    eye = jnp.where(rt == ct, 1.0, 0.0)
    levels = []
    blk = C
    while blk >= 2:
        half = blk // 2
        q_side = (lambda r: (r % blk) < half) if rev else (lambda r: (r % blk) >= half)
        pair = ((rt // blk) == (ct // blk)) & q_side(rt) & jnp.logical_not(q_side(ct))
        levels.append((blk, jnp.where(q_side(rowi), LOG2E, -LOG2E), jnp.where(pair, 1.0, 0.0)))
        blk = half

    def stack(a):
        return jnp.concatenate([a[:, :W], a[:, W:]], axis=0)

    def chunk(ci, st):
        c = (n_chunks - 1 - ci) if rev else ci
        sl = pl.ds(pl.multiple_of(c * C, C), C)
        q = q_ref[sl, :]
        k = k_ref[sl, :]
        vb = v_ref[sl, :].astype(BF16)
        b = _chunk_cumsum(lf_ref[sl, :], rev)
        tot = b[0:1, :] if rev else b[C - 1:C, :]

        o = lax.dot_general(stack((q * jnp.exp(b)).astype(BF16)), st.astype(BF16), NT_DIMS,
                            preferred_element_type=F32)
        kdec = (k * jnp.exp(tot - b)).astype(BF16)
        upd = lax.dot_general(vb, kdec, TN_DIMS, preferred_element_type=F32)
        dec = jnp.exp(tot)
        st_next = jnp.concatenate([st[:W] * dec[:, :W] + upd[:W, :W],
                                   st[W:] * dec[:, W:] + upd[W:, W:]], axis=0)

        qk = q * k
        scores = [eye * jnp.sum(qk[:, :W], axis=1, keepdims=True),
                  eye * jnp.sum(qk[:, W:], axis=1, keepdims=True)]
        for blk, side, pair in levels:
            w = jnp.exp2((b - _pivot_rows(b, blk, rev)) * side)
            z = stack((jnp.where(side > 0.0, q, k) * w).astype(BF16))
            g = lax.dot_general(z, z, NT_DIMS, preferred_element_type=F32)
            scores = [scores[0] + pair * g[:C, :C], scores[1] + pair * g[C:, C:]]
        intra = jnp.dot(jnp.concatenate(scores, axis=0).astype(BF16), vb,
                        preferred_element_type=F32)
        o_ref[sl, :] = jnp.concatenate([o[:C, :W] + intra[:C, :W], o[C:, W:] + intra[C:, W:]],
                                       axis=1)
        return st_next

    st_ref[...] = lax.fori_loop(0, n_chunks, chunk, st_ref[...], unroll=HG_UNROLL)


def _hgrn(q, k, lf, v, rev, weights, rb=2048):
    S = q.shape[0]
    nb = S // rb
    n_pairs = N_HEADS // 2
    steps = n_pairs * nb
    idx = (lambda h, j: (nb - 1 - j, h)) if rev else (lambda h, j: (j, h))
    spec = pl.BlockSpec((rb, 2 * HEAD_W), idx)
    w_specs = []
    for w in weights:
        n_exp, rows, cols = w.shape
        per_step = n_exp // steps
        assert per_step * steps == n_exp
        w_specs.append(pl.BlockSpec((per_step, rows, cols), lambda h, j: (h * nb + j, 0, 0)))
    return pl.pallas_call(
        functools.partial(_hgrn_kernel, rev=rev, n_chunks=rb // HG_CHUNK, n_cast=len(weights)),
        grid=(n_pairs, nb),
        in_specs=[spec] * 4 + w_specs,
        out_specs=[spec] + w_specs,
        out_shape=[jax.ShapeDtypeStruct((S, HALF_W), F32)]
        + [jax.ShapeDtypeStruct(w.shape, BF16) for w in weights],
        scratch_shapes=[pltpu.VMEM((2 * HEAD_W, HEAD_W), F32)],
        compiler_params=pltpu.CompilerParams(dimension_semantics=("arbitrary", "arbitrary"),
                                             vmem_limit_bytes=V7X_VMEM_LIMIT),
        name="hgrn_bwd" if rev else "hgrn_fwd",
    )(q, k, lf, v, *weights)


def _attn_kernel(lam_ref, q_ref, k_ref, vt_ref, gain_ref, o_ref, sa_ref, sb_ref, m_ref, acc_ref,
                 *, tk):
    n_kv = k_ref.shape[0] // tk
    tq = q_ref.shape[0]
    assert n_kv % KV_UNROLL == 0 and KV_UNROLL % 2 == 0 and tq % QUERY_GROUP == 0
    q = q_ref[...]
    lane = lax.broadcasted_iota(jnp.int32, q.shape, 1)
    qm = (jnp.where(lane < QK_DIM, q, jnp.zeros_like(q)),
          jnp.where(lane >= QK_DIM, q, jnp.zeros_like(q)))
    m_ref[...] = jnp.full(m_ref.shape, -jnp.inf, F32)
    acc_ref[...] = jnp.zeros_like(acc_ref)

    def chunk(j):
        return pl.ds(pl.multiple_of(j * tk, tk), tk)

    def softmax_pv(c, cols, s, vt):
        m_old = m_ref[c, :, cols]
        m_new = jnp.maximum(m_old, jnp.max(s, axis=0, keepdims=True))
        p = jnp.exp2(s - m_new).astype(BF16)
        acc_ref[c, :, cols] = jnp.exp2(m_old - m_new) * acc_ref[c, :, cols] + jnp.dot(
            vt, p, preferred_element_type=F32)
        m_ref[c, :, cols] = m_new

    def step(js, dst_ref, jp, src_ref):
        kc = None if dst_ref is None else k_ref[chunk(js), :]
        vt = None if src_ref is None else vt_ref[:, chunk(jp)]
        for c in range(2):
            for g in range(0, tq, QUERY_GROUP):
                cols = slice(g, g + QUERY_GROUP)
                if dst_ref is not None:
                    dst_ref[c, :, cols] = lax.dot_general(kc, qm[c][cols, :], NT_DIMS,
                                                          preferred_element_type=F32)
                if src_ref is not None:
                    softmax_pv(c, cols, src_ref[c, :, cols], vt)

    bufs = (sa_ref, sb_ref)
    step(0, sa_ref, None, None)

    def body(i, carry):
        for u in range(KV_UNROLL):
            j = KV_UNROLL * i + u
            step(j + 1, bufs[(u + 1) % 2], j, bufs[u % 2])
        return carry

    lax.fori_loop(0, n_kv // KV_UNROLL - 1, body, 0)
    for j in range(n_kv - KV_UNROLL, n_kv - 1):
        step(j + 1, bufs[(j + 1) % 2], j, bufs[j % 2])
    step(None, None, n_kv - 1, bufs[(n_kv - 1) % 2])

    outs = []
    for c in range(2):
        a = acc_ref[c]
        outs.append(a[:HEAD_W, :] / a[HEAD_W:HEAD_W + 1, :])
    o = (outs[0] - lam_ref[0] * outs[1]).T
    o_ref[...] = _rms(o, SUBLN_EPS) * gain_ref[...] * (1.0 - LAMBDA_INIT)


def _attn(lam, dq, dk, dvt, gain, tq=2048, tk=512):
    S = dq.shape[0]
    return pl.pallas_call(
        functools.partial(_attn_kernel, tk=tk),
        grid=(N_HEADS, S // tq),
        in_specs=[pl.BlockSpec(memory_space=pltpu.SMEM),
                  pl.BlockSpec((tq, HEAD_W), lambda h, i: (i, h)),
                  pl.BlockSpec((S, HEAD_W), lambda h, i: (0, h)),
                  pl.BlockSpec((VT_ROWS, S), lambda h, i: (h, 0)),
                  pl.BlockSpec((1, HEAD_W), lambda h, i: (0, 0))],
        out_specs=pl.BlockSpec((tq, HEAD_W), lambda h, i: (i, h)),
        out_shape=jax.ShapeDtypeStruct((S, HALF_W), F32),
        scratch_shapes=[pltpu.VMEM((2, tk, tq), F32), pltpu.VMEM((2, tk, tq), F32),
                        pltpu.VMEM((2, 1, tq), F32), pltpu.VMEM((2, VT_ROWS, tq), F32)],
        compiler_params=pltpu.CompilerParams(dimension_semantics=("arbitrary", "arbitrary"),
                                             vmem_limit_bytes=V7X_VMEM_LIMIT),
        name="diff_attn",
    )(lam, dq, dk, dvt, gain)


def _outproj_kernel(x_ref, of_ref, ob_ref, hg_ref, oda_ref, hgain_ref, wo_ref, g2_ref,
                    wr_ref, br_ref, x2_ref, rank_t_ref, gates_t_ref, cnt_ref,
                    *, tb):
    i = pl.program_id(0)
    part = x_ref.shape[0]
    n_routing = part // tb

    def project(rows):
        o = of_ref[rows, :] + ob_ref[rows, :]
        gate = hg_ref[rows, :]
        gate = gate * jax.nn.sigmoid(gate)
        parts = []
        for hd in range(N_HEADS):
            sl = slice(hd * HEAD_W, (hd + 1) * HEAD_W)
            parts.append(_rms(o[:, sl], NORM_EPS) * hgain_ref[...] * gate[:, sl])
        parts.append(oda_ref[rows, :])
        mixed = jnp.concatenate(parts, axis=1).astype(BF16)
        x2 = x_ref[rows, :] + jnp.dot(mixed, wo_ref[...], preferred_element_type=F32)
        x2_ref[rows, :] = x2
        return x2

    lane = lax.broadcasted_iota(jnp.int32, (part, ROUTER_LANES), 1)
    neg = -jnp.inf
    rt = lax.broadcasted_iota(jnp.int32, (part, part), 0)
    ct = lax.broadcasted_iota(jnp.int32, (part, part), 1)
    tri = jnp.where((rt >= ct) & (rt // tb == ct // tb), 1.0, 0.0).astype(BF16)

    def first_max(vals):
        mx = jnp.max(vals, axis=1, keepdims=True)
        idx = jnp.min(jnp.where(vals == mx, lane, ROUTER_LANES), axis=1, keepdims=True)
        return mx, idx

    def route(rows, x2):
        h2 = _rms(x2, NORM_EPS) * g2_ref[...]
        h_hi = h2.astype(BF16)
        h_lo = (h2 - h_hi.astype(F32)).astype(BF16)
        t = jnp.dot(h_hi, wr_ref[...], preferred_element_type=F32)
        logits = (t[:, :ROUTER_LANES] + t[:, ROUTER_LANES:] + br_ref[...]
                  + jnp.dot(h_lo, wr_ref[:, :ROUTER_LANES], preferred_element_type=F32))
        gl = jnp.where((lane >= N_EXPERTS) & (lane < N_EXPERTS + N_GROUPS), logits, neg)
        gm, gidx = first_max(gl)
        g_p = 1.0 / jnp.sum(jnp.exp(gl - gm), axis=1, keepdims=True)
        el = jnp.where((lane < N_EXPERTS) & ((lane // EXPERTS_PER_GROUP) == (gidx - N_EXPERTS)),
                       logits, neg)
        em1, idx1 = first_max(el)
        em2, idx2 = first_max(jnp.where(lane == idx1, neg, el))
        e2 = jnp.exp(em2 - em1)
        w1 = g_p / (1.0 + e2)
        w2 = g_p * e2 / (1.0 + e2)
        is1 = lane == idx1
        is2 = lane == idx2
        gates = jnp.where(is1, w1, 0.0) + jnp.where(is2, w2, 0.0)
        assigned = jnp.where(is1 | is2, 1.0, 0.0)
        cum = jnp.dot(tri, assigned.astype(BF16), preferred_element_type=F32)
        rank_t_ref[:, rows] = jnp.where(assigned > 0.0, cum - 1.0, -1.0).T
        gates_t_ref[:, rows] = gates.T
        for s in range(n_routing):
            last = (s + 1) * tb - 1
            cnt_ref[pl.ds(n_routing * i + s, 1), :] = cum[last:last + 1, :].astype(jnp.int32)

    rows = slice(0, part)
    route(rows, project(rows))


def _outproj(x, o_f, o_b, hg, o_da, hgain, w_out, g2, wr, br, tb, tm=1024):
    S = x.shape[0]
    row = lambda w: pl.BlockSpec((tm, w), lambda i: (i, 0))
    col = pl.BlockSpec((ROUTER_LANES, tm), lambda i: (0, i))
    full = lambda a: pl.BlockSpec(a.shape, lambda i: (0, 0))
    return pl.pallas_call(
        functools.partial(_outproj_kernel, tb=tb),
        grid=(S // tm,),
        in_specs=[row(D_MODEL), row(HALF_W), row(HALF_W), row(HALF_W), row(HALF_W),
                  full(hgain), full(w_out), full(g2), full(wr), full(br)],
        out_specs=[row(D_MODEL), col, col,
                   pl.BlockSpec((S // tb, ROUTER_LANES), lambda i: (0, 0))],
        out_shape=[jax.ShapeDtypeStruct((S, D_MODEL), F32),
                   jax.ShapeDtypeStruct((ROUTER_LANES, S), F32),
                   jax.ShapeDtypeStruct((ROUTER_LANES, S), F32),
                   jax.ShapeDtypeStruct((S // tb, ROUTER_LANES), jnp.int32)],
        compiler_params=pltpu.CompilerParams(
            dimension_semantics=("arbitrary",), vmem_limit_bytes=V7X_VMEM_LIMIT,
            allow_input_fusion=[False] * 6 + [True] + [False] * 3),
        name="outproj_router",
    )(x, o_f, o_b, hg, o_da, hgain, w_out, g2, wr, br)


def _moe_kernel(cnt_ref, x2_hbm, rank_t_ref, gates_t_ref, g2_ref, wg_ref, wu_ref,
                wd_ref, gf_ref, out_ref, h2_ref, x2_buf, x2_sem, *, tile_rows, tb):
    b = pl.program_id(0)
    p = pl.program_id(1)
    R = tile_rows
    n_exp = wg_ref.shape[0]
    rows = out_ref.shape[0]
    n_sub = rows // tb

    def fetch(blk):
        return pltpu.make_async_copy(
            x2_hbm.at[pl.ds(pl.multiple_of(blk * rows, rows), rows), :], x2_buf, x2_sem)

    @pl.when((b == 0) & (p == 0))
    def _():
        fetch(0).start()

    @pl.when(p == 0)
    def _():
        fetch(b).wait()
        x2 = x2_buf[...]
        h2_ref[...] = (_rms(x2, NORM_EPS) * g2_ref[...]).astype(BF16)
        out_ref[...] = x2

    @pl.when((p == 1) & (b + 1 < pl.num_programs(0)))
    def _():
        fetch(b + 1).start()

    experts = [n_exp * p + i for i in range(n_exp)]
    subs = [slice(s * tb, (s + 1) * tb) for s in range(n_sub)]
    rank_row = [[rank_t_ref[pl.ds(e, 1), sub] for e in experts] for sub in subs]
    gate_row = [[gates_t_ref[pl.ds(e, 1), sub] for e in experts] for sub in subs]
    n_max = jnp.int32(0)
    for s in range(n_sub):
        for e in experts:
            n_max = jnp.maximum(n_max, cnt_ref[n_sub * b + s, e])

    def tile(t, carry):
        base = (t * R).astype(F32)
        slot_col = lax.broadcasted_iota(jnp.int32, (R, 1), 0).astype(F32) + base
        sel, gather, xg = [], [], []
        for s, sub in enumerate(subs):
            sel.append([rank_row[s][i] == slot_col for i in range(n_exp)])
            gather.append(jnp.concatenate(
                [jnp.where(m, 1.0, 0.0).astype(BF16) for m in sel[s]], axis=0))
            xg.append(jnp.dot(gather[s], h2_ref[sub, :],
                              preferred_element_type=F32).astype(BF16))
        ys = [[] for _ in subs]
        for i in range(n_exp):
            xi = jnp.concatenate([xg[s][i * R:(i + 1) * R] for s in range(n_sub)], axis=0)
            a = jnp.dot(xi, wg_ref[i], preferred_element_type=F32)
            a = a * jax.nn.sigmoid(a) * jnp.dot(xi, wu_ref[i], preferred_element_type=F32)
            y = jnp.dot(a.astype(BF16), wd_ref[i], preferred_element_type=F32)
            for s in range(n_sub):
                g = jnp.sum(jnp.where(sel[s][i], gate_row[s][i], 0.0), axis=1, keepdims=True)
                ys[s].append((y[s * R:(s + 1) * R] * g).astype(BF16))
        for s, sub in enumerate(subs):
            out_ref[sub, :] += lax.dot_general(gather[s], jnp.concatenate(ys[s], axis=0),
                                               TN_DIMS, preferred_element_type=F32)
        return carry

    lax.fori_loop(0, (n_max + R - 1) // R, tile, 0)

    @pl.when(p == pl.num_programs(1) - 1)
    def _():
        out_ref[...] = _rms(out_ref[...], NORM_EPS) * gf_ref[...]


def _moe(cnt, x2, rank_t, gates_t, g2, wg, wu, wd, gf, tb, tile_rows=32,
         experts_per_step=4, blocks_per_step=8):
    S = x2.shape[0]
    eps = experts_per_step
    rows = tb * blocks_per_step
    grid_spec = pltpu.PrefetchScalarGridSpec(
        num_scalar_prefetch=1,
        grid=(S // rows, N_EXPERTS // eps),
        in_specs=[pl.BlockSpec(memory_space=pl.ANY),
                  pl.BlockSpec((N_EXPERTS, rows), lambda b, p, c: (0, b)),
                  pl.BlockSpec((N_EXPERTS, rows), lambda b, p, c: (0, b)),
                  pl.BlockSpec((1, D_MODEL), lambda b, p, c: (0, 0)),
                  pl.BlockSpec((eps, D_MODEL, D_EXPERT), lambda b, p, c: (p, 0, 0)),
                  pl.BlockSpec((eps, D_MODEL, D_EXPERT), lambda b, p, c: (p, 0, 0)),
                  pl.BlockSpec((eps, D_EXPERT, D_MODEL), lambda b, p, c: (p, 0, 0)),
                  pl.BlockSpec((1, D_MODEL), lambda b, p, c: (0, 0))],
        out_specs=pl.BlockSpec((rows, D_MODEL), lambda b, p, c: (b, 0)),
        scratch_shapes=[pltpu.VMEM((rows, D_MODEL), BF16), pltpu.VMEM((rows, D_MODEL), F32),
                        pltpu.SemaphoreType.DMA(())])
    return pl.pallas_call(
        functools.partial(_moe_kernel, tile_rows=tile_rows, tb=tb),
        grid_spec=grid_spec,
        out_shape=jax.ShapeDtypeStruct((S, D_MODEL), F32),
        compiler_params=pltpu.CompilerParams(dimension_semantics=("arbitrary", "arbitrary"),
                                             vmem_limit_bytes=V7X_VMEM_LIMIT),
        name="moe",
    )(cnt, x2, rank_t, gates_t, g2, wg, wu, wd, gf)


def kernel(x, positions, norm1_gain, w_in, hg_lower_bounds, hg_norm_gain, diff_lambda,
           diff_subln_gain, w_out, norm2_gain, router_group_w, router_group_b,
           router_expert_w, router_expert_b, moe_w_gate, moe_w_up, moe_w_down,
           final_norm_gain):
    B, S, _ = x.shape
    assert B == 1 and norm1_gain.shape[0] == 1
    layer = 0
    xs = x.reshape(S, D_MODEL)
    pos = positions.reshape(1, S).astype(F32)

    lb_cum = jnp.cumsum(jax.nn.softmax(hg_lower_bounds.astype(F32), axis=1), axis=1)
    lb = lb_cum[:, layer + 1] - lb_cum[:, 0]
    lam_p = diff_lambda[layer].astype(F32)
    lam = (jnp.exp(jnp.sum(lam_p[0] * lam_p[1])) - jnp.exp(jnp.sum(lam_p[2] * lam_p[3]))
           + LAMBDA_INIT).reshape(1)
    lane = jnp.arange(HEAD_W)
    inv_freq = jnp.float32(ROPE_THETA) ** (-jnp.arange(0, ROT_DIM, 2, dtype=F32) / ROT_DIM)
    in_rot = (lane % QK_DIM) < ROT_DIM
    n_freq = ROT_DIM // 2
    invf = inv_freq.reshape(n_freq, 1)
    sgn = jnp.where(in_rot, jnp.where((lane % QK_DIM) < n_freq, -1.0, 1.0),
                    0.0).astype(F32).reshape(1, HEAD_W)
    place = (in_rot[None, :] & (lane[None, :] % n_freq == jnp.arange(n_freq)[:, None])).astype(F32)
    pcos = jnp.tile(place, (4, 1)).astype(BF16)
    psin = jnp.tile(place * sgn, (4, 1)).astype(BF16)
    wr = jnp.zeros((D_MODEL, ROUTER_LANES), F32)
    wr = wr.at[:, :N_EXPERTS].set(router_expert_w[layer])
    wr = wr.at[:, N_EXPERTS:N_EXPERTS + N_GROUPS].set(router_group_w[layer])
    wr_hi = wr.astype(BF16)
    wr = jnp.concatenate([wr_hi, (wr - wr_hi.astype(F32)).astype(BF16)], axis=1)
    br = jnp.zeros((1, ROUTER_LANES), F32)
    br = br.at[0, :N_EXPERTS].set(router_expert_b[layer])
    br = br.at[0, N_EXPERTS:N_EXPERTS + N_GROUPS].set(router_group_b[layer])

    hq, kf, kb, lff, lfb, hi, hg, dq, dk, dv = _inproj(
        xs, pos, norm1_gain[layer].reshape(1, -1), w_in[layer].astype(BF16), lb, invf, sgn,
        pcos, psin)
    o_f, wg, wu = _hgrn(hq, kf, lff, hi, False, (moe_w_gate[layer], moe_w_up[layer]))
    o_b, wd = _hgrn(hq, kb, lfb, hi, True, (moe_w_down[layer],))
    o_da = _attn(lam, dq, dk, dv, diff_subln_gain[layer].reshape(1, -1))
    x2, rank_t, gates_t, cnt = _outproj(
        xs, o_f, o_b, hg, o_da, hg_norm_gain[layer].reshape(1, -1), w_out[layer].astype(BF16),
        norm2_gain[layer].reshape(1, -1), wr, br, tb=MOE_TOKEN_BLOCK)
    out = _moe(cnt, x2, rank_t, gates_t, norm2_gain[layer].reshape(1, -1), wg, wu, wd,
               final_norm_gain.reshape(1, -1), tb=MOE_TOKEN_BLOCK)
    return out.reshape(B, S, D_MODEL)
```
